```python
import jax, jax.numpy as jnp
from jax import lax
import numpy as np

D_MODEL = 1024
BATCH = 4
SEQ = 4096
DEPTH = 1
DEC_BATCH = 128
DEC_SEQ = 8
PAST_LEN = 8192
PAGE_SIZE = 128

MLA_HEADS = 8
MLA_NOPE = 64
MLA_ROPE = 32
MLA_V = 64
MLA_Q_LORA = 384
MLA_KV_LORA = 256
MLA_WIDTH = MLA_HEADS * MLA_V
MLA_SCALE = (MLA_NOPE + MLA_ROPE) ** -0.5
Q_BLOCK = 128
ROPE_THETA = 10000.0
GLA_HEADS = 4
GLA_DK = 64
GLA_DV = 128
GLA_WIDTH = GLA_HEADS * GLA_DV
GLA_GATE_RANK = 16
GLA_GATE_NORM = 16.0
GLA_CHUNK = 16
D_MIX = MLA_WIDTH + GLA_WIDTH
IN_SIZES = (MLA_Q_LORA, MLA_KV_LORA, MLA_ROPE, GLA_HEADS * GLA_DK, GLA_HEADS * GLA_DK,
            GLA_WIDTH, GLA_GATE_RANK, GLA_WIDTH)
D_IN = sum(IN_SIZES)
N_EXPERTS = 32
TOP_K = 4
D_FF = D_MODEL
SWIGLU_LIMIT = 7.0
SWIGLU_ALPHA = 1.702
MOE_BLOCK = 128
N_MOD = 6
EPS = 1e-6

kernel_name = 'hymba_mla_gla_moe_adaln_step'


def rms_norm(x, g):
    xf = x.astype(jnp.float32)
    y = xf * lax.rsqrt(jnp.mean(xf * xf, axis=-1, keepdims=True) + EPS)
    return (y * g.astype(jnp.float32)).astype(x.dtype)


def ada_terms(c, w, b, n):
    return jnp.split(jax.nn.silu(c) @ w + b, n, axis=-1)


def modulate(h, shift, scale):
    return h * (1.0 + scale[:, None]) + shift[:, None]


def rope(x, pos):
    half = x.shape[-1] // 2
    inv = ROPE_THETA ** (-jnp.arange(half, dtype=jnp.float32) / half)
    ang = pos.astype(jnp.float32)[:, None] * inv
    cos, sin = jnp.cos(ang)[:, None, :], jnp.sin(ang)[:, None, :]
    xf = x.astype(jnp.float32)
    x1, x2 = xf[..., :half], xf[..., half:]
    return jnp.concatenate([x1 * cos - x2 * sin, x2 * cos + x1 * sin], axis=-1).astype(x.dtype)


def mixer_inputs(h, pos, lw):
    B, T, _ = h.shape
    proj = h @ lw['w_in']
    q_a, kv_lat, k_rope, gq, gk, gv, g_lr, gg = jnp.split(
        proj, np.cumsum(IN_SIZES)[:-1].tolist(), axis=-1)
    q = jnp.einsum('btr,rhd->bthd', rms_norm(q_a, lw['g_q_a']), lw['w_q_b'])
    q_nope, q_pe = q[..., :MLA_NOPE], rope(q[..., MLA_NOPE:], pos)
    c_kv = rms_norm(kv_lat, lw['g_kv_a'])
    k_pe = rope(k_rope[:, :, None, :], pos)[:, :, 0]
    gq = gq.reshape(B, T, GLA_HEADS, GLA_DK) * GLA_DK ** -0.5
    gk = gk.reshape(B, T, GLA_HEADS, GLA_DK)
    gv = gv.reshape(B, T, GLA_HEADS, GLA_DV)
    log_a = jax.nn.log_sigmoid((g_lr @ lw['w_gk_b'] + lw['b_gk']).astype(jnp.float32)) / GLA_GATE_NORM
    log_a = log_a.reshape(B, T, GLA_HEADS, GLA_DK)
    return (q_nope, q_pe, c_kv, k_pe), (gq, gk, gv, log_a, gg)


def mla_prefill(q_nope, q_pe, c_kv, k_pe, w_kv_b):
    B, S, H, _ = q_nope.shape
    kv = jnp.einsum('bsc,chd->bshd', c_kv, w_kv_b)
    k_nope, v = kv[..., :MLA_NOPE], kv[..., MLA_NOPE:]
    nqb = S // Q_BLOCK
    qn = jnp.swapaxes(q_nope.reshape(B, nqb, Q_BLOCK, H, MLA_NOPE), 0, 1)
    qp = jnp.swapaxes(q_pe.reshape(B, nqb, Q_BLOCK, H, MLA_ROPE), 0, 1)
    kpos = jnp.arange(S)

    def block(args):
        qn_b, qp_b, i = args
        s = (jnp.einsum('bqhd,bkhd->bhqk', qn_b, k_nope)
             + jnp.einsum('bqhr,bkr->bhqk', qp_b, k_pe)).astype(jnp.float32) * MLA_SCALE
        qpos = i * Q_BLOCK + jnp.arange(Q_BLOCK)
        s = jnp.where(kpos[None, :] <= qpos[:, None], s, -jnp.inf)
        p = jax.nn.softmax(s, axis=-1).astype(v.dtype)
        return jnp.einsum('bhqk,bkhd->bqhd', p, v)

    o = lax.map(block, (qn, qp, jnp.arange(nqb)))
    return jnp.swapaxes(o, 0, 1).reshape(B, S, H, MLA_V)


def mla_decode(q_nope, q_pe, c_kv_new, k_pe_new, ckv_past, kpe_past, w_kv_b):
    T = q_nope.shape[1]
    w_uk, w_uv = w_kv_b[..., :MLA_NOPE], w_kv_b[..., MLA_NOPE:]
    q_lat = jnp.einsum('bthd,chd->bthc', q_nope, w_uk)
    s_past = (jnp.einsum('bthc,blc->bhtl', q_lat, ckv_past)
              + jnp.einsum('bthr,blr->bhtl', q_pe, kpe_past)).astype(jnp.float32) * MLA_SCALE
    s_new = (jnp.einsum('bthc,buc->bhtu', q_lat, c_kv_new)
             + jnp.einsum('bthr,bur->bhtu', q_pe, k_pe_new)).astype(jnp.float32) * MLA_SCALE
    causal = jnp.tril(jnp.ones((T, T), bool))
    s_new = jnp.where(causal, s_new, -jnp.inf)
    p = jax.nn.softmax(jnp.concatenate([s_past, s_new], axis=-1), axis=-1).astype(ckv_past.dtype)
    P = ckv_past.shape[1]
    o_lat = (jnp.einsum('bhtl,blc->bthc', p[..., :P], ckv_past)
             + jnp.einsum('bhtu,buc->bthc', p[..., P:], c_kv_new))
    return jnp.einsum('bthc,chd->bthd', o_lat, w_uv)


def gla_recurrence(q, k, v, log_a, s0):
    B, T, H, _ = q.shape
    DV = v.shape[-1]
    pad = (-T) % GLA_CHUNK
    f32 = jnp.float32

    def blocks(a):
        a = jnp.pad(a.astype(f32), ((0, 0), (0, pad), (0, 0), (0, 0)))
        return jnp.moveaxis(a.reshape(B, -1, GLA_CHUNK, H, a.shape[-1]), 1, 0)

    causal = jnp.tril(jnp.ones((GLA_CHUNK, GLA_CHUNK), bool))[None, :, :, None, None]

    def step(S, blk):
        qc, kc, vc, gc = blk
        b = jnp.cumsum(gc, axis=1)
        o_inter = jnp.einsum('bihd,bhde->bihe', qc * jnp.exp(b), S)
        rel = jnp.where(causal, b[:, :, None] - b[:, None, :], -jnp.inf)
        scores = jnp.einsum('bihd,bjhd,bijhd->bhij', qc, kc, jnp.exp(rel))
        o_intra = jnp.einsum('bhij,bjhe->bihe', scores, vc)
        b_last = b[:, -1]
        S = (jnp.exp(b_last)[..., None] * S
             + jnp.einsum('bjhd,bjhe->bhde', kc * jnp.exp(b_last[:, None] - b), vc))
        return S, o_inter + o_intra

    s_fin, o = lax.scan(step, s0.astype(f32), (blocks(q), blocks(k), blocks(v), blocks(log_a)))
    o = jnp.moveaxis(o, 0, 1).reshape(B, T + pad, H, DV)[:, :T]
    return o.astype(v.dtype), s_fin.astype(s0.dtype)


def mixer_output(o_mla, o_gla, gg, lw):
    B, T = o_mla.shape[:2]
    o_m = rms_norm(o_mla.reshape(B, T, MLA_WIDTH), lw['g_mla_out'])
    o_g = rms_norm(o_gla, lw['g_gla_out']) * jax.nn.silu(gg).reshape(B, T, GLA_HEADS, GLA_DV)
    return jnp.concatenate([o_m, o_g.reshape(B, T, GLA_WIDTH)], axis=-1) @ lw['w_o']


def moe_ffn(x2, lw):
    N, D = x2.shape
    logits = (x2 @ lw['w_router'] + lw['b_router']).astype(jnp.float32)
    top_v, top_i = lax.top_k(logits, TOP_K)
    gates = jax.nn.softmax(top_v, axis=-1).astype(x2.dtype)
    n_pairs = N * TOP_K
    flat_e = top_i.reshape(-1).astype(jnp.int32)
    order = jnp.argsort(flat_e)
    sorted_e = flat_e[order]
    counts = jnp.zeros((N_EXPERTS,), jnp.int32).at[flat_e].add(1)
    starts = jnp.cumsum(counts) - counts
    padded = (counts + MOE_BLOCK - 1) // MOE_BLOCK * MOE_BLOCK
    pad_ends = jnp.cumsum(padded)
    pad_starts = pad_ends - padded
    dest = pad_starts[sorted_e] + (jnp.arange(n_pairs, dtype=jnp.int32) - starts[sorted_e])
    n_blocks = -(-n_pairs // MOE_BLOCK) + N_EXPERTS
    xs = jnp.zeros((n_blocks * MOE_BLOCK, D), x2.dtype).at[dest].set(x2[order // TOP_K])
    block_e = jnp.minimum(
        jnp.searchsorted(pad_ends, jnp.arange(n_blocks, dtype=jnp.int32) * MOE_BLOCK, side='right'),
        N_EXPERTS - 1)

    def expert_block(args):
        xb, e = args
        hu = xb @ lw['w_up'][e] + lw['b_up'][e]
        gate = jnp.minimum(hu[:, :D_FF], SWIGLU_LIMIT)
        lin = jnp.clip(hu[:, D_FF:], -SWIGLU_LIMIT, SWIGLU_LIMIT)
        act = gate * jax.nn.sigmoid(SWIGLU_ALPHA * gate) * (lin + 1.0)
        return act @ lw['w_down'][e] + lw['b_down'][e]

    ys = lax.map(expert_block, (xs.reshape(n_blocks, MOE_BLOCK, D), block_e)).reshape(-1, D)
    y_pairs = jnp.zeros((n_pairs, D), x2.dtype).at[order].set(ys[dest])
    return jnp.einsum('nkd,nk->nd', y_pairs.reshape(N, TOP_K, D), gates)


def layer_forward(x, c, pos, gla_s0, attend, lw):
    sh_a, sc_a, gt_a, sh_f, sc_f, gt_f = ada_terms(c, lw['w_ada'], lw['b_ada'], N_MOD)
    h = modulate(rms_norm(x, lw['g_norm_mix']), sh_a, sc_a)
    (q_nope, q_pe, c_kv, k_pe), (gq, gk, gv, log_a, gg) = mixer_inputs(h, pos, lw)
    o_mla = attend(q_nope, q_pe, c_kv, k_pe)
    o_gla, gla_s = gla_recurrence(gq, gk, gv, log_a, gla_s0)
    x = x + gt_a[:, None] * mixer_output(o_mla, o_gla, gg, lw)
    h = modulate(rms_norm(x, lw['g_norm_ffn']), sh_f, sc_f)
    B, T, D = h.shape
    x = x + gt_f[:, None] * moe_ffn(h.reshape(B * T, D), lw).reshape(B, T, D)
    return x, c_kv, k_pe, gla_s


def setup_inputs(seed: int = 0) -> dict:
    key = jax.random.key(seed)
    ks = jax.random.split(key, 32)
    f32 = jnp.float32

    def nrm(k, shape, scale):
        return jax.random.normal(k, shape, f32) * scale

    def gain(k, shape):
        return 1.0 + 0.05 * jax.random.normal(k, shape, f32)

    n_pages = PAST_LEN // PAGE_SIZE
    n_used = DEC_BATCH * n_pages
    n_pool = n_used + n_used // 4
    page_table = jax.random.permutation(ks[6], n_pool)[:n_used].reshape(DEC_BATCH, n_pages).astype(jnp.int32)
    return {
        'x_prompt': nrm(ks[0], (BATCH, SEQ, D_MODEL), 1.0),
        'x_sample': nrm(ks[1], (DEC_BATCH, DEC_SEQ, D_MODEL), 1.0),
        'cache_ckv': nrm(ks[2], (DEPTH, n_pool, PAGE_SIZE, MLA_KV_LORA), 1.0),
        'cache_kpe': nrm(ks[3], (DEPTH, n_pool, PAGE_SIZE, MLA_ROPE), 1.0),
        'state_gla': nrm(ks[4], (DEPTH, DEC_BATCH, GLA_HEADS, GLA_DK, GLA_DV), 0.5),
        'page_table': page_table,
        'c_prompt': nrm(ks[5], (BATCH, D_MODEL), 1.0),
        'c_sample': nrm(ks[7], (DEC_BATCH, D_MODEL), 1.0),
        'w_ada': nrm(ks[8], (DEPTH, D_MODEL, N_MOD * D_MODEL), 0.5 * D_MODEL ** -0.5),
        'b_ada': nrm(ks[9], (DEPTH, N_MOD * D_MODEL), 0.02),
        'g_norm_mix': gain(ks[10], (DEPTH, D_MODEL)),
        'w_in': nrm(ks[11], (DEPTH, D_MODEL, D_IN), D_MODEL ** -0.5),
        'g_q_a': gain(ks[12], (DEPTH, MLA_Q_LORA)),
        'w_q_b': nrm(ks[13], (DEPTH, MLA_Q_LORA, MLA_HEADS, MLA_NOPE + MLA_ROPE), MLA_Q_LORA ** -0.5),
        'g_kv_a': gain(ks[14], (DEPTH, MLA_KV_LORA)),
        'w_kv_b': nrm(ks[15], (DEPTH, MLA_KV_LORA, MLA_HEADS, MLA_NOPE + MLA_V), MLA_KV_LORA ** -0.5),
        'w_gk_b': nrm(ks[16], (DEPTH, GLA_GATE_RANK, GLA_HEADS * GLA_DK), GLA_GATE_RANK ** -0.5),
        'b_gk': nrm(ks[17], (DEPTH, GLA_HEADS * GLA_DK), 0.1),
        'g_mla_out': gain(ks[18], (DEPTH, MLA_WIDTH)),
        'g_gla_out': gain(ks[19], (DEPTH, GLA_DV)),
        'w_o': nrm(ks[20], (DEPTH, D_MIX, D_MODEL), D_MIX ** -0.5),
        'g_norm_ffn': gain(ks[21], (DEPTH, D_MODEL)),
        'w_router': nrm(ks[22], (DEPTH, D_MODEL, N_EXPERTS), D_MODEL ** -0.5),
        'b_router': nrm(ks[23], (DEPTH, N_EXPERTS), 0.01),
        'w_up': nrm(ks[24], (DEPTH, N_EXPERTS, D_MODEL, 2 * D_FF), D_MODEL ** -0.5),
        'b_up': nrm(ks[25], (DEPTH, N_EXPERTS, 2 * D_FF), 0.01),
        'w_down': nrm(ks[26], (DEPTH, N_EXPERTS, D_FF, D_MODEL), D_FF ** -0.5),
        'b_down': nrm(ks[27], (DEPTH, N_EXPERTS, D_MODEL), 0.01),
        'g_norm_final': gain(ks[28], (D_MODEL,)),
        'w_ada_final': nrm(ks[29], (D_MODEL, 2 * D_MODEL), 0.5 * D_MODEL ** -0.5),
        'b_ada_final': nrm(ks[30], (2 * D_MODEL,), 0.02),
    }


def reference(x_prompt, x_sample, cache_ckv, cache_kpe, state_gla, page_table, c_prompt, c_sample,
              w_ada, b_ada, g_norm_mix, w_in, g_q_a, w_q_b, g_kv_a, w_kv_b, w_gk_b, b_gk,
              g_mla_out, g_gla_out, w_o, g_norm_ffn, w_router, b_router, w_up, b_up, w_down, b_down,
              g_norm_final, w_ada_final, b_ada_final):
    B, S, _ = x_prompt.shape
    DB, T, _ = x_sample.shape
    past_len = page_table.shape[1] * cache_ckv.shape[2]
    pos_p = jnp.arange(S)
    pos_s = past_len + jnp.arange(T)
    xp, xs = x_prompt, x_sample
    ckv_p_l, kpe_p_l, gla_p_l, ckv_s_l, kpe_s_l, gla_s_l = [], [], [], [], [], []
    for l in range(DEPTH):
        lw = {
            'w_ada': w_ada[l], 'b_ada': b_ada[l], 'g_norm_mix': g_norm_mix[l], 'w_in': w_in[l],
            'g_q_a': g_q_a[l], 'w_q_b': w_q_b[l], 'g_kv_a': g_kv_a[l], 'w_kv_b': w_kv_b[l],
            'w_gk_b': w_gk_b[l], 'b_gk': b_gk[l], 'g_mla_out': g_mla_out[l], 'g_gla_out': g_gla_out[l],
            'w_o': w_o[l], 'g_norm_ffn': g_norm_ffn[l], 'w_router': w_router[l], 'b_router': b_router[l],
            'w_up': w_up[l], 'b_up': b_up[l], 'w_down': w_down[l], 'b_down': b_down[l],
        }
        s0 = jnp.zeros((B, GLA_HEADS, GLA_DK, GLA_DV), state_gla.dtype)
        xp, ckv_p, kpe_p, gla_p = layer_forward(
            xp, c_prompt, pos_p, s0,
            lambda qn, qp, ck, kp: mla_prefill(qn, qp, ck, kp, lw['w_kv_b']), lw)
        ckv_past = cache_ckv[l, page_table].reshape(DB, past_len, MLA_KV_LORA)
        kpe_past = cache_kpe[l, page_table].reshape(DB, past_len, MLA_ROPE)
        xs, ckv_s, kpe_s, gla_s = layer_forward(
            xs, c_sample, pos_s, state_gla[l],
            lambda qn, qp, ck, kp: mla_decode(qn, qp, ck, kp, ckv_past, kpe_past, lw['w_kv_b']), lw)
        ckv_p_l.append(ckv_p); kpe_p_l.append(kpe_p); gla_p_l.append(gla_p)
        ckv_s_l.append(ckv_s); kpe_s_l.append(kpe_s); gla_s_l.append(gla_s)
    sh_p, sc_p = ada_terms(c_prompt, w_ada_final, b_ada_final, 2)
    sh_s, sc_s = ada_terms(c_sample, w_ada_final, b_ada_final, 2)
    y_prompt = modulate(rms_norm(xp, g_norm_final), sh_p, sc_p)
    y_sample = modulate(rms_norm(xs, g_norm_final), sh_s, sc_s)
    ckv_prompt = jnp.stack(ckv_p_l)
    kpe_prompt = jnp.stack(kpe_p_l)
    gla_prompt = jnp.stack(gla_p_l)
    ckv_sample = jnp.stack(ckv_s_l)
    kpe_sample = jnp.stack(kpe_s_l)
    gla_sample = jnp.stack(gla_s_l)
    return (y_prompt, y_sample, ckv_prompt, kpe_prompt, gla_prompt, ckv_sample, kpe_sample, gla_sample)
```

```python
import functools
import math

import jax
import jax.numpy as jnp
import numpy as np
from jax import lax
from jax.experimental import pallas as pl
from jax.experimental.pallas import tpu as pltpu

F32 = jnp.float32
BF16 = jnp.bfloat16

D_MODEL = 1024
MLA_HEADS = 8
MLA_NOPE = 64
MLA_ROPE = 32
MLA_V = 64
MLA_Q_LORA = 384
MLA_KV_LORA = 256
MLA_SCALE = (MLA_NOPE + MLA_ROPE) ** -0.5
ROPE_THETA = 10000.0
GLA_HEADS = 4
GLA_DK = 64
GLA_DV = 128
GLA_GATE_RANK = 16
GLA_GATE_NORM = 16.0
GLA_CHUNK = 16
GLA_QK = GLA_HEADS * GLA_DK
GLA_WIDTH = GLA_HEADS * GLA_DV
N_EXPERTS = 32
TOP_K = 4
D_FF = D_MODEL
SWIGLU_LIMIT = 7.0
SWIGLU_ALPHA = 1.702
N_MOD = 6
EPS = 1e-6
PAGE_SIZE = 128

LANES = 128
HEAD_PAD = LANES
MLA_PAD = MLA_HEADS * HEAD_PAD
VMEM_LIMIT = 56 * 1024 * 1024

Q_SCALE = MLA_SCALE * math.log2(math.e)

C_QA = 0
C_KV = C_QA + MLA_Q_LORA
C_GQ = C_KV + MLA_KV_LORA
C_GK = C_GQ + GLA_QK
C_GV = C_GK + GLA_QK
C_GG = C_GV + GLA_WIDTH
C_MISC = C_GG + GLA_WIDTH
C_SWAP = C_MISC + LANES
D_IN_PAD = C_SWAP + LANES

ROW_TILE = 256
ATT_TILE = 512
GLA_BLOCK = 256
DEC_PAGES = 16
MOE_TILE = 256
NEW_PAD = 16


def _cparams(sem):
    return pltpu.CompilerParams(dimension_semantics=sem, vmem_limit_bytes=VMEM_LIMIT)


def _rms(x, g):
    return x * lax.rsqrt(jnp.mean(x * x, axis=-1, keepdims=True) + EPS) * g


def _ada_kernel(c_ref, w_ref, b_ref, o_ref):
    c = c_ref[...]
    a = (c * jax.nn.sigmoid(c)).astype(BF16)
    o_ref[...] = jnp.dot(a, w_ref[...].astype(BF16), preferred_element_type=F32) + b_ref[...]


def ada_terms(c, w, b):
    rows, d = c.shape
    n = w.shape[1]
    tn = 512
    return pl.pallas_call(
        _ada_kernel,
        grid=(n // tn,),
        in_specs=[pl.BlockSpec((rows, d), lambda j: (0, 0)),
                  pl.BlockSpec((d, tn), lambda j: (0, j)),
                  pl.BlockSpec((1, tn), lambda j: (0, j))],
        out_specs=pl.BlockSpec((rows, tn), lambda j: (0, j)),
        out_shape=jax.ShapeDtypeStruct((rows, n), F32),
        compiler_params=_cparams(("arbitrary",)),
        name="ada_terms",
    )(c, w, b.reshape(1, n))


def _in_kernel(with_kv, x_ref, sh_ref, sc_ref, g_ref, win_ref, gqa_ref, wq1_ref, wq2_ref,
               gkv_ref, wk_ref, wv_ref, wgk_ref, bgk_ref, qc_ref, qs_ref, kc_ref, ks_ref, *outs):
    if with_kv:
        q_out, k_out, v_out, ckv_out, kpe_out, gq_out, gk_out, gv_out, la_out, gg_out = outs
    else:
        q_out, ckv_out, kpe_out, gq_out, gk_out, gv_out, la_out, gg_out = outs
    x = x_ref[...]
    h = _rms(x, g_ref[...])
    h = h * (1.0 + sc_ref[0]) + sh_ref[0]
    proj = jnp.dot(h.astype(BF16), win_ref[...], preferred_element_type=F32)

    qn = _rms(proj[:, C_QA:C_KV], gqa_ref[...]).astype(BF16)
    qa = jnp.dot(qn, wq1_ref[...], preferred_element_type=F32)
    qb = jnp.dot(qn, wq2_ref[...], preferred_element_type=F32)
    qc = qc_ref[...]
    qs = qs_ref[...]
    for hh in range(MLA_HEADS):
        sl = slice(hh * HEAD_PAD, (hh + 1) * HEAD_PAD)
        q_out[:, sl] = (qa[:, sl] * qc + qb[:, sl] * qs).astype(BF16)

    ckv = _rms(proj[:, C_KV:C_GQ], gkv_ref[...])
    ckv_out[...] = ckv
    kpe = proj[:, C_MISC:C_SWAP] * kc_ref[...] + proj[:, C_SWAP:D_IN_PAD] * ks_ref[...]
    kpe_out[...] = kpe[:, :MLA_ROPE]
    if with_kv:
        ckv_b = ckv.astype(BF16)
        kn = jnp.dot(ckv_b, wk_ref[...], preferred_element_type=F32)
        kpe_sh = pltpu.roll(kpe, MLA_NOPE, axis=1)
        for hh in range(MLA_HEADS):
            sl = slice(hh * HEAD_PAD, (hh + 1) * HEAD_PAD)
            k_out[:, sl] = (kn[:, sl] + kpe_sh).astype(BF16)
        v_out[...] = jnp.dot(ckv_b, wv_ref[...], preferred_element_type=F32).astype(BF16)

    gq_out[...] = proj[:, C_GQ:C_GK] * (GLA_DK ** -0.5)
    gk_out[...] = proj[:, C_GK:C_GV]
    gv_out[...] = proj[:, C_GV:C_GG].astype(BF16)
    gg_out[...] = proj[:, C_GG:C_MISC].astype(BF16)
    xg = jnp.dot(proj[:, C_MISC:C_SWAP].astype(BF16), wgk_ref[...],
                 preferred_element_type=F32) + bgk_ref[...]
    la_out[...] = (jnp.minimum(xg, 0.0) - jnp.log(1.0 + jnp.exp(-jnp.abs(xg)))) * (1.0 / GLA_GATE_NORM)


def in_proj(x, sh, sc, mod_per_row, tabs, wts, with_kv):
    n = x.shape[0]
    tm = ROW_TILE
    nt = n // tm
    qc, qs, kc, ks = tabs
    n_tab = qc.shape[0] // tm
    row = lambda i: (i, 0)
    const = lambda i: (0, 0)
    if mod_per_row:
        mod_spec = pl.BlockSpec((1, tm, D_MODEL), lambda i: (0, i, 0))
    else:
        tiles_per_seq = n // sh.shape[0] // tm
        mod_spec = pl.BlockSpec((1, 1, D_MODEL), lambda i: (i // tiles_per_seq, 0, 0))
    tab_spec = pl.BlockSpec((tm, LANES), lambda i: (i % n_tab, 0))
    w = wts
    in_specs = [
        pl.BlockSpec((tm, D_MODEL), row), mod_spec, mod_spec,
        pl.BlockSpec((1, D_MODEL), const),
        pl.BlockSpec((D_MODEL, D_IN_PAD), const),
        pl.BlockSpec((1, MLA_Q_LORA), const),
        pl.BlockSpec((MLA_Q_LORA, MLA_PAD), const),
        pl.BlockSpec((MLA_Q_LORA, MLA_PAD), const),
        pl.BlockSpec((1, MLA_KV_LORA), const),
        pl.BlockSpec((MLA_KV_LORA, MLA_PAD), const),
        pl.BlockSpec((MLA_KV_LORA, MLA_PAD), const),
        pl.BlockSpec((LANES, GLA_QK), const),
        pl.BlockSpec((1, GLA_QK), const),
        tab_spec, tab_spec, tab_spec, tab_spec,
    ]
    wide = lambda dt: (jax.ShapeDtypeStruct((n, MLA_PAD), dt), pl.BlockSpec((tm, MLA_PAD), row))
    outs = [wide(BF16)]
    if with_kv:
        outs += [wide(BF16), wide(BF16)]
    outs += [
        (jax.ShapeDtypeStruct((n, MLA_KV_LORA), F32), pl.BlockSpec((tm, MLA_KV_LORA), row)),
        (jax.ShapeDtypeStruct((n, MLA_ROPE), F32), pl.BlockSpec((tm, MLA_ROPE), row)),
        (jax.ShapeDtypeStruct((n, GLA_QK), F32), pl.BlockSpec((tm, GLA_QK), row)),
        (jax.ShapeDtypeStruct((n, GLA_QK), F32), pl.BlockSpec((tm, GLA_QK), row)),
        (jax.ShapeDtypeStruct((n, GLA_WIDTH), BF16), pl.BlockSpec((tm, GLA_WIDTH), row)),
        (jax.ShapeDtypeStruct((n, GLA_QK), F32), pl.BlockSpec((tm, GLA_QK), row)),
        (jax.ShapeDtypeStruct((n, GLA_WIDTH), BF16), pl.BlockSpec((tm, GLA_WIDTH), row)),
    ]
    return pl.pallas_call(
        functools.partial(_in_kernel, with_kv),
        grid=(nt,),
        in_specs=in_specs,
        out_specs=[o[1] for o in outs],
        out_shape=[o[0] for o in outs],
        compiler_params=_cparams(("parallel",)),
        name="inproj_prompt" if with_kv else "inproj_sample",
    )(x, sh, sc, w["g_norm_mix"], w["w_in"], w["g_q_a"], w["wq1"], w["wq2"], w["g_kv_a"],
      w["wk"], w["wv"], w["wgk"], w["b_gk"], qc, qs, kc, ks)


def _prefill_kernel(q_ref, k_ref, v_ref, o_ref):
    qi = pl.program_id(2)
    t = ATT_TILE
    q = q_ref[0]

    def scores(j):
        kj = k_ref[0, pl.ds(pl.multiple_of(j * t, t), t), :]
        return lax.dot_general(q, kj, (((1,), (1,)), ((), ())), preferred_element_type=F32)

    def update(carry, s, j):
        m, l, acc = carry
        m_new = jnp.maximum(m, jnp.max(s, axis=-1, keepdims=True))
        p = jnp.exp2(s - m_new)
        alpha = jnp.exp2(m - m_new)
        vj = v_ref[0, pl.ds(pl.multiple_of(j * t, t), t), :]
        acc = alpha * acc + jnp.dot(p.astype(BF16), vj, preferred_element_type=F32)
        l = alpha * l + jnp.sum(p, axis=-1, keepdims=True)
        return m_new, l, acc

    def body(j, carry):
        return update(carry, scores(j), j)

    init = (jnp.full((t, 1), -jnp.inf, F32), jnp.zeros((t, 1), F32), jnp.zeros((t, HEAD_PAD), F32))
    carry = lax.fori_loop(0, qi, body, init)
    s = scores(qi)
    rows = lax.broadcasted_iota(jnp.int32, (t, t), 0)
    cols = lax.broadcasted_iota(jnp.int32, (t, t), 1)
    s = jnp.where(cols <= rows, s, -jnp.inf)
    m, l, acc = update(carry, s, qi)
    o_ref[0] = (acc / l).astype(BF16)


def mla_prefill(q, k, v, batch, seq):
    t = ATT_TILE
    q3 = q.reshape(batch, seq, MLA_PAD)
    k3 = k.reshape(batch, seq, MLA_PAD)
    v3 = v.reshape(batch, seq, MLA_PAD)
    o = pl.pallas_call(
        _prefill_kernel,
        grid=(batch, MLA_HEADS, seq // t),
        in_specs=[pl.BlockSpec((1, t, HEAD_PAD), lambda b, h, i: (b, i, h)),
                  pl.BlockSpec((1, seq, HEAD_PAD), lambda b, h, i: (b, 0, h)),
                  pl.BlockSpec((1, seq, HEAD_PAD), lambda b, h, i: (b, 0, h))],
        out_specs=pl.BlockSpec((1, t, HEAD_PAD), lambda b, h, i: (b, i, h)),
        out_shape=jax.ShapeDtypeStruct((batch, seq, MLA_PAD), BF16),
        compiler_params=_cparams(("parallel", "parallel", "arbitrary")),
        name="mla_prefill",
    )(q3, k3, v3)
    return o.reshape(batch * seq, MLA_PAD)


def _absorb_kernel(q_ref, wuk_ref, qlat_ref, qpe_ref):
    q = q_ref[...]
    qlat_ref[0] = jnp.dot(q, wuk_ref[0], preferred_element_type=F32)
    qf = pltpu.roll(q.astype(F32), HEAD_PAD - MLA_NOPE, axis=1)
    lane = lax.broadcasted_iota(jnp.int32, qf.shape, 1)
    qpe_ref[0] = jnp.where(lane < MLA_ROPE, qf, 0.0)


def absorb_q(q, wuk_t):
    r = q.shape[0]
    return pl.pallas_call(
        _absorb_kernel,
        grid=(MLA_HEADS,),
        in_specs=[pl.BlockSpec((r, HEAD_PAD), lambda h: (0, h)),
                  pl.BlockSpec((1, HEAD_PAD, MLA_KV_LORA), lambda h: (h, 0, 0))],
        out_specs=[pl.BlockSpec((1, r, MLA_KV_LORA), lambda h: (h, 0, 0)),
                   pl.BlockSpec((1, r, HEAD_PAD), lambda h: (h, 0, 0))],
        out_shape=[jax.ShapeDtypeStruct((MLA_HEADS, r, MLA_KV_LORA), F32),
                   jax.ShapeDtypeStruct((MLA_HEADS, r, HEAD_PAD), F32)],
        compiler_params=_cparams(("parallel",)),
        name="absorb_q",
    )(q, wuk_t)


def _decode_kernel(n_steps, t_new, pt_ref, qlat_ref, qpe_ref, cnew_ref, knew_ref, *rest):
    ck_refs = rest[:DEC_PAGES]
    kp_refs = rest[DEC_PAGES:2 * DEC_PAGES]
    o_ref = rest[2 * DEC_PAGES]
    kbuf, pbuf, m_sc, l_sc, acc_sc = rest[2 * DEC_PAGES + 1:]
    step = pl.program_id(1)
    rows = MLA_HEADS * t_new
    qlat = qlat_ref[...].reshape(rows, MLA_KV_LORA).astype(BF16)
    qpe = qpe_ref[...].reshape(rows, HEAD_PAD)[:, :MLA_ROPE].astype(BF16)

    @pl.when(step == 0)
    def _():
        m_sc[...] = jnp.full(m_sc.shape, -jnp.inf, F32)
        l_sc[...] = jnp.zeros(l_sc.shape, F32)
        acc_sc[...] = jnp.zeros(acc_sc.shape, F32)

    for p in range(DEC_PAGES):
        kbuf[p * PAGE_SIZE:(p + 1) * PAGE_SIZE, :] = ck_refs[p][0, 0].astype(BF16)
        pbuf[p * PAGE_SIZE:(p + 1) * PAGE_SIZE, :] = kp_refs[p][0, 0].astype(BF16)

    def attend(s, values):
        m = m_sc[...]
        m_new = jnp.maximum(m, jnp.max(s, axis=-1, keepdims=True))
        p = jnp.exp2(s - m_new)
        alpha = jnp.exp2(m - m_new)
        l_sc[...] = alpha * l_sc[...] + jnp.sum(p, axis=-1, keepdims=True)
        acc_sc[...] = alpha * acc_sc[...] + jnp.dot(p.astype(BF16), values, preferred_element_type=F32)
        m_sc[...] = m_new

    dn = (((1,), (1,)), ((), ()))
    kb = kbuf[...]
    s = (lax.dot_general(qlat, kb, dn, preferred_element_type=F32)
         + lax.dot_general(qpe, pbuf[...], dn, preferred_element_type=F32))
    attend(s, kb)

    @pl.when(step == n_steps - 1)
    def _():
        cn = cnew_ref[0].astype(BF16)
        kn = knew_ref[0].astype(BF16)
        sn = (lax.dot_general(qlat, cn, dn, preferred_element_type=F32)
              + lax.dot_general(qpe, kn, dn, preferred_element_type=F32))
        tq = lax.broadcasted_iota(jnp.int32, sn.shape, 0) % t_new
        tk = lax.broadcasted_iota(jnp.int32, sn.shape, 1)
        attend(jnp.where(tk <= tq, sn, -jnp.inf), cn)
        o_ref[0] = acc_sc[...] / l_sc[...]


def mla_decode(qlat, qpe, ckv_new, kpe_new, cache_ckv, cache_kpe, page_table, dec_batch, t_new):
    n_pages = page_table.shape[1]
    n_steps = n_pages // DEC_PAGES
    rows = MLA_HEADS * t_new
    qlat4 = qlat.reshape(MLA_HEADS, dec_batch, t_new, MLA_KV_LORA)
    qpe4 = qpe.reshape(MLA_HEADS, dec_batch, t_new, HEAD_PAD)
    t_pad = NEW_PAD
    pad_new = lambda a: jnp.pad(a.reshape(dec_batch, t_new, a.shape[-1]), ((0, 0), (0, t_pad - t_new), (0, 0)))
    cnew = pad_new(ckv_new)
    knew = pad_new(kpe_new)

    def page_map(p):
        return lambda b, s, pt: (0, pt[b, s * DEC_PAGES + p], 0, 0)

    in_specs = [
        pl.BlockSpec((MLA_HEADS, 1, t_new, MLA_KV_LORA), lambda b, s, pt: (0, b, 0, 0)),
        pl.BlockSpec((MLA_HEADS, 1, t_new, HEAD_PAD), lambda b, s, pt: (0, b, 0, 0)),
        pl.BlockSpec((1, t_pad, MLA_KV_LORA), lambda b, s, pt: (b, 0, 0)),
        pl.BlockSpec((1, t_pad, MLA_ROPE), lambda b, s, pt: (b, 0, 0)),
    ]
    in_specs += [pl.BlockSpec((1, 1, PAGE_SIZE, MLA_KV_LORA), page_map(p)) for p in range(DEC_PAGES)]
    in_specs += [pl.BlockSpec((1, 1, PAGE_SIZE, MLA_ROPE), page_map(p)) for p in range(DEC_PAGES)]
    grid_spec = pltpu.PrefetchScalarGridSpec(
        num_scalar_prefetch=1,
        grid=(dec_batch, n_steps),
        in_specs=in_specs,
        out_specs=pl.BlockSpec((1, rows, MLA_KV_LORA), lambda b, s, pt: (b, 0, 0)),
        scratch_shapes=[pltpu.VMEM((DEC_PAGES * PAGE_SIZE, MLA_KV_LORA), BF16),
                        pltpu.VMEM((DEC_PAGES * PAGE_SIZE, MLA_ROPE), BF16),
                        pltpu.VMEM((rows, 1), F32), pltpu.VMEM((rows, 1), F32),
                        pltpu.VMEM((rows, MLA_KV_LORA), F32)],
    )
    return pl.pallas_call(
        functools.partial(_decode_kernel, n_steps, t_new),
        grid_spec=grid_spec,
        out_shape=jax.ShapeDtypeStruct((dec_batch, rows, MLA_KV_LORA), F32),
        compiler_params=_cparams(("parallel", "arbitrary")),
        name="mla_decode",
    )(page_table, qlat4, qpe4, cnew, knew, *([cache_ckv] * DEC_PAGES), *([cache_kpe] * DEC_PAGES))


def _uv_kernel(o_ref, wuv_ref, out_ref):
    o = o_ref[...]
    o = o.reshape(o.shape[0] * o.shape[2], MLA_KV_LORA).astype(BF16)
    out_ref[...] = jnp.dot(o, wuv_ref[0], preferred_element_type=F32).astype(BF16)


def latent_to_values(o_lat, wuv, dec_batch, t_new):
    o4 = o_lat.reshape(dec_batch, MLA_HEADS, t_new, MLA_KV_LORA)
    return pl.pallas_call(
        _uv_kernel,
        grid=(MLA_HEADS,),
        in_specs=[pl.BlockSpec((dec_batch, 1, t_new, MLA_KV_LORA), lambda h: (0, h, 0, 0)),
                  pl.BlockSpec((1, MLA_KV_LORA, HEAD_PAD), lambda h: (h, 0, 0))],
        out_specs=pl.BlockSpec((dec_batch * t_new, HEAD_PAD), lambda h: (0, h)),
        out_shape=jax.ShapeDtypeStruct((dec_batch * t_new, MLA_PAD), BF16),
        compiler_params=_cparams(("parallel",)),
        name="latent_to_values",
    )(o4, wuv)


def _gla_kernel(tb, q_ref, k_ref, la_ref, v_ref, s0_ref, o_ref, sfin_ref, st_sc, qd_sc, kd_sc, dec_sc,
                oi_sc):
    blk = pl.program_id(1)
    nblk = pl.num_programs(1)
    c = GLA_CHUNK
    nc = tb // c

    @pl.when(blk == 0)
    def _():
        for hh in range(GLA_HEADS):
            st_sc[hh] = s0_ref[0, hh].T

    q = q_ref[0]
    k = k_ref[0]
    la = la_ref[0]
    v = v_ref[0]

    r = lax.broadcasted_iota(jnp.int32, (tb, tb), 0)
    cc = lax.broadcasted_iota(jnp.int32, (tb, tb), 1)
    same = (r >> 4) == (cc >> 4)
    tri = jnp.where(same & (cc <= r), 1.0, 0.0).astype(F32)
    ones = jnp.where(same, 1.0, 0.0).astype(F32)
    b = jnp.dot(tri, la, precision=lax.Precision.HIGHEST, preferred_element_type=F32)
    b_last = jnp.dot(ones, la, precision=lax.Precision.HIGHEST, preferred_element_type=F32)
    qd_sc[...] = (q * jnp.exp(b)).astype(BF16)
    kd_sc[...] = (k * jnp.exp(b_last - b)).astype(BF16)
    dec_sc[...] = jnp.exp(b_last)

    hd = lax.broadcasted_iota(jnp.int32, (GLA_QK, GLA_WIDTH), 0) >> 6
    he = lax.broadcasted_iota(jnp.int32, (GLA_QK, GLA_WIDTH), 1) >> 7
    expand = jnp.where(hd == he, 1.0, 0.0).astype(BF16)
    b3 = b.reshape(nc, c, GLA_QK)
    k3 = k.reshape(nc, c, GLA_QK)
    q3 = q.reshape(nc, c, GLA_QK)
    v3 = v.astype(F32).reshape(nc, c, GLA_WIDTH)
    pos = lax.broadcasted_iota(jnp.int32, (nc, c, GLA_QK), 1)
    o_intra = jnp.zeros((tb, GLA_WIDTH), F32)
    for j in range(c):
        e = jnp.exp(jnp.minimum(b3 - b3[:, j:j + 1, :], 0.0))
        w = jnp.where(pos >= j, q3 * k3[:, j:j + 1, :] * e, 0.0)
        red = jnp.dot(w.reshape(tb, GLA_QK).astype(BF16), expand, preferred_element_type=F32)
        vj = jnp.broadcast_to(v3[:, j:j + 1, :], (nc, c, GLA_WIDTH)).reshape(tb, GLA_WIDTH)
        o_intra = o_intra + red * vj
    oi_sc[...] = o_intra

    def chunk(ci, _):
        r0 = pl.multiple_of(ci * c, c)
        qd = qd_sc[pl.ds(r0, c), :]
        kd = kd_sc[pl.ds(r0, c), :]
        dec = dec_sc[pl.ds(r0, c), :][0:1, :]
        vc = v_ref[0, pl.ds(r0, c), :]
        outs = []
        for hh in range(GLA_HEADS):
            ks = slice(hh * GLA_DK, (hh + 1) * GLA_DK)
            vs = slice(hh * GLA_DV, (hh + 1) * GLA_DV)
            st = st_sc[hh]
            outs.append(lax.dot_general(qd[:, ks], st.astype(BF16), (((1,), (1,)), ((), ())),
                                        preferred_element_type=F32))
            kv = lax.dot_general(vc[:, vs], kd[:, ks], (((0,), (0,)), ((), ())),
                                 preferred_element_type=F32)
            st_sc[hh] = st * dec[:, ks] + kv
        o_ref[0, pl.ds(r0, c), :] = oi_sc[pl.ds(r0, c), :] + jnp.concatenate(outs, axis=1)
        return 0

    lax.fori_loop(0, nc, chunk, 0)

    @pl.when(blk == nblk - 1)
    def _():
        for hh in range(GLA_HEADS):
            sfin_ref[0, hh] = st_sc[hh].T


def gla(gq, gk, la, gv, s0, n_seq, seq_len, tb):
    nblk = seq_len // tb
    sh3 = lambda a: a.reshape(n_seq, seq_len, a.shape[-1])
    row = lambda b, i: (b, i, 0)
    o, s_fin = pl.pallas_call(
        functools.partial(_gla_kernel, tb),
        grid=(n_seq, nblk),
        in_specs=[pl.BlockSpec((1, tb, GLA_QK), row), pl.BlockSpec((1, tb, GLA_QK), row),
                  pl.BlockSpec((1, tb, GLA_QK), row), pl.BlockSpec((1, tb, GLA_WIDTH), row),
                  pl.BlockSpec((1, GLA_HEADS, GLA_DK, GLA_DV), lambda b, i: (b, 0, 0, 0))],
        out_specs=[pl.BlockSpec((1, tb, GLA_WIDTH), row),
                   pl.BlockSpec((1, GLA_HEADS, GLA_DK, GLA_DV), lambda b, i: (b, 0, 0, 0))],
        out_shape=[jax.ShapeDtypeStruct((n_seq, seq_len, GLA_WIDTH), F32),
                   jax.ShapeDtypeStruct((n_seq, GLA_HEADS, GLA_DK, GLA_DV), F32)],
        scratch_shapes=[pltpu.VMEM((GLA_HEADS, GLA_DV, GLA_DK), F32),
                        pltpu.VMEM((tb, GLA_QK), BF16), pltpu.VMEM((tb, GLA_QK), BF16),
                        pltpu.VMEM((tb, GLA_QK), F32), pltpu.VMEM((tb, GLA_WIDTH), F32)],
        compiler_params=_cparams(("parallel", "arbitrary")),
        name="gla_prompt" if nblk > 1 else "gla_sample",
    )(sh3(gq), sh3(gk), sh3(la), sh3(gv), s0)
    return o, s_fin


def _mix_kernel(x_ref, om_ref, og_ref, gg_ref, gta_ref, shf_ref, scf_ref, gm_ref, ggl_ref, wo_ref,
                gn_ref, wr_ref, br_ref, x1_ref, h2_ref, ti_ref, tg_ref):
    om = om_ref[...].astype(F32)
    ms = jnp.sum(om * om, axis=-1, keepdims=True) * (1.0 / (MLA_HEADS * MLA_V))
    om = om * lax.rsqrt(ms + EPS) * gm_ref[...]
    og = og_ref[...]
    gg = gg_ref[...].astype(F32)
    gate = gg * jax.nn.sigmoid(gg)
    parts = []
    for hh in range(GLA_HEADS):
        sl = slice(hh * GLA_DV, (hh + 1) * GLA_DV)
        parts.append(_rms(og[:, sl], ggl_ref[...]) * gate[:, sl])
    mix = jnp.concatenate([om] + parts, axis=1).astype(BF16)
    x1 = x_ref[...] + gta_ref[0] * jnp.dot(mix, wo_ref[...], preferred_element_type=F32)
    x1_ref[...] = x1
    h2 = _rms(x1, gn_ref[...]) * (1.0 + scf_ref[0]) + shf_ref[0]
    h2_ref[...] = h2.astype(BF16)
    logits = jnp.dot(h2, wr_ref[...], precision=lax.Precision.HIGHEST,
                     preferred_element_type=F32) + br_ref[...]
    lane_i = lax.broadcasted_iota(jnp.int32, logits.shape, 1)
    lane = lane_i.astype(F32)
    vals = []
    idxs = []
    for _ in range(TOP_K):
        mx = jnp.max(logits, axis=-1, keepdims=True)
        ix = jnp.min(jnp.where(logits == mx, lane, float(LANES)), axis=-1, keepdims=True)
        vals.append(mx)
        idxs.append(ix)
        logits = jnp.where(lane == ix, -jnp.inf, logits)
    ex = [jnp.exp(vv - vals[0]) for vv in vals]
    den = ex[0] + ex[1] + ex[2] + ex[3]
    ti = jnp.zeros(logits.shape, F32)
    tg = jnp.zeros(logits.shape, F32)
    for kk in range(TOP_K):
        ti = jnp.where(lane_i == kk, idxs[kk], ti)
        tg = jnp.where(lane_i == kk, ex[kk] / den, tg)
    ti_ref[...] = ti.astype(jnp.int32)
    tg_ref[...] = tg


def mixer_out(x, o_mla, o_gla, gg, gta, shf, scf, mod_per_row, w):
    n = x.shape[0]
    tm = ROW_TILE
    row = lambda i: (i, 0)
    const = lambda i: (0, 0)
    if mod_per_row:
        mod_spec = pl.BlockSpec((1, tm, D_MODEL), lambda i: (0, i, 0))
    else:
        tiles_per_seq = n // gta.shape[0] // tm
        mod_spec = pl.BlockSpec((1, 1, D_MODEL), lambda i: (i // tiles_per_seq, 0, 0))
    d_mix = MLA_PAD + GLA_WIDTH
    return pl.pallas_call(
        _mix_kernel,
        grid=(n // tm,),
        in_specs=[pl.BlockSpec((tm, D_MODEL), row), pl.BlockSpec((tm, MLA_PAD), row),
                  pl.BlockSpec((tm, GLA_WIDTH), row), pl.BlockSpec((tm, GLA_WIDTH), row),
                  mod_spec, mod_spec, mod_spec,
                  pl.BlockSpec((1, MLA_PAD), const), pl.BlockSpec((1, GLA_DV), const),
                  pl.BlockSpec((d_mix, D_MODEL), const), pl.BlockSpec((1, D_MODEL), const),
                  pl.BlockSpec((D_MODEL, LANES), const), pl.BlockSpec((1, LANES), const)],
        out_specs=[pl.BlockSpec((tm, D_MODEL), row), pl.BlockSpec((tm, D_MODEL), row),
                   pl.BlockSpec((tm, LANES), row), pl.BlockSpec((tm, LANES), row)],
        out_shape=[jax.ShapeDtypeStruct((n, D_MODEL), F32), jax.ShapeDtypeStruct((n, D_MODEL), BF16),
                   jax.ShapeDtypeStruct((n, LANES), jnp.int32), jax.ShapeDtypeStruct((n, LANES), F32)],
        compiler_params=_cparams(("parallel",)),
        name="mixer_sample" if mod_per_row else "mixer_prompt",
    )(x, o_mla, o_gla, gg, gta, shf, scf, w["g_mla_pad"], w["g_gla_out"], w["w_o_pad"],
      w["g_norm_ffn"], w["w_router_pad"], w["b_router_pad"])


def _moe_kernel(be_ref, na_ref, xs_ref, wup_ref, bup_ref, wdn_ref, bdn_ref, ys_ref, wup_sc, wdn_sc):
    i = pl.program_id(0)
    active = i < na_ref[0]
    prev = be_ref[jnp.maximum(i - 1, 0)]
    fresh = (i == 0) | (be_ref[i] != prev)

    @pl.when(active & fresh)
    def _():
        wup_sc[...] = wup_ref[0].astype(BF16)
        wdn_sc[...] = wdn_ref[0].astype(BF16)

    @pl.when(active)
    def _():
        hu = jnp.dot(xs_ref[...], wup_sc[...], preferred_element_type=F32) + bup_ref[0]
        gate = jnp.minimum(hu[:, :D_FF], SWIGLU_LIMIT)
        lin = jnp.clip(hu[:, D_FF:], -SWIGLU_LIMIT, SWIGLU_LIMIT)
        act = gate * jax.nn.sigmoid(SWIGLU_ALPHA * gate) * (lin + 1.0)
        y = jnp.dot(act.astype(BF16), wdn_sc[...], preferred_element_type=F32) + bdn_ref[0]
        ys_ref[...] = y.astype(BF16)

    @pl.when(jnp.logical_not(active))
    def _():
        ys_ref[...] = jnp.zeros(ys_ref.shape, BF16)


def moe_experts(xs, block_e, n_active, w_up, b_up, w_down, b_down):
    n_rows = xs.shape[0]
    tm = MOE_TILE
    n_blocks = n_rows // tm
    emap3 = lambda i, be, na: (be[i], 0, 0)
    grid_spec = pltpu.PrefetchScalarGridSpec(
        num_scalar_prefetch=2,
        grid=(n_blocks,),
        in_specs=[pl.BlockSpec((tm, D_MODEL), lambda i, be, na: (i, 0)),
                  pl.BlockSpec((1, D_MODEL, 2 * D_FF), emap3),
                  pl.BlockSpec((1, 1, 2 * D_FF), emap3),
                  pl.BlockSpec((1, D_FF, D_MODEL), emap3),
                  pl.BlockSpec((1, 1, D_MODEL), emap3)],
        out_specs=pl.BlockSpec((tm, D_MODEL), lambda i, be, na: (i, 0)),
        scratch_shapes=[pltpu.VMEM((D_MODEL, 2 * D_FF), BF16), pltpu.VMEM((D_FF, D_MODEL), BF16)],
    )
    return pl.pallas_call(
        _moe_kernel,
        grid_spec=grid_spec,
        out_shape=jax.ShapeDtypeStruct((n_rows, D_MODEL), BF16),
        compiler_params=_cparams(("arbitrary",)),
        name="moe_experts",
    )(block_e, n_active, xs, w_up, b_up.reshape(N_EXPERTS, 1, 2 * D_FF), w_down,
      b_down.reshape(N_EXPERTS, 1, D_MODEL))


def _final_kernel(x1_ref, yg_ref, tg_ref, gtf_ref, sh_ref, sc_ref, g_ref, y_ref):
    tg = tg_ref[...]
    moe = jnp.zeros(x1_ref.shape, F32)
    for kk in range(TOP_K):
        moe = moe + tg[:, kk:kk + 1] * yg_ref[kk].astype(F32)
    x2 = x1_ref[...] + gtf_ref[0] * moe
    y_ref[...] = _rms(x2, g_ref[...]) * (1.0 + sc_ref[0]) + sh_ref[0]


def final_out(x1, yg, tg, row0, gtf, sh, sc, mod_per_row, g_final):
    n = x1.shape[0]
    tm = ROW_TILE
    t0 = row0 // tm
    row = lambda i: (i, 0)
    if mod_per_row:
        mod_spec = pl.BlockSpec((1, tm, D_MODEL), lambda i: (0, i, 0))
    else:
        tiles_per_seq = n // gtf.shape[0] // tm
        mod_spec = pl.BlockSpec((1, 1, D_MODEL), lambda i: (i // tiles_per_seq, 0, 0))
    return pl.pallas_call(
        _final_kernel,
        grid=(n // tm,),
        in_specs=[pl.BlockSpec((tm, D_MODEL), row),
                  pl.BlockSpec((TOP_K, tm, D_MODEL), lambda i: (0, i + t0, 0)),
                  pl.BlockSpec((tm, LANES), lambda i: (i + t0, 0)),
                  mod_spec, mod_spec, mod_spec,
                  pl.BlockSpec((1, D_MODEL), lambda i: (0, 0))],
        out_specs=pl.BlockSpec((tm, D_MODEL), row),
        out_shape=jax.ShapeDtypeStruct((n, D_MODEL), F32),
        compiler_params=_cparams(("parallel",)),
        name="final_sample" if mod_per_row else "final_prompt",
    )(x1, yg, tg, gtf, sh, sc, g_final)


def _prep_weights(w_in, g_q_a, w_q_b, g_kv_a, w_kv_b, w_gk_b, b_gk, g_mla_out, g_gla_out, w_o,
                  g_norm_mix, g_norm_ffn, w_router, b_router):
    sizes = (MLA_Q_LORA, MLA_KV_LORA, MLA_ROPE, GLA_QK, GLA_QK, GLA_WIDTH, GLA_GATE_RANK, GLA_WIDTH)
    offs = np.cumsum((0,) + sizes)
    part = lambda i: w_in[:, offs[i]:offs[i + 1]]
    half = MLA_ROPE // 2
    k_rope = part(2)
    misc = jnp.concatenate([k_rope, part(6), jnp.zeros((D_MODEL, LANES - MLA_ROPE - GLA_GATE_RANK), F32)], 1)
    swap = jnp.concatenate([-k_rope[:, half:], k_rope[:, :half],
                            jnp.zeros((D_MODEL, LANES - MLA_ROPE), F32)], 1)
    w_in_pad = jnp.concatenate([part(0), part(1), part(3), part(4), part(5), part(7), misc, swap], 1)

    pad_q = jnp.zeros((MLA_Q_LORA, MLA_HEADS, HEAD_PAD - MLA_NOPE - MLA_ROPE), F32)
    wq1 = jnp.concatenate([w_q_b, pad_q], axis=2)
    q_lo = w_q_b[:, :, MLA_NOPE:MLA_NOPE + half]
    q_hi = w_q_b[:, :, MLA_NOPE + half:]
    wq2 = jnp.concatenate([jnp.zeros((MLA_Q_LORA, MLA_HEADS, MLA_NOPE), F32), -q_hi, q_lo, pad_q], axis=2)
    pad_kv = jnp.zeros((MLA_KV_LORA, MLA_HEADS, HEAD_PAD - MLA_NOPE), F32)
    w_uk = w_kv_b[:, :, :MLA_NOPE]
    w_uv = w_kv_b[:, :, MLA_NOPE:]
    wk = jnp.concatenate([w_uk, pad_kv], axis=2)
    wv = jnp.concatenate([w_uv, pad_kv], axis=2)
    wuk_t = jnp.concatenate([jnp.transpose(w_uk, (1, 2, 0)),
                             jnp.zeros((MLA_HEADS, HEAD_PAD - MLA_NOPE, MLA_KV_LORA), F32)], axis=1)
    wuv_h = jnp.transpose(wv, (1, 0, 2))
    wgk = jnp.zeros((LANES, GLA_QK), F32).at[MLA_ROPE:MLA_ROPE + GLA_GATE_RANK].set(w_gk_b)

    wo_mla = w_o[:MLA_HEADS * MLA_V].reshape(MLA_HEADS, MLA_V, D_MODEL)
    wo_mla = jnp.concatenate([wo_mla, jnp.zeros((MLA_HEADS, HEAD_PAD - MLA_V, D_MODEL), F32)], axis=1)
    w_o_pad = jnp.concatenate([wo_mla.reshape(MLA_PAD, D_MODEL), w_o[MLA_HEADS * MLA_V:]], axis=0)
    g_mla = g_mla_out.reshape(MLA_HEADS, MLA_V)
    g_mla_pad = jnp.concatenate([g_mla, jnp.zeros((MLA_HEADS, HEAD_PAD - MLA_V), F32)], 1).reshape(1, MLA_PAD)
    w_router_pad = jnp.concatenate([w_router, jnp.zeros((D_MODEL, LANES - N_EXPERTS), F32)], axis=1)
    b_router_pad = jnp.concatenate([b_router, jnp.full((LANES - N_EXPERTS,), -jnp.inf, F32)]).reshape(1, LANES)
    return dict(
        w_in=w_in_pad.astype(BF16), g_norm_mix=g_norm_mix.reshape(1, D_MODEL),
        g_q_a=g_q_a.reshape(1, MLA_Q_LORA),
        wq1=wq1.reshape(MLA_Q_LORA, MLA_PAD).astype(BF16), wq2=wq2.reshape(MLA_Q_LORA, MLA_PAD).astype(BF16),
        g_kv_a=g_kv_a.reshape(1, MLA_KV_LORA),
        wk=wk.reshape(MLA_KV_LORA, MLA_PAD).astype(BF16), wv=wv.reshape(MLA_KV_LORA, MLA_PAD).astype(BF16),
        wuk_t=wuk_t.astype(BF16), wuv_h=wuv_h.astype(BF16),
        wgk=wgk.astype(BF16), b_gk=b_gk.reshape(1, GLA_QK),
        g_mla_pad=g_mla_pad, g_gla_out=g_gla_out.reshape(1, GLA_DV), w_o_pad=w_o_pad.astype(BF16),
        g_norm_ffn=g_norm_ffn.reshape(1, D_MODEL), w_router_pad=w_router_pad, b_router_pad=b_router_pad,
    )


def _rope_tables(pos, reps):
    half = MLA_ROPE // 2
    inv = ROPE_THETA ** (-jnp.arange(half, dtype=F32) / half)
    ang = pos.astype(F32)[:, None] * inv
    cos, sin = jnp.cos(ang), jnp.sin(ang)
    n = pos.shape[0]
    qc = jnp.concatenate([jnp.full((n, MLA_NOPE), Q_SCALE, F32), Q_SCALE * cos, Q_SCALE * cos,
                          jnp.zeros((n, HEAD_PAD - MLA_NOPE - MLA_ROPE), F32)], axis=1)
    qs = jnp.concatenate([jnp.zeros((n, MLA_NOPE), F32), Q_SCALE * sin, Q_SCALE * sin,
                          jnp.zeros((n, HEAD_PAD - MLA_NOPE - MLA_ROPE), F32)], axis=1)
    kc = jnp.concatenate([cos, cos, jnp.zeros((n, LANES - MLA_ROPE), F32)], axis=1)
    ks = jnp.concatenate([sin, sin, jnp.zeros((n, LANES - MLA_ROPE), F32)], axis=1)
    return tuple(jnp.tile(t, (reps, 1)) for t in (qc, qs, kc, ks))


def _route(top_i, n_tok):
    tm = MOE_TILE
    n_pairs = n_tok * TOP_K
    flat_e = top_i.reshape(-1)
    onehot = (flat_e[:, None] == jnp.arange(N_EXPERTS, dtype=jnp.int32)[None, :]).astype(jnp.int32)
    csum = jnp.cumsum(onehot, axis=0)
    rank = jnp.sum((csum - onehot) * onehot, axis=1)
    counts = csum[-1]
    padded = (counts + tm - 1) // tm * tm
    pad_ends = jnp.cumsum(padded)
    pad_starts = pad_ends - padded
    dest = pad_starts[flat_e] + rank
    n_blocks = n_pairs // tm + N_EXPERTS
    n_active = (pad_ends[-1] // tm).astype(jnp.int32)
    blk = jnp.minimum(jnp.arange(n_blocks, dtype=jnp.int32), n_active - 1)
    block_e = jnp.minimum(jnp.searchsorted(pad_ends, blk * tm, side="right"), N_EXPERTS - 1).astype(jnp.int32)
    src_tok = jnp.zeros((n_blocks * tm,), jnp.int32).at[dest].set(
        jnp.arange(n_pairs, dtype=jnp.int32) // TOP_K)
    return dest, src_tok, block_e, n_active.reshape(1)


def kernel(x_prompt, x_sample, cache_ckv, cache_kpe, state_gla, page_table, c_prompt, c_sample, w_ada, b_ada, g_norm_mix, w_in, g_q_a, w_q_b, g_kv_a, w_kv_b, w_gk_b, b_gk, g_mla_out, g_gla_out, w_o, g_norm_ffn, w_router, b_router, w_up, b_up, w_down, b_down, g_norm_final, w_ada_final, b_ada_final):
    B, S, D = x_prompt.shape
    DB, T, _ = x_sample.shape
    depth = w_ada.shape[0]
    assert depth == 1
    past_len = page_table.shape[1] * cache_ckv.shape[2]
    n_p, n_s = B * S, DB * T
    l = 0

    w = _prep_weights(w_in[l], g_q_a[l], w_q_b[l], g_kv_a[l], w_kv_b[l], w_gk_b[l], b_gk[l],
                      g_mla_out[l], g_gla_out[l], w_o[l], g_norm_mix[l], g_norm_ffn[l],
                      w_router[l], b_router[l])

    n_c = B + DB
    n_c_pad = (n_c + 7) // 8 * 8
    c_all = jnp.concatenate([c_prompt, c_sample, jnp.zeros((n_c_pad - n_c, D), F32)], axis=0)
    mod = ada_terms(c_all, w_ada[l], b_ada[l])
    mod_f = ada_terms(c_all, w_ada_final, b_ada_final)
    term = lambda m, i: m[:, i * D:(i + 1) * D]
    p_term = lambda m, i: term(m, i)[:B].reshape(B, 1, D)
    s_term = lambda m, i: jnp.broadcast_to(term(m, i)[B:n_c][:, None, :], (DB, T, D)).reshape(1, n_s, D)

    xp = x_prompt.reshape(n_p, D)
    xs = x_sample.reshape(n_s, D)
    tabs_p = _rope_tables(jnp.arange(S), 1)
    tabs_s = _rope_tables(past_len + jnp.arange(T), ROW_TILE // T)

    (q_p, k_p, v_p, ckv_p, kpe_p, gq_p, gk_p, gv_p, la_p, gg_p) = in_proj(
        xp, p_term(mod, 0), p_term(mod, 1), False, tabs_p, w, True)
    o_mla_p = mla_prefill(q_p, k_p, v_p, B, S)
    s0 = jnp.zeros((B, GLA_HEADS, GLA_DK, GLA_DV), F32)
    o_gla_p, gla_p = gla(gq_p, gk_p, la_p, gv_p, s0, B, S, GLA_BLOCK)
    x1_p, h2_p, ti_p, tg_p = mixer_out(xp, o_mla_p, o_gla_p.reshape(n_p, GLA_WIDTH), gg_p,
                                       p_term(mod, 2), p_term(mod, 3), p_term(mod, 4), False, w)

    (q_s, ckv_s, kpe_s, gq_s, gk_s, gv_s, la_s, gg_s) = in_proj(
        xs, s_term(mod, 0), s_term(mod, 1), True, tabs_s, w, False)
    qlat, qpe = absorb_q(q_s, w["wuk_t"])
    o_lat = mla_decode(qlat, qpe, ckv_s, kpe_s, cache_ckv, cache_kpe, page_table, DB, T)
    o_mla_s = latent_to_values(o_lat, w["wuv_h"], DB, T)
    tpad = GLA_CHUNK
    padt = lambda a: jnp.pad(a.reshape(DB, T, a.shape[-1]), ((0, 0), (0, tpad - T), (0, 0))).reshape(
        DB * tpad, a.shape[-1])
    o_gla_s, gla_s = gla(padt(gq_s), padt(gk_s), padt(la_s), padt(gv_s), state_gla[l], DB, tpad, tpad)
    o_gla_s = o_gla_s[:, :T].reshape(n_s, GLA_WIDTH)
    x1_s, h2_s, ti_s, tg_s = mixer_out(xs, o_mla_s, o_gla_s, gg_s,
                                       s_term(mod, 2), s_term(mod, 3), s_term(mod, 4), True, w)

    n_all = n_p + n_s
    h2 = jnp.concatenate([h2_p, h2_s], axis=0)
    ti = jnp.concatenate([ti_p, ti_s], axis=0)
    tg = jnp.concatenate([tg_p, tg_s], axis=0)
    dest, src_tok, block_e, n_active = _route(ti[:, :TOP_K], n_all)
    xs_sorted = jnp.take(h2, src_tok, axis=0)
    ys = moe_experts(xs_sorted, block_e, n_active, w_up[l], b_up[l], w_down[l], b_down[l])
    yg = jnp.take(ys, dest.reshape(n_all, TOP_K).T.reshape(-1), axis=0).reshape(TOP_K, n_all, D)

    g_fin = g_norm_final.reshape(1, D)
    y_p = final_out(x1_p, yg, tg, 0, p_term(mod, 5), p_term(mod_f, 0), p_term(mod_f, 1), False, g_fin)
    y_s = final_out(x1_s, yg, tg, n_p, s_term(mod, 5), s_term(mod_f, 0), s_term(mod_f, 1), True, g_fin)

    return (y_p.reshape(B, S, D), y_s.reshape(DB, T, D),
            ckv_p.reshape(1, B, S, MLA_KV_LORA), kpe_p.reshape(1, B, S, MLA_ROPE), gla_p[None],
            ckv_s.reshape(1, DB, T, MLA_KV_LORA), kpe_s.reshape(1, DB, T, MLA_ROPE), gla_s[None])
```

```python
import functools
import math

import jax
import jax.numpy as jnp
import numpy as np
from jax import lax
from jax.experimental import pallas as pl
from jax.experimental.pallas import tpu as pltpu
from jax.experimental.pallas import tpu_sc as plsc

F32 = jnp.float32
BF16 = jnp.bfloat16

D_MODEL = 1024
MLA_HEADS = 8
MLA_NOPE = 64
MLA_ROPE = 32
MLA_V = 64
MLA_Q_LORA = 384
MLA_KV_LORA = 256
MLA_SCALE = (MLA_NOPE + MLA_ROPE) ** -0.5
ROPE_THETA = 10000.0
GLA_HEADS = 4
GLA_DK = 64
GLA_DV = 128
GLA_GATE_RANK = 16
GLA_GATE_NORM = 16.0
GLA_CHUNK = 16
GLA_QK = GLA_HEADS * GLA_DK
GLA_WIDTH = GLA_HEADS * GLA_DV
N_EXPERTS = 32
TOP_K = 4
D_FF = D_MODEL
SWIGLU_LIMIT = 7.0
SWIGLU_ALPHA = 1.702
N_MOD = 6
EPS = 1e-6
PAGE_SIZE = 128

LANES = 128
HEAD_PAD = LANES
MLA_PAD = MLA_HEADS * HEAD_PAD
VMEM_LIMIT = 56 * 1024 * 1024

Q_SCALE = MLA_SCALE * math.log2(math.e)

C_QA = 0
C_KV = C_QA + MLA_Q_LORA
C_GQ = C_KV + MLA_KV_LORA
C_GK = C_GQ + GLA_QK
C_GV = C_GK + GLA_QK
C_GG = C_GV + GLA_WIDTH
C_MISC = C_GG + GLA_WIDTH
C_SWAP = C_MISC + LANES
D_IN_PAD = C_SWAP + LANES

ROW_TILE = 256
ATT_TILE = 512
GLA_BLOCK = 256
SC_WIN = 128
PERM_W = D_MODEL // 4
MOE_TILE = 256
NEW_PAD = 16


def _cparams(sem):
    return pltpu.CompilerParams(dimension_semantics=sem, vmem_limit_bytes=VMEM_LIMIT)


def _rms(x, g):
    return x * lax.rsqrt(jnp.mean(x * x, axis=-1, keepdims=True) + EPS) * g


def _pack_rows(x):
    bits = lax.bitcast_convert_type(x.astype(BF16).astype(F32), jnp.uint32)
    w = x.shape[1] // 2
    return (bits[:, :w] >> 16) | bits[:, w:]


def _unpack_rows(words):
    lo = lax.bitcast_convert_type(words << 16, F32)
    hi = lax.bitcast_convert_type(words & jnp.uint32(0xFFFF0000), F32)
    return jnp.concatenate([lo, hi], axis=1)


def _ada_kernel(c_ref, w_ref, b_ref, o_ref):
    c = c_ref[...]
    a = (c * jax.nn.sigmoid(c)).astype(BF16)
    o_ref[...] = jnp.dot(a, w_ref[...].astype(BF16), preferred_element_type=F32) + b_ref[...]


def ada_terms(c, w, b):
    rows, d = c.shape
    n = w.shape[1]
    tn = 512
    return pl.pallas_call(
        _ada_kernel,
        grid=(n // tn,),
        in_specs=[pl.BlockSpec((rows, d), lambda j: (0, 0)),
                  pl.BlockSpec((d, tn), lambda j: (0, j)),
                  pl.BlockSpec((1, tn), lambda j: (0, j))],
        out_specs=pl.BlockSpec((rows, tn), lambda j: (0, j)),
        out_shape=jax.ShapeDtypeStruct((rows, n), F32),
        compiler_params=_cparams(("arbitrary",)),
        name="ada_terms",
    )(c, w, b.reshape(1, n))


def _in_kernel(with_kv, x_ref, sh_ref, sc_ref, g_ref, win_ref, gqa_ref, wq1_ref, wq2_ref,
               gkv_ref, wk_ref, wv_ref, wgk_ref, bgk_ref, qc_ref, qs_ref, kc_ref, ks_ref, *outs):
    if with_kv:
        q_out, k_out, v_out, ckv_out, kpe_out, gq_out, gk_out, gv_out, la_out, gg_out = outs
    else:
        q_out, ckv_out, kpe_out, gq_out, gk_out, gv_out, la_out, gg_out = outs
    x = x_ref[...]
    h = _rms(x, g_ref[...])
    h = h * (1.0 + sc_ref[0]) + sh_ref[0]
    proj = jnp.dot(h.astype(BF16), win_ref[...], preferred_element_type=F32)

    qn = _rms(proj[:, C_QA:C_KV], gqa_ref[...]).astype(BF16)
    qa = jnp.dot(qn, wq1_ref[...], preferred_element_type=F32)
    qb = jnp.dot(qn, wq2_ref[...], preferred_element_type=F32)
    qc = qc_ref[...]
    qs = qs_ref[...]
    for hh in range(MLA_HEADS):
        sl = slice(hh * HEAD_PAD, (hh + 1) * HEAD_PAD)
        q_out[:, sl] = (qa[:, sl] * qc + qb[:, sl] * qs).astype(BF16)

    ckv = _rms(proj[:, C_KV:C_GQ], gkv_ref[...])
    ckv_out[...] = ckv
    kpe = proj[:, C_MISC:C_SWAP] * kc_ref[...] + proj[:, C_SWAP:D_IN_PAD] * ks_ref[...]
    kpe_out[...] = kpe[:, :MLA_ROPE]
    if with_kv:
        ckv_b = ckv.astype(BF16)
        kn = jnp.dot(ckv_b, wk_ref[...], preferred_element_type=F32)
        kpe_sh = pltpu.roll(kpe, MLA_NOPE, axis=1)
        for hh in range(MLA_HEADS):
            sl = slice(hh * HEAD_PAD, (hh + 1) * HEAD_PAD)
            k_out[:, sl] = (kn[:, sl] + kpe_sh).astype(BF16)
        v_out[...] = jnp.dot(ckv_b, wv_ref[...], preferred_element_type=F32).astype(BF16)

    gq_out[...] = proj[:, C_GQ:C_GK] * (GLA_DK ** -0.5)
    gk_out[...] = proj[:, C_GK:C_GV]
    gv_out[...] = proj[:, C_GV:C_GG].astype(BF16)
    gg_out[...] = proj[:, C_GG:C_MISC].astype(BF16)
    xg = jnp.dot(proj[:, C_MISC:C_SWAP].astype(BF16), wgk_ref[...],
                 preferred_element_type=F32) + bgk_ref[...]
    la_out[...] = (jnp.minimum(xg, 0.0) - jnp.log(1.0 + jnp.exp(-jnp.abs(xg)))) * (1.0 / GLA_GATE_NORM)


def in_proj(x, sh, sc, mod_per_row, tabs, wts, with_kv):
    n = x.shape[0]
    tm = ROW_TILE
    nt = n // tm
    qc, qs, kc, ks = tabs
    n_tab = qc.shape[0] // tm
    row = lambda i: (i, 0)
    const = lambda i: (0, 0)
    if mod_per_row:
        mod_spec = pl.BlockSpec((1, tm, D_MODEL), lambda i: (0, i, 0))
    else:
        tiles_per_seq = n // sh.shape[0] // tm
        mod_spec = pl.BlockSpec((1, 1, D_MODEL), lambda i: (i // tiles_per_seq, 0, 0))
    tab_spec = pl.BlockSpec((tm, LANES), lambda i: (i % n_tab, 0))
    w = wts
    in_specs = [
        pl.BlockSpec((tm, D_MODEL), row), mod_spec, mod_spec,
        pl.BlockSpec((1, D_MODEL), const),
        pl.BlockSpec((D_MODEL, D_IN_PAD), const),
        pl.BlockSpec((1, MLA_Q_LORA), const),
        pl.BlockSpec((MLA_Q_LORA, MLA_PAD), const),
        pl.BlockSpec((MLA_Q_LORA, MLA_PAD), const),
        pl.BlockSpec((1, MLA_KV_LORA), const),
        pl.BlockSpec((MLA_KV_LORA, MLA_PAD), const),
        pl.BlockSpec((MLA_KV_LORA, MLA_PAD), const),
        pl.BlockSpec((LANES, GLA_QK), const),
        pl.BlockSpec((1, GLA_QK), const),
        tab_spec, tab_spec, tab_spec, tab_spec,
    ]
    wide = lambda dt: (jax.ShapeDtypeStruct((n, MLA_PAD), dt), pl.BlockSpec((tm, MLA_PAD), row))
    outs = [wide(BF16)]
    if with_kv:
        outs += [wide(BF16), wide(BF16)]
    outs += [
        (jax.ShapeDtypeStruct((n, MLA_KV_LORA), F32), pl.BlockSpec((tm, MLA_KV_LORA), row)),
        (jax.ShapeDtypeStruct((n, MLA_ROPE), F32), pl.BlockSpec((tm, MLA_ROPE), row)),
        (jax.ShapeDtypeStruct((n, GLA_QK), F32), pl.BlockSpec((tm, GLA_QK), row)),
        (jax.ShapeDtypeStruct((n, GLA_QK), F32), pl.BlockSpec((tm, GLA_QK), row)),
        (jax.ShapeDtypeStruct((n, GLA_WIDTH), BF16), pl.BlockSpec((tm, GLA_WIDTH), row)),
        (jax.ShapeDtypeStruct((n, GLA_QK), F32), pl.BlockSpec((tm, GLA_QK), row)),
        (jax.ShapeDtypeStruct((n, GLA_WIDTH), BF16), pl.BlockSpec((tm, GLA_WIDTH), row)),
    ]
    return pl.pallas_call(
        functools.partial(_in_kernel, with_kv),
        grid=(nt,),
        in_specs=in_specs,
        out_specs=[o[1] for o in outs],
        out_shape=[o[0] for o in outs],
        compiler_params=_cparams(("parallel",)),
        name="inproj_prompt" if with_kv else "inproj_sample",
    )(x, sh, sc, w["g_norm_mix"], w["w_in"], w["g_q_a"], w["wq1"], w["wq2"], w["g_kv_a"],
      w["wk"], w["wv"], w["wgk"], w["b_gk"], qc, qs, kc, ks)


def _prefill_kernel(q_ref, k_ref, v_ref, o_ref):
    qi = pl.program_id(2)
    t = ATT_TILE
    q = q_ref[0]

    def scores(j):
        kj = k_ref[0, pl.ds(pl.multiple_of(j * t, t), t), :]
        return lax.dot_general(q, kj, (((1,), (1,)), ((), ())), preferred_element_type=F32)

    def update(carry, s, j):
        m, l, acc = carry
        m_new = jnp.maximum(m, jnp.max(s, axis=-1, keepdims=True))
        p = jnp.exp2(s - m_new)
        alpha = jnp.exp2(m - m_new)
        vj = v_ref[0, pl.ds(pl.multiple_of(j * t, t), t), :]
        acc = alpha * acc + jnp.dot(p.astype(BF16), vj, preferred_element_type=F32)
        l = alpha * l + jnp.sum(p, axis=-1, keepdims=True)
        return m_new, l, acc

    def body(j, carry):
        return update(carry, scores(j), j)

    init = (jnp.full((t, 1), -jnp.inf, F32), jnp.zeros((t, 1), F32), jnp.zeros((t, HEAD_PAD), F32))
    carry = lax.fori_loop(0, qi, body, init)
    s = scores(qi)
    rows = lax.broadcasted_iota(jnp.int32, (t, t), 0)
    cols = lax.broadcasted_iota(jnp.int32, (t, t), 1)
    s = jnp.where(cols <= rows, s, -jnp.inf)
    m, l, acc = update(carry, s, qi)
    o_ref[0] = (acc / l).astype(BF16)


def mla_prefill(q, k, v, batch, seq):
    t = ATT_TILE
    q3 = q.reshape(batch, seq, MLA_PAD)
    k3 = k.reshape(batch, seq, MLA_PAD)
    v3 = v.reshape(batch, seq, MLA_PAD)
    o = pl.pallas_call(
        _prefill_kernel,
        grid=(batch, MLA_HEADS, seq // t),
        in_specs=[pl.BlockSpec((1, t, HEAD_PAD), lambda b, h, i: (b, i, h)),
                  pl.BlockSpec((1, seq, HEAD_PAD), lambda b, h, i: (b, 0, h)),
                  pl.BlockSpec((1, seq, HEAD_PAD), lambda b, h, i: (b, 0, h))],
        out_specs=pl.BlockSpec((1, t, HEAD_PAD), lambda b, h, i: (b, i, h)),
        out_shape=jax.ShapeDtypeStruct((batch, seq, MLA_PAD), BF16),
        compiler_params=_cparams(("parallel", "parallel", "arbitrary")),
        name="mla_prefill",
    )(q3, k3, v3)
    return o.reshape(batch * seq, MLA_PAD)


def _absorb_kernel(q_ref, wuk_ref, qlat_ref, qpe_ref):
    q = q_ref[...]
    qlat_ref[0] = jnp.dot(q, wuk_ref[0], preferred_element_type=F32)
    qf = pltpu.roll(q.astype(F32), HEAD_PAD - MLA_NOPE, axis=1)
    lane = lax.broadcasted_iota(jnp.int32, qf.shape, 1)
    qpe_ref[0] = jnp.where(lane < MLA_ROPE, qf, 0.0)


def absorb_q(q, wuk_t):
    r = q.shape[0]
    return pl.pallas_call(
        _absorb_kernel,
        grid=(MLA_HEADS,),
        in_specs=[pl.BlockSpec((r, HEAD_PAD), lambda h: (0, h)),
                  pl.BlockSpec((1, HEAD_PAD, MLA_KV_LORA), lambda h: (h, 0, 0))],
        out_specs=[pl.BlockSpec((1, r, MLA_KV_LORA), lambda h: (h, 0, 0)),
                   pl.BlockSpec((1, r, HEAD_PAD), lambda h: (h, 0, 0))],
        out_shape=[jax.ShapeDtypeStruct((MLA_HEADS, r, MLA_KV_LORA), F32),
                   jax.ShapeDtypeStruct((MLA_HEADS, r, HEAD_PAD), F32)],
        compiler_params=_cparams(("parallel",)),
        name="absorb_q",
    )(q, wuk_t)


def _decode_kernel(t_new, pt_ref, qlat_ref, qpe_ref, cnew_ref, knew_ref, ckv_hbm, kpe_hbm, o_ref,
                   cbuf, pbuf, kb_sc, pb_sc, sem):
    b = pl.program_id(0)
    nb = pl.num_programs(0)
    n_pages = cbuf.shape[1]
    slot = lax.rem(b, 2)
    rows = MLA_HEADS * t_new

    def page_copies(bb, sl, p):
        pg = pt_ref[bb, p]
        return (pltpu.make_async_copy(ckv_hbm.at[0, pg], cbuf.at[sl, p], sem.at[0, sl]),
                pltpu.make_async_copy(kpe_hbm.at[0, pg], pbuf.at[sl, p], sem.at[1, sl]))

    def fetch(bb, sl):
        def body(p, c):
            for cp in page_copies(bb, sl, p):
                cp.start()
            return c
        lax.fori_loop(0, n_pages, body, 0)

    @pl.when(b == 0)
    def _():
        fetch(0, 0)

    @pl.when(b + 1 < nb)
    def _():
        fetch(b + 1, 1 - slot)

    pltpu.make_async_copy(ckv_hbm.at[0, pl.ds(0, n_pages)], cbuf.at[slot], sem.at[0, slot]).wait()
    pltpu.make_async_copy(kpe_hbm.at[0, pl.ds(0, n_pages)], pbuf.at[slot], sem.at[1, slot]).wait()

    def cast(p, c):
        r0 = pl.multiple_of(p * PAGE_SIZE, PAGE_SIZE)
        kb_sc[pl.ds(r0, PAGE_SIZE), :] = cbuf[slot, p].astype(BF16)
        return c
    lax.fori_loop(0, n_pages, cast, 0, unroll=4)
    for p in range(n_pages):
        pb_sc[:, p * PAGE_SIZE:(p + 1) * PAGE_SIZE] = pbuf[slot, p].astype(BF16)

    qlat = qlat_ref[...].reshape(rows, MLA_KV_LORA).astype(BF16)
    qpe = qpe_ref[...].reshape(rows, HEAD_PAD)[:, :MLA_ROPE].astype(BF16)
    dn = (((1,), (1,)), ((), ()))
    kb = kb_sc[...]
    s = (lax.dot_general(qlat, kb, dn, preferred_element_type=F32)
         + jnp.dot(qpe, pb_sc[...], preferred_element_type=F32))
    cn = cnew_ref[0].astype(BF16)
    kn = knew_ref[0].astype(BF16)
    sn = (lax.dot_general(qlat, cn, dn, preferred_element_type=F32)
          + lax.dot_general(qpe, kn, dn, preferred_element_type=F32))
    tq = lax.broadcasted_iota(jnp.int32, sn.shape, 0) % t_new
    tk = lax.broadcasted_iota(jnp.int32, sn.shape, 1)
    sn = jnp.where(tk <= tq, sn, -jnp.inf)
    m = jnp.maximum(jnp.max(s, axis=-1, keepdims=True), jnp.max(sn, axis=-1, keepdims=True))
    p = jnp.exp2(s - m)
    pn = jnp.exp2(sn - m)
    l = jnp.sum(p, axis=-1, keepdims=True) + jnp.sum(pn, axis=-1, keepdims=True)
    acc = (jnp.dot(p.astype(BF16), kb, preferred_element_type=F32)
           + jnp.dot(pn.astype(BF16), cn, preferred_element_type=F32))
    o_ref[0] = acc / l


def mla_decode(qlat, qpe, ckv_new, kpe_new, cache_ckv, cache_kpe, page_table, dec_batch, t_new):
    n_pages = page_table.shape[1]
    past_len = n_pages * PAGE_SIZE
    rows = MLA_HEADS * t_new
    kpe_t = jnp.swapaxes(cache_kpe, 2, 3)
    qlat4 = qlat.reshape(MLA_HEADS, dec_batch, t_new, MLA_KV_LORA)
    qpe4 = qpe.reshape(MLA_HEADS, dec_batch, t_new, HEAD_PAD)
    t_pad = NEW_PAD
    pad_new = lambda a: jnp.pad(a.reshape(dec_batch, t_new, a.shape[-1]), ((0, 0), (0, t_pad - t_new), (0, 0)))
    cnew = pad_new(ckv_new)
    knew = pad_new(kpe_new)

    in_specs = [
        pl.BlockSpec((MLA_HEADS, 1, t_new, MLA_KV_LORA), lambda b, pt: (0, b, 0, 0)),
        pl.BlockSpec((MLA_HEADS, 1, t_new, HEAD_PAD), lambda b, pt: (0, b, 0, 0)),
        pl.BlockSpec((1, t_pad, MLA_KV_LORA), lambda b, pt: (b, 0, 0)),
        pl.BlockSpec((1, t_pad, MLA_ROPE), lambda b, pt: (b, 0, 0)),
        pl.BlockSpec(memory_space=pl.ANY),
        pl.BlockSpec(memory_space=pl.ANY),
    ]
    grid_spec = pltpu.PrefetchScalarGridSpec(
        num_scalar_prefetch=1,
        grid=(dec_batch,),
        in_specs=in_specs,
        out_specs=pl.BlockSpec((1, rows, MLA_KV_LORA), lambda b, pt: (b, 0, 0)),
        scratch_shapes=[pltpu.VMEM((2, n_pages, PAGE_SIZE, MLA_KV_LORA), F32),
                        pltpu.VMEM((2, n_pages, MLA_ROPE, PAGE_SIZE), F32),
                        pltpu.VMEM((past_len, MLA_KV_LORA), BF16),
                        pltpu.VMEM((MLA_ROPE, past_len), BF16),
                        pltpu.SemaphoreType.DMA((2, 2))],
    )
    return pl.pallas_call(
        functools.partial(_decode_kernel, t_new),
        grid_spec=grid_spec,
        out_shape=jax.ShapeDtypeStruct((dec_batch, rows, MLA_KV_LORA), F32),
        compiler_params=_cparams(("arbitrary",)),
        name="mla_decode",
    )(page_table, qlat4, qpe4, cnew, knew, cache_ckv, kpe_t)


def _uv_kernel(o_ref, wuv_ref, out_ref):
    o = o_ref[...]
    o = o.reshape(o.shape[0] * o.shape[2], MLA_KV_LORA).astype(BF16)
    out_ref[...] = jnp.dot(o, wuv_ref[0], preferred_element_type=F32).astype(BF16)


def latent_to_values(o_lat, wuv, dec_batch, t_new):
    o4 = o_lat.reshape(dec_batch, MLA_HEADS, t_new, MLA_KV_LORA)
    return pl.pallas_call(
        _uv_kernel,
        grid=(MLA_HEADS,),
        in_specs=[pl.BlockSpec((dec_batch, 1, t_new, MLA_KV_LORA), lambda h: (0, h, 0, 0)),
                  pl.BlockSpec((1, MLA_KV_LORA, HEAD_PAD), lambda h: (h, 0, 0))],
        out_specs=pl.BlockSpec((dec_batch * t_new, HEAD_PAD), lambda h: (0, h)),
        out_shape=jax.ShapeDtypeStruct((dec_batch * t_new, MLA_PAD), BF16),
        compiler_params=_cparams(("parallel",)),
        name="latent_to_values",
    )(o4, wuv)


def _gla_kernel(tb, q_ref, k_ref, la_ref, v_ref, s0_ref, o_ref, sfin_ref, st_sc, qd_sc, kd_sc, dec_sc,
                oi_sc):
    blk = pl.program_id(1)
    nblk = pl.num_programs(1)
    c = GLA_CHUNK
    nc = tb // c

    @pl.when(blk == 0)
    def _():
        for hh in range(GLA_HEADS):
            st_sc[hh] = s0_ref[0, hh].T

    q = q_ref[0]
    k = k_ref[0]
    la = la_ref[0]
    v = v_ref[0]

    r = lax.broadcasted_iota(jnp.int32, (tb, tb), 0)
    cc = lax.broadcasted_iota(jnp.int32, (tb, tb), 1)
    same = (r >> 4) == (cc >> 4)
    tri = jnp.where(same & (cc <= r), 1.0, 0.0).astype(F32)
    ones = jnp.where(same, 1.0, 0.0).astype(F32)
    b = jnp.dot(tri, la, precision=lax.Precision.HIGHEST, preferred_element_type=F32)
    b_last = jnp.dot(ones, la, precision=lax.Precision.HIGHEST, preferred_element_type=F32)
    qd_sc[...] = (q * jnp.exp(b)).astype(BF16)
    kd_sc[...] = (k * jnp.exp(b_last - b)).astype(BF16)
    dec_sc[...] = jnp.exp(b_last)

    hd = lax.broadcasted_iota(jnp.int32, (GLA_QK, GLA_WIDTH), 0) >> 6
    he = lax.broadcasted_iota(jnp.int32, (GLA_QK, GLA_WIDTH), 1) >> 7
    expand = jnp.where(hd == he, 1.0, 0.0).astype(BF16)
    b3 = b.reshape(nc, c, GLA_QK)
    k3 = k.reshape(nc, c, GLA_QK)
    q3 = q.reshape(nc, c, GLA_QK)
    v3 = v.astype(F32).reshape(nc, c, GLA_WIDTH)
    pos = lax.broadcasted_iota(jnp.int32, (nc, c, GLA_QK), 1)
    o_intra = jnp.zeros((tb, GLA_WIDTH), F32)
    for j in range(c):
        e = jnp.exp(jnp.minimum(b3 - b3[:, j:j + 1, :], 0.0))
        w = jnp.where(pos >= j, q3 * k3[:, j:j + 1, :] * e, 0.0)
        red = jnp.dot(w.reshape(tb, GLA_QK).astype(BF16), expand, preferred_element_type=F32)
        vj = jnp.broadcast_to(v3[:, j:j + 1, :], (nc, c, GLA_WIDTH)).reshape(tb, GLA_WIDTH)
        o_intra = o_intra + red * vj
    oi_sc[...] = o_intra

    def chunk(ci, _):
        r0 = pl.multiple_of(ci * c, c)
        qd = qd_sc[pl.ds(r0, c), :]
        kd = kd_sc[pl.ds(r0, c), :]
        dec = dec_sc[pl.ds(r0, c), :][0:1, :]
        vc = v_ref[0, pl.ds(r0, c), :]
        outs = []
        for hh in range(GLA_HEADS):
            ks = slice(hh * GLA_DK, (hh + 1) * GLA_DK)
            vs = slice(hh * GLA_DV, (hh + 1) * GLA_DV)
            st = st_sc[hh]
            outs.append(lax.dot_general(qd[:, ks], st.astype(BF16), (((1,), (1,)), ((), ())),
                                        preferred_element_type=F32))
            kv = lax.dot_general(vc[:, vs], kd[:, ks], (((0,), (0,)), ((), ())),
                                 preferred_element_type=F32)
            st_sc[hh] = st * dec[:, ks] + kv
        o_ref[0, pl.ds(r0, c), :] = oi_sc[pl.ds(r0, c), :] + jnp.concatenate(outs, axis=1)
        return 0

    lax.fori_loop(0, nc, chunk, 0)

    @pl.when(blk == nblk - 1)
    def _():
        for hh in range(GLA_HEADS):
            sfin_ref[0, hh] = st_sc[hh].T


def gla(gq, gk, la, gv, s0, n_seq, seq_len, tb):
    nblk = seq_len // tb
    sh3 = lambda a: a.reshape(n_seq, seq_len, a.shape[-1])
    row = lambda b, i: (b, i, 0)
    o, s_fin = pl.pallas_call(
        functools.partial(_gla_kernel, tb),
        grid=(n_seq, nblk),
        in_specs=[pl.BlockSpec((1, tb, GLA_QK), row), pl.BlockSpec((1, tb, GLA_QK), row),
                  pl.BlockSpec((1, tb, GLA_QK), row), pl.BlockSpec((1, tb, GLA_WIDTH), row),
                  pl.BlockSpec((1, GLA_HEADS, GLA_DK, GLA_DV), lambda b, i: (b, 0, 0, 0))],
        out_specs=[pl.BlockSpec((1, tb, GLA_WIDTH), row),
                   pl.BlockSpec((1, GLA_HEADS, GLA_DK, GLA_DV), lambda b, i: (b, 0, 0, 0))],
        out_shape=[jax.ShapeDtypeStruct((n_seq, seq_len, GLA_WIDTH), F32),
                   jax.ShapeDtypeStruct((n_seq, GLA_HEADS, GLA_DK, GLA_DV), F32)],
        scratch_shapes=[pltpu.VMEM((GLA_HEADS, GLA_DV, GLA_DK), F32),
                        pltpu.VMEM((tb, GLA_QK), BF16), pltpu.VMEM((tb, GLA_QK), BF16),
                        pltpu.VMEM((tb, GLA_QK), F32), pltpu.VMEM((tb, GLA_WIDTH), F32)],
        compiler_params=_cparams(("parallel", "arbitrary")),
        name="gla_prompt" if nblk > 1 else "gla_sample",
    )(sh3(gq), sh3(gk), sh3(la), sh3(gv), s0)
    return o, s_fin


def _mix_kernel(x_ref, om_ref, og_ref, gg_ref, gta_ref, shf_ref, scf_ref, gm_ref, ggl_ref, wo_ref,
                gn_ref, wr_ref, br_ref, h2a_in, h2b_in, x1_ref, h2a_ref, h2b_ref, ti_ref, tg_ref, hist_ref):
    del h2a_in, h2b_in
    om = om_ref[...].astype(F32)
    ms = jnp.sum(om * om, axis=-1, keepdims=True) * (1.0 / (MLA_HEADS * MLA_V))
    om = om * lax.rsqrt(ms + EPS) * gm_ref[...]
    og = og_ref[...]
    gg = gg_ref[...].astype(F32)
    gate = gg * jax.nn.sigmoid(gg)
    parts = []
    for hh in range(GLA_HEADS):
        sl = slice(hh * GLA_DV, (hh + 1) * GLA_DV)
        parts.append(_rms(og[:, sl], ggl_ref[...]) * gate[:, sl])
    mix = jnp.concatenate([om] + parts, axis=1).astype(BF16)
    x1 = x_ref[...] + gta_ref[0] * jnp.dot(mix, wo_ref[...], preferred_element_type=F32)
    x1_ref[...] = x1
    h2 = _rms(x1, gn_ref[...]) * (1.0 + scf_ref[0]) + shf_ref[0]
    words = _pack_rows(h2)
    h2a_ref[...] = words[:, :PERM_W]
    h2b_ref[...] = words[:, PERM_W:]
    h_hi = h2.astype(BF16)
    h_lo = (h2 - h_hi.astype(F32)).astype(BF16)
    logits = (jnp.dot(h_hi, wr_ref[0], preferred_element_type=F32)
              + (jnp.dot(h_hi, wr_ref[1], preferred_element_type=F32)
                 + jnp.dot(h_lo, wr_ref[0], preferred_element_type=F32))) + br_ref[...]
    lane_i = lax.broadcasted_iota(jnp.int32, logits.shape, 1)
    lane = lane_i.astype(F32)
    vals = []
    idxs = []
    for _ in range(TOP_K):
        mx = jnp.max(logits, axis=-1, keepdims=True)
        ix = jnp.min(jnp.where(logits == mx, lane, float(LANES)), axis=-1, keepdims=True)
        vals.append(mx)
        idxs.append(ix)
        logits = jnp.where(lane == ix, -jnp.inf, logits)
    ex = [jnp.exp(vv - vals[0]) for vv in vals]
    den = ex[0] + ex[1] + ex[2] + ex[3]
    ti = jnp.zeros(logits.shape, F32)
    tg = jnp.zeros(logits.shape, F32)
    onehot = jnp.zeros(logits.shape, F32)
    for kk in range(TOP_K):
        ti = jnp.where(lane_i == kk, idxs[kk], ti)
        tg = jnp.where(lane_i == kk, ex[kk] / den, tg)
        onehot = onehot + jnp.where(lane == idxs[kk], 1.0, 0.0)
    ti_ref[...] = ti.astype(jnp.int32)
    tg_ref[...] = tg
    hist_ref[0] = jnp.sum(onehot, axis=0, keepdims=True)


def mixer_out(x, o_mla, o_gla, gg, gta, shf, scf, mod_per_row, w, n_all, row0, h2_buf):
    n = x.shape[0]
    tm = ROW_TILE
    t0 = row0 // tm
    row = lambda i: (i, 0)
    const = lambda i: (0, 0)
    extra_specs = [pl.BlockSpec(memory_space=pl.ANY)] * 2
    extra_args = list(h2_buf)
    if mod_per_row:
        mod_spec = pl.BlockSpec((1, tm, D_MODEL), lambda i: (0, i, 0))
    else:
        tiles_per_seq = n // gta.shape[0] // tm
        mod_spec = pl.BlockSpec((1, 1, D_MODEL), lambda i: (i // tiles_per_seq, 0, 0))
    d_mix = MLA_PAD + GLA_WIDTH
    n_in = 13
    return pl.pallas_call(
        _mix_kernel,
        grid=(n // tm,),
        in_specs=[pl.BlockSpec((tm, D_MODEL), row), pl.BlockSpec((tm, MLA_PAD), row),
                  pl.BlockSpec((tm, GLA_WIDTH), row), pl.BlockSpec((tm, GLA_WIDTH), row),
                  mod_spec, mod_spec, mod_spec,
                  pl.BlockSpec((1, MLA_PAD), const), pl.BlockSpec((1, GLA_DV), const),
                  pl.BlockSpec((d_mix, D_MODEL), const), pl.BlockSpec((1, D_MODEL), const),
                  pl.BlockSpec((2, D_MODEL, LANES), lambda i: (0, 0, 0)),
                  pl.BlockSpec((1, LANES), const)] + extra_specs,
        out_specs=[pl.BlockSpec((tm, D_MODEL), row),
                   pl.BlockSpec((tm, PERM_W), lambda i: (i + t0, 0)),
                   pl.BlockSpec((tm, PERM_W), lambda i: (i + t0, 0)),
                   pl.BlockSpec((tm, LANES), row), pl.BlockSpec((tm, LANES), row),
                   pl.BlockSpec((1, 1, LANES), lambda i: (i, 0, 0))],
        out_shape=[jax.ShapeDtypeStruct((n, D_MODEL), F32),
                   jax.ShapeDtypeStruct((n_all, PERM_W), jnp.uint32),
                   jax.ShapeDtypeStruct((n_all, PERM_W), jnp.uint32),
                   jax.ShapeDtypeStruct((n, LANES), jnp.int32), jax.ShapeDtypeStruct((n, LANES), F32),
                   jax.ShapeDtypeStruct((n // tm, 1, LANES), F32)],
        input_output_aliases={n_in: 1, n_in + 1: 2},
        compiler_params=_cparams(("parallel",)),
        name="mixer_sample" if mod_per_row else "mixer_prompt",
    )(x, o_mla, o_gla, gg, gta, shf, scf, w["g_mla_pad"], w["g_gla_out"], w["w_o_pad"],
      w["g_norm_ffn"], w["w_router_pad"], w["b_router_pad"], *extra_args)


def _rank_kernel(ti_ref, base_ref, dest_ref):
    ti = ti_ref[...]
    tm = ti.shape[0]
    lane = lax.broadcasted_iota(jnp.int32, ti.shape, 1)
    cols = [ti[:, kk:kk + 1] for kk in range(TOP_K)]
    onehot = jnp.zeros(ti.shape, F32)
    for kk in range(TOP_K):
        onehot = onehot + jnp.where(lane == cols[kk], 1.0, 0.0)
    r = lax.broadcasted_iota(jnp.int32, (tm, tm), 0)
    c = lax.broadcasted_iota(jnp.int32, (tm, tm), 1)
    earlier = jnp.where(c < r, 1.0, 0.0).astype(BF16)
    pos = jnp.dot(earlier, onehot.astype(BF16), preferred_element_type=F32) + base_ref[0]
    out = jnp.zeros(ti.shape, F32)
    for kk in range(TOP_K):
        dk = jnp.sum(jnp.where(lane == cols[kk], pos, 0.0), axis=-1, keepdims=True)
        out = jnp.where(lane == kk, dk, out)
    dest_ref[...] = out.astype(jnp.int32)


def route_rank(ti, base):
    n = ti.shape[0]
    tm = ROW_TILE
    return pl.pallas_call(
        _rank_kernel,
        grid=(n // tm,),
        in_specs=[pl.BlockSpec((tm, LANES), lambda i: (i, 0)),
                  pl.BlockSpec((1, 1, LANES), lambda i: (i, 0, 0))],
        out_specs=pl.BlockSpec((tm, LANES), lambda i: (i, 0)),
        out_shape=jax.ShapeDtypeStruct((n, LANES), jnp.int32),
        compiler_params=_cparams(("parallel",)),
        name="route_rank",
    )(ti, base)


def _route_tables(hist, n_tok):
    tm = MOE_TILE
    h = hist[:, 0, :].astype(jnp.int32)
    counts = jnp.sum(h, axis=0)
    padded = (counts + tm - 1) // tm * tm
    pad_ends = jnp.cumsum(padded)
    pad_starts = pad_ends - padded
    base = (pad_starts[None, :] + jnp.cumsum(h, axis=0) - h).astype(F32)[:, None, :]
    n_blocks = n_tok * TOP_K // tm + N_EXPERTS
    n_active = (pad_ends[N_EXPERTS - 1] // tm).astype(jnp.int32)
    blk = jnp.minimum(jnp.arange(n_blocks, dtype=jnp.int32), n_active - 1)
    ends = pad_ends[:N_EXPERTS]
    block_e = jnp.minimum(jnp.sum((ends[None, :] <= (blk * tm)[:, None]).astype(jnp.int32), axis=1),
                          N_EXPERTS - 1).astype(jnp.int32)
    return base, block_e, n_active.reshape(1), n_blocks


def _sc_mesh():
    return plsc.VectorSubcoreMesh(core_axis_name="c", subcore_axis_name="s")


def sc_dispatch(x_rows, idx, n_out):
    n, wd = x_rows.shape
    win = SC_WIN
    nwin = n // win

    @functools.partial(pl.kernel, out_type=jax.ShapeDtypeStruct((n_out, wd), x_rows.dtype),
                       mesh=_sc_mesh(), scratch_types=[])
    def k(x_hbm, i_hbm, o_hbm):
        def body(x_vmem, i_vmem):
            pltpu.sync_copy(x_vmem, o_hbm.at[i_vmem.at[0]])

        pltpu.emit_pipeline(
            body,
            grid=(idx.shape[1] // win,),
            in_specs=[pl.BlockSpec((win, wd), lambda i: (i % nwin, 0)),
                      pl.BlockSpec((1, win), lambda i: (0, i))],
            out_specs=[],
            core_axis_name=("c", "s"),
            dimension_semantics=(pltpu.PARALLEL,),
        )(x_hbm, i_hbm)

    return k(x_rows, idx)


def sc_combine(y_rows, idx):
    wd = y_rows.shape[1]
    m = idx.shape[1]
    win = SC_WIN

    @functools.partial(pl.kernel, out_type=jax.ShapeDtypeStruct((m, wd), y_rows.dtype),
                       mesh=_sc_mesh(), scratch_types=[])
    def k(y_hbm, i_hbm, o_hbm):
        def body(i_vmem, o_vmem):
            pltpu.sync_copy(y_hbm.at[i_vmem.at[0]], o_vmem)

        pltpu.emit_pipeline(
            body,
            grid=(m // win,),
            in_specs=[pl.BlockSpec((1, win), lambda i: (0, i))],
            out_specs=[pl.BlockSpec((win, wd), lambda i: (i, 0))],
            core_axis_name=("c", "s"),
            dimension_semantics=(pltpu.PARALLEL,),
        )(i_hbm, o_hbm)

    return k(y_rows, idx)


def _moe_kernel(be_ref, na_ref, xa_ref, xb_ref, wup_ref, bup_ref, wdn_ref, bdn_ref, ya_ref, yb_ref,
                wup_sc, wdn_sc):
    i = pl.program_id(0)
    active = i < na_ref[0]
    prev = be_ref[jnp.maximum(i - 1, 0)]
    fresh = (i == 0) | (be_ref[i] != prev)

    @pl.when(active & fresh)
    def _():
        wup_sc[...] = wup_ref[0].astype(BF16)
        wdn_sc[...] = wdn_ref[0].astype(BF16)

    @pl.when(active)
    def _():
        xb = _unpack_rows(jnp.concatenate([xa_ref[...], xb_ref[...]], axis=1)).astype(BF16)
        hu = jnp.dot(xb, wup_sc[...], preferred_element_type=F32) + bup_ref[0]
        gate = jnp.minimum(hu[:, :D_FF], SWIGLU_LIMIT)
        lin = jnp.clip(hu[:, D_FF:], -SWIGLU_LIMIT, SWIGLU_LIMIT)
        act = gate * jax.nn.sigmoid(SWIGLU_ALPHA * gate) * (lin + 1.0)
        y = jnp.dot(act.astype(BF16), wdn_sc[...], preferred_element_type=F32) + bdn_ref[0]
        words = _pack_rows(y)
        ya_ref[...] = words[:, :PERM_W]
        yb_ref[...] = words[:, PERM_W:]

    @pl.when(jnp.logical_not(active))
    def _():
        ya_ref[...] = jnp.zeros(ya_ref.shape, jnp.uint32)
        yb_ref[...] = jnp.zeros(yb_ref.shape, jnp.uint32)


def moe_experts(xs_a, xs_b, block_e, n_active, w_up, b_up, w_down, b_down):
    n_rows = xs_a.shape[0]
    tm = MOE_TILE
    n_blocks = n_rows // tm
    emap3 = lambda i, be, na: (be[i], 0, 0)
    grid_spec = pltpu.PrefetchScalarGridSpec(
        num_scalar_prefetch=2,
        grid=(n_blocks,),
        in_specs=[pl.BlockSpec((tm, PERM_W), lambda i, be, na: (i, 0)),
                  pl.BlockSpec((tm, PERM_W), lambda i, be, na: (i, 0)),
                  pl.BlockSpec((1, D_MODEL, 2 * D_FF), emap3),
                  pl.BlockSpec((1, 1, 2 * D_FF), emap3),
                  pl.BlockSpec((1, D_FF, D_MODEL), emap3),
                  pl.BlockSpec((1, 1, D_MODEL), emap3)],
        out_specs=[pl.BlockSpec((tm, PERM_W), lambda i, be, na: (i, 0)),
                   pl.BlockSpec((tm, PERM_W), lambda i, be, na: (i, 0))],
        scratch_shapes=[pltpu.VMEM((D_MODEL, 2 * D_FF), BF16), pltpu.VMEM((D_FF, D_MODEL), BF16)],
    )
    return pl.pallas_call(
        _moe_kernel,
        grid_spec=grid_spec,
        out_shape=[jax.ShapeDtypeStruct((n_rows, PERM_W), jnp.uint32)] * 2,
        compiler_params=_cparams(("arbitrary",)),
        name="moe_experts",
    )(block_e, n_active, xs_a, xs_b, w_up, b_up.reshape(N_EXPERTS, 1, 2 * D_FF), w_down,
      b_down.reshape(N_EXPERTS, 1, D_MODEL))


def _final_kernel(x1_ref, yga_ref, ygb_ref, tg_ref, gtf_ref, sh_ref, sc_ref, g_ref, y_ref):
    tg = tg_ref[...]
    moe = jnp.zeros(x1_ref.shape, F32)
    for kk in range(TOP_K):
        moe = moe + tg[:, kk:kk + 1] * _unpack_rows(jnp.concatenate([yga_ref[kk], ygb_ref[kk]], axis=1))
    x2 = x1_ref[...] + gtf_ref[0] * moe
    y_ref[...] = _rms(x2, g_ref[...]) * (1.0 + sc_ref[0]) + sh_ref[0]


def final_out(x1, yg_a, yg_b, tg, row0, gtf, sh, sc, mod_per_row, g_final):
    n = x1.shape[0]
    tm = ROW_TILE
    t0 = row0 // tm
    row = lambda i: (i, 0)
    if mod_per_row:
        mod_spec = pl.BlockSpec((1, tm, D_MODEL), lambda i: (0, i, 0))
    else:
        tiles_per_seq = n // gtf.shape[0] // tm
        mod_spec = pl.BlockSpec((1, 1, D_MODEL), lambda i: (i // tiles_per_seq, 0, 0))
    return pl.pallas_call(
        _final_kernel,
        grid=(n // tm,),
        in_specs=[pl.BlockSpec((tm, D_MODEL), row),
                  pl.BlockSpec((TOP_K, tm, PERM_W), lambda i: (0, i + t0, 0)),
                  pl.BlockSpec((TOP_K, tm, PERM_W), lambda i: (0, i + t0, 0)),
                  pl.BlockSpec((tm, LANES), lambda i: (i + t0, 0)),
                  mod_spec, mod_spec, mod_spec,
                  pl.BlockSpec((1, D_MODEL), lambda i: (0, 0))],
        out_specs=pl.BlockSpec((tm, D_MODEL), row),
        out_shape=jax.ShapeDtypeStruct((n, D_MODEL), F32),
        compiler_params=_cparams(("parallel",)),
        name="final_sample" if mod_per_row else "final_prompt",
    )(x1, yg_a, yg_b, tg, gtf, sh, sc, g_final)


def _prep_weights(w_in, g_q_a, w_q_b, g_kv_a, w_kv_b, w_gk_b, b_gk, g_mla_out, g_gla_out, w_o,
                  g_norm_mix, g_norm_ffn, w_router, b_router):
    sizes = (MLA_Q_LORA, MLA_KV_LORA, MLA_ROPE, GLA_QK, GLA_QK, GLA_WIDTH, GLA_GATE_RANK, GLA_WIDTH)
    offs = np.cumsum((0,) + sizes)
    part = lambda i: w_in[:, offs[i]:offs[i + 1]]
    half = MLA_ROPE // 2
    k_rope = part(2)
    misc = jnp.concatenate([k_rope, part(6), jnp.zeros((D_MODEL, LANES - MLA_ROPE - GLA_GATE_RANK), F32)], 1)
    swap = jnp.concatenate([-k_rope[:, half:], k_rope[:, :half],
                            jnp.zeros((D_MODEL, LANES - MLA_ROPE), F32)], 1)
    w_in_pad = jnp.concatenate([part(0), part(1), part(3), part(4), part(5), part(7), misc, swap], 1)

    pad_q = jnp.zeros((MLA_Q_LORA, MLA_HEADS, HEAD_PAD - MLA_NOPE - MLA_ROPE), F32)
    wq1 = jnp.concatenate([w_q_b, pad_q], axis=2)
    q_lo = w_q_b[:, :, MLA_NOPE:MLA_NOPE + half]
    q_hi = w_q_b[:, :, MLA_NOPE + half:]
    wq2 = jnp.concatenate([jnp.zeros((MLA_Q_LORA, MLA_HEADS, MLA_NOPE), F32), -q_hi, q_lo, pad_q], axis=2)
    pad_kv = jnp.zeros((MLA_KV_LORA, MLA_HEADS, HEAD_PAD - MLA_NOPE), F32)
    w_uk = w_kv_b[:, :, :MLA_NOPE]
    w_uv = w_kv_b[:, :, MLA_NOPE:]
    wk = jnp.concatenate([w_uk, pad_kv], axis=2)
    wv = jnp.concatenate([w_uv, pad_kv], axis=2)
    wuk_t = jnp.concatenate([jnp.transpose(w_uk, (1, 2, 0)),
                             jnp.zeros((MLA_HEADS, HEAD_PAD - MLA_NOPE, MLA_KV_LORA), F32)], axis=1)
    wuv_h = jnp.transpose(wv, (1, 0, 2))
    wgk = jnp.zeros((LANES, GLA_QK), F32).at[MLA_ROPE:MLA_ROPE + GLA_GATE_RANK].set(w_gk_b)

    wo_mla = w_o[:MLA_HEADS * MLA_V].reshape(MLA_HEADS, MLA_V, D_MODEL)
    wo_mla = jnp.concatenate([wo_mla, jnp.zeros((MLA_HEADS, HEAD_PAD - MLA_V, D_MODEL), F32)], axis=1)
    w_o_pad = jnp.concatenate([wo_mla.reshape(MLA_PAD, D_MODEL), w_o[MLA_HEADS * MLA_V:]], axis=0)
    g_mla = g_mla_out.reshape(MLA_HEADS, MLA_V)
    g_mla_pad = jnp.concatenate([g_mla, jnp.zeros((MLA_HEADS, HEAD_PAD - MLA_V), F32)], 1).reshape(1, MLA_PAD)
    w_router_pad = jnp.concatenate([w_router, jnp.zeros((D_MODEL, LANES - N_EXPERTS), F32)], axis=1)
    wr_hi = w_router_pad.astype(BF16)
    w_router_pad = jnp.stack([wr_hi, (w_router_pad - wr_hi.astype(F32)).astype(BF16)])
    b_router_pad = jnp.concatenate([b_router, jnp.full((LANES - N_EXPERTS,), -jnp.inf, F32)]).reshape(1, LANES)
    return dict(
        w_in=w_in_pad.astype(BF16), g_norm_mix=g_norm_mix.reshape(1, D_MODEL),
        g_q_a=g_q_a.reshape(1, MLA_Q_LORA),
        wq1=wq1.reshape(MLA_Q_LORA, MLA_PAD).astype(BF16), wq2=wq2.reshape(MLA_Q_LORA, MLA_PAD).astype(BF16),
        g_kv_a=g_kv_a.reshape(1, MLA_KV_LORA),
        wk=wk.reshape(MLA_KV_LORA, MLA_PAD).astype(BF16), wv=wv.reshape(MLA_KV_LORA, MLA_PAD).astype(BF16),
        wuk_t=wuk_t.astype(BF16), wuv_h=wuv_h.astype(BF16),
        wgk=wgk.astype(BF16), b_gk=b_gk.reshape(1, GLA_QK),
        g_mla_pad=g_mla_pad, g_gla_out=g_gla_out.reshape(1, GLA_DV), w_o_pad=w_o_pad.astype(BF16),
        g_norm_ffn=g_norm_ffn.reshape(1, D_MODEL), w_router_pad=w_router_pad, b_router_pad=b_router_pad,
    )


def _rope_tables(pos, reps):
    half = MLA_ROPE // 2
    inv = ROPE_THETA ** (-jnp.arange(half, dtype=F32) / half)
    ang = pos.astype(F32)[:, None] * inv
    cos, sin = jnp.cos(ang), jnp.sin(ang)
    n = pos.shape[0]
    qc = jnp.concatenate([jnp.full((n, MLA_NOPE), Q_SCALE, F32), Q_SCALE * cos, Q_SCALE * cos,
                          jnp.zeros((n, HEAD_PAD - MLA_NOPE - MLA_ROPE), F32)], axis=1)
    qs = jnp.concatenate([jnp.zeros((n, MLA_NOPE), F32), Q_SCALE * sin, Q_SCALE * sin,
                          jnp.zeros((n, HEAD_PAD - MLA_NOPE - MLA_ROPE), F32)], axis=1)
    kc = jnp.concatenate([cos, cos, jnp.zeros((n, LANES - MLA_ROPE), F32)], axis=1)
    ks = jnp.concatenate([sin, sin, jnp.zeros((n, LANES - MLA_ROPE), F32)], axis=1)
    return tuple(jnp.tile(t, (reps, 1)) for t in (qc, qs, kc, ks))


def kernel(x_prompt, x_sample, cache_ckv, cache_kpe, state_gla, page_table, c_prompt, c_sample, w_ada, b_ada, g_norm_mix, w_in, g_q_a, w_q_b, g_kv_a, w_kv_b, w_gk_b, b_gk, g_mla_out, g_gla_out, w_o, g_norm_ffn, w_router, b_router, w_up, b_up, w_down, b_down, g_norm_final, w_ada_final, b_ada_final):
    B, S, D = x_prompt.shape
    DB, T, _ = x_sample.shape
    depth = w_ada.shape[0]
    assert depth == 1
    past_len = page_table.shape[1] * cache_ckv.shape[2]
    n_p, n_s = B * S, DB * T
    l = 0

    w = _prep_weights(w_in[l], g_q_a[l], w_q_b[l], g_kv_a[l], w_kv_b[l], w_gk_b[l], b_gk[l],
                      g_mla_out[l], g_gla_out[l], w_o[l], g_norm_mix[l], g_norm_ffn[l],
                      w_router[l], b_router[l])

    n_c = B + DB
    n_c_pad = (n_c + 7) // 8 * 8
    c_all = jnp.concatenate([c_prompt, c_sample, jnp.zeros((n_c_pad - n_c, D), F32)], axis=0)
    mod = ada_terms(c_all, w_ada[l], b_ada[l])
    mod_f = ada_terms(c_all, w_ada_final, b_ada_final)
    term = lambda m, i: m[:, i * D:(i + 1) * D]
    p_term = lambda m, i: term(m, i)[:B].reshape(B, 1, D)
    s_term = lambda m, i: jnp.broadcast_to(term(m, i)[B:n_c][:, None, :], (DB, T, D)).reshape(1, n_s, D)

    xp = x_prompt.reshape(n_p, D)
    xs = x_sample.reshape(n_s, D)
    tabs_p = _rope_tables(jnp.arange(S), 1)
    tabs_s = _rope_tables(past_len + jnp.arange(T), ROW_TILE // T)

    (q_p, k_p, v_p, ckv_p, kpe_p, gq_p, gk_p, gv_p, la_p, gg_p) = in_proj(
        xp, p_term(mod, 0), p_term(mod, 1), False, tabs_p, w, True)
    o_mla_p = mla_prefill(q_p, k_p, v_p, B, S)
    s0 = jnp.zeros((B, GLA_HEADS, GLA_DK, GLA_DV), F32)
    o_gla_p, gla_p = gla(gq_p, gk_p, la_p, gv_p, s0, B, S, GLA_BLOCK)
    n_all = n_p + n_s
    h2_init = (jnp.zeros((n_all, PERM_W), jnp.uint32), jnp.zeros((n_all, PERM_W), jnp.uint32))
    x1_p, h2a, h2b, ti_p, tg_p, hist_p = mixer_out(xp, o_mla_p, o_gla_p.reshape(n_p, GLA_WIDTH), gg_p,
                                                   p_term(mod, 2), p_term(mod, 3), p_term(mod, 4), False, w,
                                                   n_all, 0, h2_init)

    (q_s, ckv_s, kpe_s, gq_s, gk_s, gv_s, la_s, gg_s) = in_proj(
        xs, s_term(mod, 0), s_term(mod, 1), True, tabs_s, w, False)
    qlat, qpe = absorb_q(q_s, w["wuk_t"])
    o_lat = mla_decode(qlat, qpe, ckv_s, kpe_s, cache_ckv, cache_kpe, page_table, DB, T)
    o_mla_s = latent_to_values(o_lat, w["wuv_h"], DB, T)
    tpad = GLA_CHUNK
    padt = lambda a: jnp.pad(a.reshape(DB, T, a.shape[-1]), ((0, 0), (0, tpad - T), (0, 0))).reshape(
        DB * tpad, a.shape[-1])
    o_gla_s, gla_s = gla(padt(gq_s), padt(gk_s), padt(la_s), padt(gv_s), state_gla[l], DB, tpad, tpad)
    o_gla_s = o_gla_s[:, :T].reshape(n_s, GLA_WIDTH)
    x1_s, h2a, h2b, ti_s, tg_s, hist_s = mixer_out(xs, o_mla_s, o_gla_s, gg_s,
                                                   s_term(mod, 2), s_term(mod, 3), s_term(mod, 4), True, w,
                                                   n_all, n_p, (h2a, h2b))

    ti = jnp.concatenate([ti_p, ti_s], axis=0)
    tg = jnp.concatenate([tg_p, tg_s], axis=0)
    base, block_e, n_active, n_blocks = _route_tables(jnp.concatenate([hist_p, hist_s], axis=0), n_all)
    dest = route_rank(ti, base)
    idx = dest[:, :TOP_K].T.reshape(1, TOP_K * n_all)
    n_rows = n_blocks * MOE_TILE
    xs_a = sc_dispatch(h2a, idx, n_rows)
    xs_b = sc_dispatch(h2b, idx, n_rows)
    ys_a, ys_b = moe_experts(xs_a, xs_b, block_e, n_active, w_up[l], b_up[l], w_down[l], b_down[l])
    yg_a = sc_combine(ys_a, idx).reshape(TOP_K, n_all, PERM_W)
    yg_b = sc_combine(ys_b, idx).reshape(TOP_K, n_all, PERM_W)

    g_fin = g_norm_final.reshape(1, D)
    y_p = final_out(x1_p, yg_a, yg_b, tg, 0, p_term(mod, 5), p_term(mod_f, 0), p_term(mod_f, 1), False, g_fin)
    y_s = final_out(x1_s, yg_a, yg_b, tg, n_p, s_term(mod, 5), s_term(mod_f, 0), s_term(mod_f, 1), True, g_fin)

    return (y_p.reshape(B, S, D), y_s.reshape(DB, T, D),
            ckv_p.reshape(1, B, S, MLA_KV_LORA), kpe_p.reshape(1, B, S, MLA_ROPE), gla_p[None],
            ckv_s.reshape(1, DB, T, MLA_KV_LORA), kpe_s.reshape(1, DB, T, MLA_ROPE), gla_s[None])
```

```python
import functools
import math

import jax
import jax.numpy as jnp
import numpy as np
from jax import lax
from jax.experimental import pallas as pl
from jax.experimental.pallas import tpu as pltpu
from jax.experimental.pallas import tpu_sc as plsc

F32 = jnp.float32
BF16 = jnp.bfloat16

D_MODEL = 1024
MLA_HEADS = 8
MLA_NOPE = 64
MLA_ROPE = 32
MLA_V = 64
MLA_Q_LORA = 384
MLA_KV_LORA = 256
MLA_SCALE = (MLA_NOPE + MLA_ROPE) ** -0.5
ROPE_THETA = 10000.0
GLA_HEADS = 4
GLA_DK = 64
GLA_DV = 128
GLA_GATE_RANK = 16
GLA_GATE_NORM = 16.0
GLA_CHUNK = 16
GLA_QK = GLA_HEADS * GLA_DK
GLA_WIDTH = GLA_HEADS * GLA_DV
N_EXPERTS = 32
TOP_K = 4
D_FF = D_MODEL
SWIGLU_LIMIT = 7.0
SWIGLU_ALPHA = 1.702
N_MOD = 6
EPS = 1e-6
PAGE_SIZE = 128

LANES = 128
HEAD_PAD = LANES
MLA_PAD = MLA_HEADS * HEAD_PAD
VMEM_LIMIT = 56 * 1024 * 1024

Q_SCALE = MLA_SCALE * math.log2(math.e)

C_QA = 0
C_KV = C_QA + MLA_Q_LORA
C_GQ = C_KV + MLA_KV_LORA
C_GK = C_GQ + GLA_QK
C_GV = C_GK + GLA_QK
C_GG = C_GV + GLA_WIDTH
C_MISC = C_GG + GLA_WIDTH
C_SWAP = C_MISC + LANES
D_IN_PAD = C_SWAP + LANES

ROW_TILE = 256
ATT_TILE = 512
ATT_HEADS = 4
GLA_BLOCK = 256
DEC_SPANS = 4
SC_WIN = 128
PERM_W = D_MODEL // 4
MOE_TILE = 512
NEW_PAD = 16


def _cparams(sem):
    return pltpu.CompilerParams(dimension_semantics=sem, vmem_limit_bytes=VMEM_LIMIT)


def _rms(x, g):
    return x * lax.rsqrt(jnp.mean(x * x, axis=-1, keepdims=True) + EPS) * g


def _pack_rows(x):
    bits = lax.bitcast_convert_type(x.astype(BF16).astype(F32), jnp.uint32)
    w = x.shape[1] // 2
    return (bits[:, :w] >> 16) | bits[:, w:]


def _unpack_rows(words):
    lo = lax.bitcast_convert_type(words << 16, F32)
    hi = lax.bitcast_convert_type(words & jnp.uint32(0xFFFF0000), F32)
    return jnp.concatenate([lo, hi], axis=1)


def _ada_kernel(c_ref, w_ref, b_ref, o_ref):
    c = c_ref[...]
    a = (c * jax.nn.sigmoid(c)).astype(BF16)
    o_ref[...] = jnp.dot(a, w_ref[...].astype(BF16), preferred_element_type=F32) + b_ref[...]


def ada_terms(c, w, b):
    rows, d = c.shape
    n = w.shape[1]
    tn = 512
    return pl.pallas_call(
        _ada_kernel,
        grid=(n // tn,),
        in_specs=[pl.BlockSpec((rows, d), lambda j: (0, 0)),
                  pl.BlockSpec((d, tn), lambda j: (0, j)),
                  pl.BlockSpec((1, tn), lambda j: (0, j))],
        out_specs=pl.BlockSpec((rows, tn), lambda j: (0, j)),
        out_shape=jax.ShapeDtypeStruct((rows, n), F32),
        compiler_params=_cparams(("arbitrary",)),
        name="ada_terms",
    )(c, w, b.reshape(1, n))


def _in_kernel(with_kv, x_ref, sh_ref, sc_ref, g_ref, win_ref, gqa_ref, wq1_ref, wq2_ref,
               gkv_ref, wk_ref, wv_ref, wgk_ref, bgk_ref, qc_ref, qs_ref, kc_ref, ks_ref, *outs):
    if with_kv:
        q_out, k_out, v_out, ckv_out, kpe_out, gq_out, gk_out, gv_out, la_out, gg_out = outs
    else:
        q_out, ckv_out, kpe_out, gq_out, gk_out, gv_out, la_out, gg_out = outs
    x = x_ref[...]
    h = _rms(x, g_ref[...])
    h = h * (1.0 + sc_ref[0]) + sh_ref[0]
    proj = jnp.dot(h.astype(BF16), win_ref[...], preferred_element_type=F32)

    qn = _rms(proj[:, C_QA:C_KV], gqa_ref[...]).astype(BF16)
    qa = jnp.dot(qn, wq1_ref[...], preferred_element_type=F32)
    qb = jnp.dot(qn, wq2_ref[...], preferred_element_type=F32)
    qc = qc_ref[...]
    qs = qs_ref[...]
    for hh in range(MLA_HEADS):
        sl = slice(hh * HEAD_PAD, (hh + 1) * HEAD_PAD)
        q_out[:, sl] = (qa[:, sl] * qc + qb[:, sl] * qs).astype(BF16)

    ckv = _rms(proj[:, C_KV:C_GQ], gkv_ref[...])
    ckv_out[...] = ckv
    kpe = proj[:, C_MISC:C_SWAP] * kc_ref[...] + proj[:, C_SWAP:D_IN_PAD] * ks_ref[...]
    kpe_out[...] = kpe[:, :MLA_ROPE]
    if with_kv:
        ckv_b = ckv.astype(BF16)
        kn = jnp.dot(ckv_b, wk_ref[...], preferred_element_type=F32)
        kpe_sh = pltpu.roll(kpe, MLA_NOPE, axis=1)
        for hh in range(MLA_HEADS):
            sl = slice(hh * HEAD_PAD, (hh + 1) * HEAD_PAD)
            k_out[:, sl] = (kn[:, sl] + kpe_sh).astype(BF16)
        v_out[...] = jnp.dot(ckv_b, wv_ref[...], preferred_element_type=F32).astype(BF16)

    gq_out[...] = proj[:, C_GQ:C_GK] * (GLA_DK ** -0.5)
    gk_out[...] = proj[:, C_GK:C_GV]
    gv_out[...] = proj[:, C_GV:C_GG].astype(BF16)
    gg_out[...] = proj[:, C_GG:C_MISC].astype(BF16)
    xg = jnp.dot(proj[:, C_MISC:C_SWAP].astype(BF16), wgk_ref[...],
                 preferred_element_type=F32) + bgk_ref[...]
    la_out[...] = (jnp.minimum(xg, 0.0) - jnp.log(1.0 + jnp.exp(-jnp.abs(xg)))) * (1.0 / GLA_GATE_NORM)


def in_proj(x, sh, sc, mod_per_row, tabs, wts, with_kv):
    n = x.shape[0]
    tm = ROW_TILE
    nt = n // tm
    qc, qs, kc, ks = tabs
    n_tab = qc.shape[0] // tm
    row = lambda i: (i, 0)
    const = lambda i: (0, 0)
    if mod_per_row:
        mod_spec = pl.BlockSpec((1, tm, D_MODEL), lambda i: (0, i, 0))
    else:
        tiles_per_seq = n // sh.shape[0] // tm
        mod_spec = pl.BlockSpec((1, 1, D_MODEL), lambda i: (i // tiles_per_seq, 0, 0))
    tab_spec = pl.BlockSpec((tm, LANES), lambda i: (i % n_tab, 0))
    w = wts
    in_specs = [
        pl.BlockSpec((tm, D_MODEL), row), mod_spec, mod_spec,
        pl.BlockSpec((1, D_MODEL), const),
        pl.BlockSpec((D_MODEL, D_IN_PAD), const),
        pl.BlockSpec((1, MLA_Q_LORA), const),
        pl.BlockSpec((MLA_Q_LORA, MLA_PAD), const),
        pl.BlockSpec((MLA_Q_LORA, MLA_PAD), const),
        pl.BlockSpec((1, MLA_KV_LORA), const),
        pl.BlockSpec((MLA_KV_LORA, MLA_PAD), const),
        pl.BlockSpec((MLA_KV_LORA, MLA_PAD), const),
        pl.BlockSpec((LANES, GLA_QK), const),
        pl.BlockSpec((1, GLA_QK), const),
        tab_spec, tab_spec, tab_spec, tab_spec,
    ]
    wide = lambda dt: (jax.ShapeDtypeStruct((n, MLA_PAD), dt), pl.BlockSpec((tm, MLA_PAD), row))
    outs = [wide(BF16)]
    if with_kv:
        outs += [wide(BF16), wide(BF16)]
    outs += [
        (jax.ShapeDtypeStruct((n, MLA_KV_LORA), F32), pl.BlockSpec((tm, MLA_KV_LORA), row)),
        (jax.ShapeDtypeStruct((n, MLA_ROPE), F32), pl.BlockSpec((tm, MLA_ROPE), row)),
        (jax.ShapeDtypeStruct((n, GLA_QK), F32), pl.BlockSpec((tm, GLA_QK), row)),
        (jax.ShapeDtypeStruct((n, GLA_QK), F32), pl.BlockSpec((tm, GLA_QK), row)),
        (jax.ShapeDtypeStruct((n, GLA_WIDTH), BF16), pl.BlockSpec((tm, GLA_WIDTH), row)),
        (jax.ShapeDtypeStruct((n, GLA_QK), F32), pl.BlockSpec((tm, GLA_QK), row)),
        (jax.ShapeDtypeStruct((n, GLA_WIDTH), BF16), pl.BlockSpec((tm, GLA_WIDTH), row)),
    ]
    return pl.pallas_call(
        functools.partial(_in_kernel, with_kv),
        grid=(nt,),
        in_specs=in_specs,
        out_specs=[o[1] for o in outs],
        out_shape=[o[0] for o in outs],
        compiler_params=_cparams(("parallel",)),
        name="inproj_prompt" if with_kv else "inproj_sample",
    )(x, sh, sc, w["g_norm_mix"], w["w_in"], w["g_q_a"], w["wq1"], w["wq2"], w["g_kv_a"],
      w["wk"], w["wv"], w["wgk"], w["b_gk"], qc, qs, kc, ks)


def _prefill_kernel(q_ref, k_ref, v_ref, o_ref):
    qi = pl.program_id(2)
    t = ATT_TILE
    lanes = [slice(g * HEAD_PAD, (g + 1) * HEAD_PAD) for g in range(ATT_HEADS)]
    qs = [q_ref[0, :, sl] for sl in lanes]

    def update(carry, q, sl, j, mask):
        m, l, acc = carry
        r0 = pl.multiple_of(j * t, t)
        kj = k_ref[0, pl.ds(r0, t), sl]
        vj = v_ref[0, pl.ds(r0, t), sl]
        s = lax.dot_general(q, kj, (((1,), (1,)), ((), ())), preferred_element_type=F32)
        if mask is not None:
            s = jnp.where(mask, s, -jnp.inf)
        m_new = jnp.maximum(m, jnp.max(s, axis=-1, keepdims=True))
        p = jnp.exp2(s - m_new)
        alpha = jnp.exp2(m - m_new)
        acc = alpha * acc + jnp.dot(p.astype(BF16), vj, preferred_element_type=F32)
        l = alpha * l + jnp.sum(p, axis=-1, keepdims=True)
        return m_new, l, acc

    def body(j, carries):
        return tuple(update(c, q, sl, j, None) for c, q, sl in zip(carries, qs, lanes))

    init = (jnp.full((t, 1), -jnp.inf, F32), jnp.zeros((t, 1), F32), jnp.zeros((t, HEAD_PAD), F32))
    carries = lax.fori_loop(0, qi, body, (init,) * ATT_HEADS)
    rows = lax.broadcasted_iota(jnp.int32, (t, t), 0)
    cols = lax.broadcasted_iota(jnp.int32, (t, t), 1)
    causal = cols <= rows
    for c, q, sl in zip(carries, qs, lanes):
        m, l, acc = update(c, q, sl, qi, causal)
        o_ref[0, :, sl] = (acc / l).astype(BF16)


def mla_prefill(q, k, v, batch, seq):
    t = ATT_TILE
    q3 = q.reshape(batch, seq, MLA_PAD)
    k3 = k.reshape(batch, seq, MLA_PAD)
    v3 = v.reshape(batch, seq, MLA_PAD)
    o = pl.pallas_call(
        _prefill_kernel,
        grid=(batch, MLA_HEADS // ATT_HEADS, seq // t),
        in_specs=[pl.BlockSpec((1, t, ATT_HEADS * HEAD_PAD), lambda b, h, i: (b, i, h)),
                  pl.BlockSpec((1, seq, ATT_HEADS * HEAD_PAD), lambda b, h, i: (b, 0, h)),
                  pl.BlockSpec((1, seq, ATT_HEADS * HEAD_PAD), lambda b, h, i: (b, 0, h))],
        out_specs=pl.BlockSpec((1, t, ATT_HEADS * HEAD_PAD), lambda b, h, i: (b, i, h)),
        out_shape=jax.ShapeDtypeStruct((batch, seq, MLA_PAD), BF16),
        compiler_params=_cparams(("parallel", "parallel", "arbitrary")),
        name="mla_prefill",
    )(q3, k3, v3)
    return o.reshape(batch * seq, MLA_PAD)


def _absorb_kernel(q_ref, wuk_ref, qlat_ref, qpe_ref):
    q = q_ref[...]
    qlat_ref[0] = jnp.dot(q, wuk_ref[0], preferred_element_type=F32)
    qf = pltpu.roll(q.astype(F32), HEAD_PAD - MLA_NOPE, axis=1)
    lane = lax.broadcasted_iota(jnp.int32, qf.shape, 1)
    qpe_ref[0] = jnp.where(lane < MLA_ROPE, qf, 0.0)


def absorb_q(q, wuk_t):
    r = q.shape[0]
    return pl.pallas_call(
        _absorb_kernel,
        grid=(MLA_HEADS,),
        in_specs=[pl.BlockSpec((r, HEAD_PAD), lambda h: (0, h)),
                  pl.BlockSpec((1, HEAD_PAD, MLA_KV_LORA), lambda h: (h, 0, 0))],
        out_specs=[pl.BlockSpec((1, r, MLA_KV_LORA), lambda h: (h, 0, 0)),
                   pl.BlockSpec((1, r, HEAD_PAD), lambda h: (h, 0, 0))],
        out_shape=[jax.ShapeDtypeStruct((MLA_HEADS, r, MLA_KV_LORA), F32),
                   jax.ShapeDtypeStruct((MLA_HEADS, r, HEAD_PAD), F32)],
        compiler_params=_cparams(("parallel",)),
        name="absorb_q",
    )(q, wuk_t)


def _decode_kernel(t_new, pt_ref, qlat_ref, qpe_ref, cnew_ref, knew_ref, ckv_hbm, kpe_hbm, o_ref,
                   cbuf, pbuf, sem):
    b = pl.program_id(0)
    nb = pl.num_programs(0)
    n_pages = cbuf.shape[1]
    slot = lax.rem(b, 2)
    rows = MLA_HEADS * t_new

    def page_copies(bb, sl, p):
        pg = pt_ref[bb, p]
        return (pltpu.make_async_copy(ckv_hbm.at[0, pg], cbuf.at[sl, p], sem.at[0, sl]),
                pltpu.make_async_copy(kpe_hbm.at[0, pg], pbuf.at[sl, p], sem.at[1, sl]))

    def fetch(bb, sl):
        def body(p, c):
            for cp in page_copies(bb, sl, p):
                cp.start()
            return c
        lax.fori_loop(0, n_pages, body, 0)

    @pl.when(b == 0)
    def _():
        fetch(0, 0)

    @pl.when(b + 1 < nb)
    def _():
        fetch(b + 1, 1 - slot)

    pltpu.make_async_copy(ckv_hbm.at[0, pl.ds(0, n_pages)], cbuf.at[slot], sem.at[0, slot]).wait()
    pltpu.make_async_copy(kpe_hbm.at[0, pl.ds(0, n_pages)], pbuf.at[slot], sem.at[1, slot]).wait()

    qlat = qlat_ref[...].reshape(rows, MLA_KV_LORA).astype(BF16)
    qpe = qpe_ref[...].reshape(rows, HEAD_PAD)[:, :MLA_ROPE].astype(BF16)
    dn = (((1,), (1,)), ((), ()))

    def partial_attention(s, values):
        m = jnp.max(s, axis=-1, keepdims=True)
        p = jnp.exp2(s - m)
        return m, jnp.sum(p, axis=-1, keepdims=True), jnp.dot(p.astype(BF16), values,
                                                               preferred_element_type=F32)

    span_pages = n_pages // DEC_SPANS
    span = span_pages * PAGE_SIZE
    parts = []
    for c in range(DEC_SPANS):
        pages = range(c * span_pages, (c + 1) * span_pages)
        kb = jnp.concatenate([cbuf[slot, p].astype(BF16) for p in pages], axis=0)
        pb = jnp.concatenate([pbuf[slot, p].astype(BF16) for p in pages], axis=1)
        s = (lax.dot_general(qlat, kb, dn, preferred_element_type=F32)
             + jnp.dot(qpe, pb, preferred_element_type=F32))
        parts.append(partial_attention(s, kb))

    cn = cnew_ref[0].astype(BF16)
    kn = knew_ref[0].astype(BF16)
    sn = (lax.dot_general(qlat, cn, dn, preferred_element_type=F32)
          + lax.dot_general(qpe, kn, dn, preferred_element_type=F32))
    tq = lax.broadcasted_iota(jnp.int32, sn.shape, 0) % t_new
    tk = lax.broadcasted_iota(jnp.int32, sn.shape, 1)
    parts.append(partial_attention(jnp.where(tk <= tq, sn, -jnp.inf), cn))

    m = parts[0][0]
    for pm, _, _ in parts[1:]:
        m = jnp.maximum(m, pm)
    l = jnp.zeros_like(m)
    acc = jnp.zeros((rows, MLA_KV_LORA), F32)
    for pm, pl_, pacc in parts:
        scale = jnp.exp2(pm - m)
        l = l + scale * pl_
        acc = acc + scale * pacc
    o_ref[0] = acc / l


def mla_decode(qlat, qpe, ckv_new, kpe_new, cache_ckv, cache_kpe, page_table, dec_batch, t_new):
    n_pages = page_table.shape[1]
    past_len = n_pages * PAGE_SIZE
    rows = MLA_HEADS * t_new
    kpe_t = jnp.swapaxes(cache_kpe, 2, 3)
    qlat4 = qlat.reshape(MLA_HEADS, dec_batch, t_new, MLA_KV_LORA)
    qpe4 = qpe.reshape(MLA_HEADS, dec_batch, t_new, HEAD_PAD)
    t_pad = NEW_PAD
    pad_new = lambda a: jnp.pad(a.reshape(dec_batch, t_new, a.shape[-1]), ((0, 0), (0, t_pad - t_new), (0, 0)))
    cnew = pad_new(ckv_new)
    knew = pad_new(kpe_new)

    in_specs = [
        pl.BlockSpec((MLA_HEADS, 1, t_new, MLA_KV_LORA), lambda b, pt: (0, b, 0, 0)),
        pl.BlockSpec((MLA_HEADS, 1, t_new, HEAD_PAD), lambda b, pt: (0, b, 0, 0)),
        pl.BlockSpec((1, t_pad, MLA_KV_LORA), lambda b, pt: (b, 0, 0)),
        pl.BlockSpec((1, t_pad, MLA_ROPE), lambda b, pt: (b, 0, 0)),
        pl.BlockSpec(memory_space=pl.ANY),
        pl.BlockSpec(memory_space=pl.ANY),
    ]
    grid_spec = pltpu.PrefetchScalarGridSpec(
        num_scalar_prefetch=1,
        grid=(dec_batch,),
        in_specs=in_specs,
        out_specs=pl.BlockSpec((1, rows, MLA_KV_LORA), lambda b, pt: (b, 0, 0)),
        scratch_shapes=[pltpu.VMEM((2, n_pages, PAGE_SIZE, MLA_KV_LORA), F32),
                        pltpu.VMEM((2, n_pages, MLA_ROPE, PAGE_SIZE), F32),
                        pltpu.SemaphoreType.DMA((2, 2))],
    )
    return pl.pallas_call(
        functools.partial(_decode_kernel, t_new),
        grid_spec=grid_spec,
        out_shape=jax.ShapeDtypeStruct((dec_batch, rows, MLA_KV_LORA), F32),
        compiler_params=_cparams(("arbitrary",)),
        name="mla_decode",
    )(page_table, qlat4, qpe4, cnew, knew, cache_ckv, kpe_t)


def _uv_kernel(o_ref, wuv_ref, out_ref):
    o = o_ref[...]
    o = o.reshape(o.shape[0] * o.shape[2], MLA_KV_LORA).astype(BF16)
    out_ref[...] = jnp.dot(o, wuv_ref[0], preferred_element_type=F32).astype(BF16)


def latent_to_values(o_lat, wuv, dec_batch, t_new):
    o4 = o_lat.reshape(dec_batch, MLA_HEADS, t_new, MLA_KV_LORA)
    return pl.pallas_call(
        _uv_kernel,
        grid=(MLA_HEADS,),
        in_specs=[pl.BlockSpec((dec_batch, 1, t_new, MLA_KV_LORA), lambda h: (0, h, 0, 0)),
                  pl.BlockSpec((1, MLA_KV_LORA, HEAD_PAD), lambda h: (h, 0, 0))],
        out_specs=pl.BlockSpec((dec_batch * t_new, HEAD_PAD), lambda h: (0, h)),
        out_shape=jax.ShapeDtypeStruct((dec_batch * t_new, MLA_PAD), BF16),
        compiler_params=_cparams(("parallel",)),
        name="latent_to_values",
    )(o4, wuv)


def _gla_kernel(nsb, q_ref, k_ref, la_ref, v_ref, s0_ref, o_ref, sfin_ref, st_sc, kv_sc, sall_sc):
    blk = pl.program_id(1)
    nblk = pl.num_programs(1)
    c = GLA_CHUNK
    tb = q_ref.shape[1]
    nc = tb // c
    cps = nc // nsb
    head_k = [slice(hh * GLA_DK, (hh + 1) * GLA_DK) for hh in range(GLA_HEADS)]
    head_v = [slice(hh * GLA_DV, (hh + 1) * GLA_DV) for hh in range(GLA_HEADS)]

    @pl.when(blk == 0)
    def _():
        for sq in range(nsb):
            for hh in range(GLA_HEADS):
                st_sc[sq, hh] = s0_ref[sq, hh].T

    q = q_ref[0]
    k = k_ref[0]
    la = la_ref[0]
    v = v_ref[0]

    r = lax.broadcasted_iota(jnp.int32, (tb, tb), 0)
    cc = lax.broadcasted_iota(jnp.int32, (tb, tb), 1)
    same = (r >> 4) == (cc >> 4)
    tri = jnp.where(same & (cc <= r), 1.0, 0.0).astype(F32)
    ones = jnp.where(same, 1.0, 0.0).astype(F32)
    b = jnp.dot(tri, la, precision=lax.Precision.HIGHEST, preferred_element_type=F32)
    b_last = jnp.dot(ones, la, precision=lax.Precision.HIGHEST, preferred_element_type=F32)
    qd = (q * jnp.exp(b)).astype(BF16)
    kd = (k * jnp.exp(b_last - b)).astype(BF16)
    dec = jnp.exp(b_last)

    hd = lax.broadcasted_iota(jnp.int32, (GLA_QK, GLA_WIDTH), 0) >> 6
    he = lax.broadcasted_iota(jnp.int32, (GLA_QK, GLA_WIDTH), 1) >> 7
    expand = jnp.where(hd == he, 1.0, 0.0).astype(BF16)
    b3 = b.reshape(nc, c, GLA_QK)
    k3 = k.reshape(nc, c, GLA_QK)
    q3 = q.reshape(nc, c, GLA_QK)
    v3 = v.astype(F32).reshape(nc, c, GLA_WIDTH)
    pos = lax.broadcasted_iota(jnp.int32, (nc, c, GLA_QK), 1)
    o_intra = jnp.zeros((tb, GLA_WIDTH), F32)
    for j in range(c):
        e = jnp.exp(b3 - b3[:, j:j + 1, :])
        w = jnp.where(pos >= j, q3 * k3[:, j:j + 1, :] * e, 0.0)
        red = jnp.dot(w.reshape(tb, GLA_QK).astype(BF16), expand, preferred_element_type=F32)
        vj = jnp.broadcast_to(v3[:, j:j + 1, :], (nc, c, GLA_WIDTH)).reshape(tb, GLA_WIDTH)
        o_intra = o_intra + red * vj
    for ci in range(nc):
        rs = slice(ci * c, (ci + 1) * c)
        for hh in range(GLA_HEADS):
            kv_sc[ci, hh] = lax.dot_general(v[rs, head_v[hh]], kd[rs, head_k[hh]], (((0,), (0,)), ((), ())),
                                            preferred_element_type=F32)

    for sq in range(nsb):
        states = [st_sc[sq, hh] for hh in range(GLA_HEADS)]
        for cj in range(cps):
            ci = sq * cps + cj
            dec_row = dec[ci * c:ci * c + 1, :]
            for hh in range(GLA_HEADS):
                sall_sc[ci, hh] = states[hh].astype(BF16)
                states[hh] = states[hh] * dec_row[:, head_k[hh]] + kv_sc[ci, hh]
        for hh in range(GLA_HEADS):
            st_sc[sq, hh] = states[hh]

    for ci in range(nc):
        rs = slice(ci * c, (ci + 1) * c)
        outs = [lax.dot_general(qd[rs, head_k[hh]], sall_sc[ci, hh], (((1,), (1,)), ((), ())),
                                preferred_element_type=F32) for hh in range(GLA_HEADS)]
        o_ref[0, rs, :] = o_intra[rs, :] + jnp.concatenate(outs, axis=1)

    @pl.when(blk == nblk - 1)
    def _():
        for sq in range(nsb):
            for hh in range(GLA_HEADS):
                sfin_ref[sq, hh] = st_sc[sq, hh].T


def gla(gq, gk, la, gv, s0, n_seq, seq_len):
    tb = GLA_BLOCK
    nsb = max(1, tb // seq_len)
    nblk = max(1, seq_len // tb)
    n_outer = n_seq // nsb
    nc = tb // GLA_CHUNK
    sh3 = lambda a: a.reshape(n_outer, nblk * tb, a.shape[-1])
    row = lambda b, i: (b, i, 0)
    st_spec = pl.BlockSpec((nsb, GLA_HEADS, GLA_DK, GLA_DV), lambda b, i: (b, 0, 0, 0))
    o, s_fin = pl.pallas_call(
        functools.partial(_gla_kernel, nsb),
        grid=(n_outer, nblk),
        in_specs=[pl.BlockSpec((1, tb, GLA_QK), row), pl.BlockSpec((1, tb, GLA_QK), row),
                  pl.BlockSpec((1, tb, GLA_QK), row), pl.BlockSpec((1, tb, GLA_WIDTH), row), st_spec],
        out_specs=[pl.BlockSpec((1, tb, GLA_WIDTH), row), st_spec],
        out_shape=[jax.ShapeDtypeStruct((n_outer, nblk * tb, GLA_WIDTH), F32),
                   jax.ShapeDtypeStruct((n_seq, GLA_HEADS, GLA_DK, GLA_DV), F32)],
        scratch_shapes=[pltpu.VMEM((nsb, GLA_HEADS, GLA_DV, GLA_DK), F32),
                        pltpu.VMEM((nc, GLA_HEADS, GLA_DV, GLA_DK), F32),
                        pltpu.VMEM((nc, GLA_HEADS, GLA_DV, GLA_DK), BF16)],
        compiler_params=_cparams(("parallel", "arbitrary")),
        name="gla_prompt" if nblk > 1 else "gla_sample",
    )(sh3(gq), sh3(gk), sh3(la), sh3(gv), s0)
    return o.reshape(n_seq * seq_len, GLA_WIDTH), s_fin


def _mix_kernel(x_ref, om_ref, og_ref, gg_ref, gta_ref, shf_ref, scf_ref, gm_ref, ggl_ref, wo_ref,
                gn_ref, wr_ref, br_ref, h2a_in, h2b_in, x1_ref, h2a_ref, h2b_ref, ti_ref, tg_ref, hist_ref):
    del h2a_in, h2b_in
    om = om_ref[...].astype(F32)
    ms = jnp.sum(om * om, axis=-1, keepdims=True) * (1.0 / (MLA_HEADS * MLA_V))
    om = om * lax.rsqrt(ms + EPS) * gm_ref[...]
    og = og_ref[...]
    gg = gg_ref[...].astype(F32)
    gate = gg * jax.nn.sigmoid(gg)
    parts = []
    for hh in range(GLA_HEADS):
        sl = slice(hh * GLA_DV, (hh + 1) * GLA_DV)
        parts.append(_rms(og[:, sl], ggl_ref[...]) * gate[:, sl])
    mix = jnp.concatenate([om] + parts, axis=1).astype(BF16)
    x1 = x_ref[...] + gta_ref[0] * jnp.dot(mix, wo_ref[...], preferred_element_type=F32)
    x1_ref[...] = x1
    h2 = _rms(x1, gn_ref[...]) * (1.0 + scf_ref[0]) + shf_ref[0]
    words = _pack_rows(h2)
    h2a_ref[...] = words[:, :PERM_W]
    h2b_ref[...] = words[:, PERM_W:]
    h_hi = h2.astype(BF16)
    h_lo = (h2 - h_hi.astype(F32)).astype(BF16)
    logits = (jnp.dot(h_hi, wr_ref[0], preferred_element_type=F32)
              + (jnp.dot(h_hi, wr_ref[1], preferred_element_type=F32)
                 + jnp.dot(h_lo, wr_ref[0], preferred_element_type=F32))) + br_ref[...]
    lane_i = lax.broadcasted_iota(jnp.int32, logits.shape, 1)
    lane = lane_i.astype(F32)
    vals = []
    idxs = []
    for _ in range(TOP_K):
        mx = jnp.max(logits, axis=-1, keepdims=True)
        ix = jnp.min(jnp.where(logits == mx, lane, float(LANES)), axis=-1, keepdims=True)
        vals.append(mx)
        idxs.append(ix)
        logits = jnp.where(lane == ix, -jnp.inf, logits)
    ex = [jnp.exp(vv - vals[0]) for vv in vals]
    den = ex[0] + ex[1] + ex[2] + ex[3]
    ti = jnp.zeros(logits.shape, F32)
    tg = jnp.zeros(logits.shape, F32)
    onehot = jnp.zeros(logits.shape, F32)
    for kk in range(TOP_K):
        ti = jnp.where(lane_i == kk, idxs[kk], ti)
        tg = jnp.where(lane_i == kk, ex[kk] / den, tg)
        onehot = onehot + jnp.where(lane == idxs[kk], 1.0, 0.0)
    ti_ref[...] = ti.astype(jnp.int32)
    tg_ref[...] = tg
    hist_ref[0] = jnp.sum(onehot, axis=0, keepdims=True)


def mixer_out(x, o_mla, o_gla, gg, gta, shf, scf, mod_per_row, w, n_all, row0, h2_buf):
    n = x.shape[0]
    tm = ROW_TILE
    t0 = row0 // tm
    row = lambda i: (i, 0)
    const = lambda i: (0, 0)
    extra_specs = [pl.BlockSpec(memory_space=pl.ANY)] * 2
    extra_args = list(h2_buf)
    if mod_per_row:
        mod_spec = pl.BlockSpec((1, tm, D_MODEL), lambda i: (0, i, 0))
    else:
        tiles_per_seq = n // gta.shape[0] // tm
        mod_spec = pl.BlockSpec((1, 1, D_MODEL), lambda i: (i // tiles_per_seq, 0, 0))
    d_mix = MLA_PAD + GLA_WIDTH
    n_in = 13
    return pl.pallas_call(
        _mix_kernel,
        grid=(n // tm,),
        in_specs=[pl.BlockSpec((tm, D_MODEL), row), pl.BlockSpec((tm, MLA_PAD), row),
                  pl.BlockSpec((tm, GLA_WIDTH), row), pl.BlockSpec((tm, GLA_WIDTH), row),
                  mod_spec, mod_spec, mod_spec,
                  pl.BlockSpec((1, MLA_PAD), const), pl.BlockSpec((1, GLA_DV), const),
                  pl.BlockSpec((d_mix, D_MODEL), const), pl.BlockSpec((1, D_MODEL), const),
                  pl.BlockSpec((2, D_MODEL, LANES), lambda i: (0, 0, 0)),
                  pl.BlockSpec((1, LANES), const)] + extra_specs,
        out_specs=[pl.BlockSpec((tm, D_MODEL), row),
                   pl.BlockSpec((tm, PERM_W), lambda i: (i + t0, 0)),
                   pl.BlockSpec((tm, PERM_W), lambda i: (i + t0, 0)),
                   pl.BlockSpec((tm, LANES), row), pl.BlockSpec((tm, LANES), row),
                   pl.BlockSpec((1, 1, LANES), lambda i: (i, 0, 0))],
        out_shape=[jax.ShapeDtypeStruct((n, D_MODEL), F32),
                   jax.ShapeDtypeStruct((n_all, PERM_W), jnp.uint32),
                   jax.ShapeDtypeStruct((n_all, PERM_W), jnp.uint32),
                   jax.ShapeDtypeStruct((n, LANES), jnp.int32), jax.ShapeDtypeStruct((n, LANES), F32),
                   jax.ShapeDtypeStruct((n // tm, 1, LANES), F32)],
        input_output_aliases={n_in: 1, n_in + 1: 2},
        compiler_params=_cparams(("parallel",)),
        name="mixer_sample" if mod_per_row else "mixer_prompt",
    )(x, o_mla, o_gla, gg, gta, shf, scf, w["g_mla_pad"], w["g_gla_out"], w["w_o_pad"],
      w["g_norm_ffn"], w["w_router_pad"], w["b_router_pad"], *extra_args)


def _rank_kernel(ti_ref, base_ref, dest_ref):
    ti = ti_ref[...]
    tm = ti.shape[0]
    lane = lax.broadcasted_iota(jnp.int32, ti.shape, 1)
    cols = [ti[:, kk:kk + 1] for kk in range(TOP_K)]
    onehot = jnp.zeros(ti.shape, F32)
    for kk in range(TOP_K):
        onehot = onehot + jnp.where(lane == cols[kk], 1.0, 0.0)
    r = lax.broadcasted_iota(jnp.int32, (tm, tm), 0)
    c = lax.broadcasted_iota(jnp.int32, (tm, tm), 1)
    earlier = jnp.where(c < r, 1.0, 0.0).astype(BF16)
    pos = jnp.dot(earlier, onehot.astype(BF16), preferred_element_type=F32) + base_ref[0]
    out = jnp.zeros(ti.shape, F32)
    for kk in range(TOP_K):
        dk = jnp.sum(jnp.where(lane == cols[kk], pos, 0.0), axis=-1, keepdims=True)
        out = jnp.where(lane == kk, dk, out)
    dest_ref[...] = out.astype(jnp.int32)


def route_rank(ti, base):
    n = ti.shape[0]
    tm = ROW_TILE
    return pl.pallas_call(
        _rank_kernel,
        grid=(n // tm,),
        in_specs=[pl.BlockSpec((tm, LANES), lambda i: (i, 0)),
                  pl.BlockSpec((1, 1, LANES), lambda i: (i, 0, 0))],
        out_specs=pl.BlockSpec((tm, LANES), lambda i: (i, 0)),
        out_shape=jax.ShapeDtypeStruct((n, LANES), jnp.int32),
        compiler_params=_cparams(("parallel",)),
        name="route_rank",
    )(ti, base)


def _route_tables(hist, n_tok):
    tm = MOE_TILE
    h = hist[:, 0, :].astype(jnp.int32)
    counts = jnp.sum(h, axis=0)
    padded = (counts + tm - 1) // tm * tm
    pad_ends = jnp.cumsum(padded)
    pad_starts = pad_ends - padded
    base = (pad_starts[None, :] + jnp.cumsum(h, axis=0) - h).astype(F32)[:, None, :]
    n_blocks = n_tok * TOP_K // tm + N_EXPERTS
    n_active = (pad_ends[N_EXPERTS - 1] // tm).astype(jnp.int32)
    blk = jnp.minimum(jnp.arange(n_blocks, dtype=jnp.int32), n_active - 1)
    ends = pad_ends[:N_EXPERTS]
    block_e = jnp.minimum(jnp.sum((ends[None, :] <= (blk * tm)[:, None]).astype(jnp.int32), axis=1),
                          N_EXPERTS - 1).astype(jnp.int32)
    return base, block_e, n_active.reshape(1), n_blocks


def _sc_mesh():
    return plsc.VectorSubcoreMesh(core_axis_name="c", subcore_axis_name="s")


def sc_dispatch(x_rows, idx, n_out):
    n, wd = x_rows.shape
    win = SC_WIN
    nwin = n // win

    @functools.partial(pl.kernel, out_type=jax.ShapeDtypeStruct((n_out, wd), x_rows.dtype),
                       mesh=_sc_mesh(), scratch_types=[])
    def k(x_hbm, i_hbm, o_hbm):
        def body(x_vmem, i_vmem):
            pltpu.sync_copy(x_vmem, o_hbm.at[i_vmem.at[0]])

        pltpu.emit_pipeline(
            body,
            grid=(idx.shape[1] // win,),
            in_specs=[pl.BlockSpec((win, wd), lambda i: (i % nwin, 0)),
                      pl.BlockSpec((1, win), lambda i: (0, i))],
            out_specs=[],
            core_axis_name=("c", "s"),
            dimension_semantics=(pltpu.PARALLEL,),
        )(x_hbm, i_hbm)

    return k(x_rows, idx)


def sc_combine(y_rows, idx):
    wd = y_rows.shape[1]
    m = idx.shape[1]
    win = SC_WIN

    @functools.partial(pl.kernel, out_type=jax.ShapeDtypeStruct((m, wd), y_rows.dtype),
                       mesh=_sc_mesh(), scratch_types=[])
    def k(y_hbm, i_hbm, o_hbm):
        def body(i_vmem, o_vmem):
            pltpu.sync_copy(y_hbm.at[i_vmem.at[0]], o_vmem)

        pltpu.emit_pipeline(
            body,
            grid=(m // win,),
            in_specs=[pl.BlockSpec((1, win), lambda i: (0, i))],
            out_specs=[pl.BlockSpec((win, wd), lambda i: (i, 0))],
            core_axis_name=("c", "s"),
            dimension_semantics=(pltpu.PARALLEL,),
        )(i_hbm, o_hbm)

    return k(y_rows, idx)


def _moe_kernel(be_ref, na_ref, xa_ref, xb_ref, wup_ref, bup_ref, wdn_ref, bdn_ref, ya_ref, yb_ref,
                wup_sc, wdn_sc):
    i = pl.program_id(0)
    active = i < na_ref[0]
    prev = be_ref[jnp.maximum(i - 1, 0)]
    fresh = (i == 0) | (be_ref[i] != prev)

    @pl.when(active & fresh)
    def _():
        wup_sc[...] = wup_ref[0].astype(BF16)
        wdn_sc[...] = wdn_ref[0].astype(BF16)

    @pl.when(active)
    def _():
        xb = _unpack_rows(jnp.concatenate([xa_ref[...], xb_ref[...]], axis=1)).astype(BF16)
        hu = jnp.dot(xb, wup_sc[...], preferred_element_type=F32) + bup_ref[0]
        gate = jnp.minimum(hu[:, :D_FF], SWIGLU_LIMIT)
        lin = jnp.clip(hu[:, D_FF:], -SWIGLU_LIMIT, SWIGLU_LIMIT)
        act = gate * jax.nn.sigmoid(SWIGLU_ALPHA * gate) * (lin + 1.0)
        y = jnp.dot(act.astype(BF16), wdn_sc[...], preferred_element_type=F32) + bdn_ref[0]
        words = _pack_rows(y)
        ya_ref[...] = words[:, :PERM_W]
        yb_ref[...] = words[:, PERM_W:]

    @pl.when(jnp.logical_not(active))
    def _():
        ya_ref[...] = jnp.zeros(ya_ref.shape, jnp.uint32)
        yb_ref[...] = jnp.zeros(yb_ref.shape, jnp.uint32)


def moe_experts(xs_a, xs_b, block_e, n_active, w_up, b_up, w_down, b_down):
    n_rows = xs_a.shape[0]
    tm = MOE_TILE
    n_blocks = n_rows // tm
    emap3 = lambda i, be, na: (be[i], 0, 0)
    grid_spec = pltpu.PrefetchScalarGridSpec(
        num_scalar_prefetch=2,
        grid=(n_blocks,),
        in_specs=[pl.BlockSpec((tm, PERM_W), lambda i, be, na: (i, 0)),
                  pl.BlockSpec((tm, PERM_W), lambda i, be, na: (i, 0)),
                  pl.BlockSpec((1, D_MODEL, 2 * D_FF), emap3),
                  pl.BlockSpec((1, 1, 2 * D_FF), emap3),
                  pl.BlockSpec((1, D_FF, D_MODEL), emap3),
                  pl.BlockSpec((1, 1, D_MODEL), emap3)],
        out_specs=[pl.BlockSpec((tm, PERM_W), lambda i, be, na: (i, 0)),
                   pl.BlockSpec((tm, PERM_W), lambda i, be, na: (i, 0))],
        scratch_shapes=[pltpu.VMEM((D_MODEL, 2 * D_FF), BF16), pltpu.VMEM((D_FF, D_MODEL), BF16)],
    )
    return pl.pallas_call(
        _moe_kernel,
        grid_spec=grid_spec,
        out_shape=[jax.ShapeDtypeStruct((n_rows, PERM_W), jnp.uint32)] * 2,
        compiler_params=_cparams(("arbitrary",)),
        name="moe_experts",
    )(block_e, n_active, xs_a, xs_b, w_up, b_up.reshape(N_EXPERTS, 1, 2 * D_FF), w_down,
      b_down.reshape(N_EXPERTS, 1, D_MODEL))


def _final_kernel(x1_ref, yga_ref, ygb_ref, tg_ref, gtf_ref, sh_ref, sc_ref, g_ref, y_ref):
    tg = tg_ref[...]
    moe = jnp.zeros(x1_ref.shape, F32)
    for kk in range(TOP_K):
        moe = moe + tg[:, kk:kk + 1] * _unpack_rows(jnp.concatenate([yga_ref[kk], ygb_ref[kk]], axis=1))
    x2 = x1_ref[...] + gtf_ref[0] * moe
    y_ref[...] = _rms(x2, g_ref[...]) * (1.0 + sc_ref[0]) + sh_ref[0]


def final_out(x1, yg_a, yg_b, tg, row0, gtf, sh, sc, mod_per_row, g_final):
    n = x1.shape[0]
    tm = ROW_TILE
    t0 = row0 // tm
    row = lambda i: (i, 0)
    if mod_per_row:
        mod_spec = pl.BlockSpec((1, tm, D_MODEL), lambda i: (0, i, 0))
    else:
        tiles_per_seq = n // gtf.shape[0] // tm
        mod_spec = pl.BlockSpec((1, 1, D_MODEL), lambda i: (i // tiles_per_seq, 0, 0))
    return pl.pallas_call(
        _final_kernel,
        grid=(n // tm,),
        in_specs=[pl.BlockSpec((tm, D_MODEL), row),
                  pl.BlockSpec((TOP_K, tm, PERM_W), lambda i: (0, i + t0, 0)),
                  pl.BlockSpec((TOP_K, tm, PERM_W), lambda i: (0, i + t0, 0)),
                  pl.BlockSpec((tm, LANES), lambda i: (i + t0, 0)),
                  mod_spec, mod_spec, mod_spec,
                  pl.BlockSpec((1, D_MODEL), lambda i: (0, 0))],
        out_specs=pl.BlockSpec((tm, D_MODEL), row),
        out_shape=jax.ShapeDtypeStruct((n, D_MODEL), F32),
        compiler_params=_cparams(("parallel",)),
        name="final_sample" if mod_per_row else "final_prompt",
    )(x1, yg_a, yg_b, tg, gtf, sh, sc, g_final)


def _prep_weights(w_in, g_q_a, w_q_b, g_kv_a, w_kv_b, w_gk_b, b_gk, g_mla_out, g_gla_out, w_o,
                  g_norm_mix, g_norm_ffn, w_router, b_router):
    sizes = (MLA_Q_LORA, MLA_KV_LORA, MLA_ROPE, GLA_QK, GLA_QK, GLA_WIDTH, GLA_GATE_RANK, GLA_WIDTH)
    offs = np.cumsum((0,) + sizes)
    part = lambda i: w_in[:, offs[i]:offs[i + 1]]
    half = MLA_ROPE // 2
    k_rope = part(2)
    misc = jnp.concatenate([k_rope, part(6), jnp.zeros((D_MODEL, LANES - MLA_ROPE - GLA_GATE_RANK), F32)], 1)
    swap = jnp.concatenate([-k_rope[:, half:], k_rope[:, :half],
                            jnp.zeros((D_MODEL, LANES - MLA_ROPE), F32)], 1)
    w_in_pad = jnp.concatenate([part(0), part(1), part(3), part(4), part(5), part(7), misc, swap], 1)

    pad_q = jnp.zeros((MLA_Q_LORA, MLA_HEADS, HEAD_PAD - MLA_NOPE - MLA_ROPE), F32)
    wq1 = jnp.concatenate([w_q_b, pad_q], axis=2)
    q_lo = w_q_b[:, :, MLA_NOPE:MLA_NOPE + half]
    q_hi = w_q_b[:, :, MLA_NOPE + half:]
    wq2 = jnp.concatenate([jnp.zeros((MLA_Q_LORA, MLA_HEADS, MLA_NOPE), F32), -q_hi, q_lo, pad_q], axis=2)
    pad_kv = jnp.zeros((MLA_KV_LORA, MLA_HEADS, HEAD_PAD - MLA_NOPE), F32)
    w_uk = w_kv_b[:, :, :MLA_NOPE]
    w_uv = w_kv_b[:, :, MLA_NOPE:]
    wk = jnp.concatenate([w_uk, pad_kv], axis=2)
    wv = jnp.concatenate([w_uv, pad_kv], axis=2)
    wuk_t = jnp.concatenate([jnp.transpose(w_uk, (1, 2, 0)),
                             jnp.zeros((MLA_HEADS, HEAD_PAD - MLA_NOPE, MLA_KV_LORA), F32)], axis=1)
    wuv_h = jnp.transpose(wv, (1, 0, 2))
    wgk = jnp.zeros((LANES, GLA_QK), F32).at[MLA_ROPE:MLA_ROPE + GLA_GATE_RANK].set(w_gk_b)

    wo_mla = w_o[:MLA_HEADS * MLA_V].reshape(MLA_HEADS, MLA_V, D_MODEL)
    wo_mla = jnp.concatenate([wo_mla, jnp.zeros((MLA_HEADS, HEAD_PAD - MLA_V, D_MODEL), F32)], axis=1)
    w_o_pad = jnp.concatenate([wo_mla.reshape(MLA_PAD, D_MODEL), w_o[MLA_HEADS * MLA_V:]], axis=0)
    g_mla = g_mla_out.reshape(MLA_HEADS, MLA_V)
    g_mla_pad = jnp.concatenate([g_mla, jnp.zeros((MLA_HEADS, HEAD_PAD - MLA_V), F32)], 1).reshape(1, MLA_PAD)
    w_router_pad = jnp.concatenate([w_router, jnp.zeros((D_MODEL, LANES - N_EXPERTS), F32)], axis=1)
    wr_hi = w_router_pad.astype(BF16)
    w_router_pad = jnp.stack([wr_hi, (w_router_pad - wr_hi.astype(F32)).astype(BF16)])
    b_router_pad = jnp.concatenate([b_router, jnp.full((LANES - N_EXPERTS,), -jnp.inf, F32)]).reshape(1, LANES)
    return dict(
        w_in=w_in_pad.astype(BF16), g_norm_mix=g_norm_mix.reshape(1, D_MODEL),
        g_q_a=g_q_a.reshape(1, MLA_Q_LORA),
        wq1=wq1.reshape(MLA_Q_LORA, MLA_PAD).astype(BF16), wq2=wq2.reshape(MLA_Q_LORA, MLA_PAD).astype(BF16),
        g_kv_a=g_kv_a.reshape(1, MLA_KV_LORA),
        wk=wk.reshape(MLA_KV_LORA, MLA_PAD).astype(BF16), wv=wv.reshape(MLA_KV_LORA, MLA_PAD).astype(BF16),
        wuk_t=wuk_t.astype(BF16), wuv_h=wuv_h.astype(BF16),
        wgk=wgk.astype(BF16), b_gk=b_gk.reshape(1, GLA_QK),
        g_mla_pad=g_mla_pad, g_gla_out=g_gla_out.reshape(1, GLA_DV), w_o_pad=w_o_pad.astype(BF16),
        g_norm_ffn=g_norm_ffn.reshape(1, D_MODEL), w_router_pad=w_router_pad, b_router_pad=b_router_pad,
    )


def _rope_tables(pos, reps):
    half = MLA_ROPE // 2
    inv = ROPE_THETA ** (-jnp.arange(half, dtype=F32) / half)
    ang = pos.astype(F32)[:, None] * inv
    cos, sin = jnp.cos(ang), jnp.sin(ang)
    n = pos.shape[0]
    qc = jnp.concatenate([jnp.full((n, MLA_NOPE), Q_SCALE, F32), Q_SCALE * cos, Q_SCALE * cos,
                          jnp.zeros((n, HEAD_PAD - MLA_NOPE - MLA_ROPE), F32)], axis=1)
    qs = jnp.concatenate([jnp.zeros((n, MLA_NOPE), F32), Q_SCALE * sin, Q_SCALE * sin,
                          jnp.zeros((n, HEAD_PAD - MLA_NOPE - MLA_ROPE), F32)], axis=1)
    kc = jnp.concatenate([cos, cos, jnp.zeros((n, LANES - MLA_ROPE), F32)], axis=1)
    ks = jnp.concatenate([sin, sin, jnp.zeros((n, LANES - MLA_ROPE), F32)], axis=1)
    return tuple(jnp.tile(t, (reps, 1)) for t in (qc, qs, kc, ks))


def kernel(x_prompt, x_sample, cache_ckv, cache_kpe, state_gla, page_table, c_prompt, c_sample, w_ada, b_ada, g_norm_mix, w_in, g_q_a, w_q_b, g_kv_a, w_kv_b, w_gk_b, b_gk, g_mla_out, g_gla_out, w_o, g_norm_ffn, w_router, b_router, w_up, b_up, w_down, b_down, g_norm_final, w_ada_final, b_ada_final):
    B, S, D = x_prompt.shape
    DB, T, _ = x_sample.shape
    depth = w_ada.shape[0]
    assert depth == 1
    past_len = page_table.shape[1] * cache_ckv.shape[2]
    n_p, n_s = B * S, DB * T
    l = 0

    w = _prep_weights(w_in[l], g_q_a[l], w_q_b[l], g_kv_a[l], w_kv_b[l], w_gk_b[l], b_gk[l],
                      g_mla_out[l], g_gla_out[l], w_o[l], g_norm_mix[l], g_norm_ffn[l],
                      w_router[l], b_router[l])

    n_c = B + DB
    n_c_pad = (n_c + 7) // 8 * 8
    c_all = jnp.concatenate([c_prompt, c_sample, jnp.zeros((n_c_pad - n_c, D), F32)], axis=0)
    mod = ada_terms(c_all, w_ada[l], b_ada[l])
    mod_f = ada_terms(c_all, w_ada_final, b_ada_final)
    term = lambda m, i: m[:, i * D:(i + 1) * D]
    p_term = lambda m, i: term(m, i)[:B].reshape(B, 1, D)
    s_term = lambda m, i: jnp.broadcast_to(term(m, i)[B:n_c][:, None, :], (DB, T, D)).reshape(1, n_s, D)

    xp = x_prompt.reshape(n_p, D)
    xs = x_sample.reshape(n_s, D)
    tabs_p = _rope_tables(jnp.arange(S), 1)
    tabs_s = _rope_tables(past_len + jnp.arange(T), ROW_TILE // T)

    (q_p, k_p, v_p, ckv_p, kpe_p, gq_p, gk_p, gv_p, la_p, gg_p) = in_proj(
        xp, p_term(mod, 0), p_term(mod, 1), False, tabs_p, w, True)
    o_mla_p = mla_prefill(q_p, k_p, v_p, B, S)
    s0 = jnp.zeros((B, GLA_HEADS, GLA_DK, GLA_DV), F32)
    o_gla_p, gla_p = gla(gq_p, gk_p, la_p, gv_p, s0, B, S)
    n_all = n_p + n_s
    h2_init = (jnp.zeros((n_all, PERM_W), jnp.uint32), jnp.zeros((n_all, PERM_W), jnp.uint32))
    x1_p, h2a, h2b, ti_p, tg_p, hist_p = mixer_out(xp, o_mla_p, o_gla_p.reshape(n_p, GLA_WIDTH), gg_p,
                                                   p_term(mod, 2), p_term(mod, 3), p_term(mod, 4), False, w,
                                                   n_all, 0, h2_init)

    (q_s, ckv_s, kpe_s, gq_s, gk_s, gv_s, la_s, gg_s) = in_proj(
        xs, s_term(mod, 0), s_term(mod, 1), True, tabs_s, w, False)
    qlat, qpe = absorb_q(q_s, w["wuk_t"])
    o_lat = mla_decode(qlat, qpe, ckv_s, kpe_s, cache_ckv, cache_kpe, page_table, DB, T)
    o_mla_s = latent_to_values(o_lat, w["wuv_h"], DB, T)
    tpad = GLA_CHUNK
    padt = lambda a: jnp.pad(a.reshape(DB, T, a.shape[-1]), ((0, 0), (0, tpad - T), (0, 0))).reshape(
        DB * tpad, a.shape[-1])
    o_gla_s, gla_s = gla(padt(gq_s), padt(gk_s), padt(la_s), padt(gv_s), state_gla[l], DB, tpad)
    o_gla_s = o_gla_s.reshape(DB, tpad, GLA_WIDTH)[:, :T].reshape(n_s, GLA_WIDTH)
    x1_s, h2a, h2b, ti_s, tg_s, hist_s = mixer_out(xs, o_mla_s, o_gla_s, gg_s,
                                                   s_term(mod, 2), s_term(mod, 3), s_term(mod, 4), True, w,
                                                   n_all, n_p, (h2a, h2b))

    ti = jnp.concatenate([ti_p, ti_s], axis=0)
    tg = jnp.concatenate([tg_p, tg_s], axis=0)
    base, block_e, n_active, n_blocks = _route_tables(jnp.concatenate([hist_p, hist_s], axis=0), n_all)
    dest = route_rank(ti, base)
    idx = dest[:, :TOP_K].T.reshape(1, TOP_K * n_all)
    n_rows = n_blocks * MOE_TILE
    xs_a = sc_dispatch(h2a, idx, n_rows)
    xs_b = sc_dispatch(h2b, idx, n_rows)
    ys_a, ys_b = moe_experts(xs_a, xs_b, block_e, n_active, w_up[l], b_up[l], w_down[l], b_down[l])
    yg_a = sc_combine(ys_a, idx).reshape(TOP_K, n_all, PERM_W)
    yg_b = sc_combine(ys_b, idx).reshape(TOP_K, n_all, PERM_W)

    g_fin = g_norm_final.reshape(1, D)
    y_p = final_out(x1_p, yg_a, yg_b, tg, 0, p_term(mod, 5), p_term(mod_f, 0), p_term(mod_f, 1), False, g_fin)
    y_s = final_out(x1_s, yg_a, yg_b, tg, n_p, s_term(mod, 5), s_term(mod_f, 0), s_term(mod_f, 1), True, g_fin)

    return (y_p.reshape(B, S, D), y_s.reshape(DB, T, D),
            ckv_p.reshape(1, B, S, MLA_KV_LORA), kpe_p.reshape(1, B, S, MLA_ROPE), gla_p[None],
            ckv_s.reshape(1, DB, T, MLA_KV_LORA), kpe_s.reshape(1, DB, T, MLA_ROPE), gla_s[None])
```

```python
import functools
import math

import jax
import jax.numpy as jnp
import numpy as np
from jax import lax
from jax.experimental import pallas as pl
from jax.experimental.pallas import tpu as pltpu
from jax.experimental.pallas import tpu_sc as plsc

F32 = jnp.float32
BF16 = jnp.bfloat16

D_MODEL = 1024
MLA_HEADS = 8
MLA_NOPE = 64
MLA_ROPE = 32
MLA_V = 64
MLA_Q_LORA = 384
MLA_KV_LORA = 256
MLA_SCALE = (MLA_NOPE + MLA_ROPE) ** -0.5
ROPE_THETA = 10000.0
GLA_HEADS = 4
GLA_DK = 64
GLA_DV = 128
GLA_GATE_RANK = 16
GLA_GATE_NORM = 16.0
GLA_CHUNK = 16
GLA_QK = GLA_HEADS * GLA_DK
GLA_WIDTH = GLA_HEADS * GLA_DV
N_EXPERTS = 32
TOP_K = 4
D_FF = D_MODEL
SWIGLU_LIMIT = 7.0
SWIGLU_ALPHA = 1.702
N_MOD = 6
EPS = 1e-6
PAGE_SIZE = 128

LANES = 128
HEAD_PAD = LANES
MLA_PAD = MLA_HEADS * HEAD_PAD
VMEM_LIMIT = 56 * 1024 * 1024

Q_SCALE = MLA_SCALE * math.log2(math.e)

C_QA = 0
C_KV = C_QA + MLA_Q_LORA
C_GQ = C_KV + MLA_KV_LORA
C_GK = C_GQ + GLA_QK
C_GV = C_GK + GLA_QK
C_GG = C_GV + GLA_WIDTH
C_MISC = C_GG + GLA_WIDTH
C_SWAP = C_MISC + LANES
D_IN_PAD = C_SWAP + LANES

ROW_TILE = 256
RANK_TILE = 1024
ATT_TILE = 512
ATT_HEADS = 4
GLA_BLOCK = 256
DEC_SPANS = 4
SC_WIN = 128
PERM_W = D_MODEL // 4
MOE_TILE = 384
NEW_PAD = 16


def _cparams(sem):
    return pltpu.CompilerParams(dimension_semantics=sem, vmem_limit_bytes=VMEM_LIMIT)


def _rms(x, g):
    return x * lax.rsqrt(jnp.mean(x * x, axis=-1, keepdims=True) + EPS) * g


def _pack_rows(x):
    bits = lax.bitcast_convert_type(x.astype(BF16).astype(F32), jnp.uint32)
    w = x.shape[1] // 2
    return (bits[:, :w] >> 16) | bits[:, w:]


def _unpack_rows(words):
    lo = lax.bitcast_convert_type(words << 16, F32)
    hi = lax.bitcast_convert_type(words & jnp.uint32(0xFFFF0000), F32)
    return jnp.concatenate([lo, hi], axis=1)


def _ada_kernel(c_ref, w_ref, b_ref, o_ref):
    c = c_ref[...]
    a = (c * jax.nn.sigmoid(c)).astype(BF16)
    o_ref[...] = jnp.dot(a, w_ref[...].astype(BF16), preferred_element_type=F32) + b_ref[...]


def ada_terms(c, w, b):
    rows, d = c.shape
    n = w.shape[1]
    tn = 512
    return pl.pallas_call(
        _ada_kernel,
        grid=(n // tn,),
        in_specs=[pl.BlockSpec((rows, d), lambda j: (0, 0)),
                  pl.BlockSpec((d, tn), lambda j: (0, j)),
                  pl.BlockSpec((1, tn), lambda j: (0, j))],
        out_specs=pl.BlockSpec((rows, tn), lambda j: (0, j)),
        out_shape=jax.ShapeDtypeStruct((rows, n), F32),
        compiler_params=_cparams(("arbitrary",)),
        name="ada_terms",
    )(c, w, b.reshape(1, n))


def _in_kernel(with_kv, x_ref, sh_ref, sc_ref, g_ref, win_ref, gqa_ref, wq1_ref, wq2_ref,
               gkv_ref, wk_ref, wv_ref, wgk_ref, bgk_ref, qc_ref, qs_ref, kc_ref, ks_ref, *outs):
    if with_kv:
        q_out, k_out, v_out, ckv_out, kpe_out, gq_out, gk_out, gv_out, la_out, gg_out = outs
    else:
        q_out, ckv_out, kpe_out, gq_out, gk_out, gv_out, la_out, gg_out = outs
    x = x_ref[...]
    h = _rms(x, g_ref[...])
    h = h * (1.0 + sc_ref[0]) + sh_ref[0]
    proj = jnp.dot(h.astype(BF16), win_ref[...], preferred_element_type=F32)

    qn = _rms(proj[:, C_QA:C_KV], gqa_ref[...]).astype(BF16)
    qa = jnp.dot(qn, wq1_ref[...], preferred_element_type=F32)
    qb = jnp.dot(qn, wq2_ref[...], preferred_element_type=F32)
    qc = qc_ref[...]
    qs = qs_ref[...]
    for hh in range(MLA_HEADS):
        sl = slice(hh * HEAD_PAD, (hh + 1) * HEAD_PAD)
        q_out[:, sl] = (qa[:, sl] * qc + qb[:, sl] * qs).astype(BF16)

    ckv = _rms(proj[:, C_KV:C_GQ], gkv_ref[...])
    ckv_out[...] = ckv
    kpe = proj[:, C_MISC:C_SWAP] * kc_ref[...] + proj[:, C_SWAP:D_IN_PAD] * ks_ref[...]
    kpe_out[...] = kpe[:, :MLA_ROPE]
    if with_kv:
        ckv_b = ckv.astype(BF16)
        kn = jnp.dot(ckv_b, wk_ref[...], preferred_element_type=F32)
        kpe_sh = pltpu.roll(kpe, MLA_NOPE, axis=1)
        for hh in range(MLA_HEADS):
            sl = slice(hh * HEAD_PAD, (hh + 1) * HEAD_PAD)
            k_out[:, sl] = (kn[:, sl] + kpe_sh).astype(BF16)
        v_out[...] = jnp.dot(ckv_b, wv_ref[...], preferred_element_type=F32).astype(BF16)

    gq_out[...] = proj[:, C_GQ:C_GK] * (GLA_DK ** -0.5)
    gk_out[...] = proj[:, C_GK:C_GV]
    gv_out[...] = proj[:, C_GV:C_GG].astype(BF16)
    gg_out[...] = proj[:, C_GG:C_MISC].astype(BF16)
    xg = jnp.dot(proj[:, C_MISC:C_SWAP].astype(BF16), wgk_ref[...],
                 preferred_element_type=F32) + bgk_ref[...]
    la_out[...] = (jnp.minimum(xg, 0.0) - jnp.log(1.0 + jnp.exp(-jnp.abs(xg)))) * (1.0 / GLA_GATE_NORM)


def in_proj(x, sh, sc, mod_per_row, tabs, wts, with_kv):
    n = x.shape[0]
    tm = ROW_TILE
    nt = n // tm
    qc, qs, kc, ks = tabs
    n_tab = qc.shape[0] // tm
    row = lambda i: (i, 0)
    const = lambda i: (0, 0)
    if mod_per_row:
        mod_spec = pl.BlockSpec((1, tm, D_MODEL), lambda i: (0, i, 0))
    else:
        tiles_per_seq = n // sh.shape[0] // tm
        mod_spec = pl.BlockSpec((1, 1, D_MODEL), lambda i: (i // tiles_per_seq, 0, 0))
    tab_spec = pl.BlockSpec((tm, LANES), lambda i: (i % n_tab, 0))
    w = wts
    in_specs = [
        pl.BlockSpec((tm, D_MODEL), row), mod_spec, mod_spec,
        pl.BlockSpec((1, D_MODEL), const),
        pl.BlockSpec((D_MODEL, D_IN_PAD), const),
        pl.BlockSpec((1, MLA_Q_LORA), const),
        pl.BlockSpec((MLA_Q_LORA, MLA_PAD), const),
        pl.BlockSpec((MLA_Q_LORA, MLA_PAD), const),
        pl.BlockSpec((1, MLA_KV_LORA), const),
        pl.BlockSpec((MLA_KV_LORA, MLA_PAD), const),
        pl.BlockSpec((MLA_KV_LORA, MLA_PAD), const),
        pl.BlockSpec((LANES, GLA_QK), const),
        pl.BlockSpec((1, GLA_QK), const),
        tab_spec, tab_spec, tab_spec, tab_spec,
    ]
    wide = lambda dt: (jax.ShapeDtypeStruct((n, MLA_PAD), dt), pl.BlockSpec((tm, MLA_PAD), row))
    outs = [wide(BF16)]
    if with_kv:
        outs += [wide(BF16), wide(BF16)]
    outs += [
        (jax.ShapeDtypeStruct((n, MLA_KV_LORA), F32), pl.BlockSpec((tm, MLA_KV_LORA), row)),
        (jax.ShapeDtypeStruct((n, MLA_ROPE), F32), pl.BlockSpec((tm, MLA_ROPE), row)),
        (jax.ShapeDtypeStruct((n, GLA_QK), F32), pl.BlockSpec((tm, GLA_QK), row)),
        (jax.ShapeDtypeStruct((n, GLA_QK), F32), pl.BlockSpec((tm, GLA_QK), row)),
        (jax.ShapeDtypeStruct((n, GLA_WIDTH), BF16), pl.BlockSpec((tm, GLA_WIDTH), row)),
        (jax.ShapeDtypeStruct((n, GLA_QK), F32), pl.BlockSpec((tm, GLA_QK), row)),
        (jax.ShapeDtypeStruct((n, GLA_WIDTH), BF16), pl.BlockSpec((tm, GLA_WIDTH), row)),
    ]
    return pl.pallas_call(
        functools.partial(_in_kernel, with_kv),
        grid=(nt,),
        in_specs=in_specs,
        out_specs=[o[1] for o in outs],
        out_shape=[o[0] for o in outs],
        compiler_params=_cparams(("parallel",)),
        name="inproj_prompt" if with_kv else "inproj_sample",
    )(x, sh, sc, w["g_norm_mix"], w["w_in"], w["g_q_a"], w["wq1"], w["wq2"], w["g_kv_a"],
      w["wk"], w["wv"], w["wgk"], w["b_gk"], qc, qs, kc, ks)


def _prefill_kernel(q_ref, k_ref, v_ref, o_ref):
    qi = pl.program_id(2)
    t = ATT_TILE
    lanes = [slice(g * HEAD_PAD, (g + 1) * HEAD_PAD) for g in range(ATT_HEADS)]
    qs = [q_ref[0, :, sl] for sl in lanes]

    def update_all(carries, j, mask):
        r0 = pl.multiple_of(j * t, t)
        scores = [lax.dot_general(k_ref[0, pl.ds(r0, t), sl], q, (((1,), (1,)), ((), ())),
                                  preferred_element_type=F32) for q, sl in zip(qs, lanes)]
        stats = []
        for (m, l, acc), s in zip(carries, scores):
            if mask is not None:
                s = jnp.where(mask, s, -jnp.inf)
            m_new = jnp.maximum(m, jnp.max(s, axis=0, keepdims=True))
            p = jnp.exp2(s - m_new)
            alpha = jnp.exp2(m - m_new)
            stats.append((m_new, alpha * l + jnp.sum(p, axis=0, keepdims=True), alpha * acc, p.astype(BF16)))
        out = []
        for (m_new, l_new, acc_scaled, p), sl in zip(stats, lanes):
            pv = lax.dot_general(v_ref[0, pl.ds(r0, t), sl], p, (((0,), (0,)), ((), ())),
                                 preferred_element_type=F32)
            out.append((m_new, l_new, acc_scaled + pv))
        return tuple(out)

    def body(j, carries):
        return update_all(carries, j, None)

    init = (jnp.full((1, t), -jnp.inf, F32), jnp.zeros((1, t), F32), jnp.zeros((HEAD_PAD, t), F32))
    carries = lax.fori_loop(0, qi, body, (init,) * ATT_HEADS)
    key_pos = lax.broadcasted_iota(jnp.int32, (t, t), 0)
    qry_pos = lax.broadcasted_iota(jnp.int32, (t, t), 1)
    carries = update_all(carries, qi, key_pos <= qry_pos)
    for (m, l, acc), sl in zip(carries, lanes):
        o_ref[0, :, sl] = (acc / l).T.astype(BF16)


def mla_prefill(q, k, v, batch, seq):
    t = ATT_TILE
    q3 = q.reshape(batch, seq, MLA_PAD)
    k3 = k.reshape(batch, seq, MLA_PAD)
    v3 = v.reshape(batch, seq, MLA_PAD)
    o = pl.pallas_call(
        _prefill_kernel,
        grid=(batch, MLA_HEADS // ATT_HEADS, seq // t),
        in_specs=[pl.BlockSpec((1, t, ATT_HEADS * HEAD_PAD), lambda b, h, i: (b, i, h)),
                  pl.BlockSpec((1, seq, ATT_HEADS * HEAD_PAD), lambda b, h, i: (b, 0, h)),
                  pl.BlockSpec((1, seq, ATT_HEADS * HEAD_PAD), lambda b, h, i: (b, 0, h))],
        out_specs=pl.BlockSpec((1, t, ATT_HEADS * HEAD_PAD), lambda b, h, i: (b, i, h)),
        out_shape=jax.ShapeDtypeStruct((batch, seq, MLA_PAD), BF16),
        compiler_params=_cparams(("parallel", "parallel", "arbitrary")),
        name="mla_prefill",
    )(q3, k3, v3)
    return o.reshape(batch * seq, MLA_PAD)


def _absorb_kernel(q_ref, wuk_ref, qlat_ref, qpe_ref):
    q = q_ref[...]
    qlat_ref[0] = jnp.dot(q, wuk_ref[0], preferred_element_type=F32)
    qf = pltpu.roll(q.astype(F32), HEAD_PAD - MLA_NOPE, axis=1)
    lane = lax.broadcasted_iota(jnp.int32, qf.shape, 1)
    qpe_ref[0] = jnp.where(lane < MLA_ROPE, qf, 0.0)


def absorb_q(q, wuk_t):
    r = q.shape[0]
    return pl.pallas_call(
        _absorb_kernel,
        grid=(MLA_HEADS,),
        in_specs=[pl.BlockSpec((r, HEAD_PAD), lambda h: (0, h)),
                  pl.BlockSpec((1, HEAD_PAD, MLA_KV_LORA), lambda h: (h, 0, 0))],
        out_specs=[pl.BlockSpec((1, r, MLA_KV_LORA), lambda h: (h, 0, 0)),
                   pl.BlockSpec((1, r, HEAD_PAD), lambda h: (h, 0, 0))],
        out_shape=[jax.ShapeDtypeStruct((MLA_HEADS, r, MLA_KV_LORA), F32),
                   jax.ShapeDtypeStruct((MLA_HEADS, r, HEAD_PAD), F32)],
        compiler_params=_cparams(("parallel",)),
        name="absorb_q",
    )(q, wuk_t)


def _decode_kernel(t_new, pt_ref, qlat_ref, qpe_ref, cnew_ref, knew_ref, ckv_hbm, kpe_hbm, o_ref,
                   cbuf, pbuf, sem):
    b = pl.program_id(0)
    nb = pl.num_programs(0)
    n_pages = cbuf.shape[1]
    slot = lax.rem(b, 2)
    rows = MLA_HEADS * t_new

    def page_copies(bb, sl, p):
        pg = pt_ref[bb, p]
        return (pltpu.make_async_copy(ckv_hbm.at[0, pg], cbuf.at[sl, p], sem.at[0, sl]),
                pltpu.make_async_copy(kpe_hbm.at[0, pg], pbuf.at[sl, p], sem.at[1, sl]))

    def fetch(bb, sl):
        def body(p, c):
            for cp in page_copies(bb, sl, p):
                cp.start()
            return c
        lax.fori_loop(0, n_pages, body, 0)

    @pl.when(b == 0)
    def _():
        fetch(0, 0)

    @pl.when(b + 1 < nb)
    def _():
        fetch(b + 1, 1 - slot)

    pltpu.make_async_copy(ckv_hbm.at[0, pl.ds(0, n_pages)], cbuf.at[slot], sem.at[0, slot]).wait()
    pltpu.make_async_copy(kpe_hbm.at[0, pl.ds(0, n_pages)], pbuf.at[slot], sem.at[1, slot]).wait()

    qlat = qlat_ref[...].reshape(rows, MLA_KV_LORA).astype(BF16)
    qpe = qpe_ref[...].reshape(rows, HEAD_PAD)[:, :MLA_ROPE].astype(BF16)
    dn = (((1,), (1,)), ((), ()))

    span_pages = n_pages // DEC_SPANS
    values, scores = [], []
    for c in range(DEC_SPANS):
        pages = range(c * span_pages, (c + 1) * span_pages)
        kb = jnp.concatenate([cbuf[slot, p].astype(BF16) for p in pages], axis=0)
        pb = jnp.concatenate([pbuf[slot, p].astype(BF16) for p in pages], axis=1)
        values.append(kb)
        scores.append(lax.dot_general(qlat, kb, dn, preferred_element_type=F32)
                      + jnp.dot(qpe, pb, preferred_element_type=F32))
    cn = cnew_ref[0].astype(BF16)
    kn = knew_ref[0].astype(BF16)
    sn = (lax.dot_general(qlat, cn, dn, preferred_element_type=F32)
          + lax.dot_general(qpe, kn, dn, preferred_element_type=F32))
    tq = lax.broadcasted_iota(jnp.int32, sn.shape, 0) % t_new
    tk = lax.broadcasted_iota(jnp.int32, sn.shape, 1)
    values.append(cn)
    scores.append(jnp.where(tk <= tq, sn, -jnp.inf))

    maxes = [jnp.max(s, axis=-1, keepdims=True) for s in scores]
    m = maxes[0]
    for pm in maxes[1:]:
        m = jnp.maximum(m, pm)
    probs = [jnp.exp2(s - pm) for s, pm in zip(scores, maxes)]
    l = jnp.zeros_like(m)
    acc = jnp.zeros((rows, MLA_KV_LORA), F32)
    for p, pm, vals in zip(probs, maxes, values):
        scale = jnp.exp2(pm - m)
        l = l + scale * jnp.sum(p, axis=-1, keepdims=True)
        acc = acc + scale * jnp.dot(p.astype(BF16), vals, preferred_element_type=F32)
    o_ref[0] = acc / l


def mla_decode(qlat, qpe, ckv_new, kpe_new, cache_ckv, cache_kpe, page_table, dec_batch, t_new):
    n_pages = page_table.shape[1]
    past_len = n_pages * PAGE_SIZE
    rows = MLA_HEADS * t_new
    kpe_t = jnp.swapaxes(cache_kpe, 2, 3)
    qlat4 = qlat.reshape(MLA_HEADS, dec_batch, t_new, MLA_KV_LORA)
    qpe4 = qpe.reshape(MLA_HEADS, dec_batch, t_new, HEAD_PAD)
    t_pad = NEW_PAD
    pad_new = lambda a: jnp.pad(a.reshape(dec_batch, t_new, a.shape[-1]), ((0, 0), (0, t_pad - t_new), (0, 0)))
    cnew = pad_new(ckv_new)
    knew = pad_new(kpe_new)

    in_specs = [
        pl.BlockSpec((MLA_HEADS, 1, t_new, MLA_KV_LORA), lambda b, pt: (0, b, 0, 0)),
        pl.BlockSpec((MLA_HEADS, 1, t_new, HEAD_PAD), lambda b, pt: (0, b, 0, 0)),
        pl.BlockSpec((1, t_pad, MLA_KV_LORA), lambda b, pt: (b, 0, 0)),
        pl.BlockSpec((1, t_pad, MLA_ROPE), lambda b, pt: (b, 0, 0)),
        pl.BlockSpec(memory_space=pl.ANY),
        pl.BlockSpec(memory_space=pl.ANY),
    ]
    grid_spec = pltpu.PrefetchScalarGridSpec(
        num_scalar_prefetch=1,
        grid=(dec_batch,),
        in_specs=in_specs,
        out_specs=pl.BlockSpec((1, rows, MLA_KV_LORA), lambda b, pt: (b, 0, 0)),
        scratch_shapes=[pltpu.VMEM((2, n_pages, PAGE_SIZE, MLA_KV_LORA), F32),
                        pltpu.VMEM((2, n_pages, MLA_ROPE, PAGE_SIZE), F32),
                        pltpu.SemaphoreType.DMA((2, 2))],
    )
    return pl.pallas_call(
        functools.partial(_decode_kernel, t_new),
        grid_spec=grid_spec,
        out_shape=jax.ShapeDtypeStruct((dec_batch, rows, MLA_KV_LORA), F32),
        compiler_params=_cparams(("arbitrary",)),
        name="mla_decode",
    )(page_table, qlat4, qpe4, cnew, knew, cache_ckv, kpe_t)


def _uv_kernel(o_ref, wuv_ref, out_ref):
    o = o_ref[...]
    o = o.reshape(o.shape[0] * o.shape[2], MLA_KV_LORA).astype(BF16)
    out_ref[...] = jnp.dot(o, wuv_ref[0], preferred_element_type=F32).astype(BF16)


def latent_to_values(o_lat, wuv, dec_batch, t_new):
    o4 = o_lat.reshape(dec_batch, MLA_HEADS, t_new, MLA_KV_LORA)
    return pl.pallas_call(
        _uv_kernel,
        grid=(MLA_HEADS,),
        in_specs=[pl.BlockSpec((dec_batch, 1, t_new, MLA_KV_LORA), lambda h: (0, h, 0, 0)),
                  pl.BlockSpec((1, MLA_KV_LORA, HEAD_PAD), lambda h: (h, 0, 0))],
        out_specs=pl.BlockSpec((dec_batch * t_new, HEAD_PAD), lambda h: (0, h)),
        out_shape=jax.ShapeDtypeStruct((dec_batch * t_new, MLA_PAD), BF16),
        compiler_params=_cparams(("parallel",)),
        name="latent_to_values",
    )(o4, wuv)


def _gla_kernel(nsb, q_ref, k_ref, la_ref, v_ref, s0_ref, o_ref, sfin_ref, st_sc, kv_sc, sall_sc):
    blk = pl.program_id(1)
    nblk = pl.num_programs(1)
    c = GLA_CHUNK
    tb = q_ref.shape[1]
    nc = tb // c
    cps = nc // nsb
    head_k = [slice(hh * GLA_DK, (hh + 1) * GLA_DK) for hh in range(GLA_HEADS)]
    head_v = [slice(hh * GLA_DV, (hh + 1) * GLA_DV) for hh in range(GLA_HEADS)]

    @pl.when(blk == 0)
    def _():
        for sq in range(nsb):
            for hh in range(GLA_HEADS):
                st_sc[sq, hh] = s0_ref[sq, hh].T

    q = q_ref[0]
    k = k_ref[0]
    la = la_ref[0]
    v = v_ref[0]

    r = lax.broadcasted_iota(jnp.int32, (tb, tb), 0)
    cc = lax.broadcasted_iota(jnp.int32, (tb, tb), 1)
    same = (r >> 4) == (cc >> 4)
    tri = jnp.where(same & (cc <= r), 1.0, 0.0).astype(F32)
    ones = jnp.where(same, 1.0, 0.0).astype(F32)
    b = jnp.dot(tri, la, precision=lax.Precision.HIGHEST, preferred_element_type=F32)
    b_last = jnp.dot(ones, la, precision=lax.Precision.HIGHEST, preferred_element_type=F32)
    qd = (q * jnp.exp(b)).astype(BF16)
    kd = (k * jnp.exp(b_last - b)).astype(BF16)
    dec = jnp.exp(b_last)

    hd = lax.broadcasted_iota(jnp.int32, (GLA_QK, GLA_WIDTH), 0) >> 6
    he = lax.broadcasted_iota(jnp.int32, (GLA_QK, GLA_WIDTH), 1) >> 7
    expand = jnp.where(hd == he, 1.0, 0.0).astype(BF16)
    b3 = b.reshape(nc, c, GLA_QK)
    k3 = k.reshape(nc, c, GLA_QK)
    q3 = q.reshape(nc, c, GLA_QK)
    v3 = v.astype(F32).reshape(nc, c, GLA_WIDTH)
    pos = lax.broadcasted_iota(jnp.int32, (nc, c, GLA_QK), 1)
    o_intra = jnp.zeros((tb, GLA_WIDTH), F32)
    for j in range(c):
        e = jnp.exp(b3 - b3[:, j:j + 1, :])
        w = jnp.where(pos >= j, q3 * k3[:, j:j + 1, :] * e, 0.0)
        red = jnp.dot(w.reshape(tb, GLA_QK).astype(BF16), expand, preferred_element_type=F32)
        vj = jnp.broadcast_to(v3[:, j:j + 1, :], (nc, c, GLA_WIDTH)).reshape(tb, GLA_WIDTH)
        o_intra = o_intra + red * vj
    for ci in range(nc):
        rs = slice(ci * c, (ci + 1) * c)
        for hh in range(GLA_HEADS):
            kv_sc[ci, hh] = lax.dot_general(v[rs, head_v[hh]], kd[rs, head_k[hh]], (((0,), (0,)), ((), ())),
                                            preferred_element_type=F32)

    for sq in range(nsb):
        states = [st_sc[sq, hh] for hh in range(GLA_HEADS)]
        for cj in range(cps):
            ci = sq * cps + cj
            dec_row = dec[ci * c:ci * c + 1, :]
            for hh in range(GLA_HEADS):
                sall_sc[ci, hh] = states[hh].astype(BF16)
                states[hh] = states[hh] * dec_row[:, head_k[hh]] + kv_sc[ci, hh]
        for hh in range(GLA_HEADS):
            st_sc[sq, hh] = states[hh]

    for ci in range(nc):
        rs = slice(ci * c, (ci + 1) * c)
        outs = [lax.dot_general(qd[rs, head_k[hh]], sall_sc[ci, hh], (((1,), (1,)), ((), ())),
                                preferred_element_type=F32) for hh in range(GLA_HEADS)]
        o_ref[0, rs, :] = o_intra[rs, :] + jnp.concatenate(outs, axis=1)

    @pl.when(blk == nblk - 1)
    def _():
        for sq in range(nsb):
            for hh in range(GLA_HEADS):
                sfin_ref[sq, hh] = st_sc[sq, hh].T


def gla(gq, gk, la, gv, s0, n_seq, seq_len):
    tb = GLA_BLOCK
    nsb = max(1, tb // seq_len)
    nblk = max(1, seq_len // tb)
    n_outer = n_seq // nsb
    nc = tb // GLA_CHUNK
    sh3 = lambda a: a.reshape(n_outer, nblk * tb, a.shape[-1])
    row = lambda b, i: (b, i, 0)
    st_spec = pl.BlockSpec((nsb, GLA_HEADS, GLA_DK, GLA_DV), lambda b, i: (b, 0, 0, 0))
    o, s_fin = pl.pallas_call(
        functools.partial(_gla_kernel, nsb),
        grid=(n_outer, nblk),
        in_specs=[pl.BlockSpec((1, tb, GLA_QK), row), pl.BlockSpec((1, tb, GLA_QK), row),
                  pl.BlockSpec((1, tb, GLA_QK), row), pl.BlockSpec((1, tb, GLA_WIDTH), row), st_spec],
        out_specs=[pl.BlockSpec((1, tb, GLA_WIDTH), row), st_spec],
        out_shape=[jax.ShapeDtypeStruct((n_outer, nblk * tb, GLA_WIDTH), F32),
                   jax.ShapeDtypeStruct((n_seq, GLA_HEADS, GLA_DK, GLA_DV), F32)],
        scratch_shapes=[pltpu.VMEM((nsb, GLA_HEADS, GLA_DV, GLA_DK), F32),
                        pltpu.VMEM((nc, GLA_HEADS, GLA_DV, GLA_DK), F32),
                        pltpu.VMEM((nc, GLA_HEADS, GLA_DV, GLA_DK), BF16)],
        compiler_params=_cparams(("parallel", "arbitrary")),
        name="gla_prompt" if nblk > 1 else "gla_sample",
    )(sh3(gq), sh3(gk), sh3(la), sh3(gv), s0)
    return o.reshape(n_seq * seq_len, GLA_WIDTH), s_fin


def _mix_kernel(x_ref, om_ref, og_ref, gg_ref, gta_ref, shf_ref, scf_ref, gm_ref, ggl_ref, wo_ref,
                gn_ref, wr_ref, br_ref, h2a_in, h2b_in, x1_ref, h2a_ref, h2b_ref, ti_ref, tg_ref, hist_ref):
    del h2a_in, h2b_in
    om = om_ref[...].astype(F32)
    ms = jnp.sum(om * om, axis=-1, keepdims=True) * (1.0 / (MLA_HEADS * MLA_V))
    om = om * lax.rsqrt(ms + EPS) * gm_ref[...]
    og = og_ref[...]
    gg = gg_ref[...].astype(F32)
    gate = gg * jax.nn.sigmoid(gg)
    parts = []
    for hh in range(GLA_HEADS):
        sl = slice(hh * GLA_DV, (hh + 1) * GLA_DV)
        parts.append(_rms(og[:, sl], ggl_ref[...]) * gate[:, sl])
    mix = jnp.concatenate([om] + parts, axis=1).astype(BF16)
    x1 = x_ref[...] + gta_ref[0] * jnp.dot(mix, wo_ref[...], preferred_element_type=F32)
    x1_ref[...] = x1
    h2 = _rms(x1, gn_ref[...]) * (1.0 + scf_ref[0]) + shf_ref[0]
    words = _pack_rows(h2)
    h2a_ref[...] = words[:, :PERM_W]
    h2b_ref[...] = words[:, PERM_W:]
    h_hi = h2.astype(BF16)
    h_lo = (h2 - h_hi.astype(F32)).astype(BF16)
    logits = (jnp.dot(h_hi, wr_ref[0], preferred_element_type=F32)
              + (jnp.dot(h_hi, wr_ref[1], preferred_element_type=F32)
                 + jnp.dot(h_lo, wr_ref[0], preferred_element_type=F32))) + br_ref[...]
    lane_i = lax.broadcasted_iota(jnp.int32, logits.shape, 1)
    lane = lane_i.astype(F32)
    vals = []
    idxs = []
    for _ in range(TOP_K):
        mx = jnp.max(logits, axis=-1, keepdims=True)
        ix = jnp.min(jnp.where(logits == mx, lane, float(LANES)), axis=-1, keepdims=True)
        vals.append(mx)
        idxs.append(ix)
        logits = jnp.where(lane == ix, -jnp.inf, logits)
    ex = [jnp.exp(vv - vals[0]) for vv in vals]
    den = ex[0] + ex[1] + ex[2] + ex[3]
    ti = jnp.zeros(logits.shape, F32)
    tg = jnp.zeros(logits.shape, F32)
    onehot = jnp.zeros(logits.shape, F32)
    for kk in range(TOP_K):
        ti = jnp.where(lane_i == kk, idxs[kk], ti)
        tg = jnp.where(lane_i == kk, ex[kk] / den, tg)
        onehot = onehot + jnp.where(lane == idxs[kk], 1.0, 0.0)
    ti_ref[...] = ti.astype(jnp.int32)
    tg_ref[...] = tg
    hist_ref[0] = jnp.sum(onehot, axis=0, keepdims=True)


def mixer_out(x, o_mla, o_gla, gg, gta, shf, scf, mod_per_row, w, n_all, row0, h2_buf):
    n = x.shape[0]
    tm = ROW_TILE
    t0 = row0 // tm
    row = lambda i: (i, 0)
    const = lambda i: (0, 0)
    extra_specs = [pl.BlockSpec(memory_space=pl.ANY)] * 2
    extra_args = list(h2_buf)
    if mod_per_row:
        mod_spec = pl.BlockSpec((1, tm, D_MODEL), lambda i: (0, i, 0))
    else:
        tiles_per_seq = n // gta.shape[0] // tm
        mod_spec = pl.BlockSpec((1, 1, D_MODEL), lambda i: (i // tiles_per_seq, 0, 0))
    d_mix = MLA_PAD + GLA_WIDTH
    n_in = 13
    return pl.pallas_call(
        _mix_kernel,
        grid=(n // tm,),
        in_specs=[pl.BlockSpec((tm, D_MODEL), row), pl.BlockSpec((tm, MLA_PAD), row),
                  pl.BlockSpec((tm, GLA_WIDTH), row), pl.BlockSpec((tm, GLA_WIDTH), row),
                  mod_spec, mod_spec, mod_spec,
                  pl.BlockSpec((1, MLA_PAD), const), pl.BlockSpec((1, GLA_DV), const),
                  pl.BlockSpec((d_mix, D_MODEL), const), pl.BlockSpec((1, D_MODEL), const),
                  pl.BlockSpec((2, D_MODEL, LANES), lambda i: (0, 0, 0)),
                  pl.BlockSpec((1, LANES), const)] + extra_specs,
        out_specs=[pl.BlockSpec((tm, D_MODEL), row),
                   pl.BlockSpec((tm, PERM_W), lambda i: (i + t0, 0)),
                   pl.BlockSpec((tm, PERM_W), lambda i: (i + t0, 0)),
                   pl.BlockSpec((tm, LANES), row), pl.BlockSpec((tm, LANES), row),
                   pl.BlockSpec((1, 1, LANES), lambda i: (i, 0, 0))],
        out_shape=[jax.ShapeDtypeStruct((n, D_MODEL), F32),
                   jax.ShapeDtypeStruct((n_all, PERM_W), jnp.uint32),
                   jax.ShapeDtypeStruct((n_all, PERM_W), jnp.uint32),
                   jax.ShapeDtypeStruct((n, LANES), jnp.int32), jax.ShapeDtypeStruct((n, LANES), F32),
                   jax.ShapeDtypeStruct((n // tm, 1, LANES), F32)],
        input_output_aliases={n_in: 1, n_in + 1: 2},
        compiler_params=_cparams(("parallel",)),
        name="mixer_sample" if mod_per_row else "mixer_prompt",
    )(x, o_mla, o_gla, gg, gta, shf, scf, w["g_mla_pad"], w["g_gla_out"], w["w_o_pad"],
      w["g_norm_ffn"], w["w_router_pad"], w["b_router_pad"], *extra_args)


def _rank_kernel(ti_ref, base_ref, dest_ref):
    ti = ti_ref[...]
    tm = ti.shape[0]
    lane = lax.broadcasted_iota(jnp.int32, ti.shape, 1)
    cols = [ti[:, kk:kk + 1] for kk in range(TOP_K)]
    onehot = jnp.zeros(ti.shape, F32)
    for kk in range(TOP_K):
        onehot = onehot + jnp.where(lane == cols[kk], 1.0, 0.0)
    r = lax.broadcasted_iota(jnp.int32, (tm, tm), 0)
    c = lax.broadcasted_iota(jnp.int32, (tm, tm), 1)
    earlier = jnp.where(c < r, 1.0, 0.0).astype(BF16)
    pos = jnp.dot(earlier, onehot.astype(BF16), preferred_element_type=F32) + base_ref[0]
    out = jnp.zeros(ti.shape, F32)
    for kk in range(TOP_K):
        dk = jnp.sum(jnp.where(lane == cols[kk], pos, 0.0), axis=-1, keepdims=True)
        out = jnp.where(lane == kk, dk, out)
    dest_ref[...] = out.astype(jnp.int32)


def route_rank(ti, base):
    n = ti.shape[0]
    tm = RANK_TILE
    return pl.pallas_call(
        _rank_kernel,
        grid=(n // tm,),
        in_specs=[pl.BlockSpec((tm, LANES), lambda i: (i, 0)),
                  pl.BlockSpec((1, 1, LANES), lambda i: (i, 0, 0))],
        out_specs=pl.BlockSpec((tm, LANES), lambda i: (i, 0)),
        out_shape=jax.ShapeDtypeStruct((n, LANES), jnp.int32),
        compiler_params=_cparams(("parallel",)),
        name="route_rank",
    )(ti, base)


def _route_tables(hist, n_tok):
    tm = MOE_TILE
    h = hist[:, 0, :].astype(jnp.int32)
    h = h.reshape(-1, RANK_TILE // ROW_TILE, LANES).sum(axis=1)
    counts = jnp.sum(h, axis=0)
    padded = (counts + tm - 1) // tm * tm
    pad_ends = jnp.cumsum(padded)
    pad_starts = pad_ends - padded
    base = (pad_starts[None, :] + jnp.cumsum(h, axis=0) - h).astype(F32)[:, None, :]
    n_blocks = pl.cdiv(n_tok * TOP_K, tm) + N_EXPERTS
    n_active = (pad_ends[N_EXPERTS - 1] // tm).astype(jnp.int32)
    blk = jnp.minimum(jnp.arange(n_blocks, dtype=jnp.int32), n_active - 1)
    ends = pad_ends[:N_EXPERTS]
    block_e = jnp.minimum(jnp.sum((ends[None, :] <= (blk * tm)[:, None]).astype(jnp.int32), axis=1),
                          N_EXPERTS - 1).astype(jnp.int32)
    return base, block_e, n_active.reshape(1), n_blocks


def _sc_mesh():
    return plsc.VectorSubcoreMesh(core_axis_name="c", subcore_axis_name="s")


def sc_dispatch(x_rows, idx, n_out):
    n, wd = x_rows.shape
    win = SC_WIN
    nwin = n // win

    @functools.partial(pl.kernel, out_type=jax.ShapeDtypeStruct((n_out, wd), x_rows.dtype),
                       mesh=_sc_mesh(), scratch_types=[])
    def k(x_hbm, i_hbm, o_hbm):
        def body(x_vmem, i_vmem):
            pltpu.sync_copy(x_vmem, o_hbm.at[i_vmem.at[0]])

        pltpu.emit_pipeline(
            body,
            grid=(idx.shape[1] // win,),
            in_specs=[pl.BlockSpec((win, wd), lambda i: (i % nwin, 0)),
                      pl.BlockSpec((1, win), lambda i: (0, i))],
            out_specs=[],
            core_axis_name=("c", "s"),
            dimension_semantics=(pltpu.PARALLEL,),
        )(x_hbm, i_hbm)

    return k(x_rows, idx)


def sc_combine(y_rows, idx):
    wd = y_rows.shape[1]
    m = idx.shape[1]
    win = SC_WIN

    @functools.partial(pl.kernel, out_type=jax.ShapeDtypeStruct((m, wd), y_rows.dtype),
                       mesh=_sc_mesh(), scratch_types=[])
    def k(y_hbm, i_hbm, o_hbm):
        def body(i_vmem, o_vmem):
            pltpu.sync_copy(y_hbm.at[i_vmem.at[0]], o_vmem)

        pltpu.emit_pipeline(
            body,
            grid=(m // win,),
            in_specs=[pl.BlockSpec((1, win), lambda i: (0, i))],
            out_specs=[pl.BlockSpec((win, wd), lambda i: (i, 0))],
            core_axis_name=("c", "s"),
            dimension_semantics=(pltpu.PARALLEL,),
        )(i_hbm, o_hbm)

    return k(y_rows, idx)


def _moe_kernel(be_ref, na_ref, xa_ref, xb_ref, wup_ref, bup_ref, wdn_ref, bdn_ref, ya_ref, yb_ref,
                wup_sc, wdn_sc):
    i = pl.program_id(0)
    active = i < na_ref[0]
    prev = be_ref[jnp.maximum(i - 1, 0)]
    fresh = (i == 0) | (be_ref[i] != prev)

    @pl.when(active & fresh)
    def _():
        wup_sc[...] = wup_ref[0].astype(BF16)
        wdn_sc[...] = wdn_ref[0].astype(BF16)

    @pl.when(active)
    def _():
        xb = _unpack_rows(jnp.concatenate([xa_ref[...], xb_ref[...]], axis=1)).astype(BF16)
        hu = jnp.dot(xb, wup_sc[...], preferred_element_type=F32) + bup_ref[0]
        gate = jnp.minimum(hu[:, :D_FF], SWIGLU_LIMIT)
        lin = jnp.clip(hu[:, D_FF:], -SWIGLU_LIMIT, SWIGLU_LIMIT)
        act = gate * jax.nn.sigmoid(SWIGLU_ALPHA * gate) * (lin + 1.0)
        y = jnp.dot(act.astype(BF16), wdn_sc[...], preferred_element_type=F32) + bdn_ref[0]
        words = _pack_rows(y)
        ya_ref[...] = words[:, :PERM_W]
        yb_ref[...] = words[:, PERM_W:]

    @pl.when(jnp.logical_not(active))
    def _():
        ya_ref[...] = jnp.zeros(ya_ref.shape, jnp.uint32)
        yb_ref[...] = jnp.zeros(yb_ref.shape, jnp.uint32)


def moe_experts(xs_a, xs_b, block_e, n_active, w_up, b_up, w_down, b_down):
    n_rows = xs_a.shape[0]
    tm = MOE_TILE
    n_blocks = n_rows // tm
    emap3 = lambda i, be, na: (be[i], 0, 0)
    grid_spec = pltpu.PrefetchScalarGridSpec(
        num_scalar_prefetch=2,
        grid=(n_blocks,),
        in_specs=[pl.BlockSpec((tm, PERM_W), lambda i, be, na: (i, 0)),
                  pl.BlockSpec((tm, PERM_W), lambda i, be, na: (i, 0)),
                  pl.BlockSpec((1, D_MODEL, 2 * D_FF), emap3),
                  pl.BlockSpec((1, 1, 2 * D_FF), emap3),
                  pl.BlockSpec((1, D_FF, D_MODEL), emap3),
                  pl.BlockSpec((1, 1, D_MODEL), emap3)],
        out_specs=[pl.BlockSpec((tm, PERM_W), lambda i, be, na: (i, 0)),
                   pl.BlockSpec((tm, PERM_W), lambda i, be, na: (i, 0))],
        scratch_shapes=[pltpu.VMEM((D_MODEL, 2 * D_FF), BF16), pltpu.VMEM((D_FF, D_MODEL), BF16)],
    )
    return pl.pallas_call(
        _moe_kernel,
        grid_spec=grid_spec,
        out_shape=[jax.ShapeDtypeStruct((n_rows, PERM_W), jnp.uint32)] * 2,
        compiler_params=_cparams(("arbitrary",)),
        name="moe_experts",
    )(block_e, n_active, xs_a, xs_b, w_up, b_up.reshape(N_EXPERTS, 1, 2 * D_FF), w_down,
      b_down.reshape(N_EXPERTS, 1, D_MODEL))


def _final_kernel(x1_ref, yga_ref, ygb_ref, tg_ref, gtf_ref, sh_ref, sc_ref, g_ref, y_ref):
    tg = tg_ref[...]
    moe = jnp.zeros(x1_ref.shape, F32)
    for kk in range(TOP_K):
        moe = moe + tg[:, kk:kk + 1] * _unpack_rows(jnp.concatenate([yga_ref[kk], ygb_ref[kk]], axis=1))
    x2 = x1_ref[...] + gtf_ref[0] * moe
    y_ref[...] = _rms(x2, g_ref[...]) * (1.0 + sc_ref[0]) + sh_ref[0]


def final_out(x1, yg_a, yg_b, tg, row0, gtf, sh, sc, mod_per_row, g_final):
    n = x1.shape[0]
    tm = ROW_TILE
    t0 = row0 // tm
    row = lambda i: (i, 0)
    if mod_per_row:
        mod_spec = pl.BlockSpec((1, tm, D_MODEL), lambda i: (0, i, 0))
    else:
        tiles_per_seq = n // gtf.shape[0] // tm
        mod_spec = pl.BlockSpec((1, 1, D_MODEL), lambda i: (i // tiles_per_seq, 0, 0))
    return pl.pallas_call(
        _final_kernel,
        grid=(n // tm,),
        in_specs=[pl.BlockSpec((tm, D_MODEL), row),
                  pl.BlockSpec((TOP_K, tm, PERM_W), lambda i: (0, i + t0, 0)),
                  pl.BlockSpec((TOP_K, tm, PERM_W), lambda i: (0, i + t0, 0)),
                  pl.BlockSpec((tm, LANES), lambda i: (i + t0, 0)),
                  mod_spec, mod_spec, mod_spec,
                  pl.BlockSpec((1, D_MODEL), lambda i: (0, 0))],
        out_specs=pl.BlockSpec((tm, D_MODEL), row),
        out_shape=jax.ShapeDtypeStruct((n, D_MODEL), F32),
        compiler_params=_cparams(("parallel",)),
        name="final_sample" if mod_per_row else "final_prompt",
    )(x1, yg_a, yg_b, tg, gtf, sh, sc, g_final)


def _prep_weights(w_in, g_q_a, w_q_b, g_kv_a, w_kv_b, w_gk_b, b_gk, g_mla_out, g_gla_out, w_o,
                  g_norm_mix, g_norm_ffn, w_router, b_router):
    sizes = (MLA_Q_LORA, MLA_KV_LORA, MLA_ROPE, GLA_QK, GLA_QK, GLA_WIDTH, GLA_GATE_RANK, GLA_WIDTH)
    offs = np.cumsum((0,) + sizes)
    part = lambda i: w_in[:, offs[i]:offs[i + 1]]
    half = MLA_ROPE // 2
    k_rope = part(2)
    misc = jnp.concatenate([k_rope, part(6), jnp.zeros((D_MODEL, LANES - MLA_ROPE - GLA_GATE_RANK), F32)], 1)
    swap = jnp.concatenate([-k_rope[:, half:], k_rope[:, :half],
                            jnp.zeros((D_MODEL, LANES - MLA_ROPE), F32)], 1)
    w_in_pad = jnp.concatenate([part(0), part(1), part(3), part(4), part(5), part(7), misc, swap], 1)

    pad_q = jnp.zeros((MLA_Q_LORA, MLA_HEADS, HEAD_PAD - MLA_NOPE - MLA_ROPE), F32)
    wq1 = jnp.concatenate([w_q_b, pad_q], axis=2)
    q_lo = w_q_b[:, :, MLA_NOPE:MLA_NOPE + half]
    q_hi = w_q_b[:, :, MLA_NOPE + half:]
    wq2 = jnp.concatenate([jnp.zeros((MLA_Q_LORA, MLA_HEADS, MLA_NOPE), F32), -q_hi, q_lo, pad_q], axis=2)
    pad_kv = jnp.zeros((MLA_KV_LORA, MLA_HEADS, HEAD_PAD - MLA_NOPE), F32)
    w_uk = w_kv_b[:, :, :MLA_NOPE]
    w_uv = w_kv_b[:, :, MLA_NOPE:]
    wk = jnp.concatenate([w_uk, pad_kv], axis=2)
    wv = jnp.concatenate([w_uv, pad_kv], axis=2)
    wuk_t = jnp.concatenate([jnp.transpose(w_uk, (1, 2, 0)),
                             jnp.zeros((MLA_HEADS, HEAD_PAD - MLA_NOPE, MLA_KV_LORA), F32)], axis=1)
    wuv_h = jnp.transpose(wv, (1, 0, 2))
    wgk = jnp.zeros((LANES, GLA_QK), F32).at[MLA_ROPE:MLA_ROPE + GLA_GATE_RANK].set(w_gk_b)

    wo_mla = w_o[:MLA_HEADS * MLA_V].reshape(MLA_HEADS, MLA_V, D_MODEL)
    wo_mla = jnp.concatenate([wo_mla, jnp.zeros((MLA_HEADS, HEAD_PAD - MLA_V, D_MODEL), F32)], axis=1)
    w_o_pad = jnp.concatenate([wo_mla.reshape(MLA_PAD, D_MODEL), w_o[MLA_HEADS * MLA_V:]], axis=0)
    g_mla = g_mla_out.reshape(MLA_HEADS, MLA_V)
    g_mla_pad = jnp.concatenate([g_mla, jnp.zeros((MLA_HEADS, HEAD_PAD - MLA_V), F32)], 1).reshape(1, MLA_PAD)
    w_router_pad = jnp.concatenate([w_router, jnp.zeros((D_MODEL, LANES - N_EXPERTS), F32)], axis=1)
    wr_hi = w_router_pad.astype(BF16)
    w_router_pad = jnp.stack([wr_hi, (w_router_pad - wr_hi.astype(F32)).astype(BF16)])
    b_router_pad = jnp.concatenate([b_router, jnp.full((LANES - N_EXPERTS,), -jnp.inf, F32)]).reshape(1, LANES)
    return dict(
        w_in=w_in_pad.astype(BF16), g_norm_mix=g_norm_mix.reshape(1, D_MODEL),
        g_q_a=g_q_a.reshape(1, MLA_Q_LORA),
        wq1=wq1.reshape(MLA_Q_LORA, MLA_PAD).astype(BF16), wq2=wq2.reshape(MLA_Q_LORA, MLA_PAD).astype(BF16),
        g_kv_a=g_kv_a.reshape(1, MLA_KV_LORA),
        wk=wk.reshape(MLA_KV_LORA, MLA_PAD).astype(BF16), wv=wv.reshape(MLA_KV_LORA, MLA_PAD).astype(BF16),
        wuk_t=wuk_t.astype(BF16), wuv_h=wuv_h.astype(BF16),
        wgk=wgk.astype(BF16), b_gk=b_gk.reshape(1, GLA_QK),
        g_mla_pad=g_mla_pad, g_gla_out=g_gla_out.reshape(1, GLA_DV), w_o_pad=w_o_pad.astype(BF16),
        g_norm_ffn=g_norm_ffn.reshape(1, D_MODEL), w_router_pad=w_router_pad, b_router_pad=b_router_pad,
    )


def _rope_tables(pos, reps):
    half = MLA_ROPE // 2
    inv = ROPE_THETA ** (-jnp.arange(half, dtype=F32) / half)
    ang = pos.astype(F32)[:, None] * inv
    cos, sin = jnp.cos(ang), jnp.sin(ang)
    n = pos.shape[0]
    qc = jnp.concatenate([jnp.full((n, MLA_NOPE), Q_SCALE, F32), Q_SCALE * cos, Q_SCALE * cos,
                          jnp.zeros((n, HEAD_PAD - MLA_NOPE - MLA_ROPE), F32)], axis=1)
    qs = jnp.concatenate([jnp.zeros((n, MLA_NOPE), F32), Q_SCALE * sin, Q_SCALE * sin,
                          jnp.zeros((n, HEAD_PAD - MLA_NOPE - MLA_ROPE), F32)], axis=1)
    kc = jnp.concatenate([cos, cos, jnp.zeros((n, LANES - MLA_ROPE), F32)], axis=1)
    ks = jnp.concatenate([sin, sin, jnp.zeros((n, LANES - MLA_ROPE), F32)], axis=1)
    return tuple(jnp.tile(t, (reps, 1)) for t in (qc, qs, kc, ks))


def kernel(x_prompt, x_sample, cache_ckv, cache_kpe, state_gla, page_table, c_prompt, c_sample, w_ada, b_ada, g_norm_mix, w_in, g_q_a, w_q_b, g_kv_a, w_kv_b, w_gk_b, b_gk, g_mla_out, g_gla_out, w_o, g_norm_ffn, w_router, b_router, w_up, b_up, w_down, b_down, g_norm_final, w_ada_final, b_ada_final):
    B, S, D = x_prompt.shape
    DB, T, _ = x_sample.shape
    depth = w_ada.shape[0]
    assert depth == 1
    past_len = page_table.shape[1] * cache_ckv.shape[2]
    n_p, n_s = B * S, DB * T
    l = 0

    w = _prep_weights(w_in[l], g_q_a[l], w_q_b[l], g_kv_a[l], w_kv_b[l], w_gk_b[l], b_gk[l],
                      g_mla_out[l], g_gla_out[l], w_o[l], g_norm_mix[l], g_norm_ffn[l],
                      w_router[l], b_router[l])

    n_c = B + DB
    n_c_pad = (n_c + 7) // 8 * 8
    c_all = jnp.concatenate([c_prompt, c_sample, jnp.zeros((n_c_pad - n_c, D), F32)], axis=0)
    mod = ada_terms(c_all, w_ada[l], b_ada[l])
    mod_f = ada_terms(c_all, w_ada_final, b_ada_final)
    term = lambda m, i: m[:, i * D:(i + 1) * D]
    p_term = lambda m, i: term(m, i)[:B].reshape(B, 1, D)
    s_term = lambda m, i: jnp.broadcast_to(term(m, i)[B:n_c][:, None, :], (DB, T, D)).reshape(1, n_s, D)

    xp = x_prompt.reshape(n_p, D)
    xs = x_sample.reshape(n_s, D)
    tabs_p = _rope_tables(jnp.arange(S), 1)
    tabs_s = _rope_tables(past_len + jnp.arange(T), ROW_TILE // T)

    (q_p, k_p, v_p, ckv_p, kpe_p, gq_p, gk_p, gv_p, la_p, gg_p) = in_proj(
        xp, p_term(mod, 0), p_term(mod, 1), False, tabs_p, w, True)
    o_mla_p = mla_prefill(q_p, k_p, v_p, B, S)
    s0 = jnp.zeros((B, GLA_HEADS, GLA_DK, GLA_DV), F32)
    o_gla_p, gla_p = gla(gq_p, gk_p, la_p, gv_p, s0, B, S)
    n_all = n_p + n_s
    h2_init = (jnp.zeros((n_all, PERM_W), jnp.uint32), jnp.zeros((n_all, PERM_W), jnp.uint32))
    x1_p, h2a, h2b, ti_p, tg_p, hist_p = mixer_out(xp, o_mla_p, o_gla_p.reshape(n_p, GLA_WIDTH), gg_p,
                                                   p_term(mod, 2), p_term(mod, 3), p_term(mod, 4), False, w,
                                                   n_all, 0, h2_init)

    (q_s, ckv_s, kpe_s, gq_s, gk_s, gv_s, la_s, gg_s) = in_proj(
        xs, s_term(mod, 0), s_term(mod, 1), True, tabs_s, w, False)
    qlat, qpe = absorb_q(q_s, w["wuk_t"])
    o_lat = mla_decode(qlat, qpe, ckv_s, kpe_s, cache_ckv, cache_kpe, page_table, DB, T)
    o_mla_s = latent_to_values(o_lat, w["wuv_h"], DB, T)
    tpad = GLA_CHUNK
    padt = lambda a: jnp.pad(a.reshape(DB, T, a.shape[-1]), ((0, 0), (0, tpad - T), (0, 0))).reshape(
        DB * tpad, a.shape[-1])
    o_gla_s, gla_s = gla(padt(gq_s), padt(gk_s), padt(la_s), padt(gv_s), state_gla[l], DB, tpad)
    o_gla_s = o_gla_s.reshape(DB, tpad, GLA_WIDTH)[:, :T].reshape(n_s, GLA_WIDTH)
    x1_s, h2a, h2b, ti_s, tg_s, hist_s = mixer_out(xs, o_mla_s, o_gla_s, gg_s,
                                                   s_term(mod, 2), s_term(mod, 3), s_term(mod, 4), True, w,
                                                   n_all, n_p, (h2a, h2b))

    ti = jnp.concatenate([ti_p, ti_s], axis=0)
    tg = jnp.concatenate([tg_p, tg_s], axis=0)
    base, block_e, n_active, n_blocks = _route_tables(jnp.concatenate([hist_p, hist_s], axis=0), n_all)
    dest = route_rank(ti, base)
    idx = dest[:, :TOP_K].T.reshape(1, TOP_K * n_all)
    n_rows = n_blocks * MOE_TILE
    xs_a = sc_dispatch(h2a, idx, n_rows)
    xs_b = sc_dispatch(h2b, idx, n_rows)
    ys_a, ys_b = moe_experts(xs_a, xs_b, block_e, n_active, w_up[l], b_up[l], w_down[l], b_down[l])
    yg_a = sc_combine(ys_a, idx).reshape(TOP_K, n_all, PERM_W)
    yg_b = sc_combine(ys_b, idx).reshape(TOP_K, n_all, PERM_W)

    g_fin = g_norm_final.reshape(1, D)
    y_p = final_out(x1_p, yg_a, yg_b, tg, 0, p_term(mod, 5), p_term(mod_f, 0), p_term(mod_f, 1), False, g_fin)
    y_s = final_out(x1_s, yg_a, yg_b, tg, n_p, s_term(mod, 5), s_term(mod_f, 0), s_term(mod_f, 1), True, g_fin)

    return (y_p.reshape(B, S, D), y_s.reshape(DB, T, D),
            ckv_p.reshape(1, B, S, MLA_KV_LORA), kpe_p.reshape(1, B, S, MLA_ROPE), gla_p[None],
            ckv_s.reshape(1, DB, T, MLA_KV_LORA), kpe_s.reshape(1, DB, T, MLA_ROPE), gla_s[None])
```

```python
import functools
import math

import jax
import jax.numpy as jnp
import numpy as np
from jax import lax
from jax.experimental import pallas as pl
from jax.experimental.pallas import tpu as pltpu
from jax.experimental.pallas import tpu_sc as plsc

F32 = jnp.float32
BF16 = jnp.bfloat16

D_MODEL = 1024
MLA_HEADS = 8
MLA_NOPE = 64
MLA_ROPE = 32
MLA_V = 64
MLA_Q_LORA = 384
MLA_KV_LORA = 256
MLA_SCALE = (MLA_NOPE + MLA_ROPE) ** -0.5
ROPE_THETA = 10000.0
GLA_HEADS = 4
GLA_DK = 64
GLA_DV = 128
GLA_GATE_RANK = 16
GLA_GATE_NORM = 16.0
GLA_CHUNK = 16
GLA_QK = GLA_HEADS * GLA_DK
GLA_WIDTH = GLA_HEADS * GLA_DV
N_EXPERTS = 32
TOP_K = 4
D_FF = D_MODEL
SWIGLU_LIMIT = 7.0
SWIGLU_ALPHA = 1.702
N_MOD = 6
EPS = 1e-6
PAGE_SIZE = 128

LANES = 128
HEAD_PAD = LANES
MLA_PAD = MLA_HEADS * HEAD_PAD
VMEM_LIMIT = 56 * 1024 * 1024

Q_SCALE = MLA_SCALE * math.log2(math.e)

C_QA = 0
C_KV = C_QA + MLA_Q_LORA
C_GQ = C_KV + MLA_KV_LORA
C_GK = C_GQ + GLA_QK
C_GV = C_GK + GLA_QK
C_GG = C_GV + GLA_WIDTH
C_MISC = C_GG + GLA_WIDTH
C_SWAP = C_MISC + LANES
D_IN_PAD = C_SWAP + LANES

ROW_TILE = 256
RANK_TILE = 1024
ATT_TILE = 512
ATT_HEADS = 4
GLA_BLOCK = 256
DEC_SPANS = 4
SC_WIN = 128
PERM_W = D_MODEL // 4
MOE_TILE = 768
MOE_SUB = 256
NEW_PAD = 16


def _cparams(sem):
    return pltpu.CompilerParams(dimension_semantics=sem, vmem_limit_bytes=VMEM_LIMIT)


def _rms(x, g):
    return x * lax.rsqrt(jnp.mean(x * x, axis=-1, keepdims=True) + EPS) * g


def _pack_rows(x):
    bits = lax.bitcast_convert_type(x.astype(BF16).astype(F32), jnp.uint32)
    w = x.shape[1] // 2
    return (bits[:, :w] >> 16) | bits[:, w:]


def _unpack_rows(words):
    lo = lax.bitcast_convert_type(words << 16, F32)
    hi = lax.bitcast_convert_type(words & jnp.uint32(0xFFFF0000), F32)
    return jnp.concatenate([lo, hi], axis=1)


def _ada_kernel(c_ref, w_ref, b_ref, o_ref):
    c = c_ref[...]
    a = (c * jax.nn.sigmoid(c)).astype(BF16)
    o_ref[...] = jnp.dot(a, w_ref[...].astype(BF16), preferred_element_type=F32) + b_ref[...]


def ada_terms(c, w, b):
    rows, d = c.shape
    n = w.shape[1]
    tn = 512
    return pl.pallas_call(
        _ada_kernel,
        grid=(n // tn,),
        in_specs=[pl.BlockSpec((rows, d), lambda j: (0, 0)),
                  pl.BlockSpec((d, tn), lambda j: (0, j)),
                  pl.BlockSpec((1, tn), lambda j: (0, j))],
        out_specs=pl.BlockSpec((rows, tn), lambda j: (0, j)),
        out_shape=jax.ShapeDtypeStruct((rows, n), F32),
        compiler_params=_cparams(("arbitrary",)),
        name="ada_terms",
    )(c, w, b.reshape(1, n))


def _in_kernel(with_kv, x_ref, sh_ref, sc_ref, g_ref, win_ref, gqa_ref, wq1_ref, wq2_ref,
               gkv_ref, wk_ref, wv_ref, wgk_ref, bgk_ref, qc_ref, qs_ref, kc_ref, ks_ref, *outs):
    if with_kv:
        q_out, k_out, v_out, ckv_out, kpe_out, gq_out, gk_out, gv_out, la_out, gg_out = outs
    else:
        q_out, ckv_out, kpe_out, gq_out, gk_out, gv_out, la_out, gg_out = outs
    x = x_ref[...]
    h = _rms(x, g_ref[...])
    h = h * (1.0 + sc_ref[0]) + sh_ref[0]
    proj = jnp.dot(h.astype(BF16), win_ref[...], preferred_element_type=F32)

    qn = _rms(proj[:, C_QA:C_KV], gqa_ref[...]).astype(BF16)
    qa = jnp.dot(qn, wq1_ref[...], preferred_element_type=F32)
    qb = jnp.dot(qn, wq2_ref[...], preferred_element_type=F32)
    qc = qc_ref[...]
    qs = qs_ref[...]
    for hh in range(MLA_HEADS):
        sl = slice(hh * HEAD_PAD, (hh + 1) * HEAD_PAD)
        q_out[:, sl] = (qa[:, sl] * qc + qb[:, sl] * qs).astype(BF16)

    ckv = _rms(proj[:, C_KV:C_GQ], gkv_ref[...])
    ckv_out[...] = ckv
    kpe = proj[:, C_MISC:C_SWAP] * kc_ref[...] + proj[:, C_SWAP:D_IN_PAD] * ks_ref[...]
    kpe_out[...] = kpe[:, :MLA_ROPE]
    if with_kv:
        ckv_b = ckv.astype(BF16)
        kn = jnp.dot(ckv_b, wk_ref[...], preferred_element_type=F32)
        kpe_sh = pltpu.roll(kpe, MLA_NOPE, axis=1)
        for hh in range(MLA_HEADS):
            sl = slice(hh * HEAD_PAD, (hh + 1) * HEAD_PAD)
            k_out[:, sl] = (kn[:, sl] + kpe_sh).astype(BF16)
        v_out[...] = jnp.dot(ckv_b, wv_ref[...], preferred_element_type=F32).astype(BF16)

    gq_out[...] = proj[:, C_GQ:C_GK] * (GLA_DK ** -0.5)
    gk_out[...] = proj[:, C_GK:C_GV]
    gv_out[...] = proj[:, C_GV:C_GG].astype(BF16)
    gg_out[...] = proj[:, C_GG:C_MISC].astype(BF16)
    xg = jnp.dot(proj[:, C_MISC:C_SWAP].astype(BF16), wgk_ref[...],
                 preferred_element_type=F32) + bgk_ref[...]
    la_out[...] = (jnp.minimum(xg, 0.0) - jnp.log(1.0 + jnp.exp(-jnp.abs(xg)))) * (1.0 / GLA_GATE_NORM)


def in_proj(x, sh, sc, mod_per_row, tabs, wts, with_kv):
    n = x.shape[0]
    tm = ROW_TILE
    nt = n // tm
    qc, qs, kc, ks = tabs
    n_tab = qc.shape[0] // tm
    row = lambda i: (i, 0)
    const = lambda i: (0, 0)
    if mod_per_row:
        mod_spec = pl.BlockSpec((1, tm, D_MODEL), lambda i: (0, i, 0))
    else:
        tiles_per_seq = n // sh.shape[0] // tm
        mod_spec = pl.BlockSpec((1, 1, D_MODEL), lambda i: (i // tiles_per_seq, 0, 0))
    tab_spec = pl.BlockSpec((tm, LANES), lambda i: (i % n_tab, 0))
    w = wts
    in_specs = [
        pl.BlockSpec((tm, D_MODEL), row), mod_spec, mod_spec,
        pl.BlockSpec((1, D_MODEL), const),
        pl.BlockSpec((D_MODEL, D_IN_PAD), const),
        pl.BlockSpec((1, MLA_Q_LORA), const),
        pl.BlockSpec((MLA_Q_LORA, MLA_PAD), const),
        pl.BlockSpec((MLA_Q_LORA, MLA_PAD), const),
        pl.BlockSpec((1, MLA_KV_LORA), const),
        pl.BlockSpec((MLA_KV_LORA, MLA_PAD), const),
        pl.BlockSpec((MLA_KV_LORA, MLA_PAD), const),
        pl.BlockSpec((LANES, GLA_QK), const),
        pl.BlockSpec((1, GLA_QK), const),
        tab_spec, tab_spec, tab_spec, tab_spec,
    ]
    wide = lambda dt: (jax.ShapeDtypeStruct((n, MLA_PAD), dt), pl.BlockSpec((tm, MLA_PAD), row))
    outs = [wide(BF16)]
    if with_kv:
        outs += [wide(BF16), wide(BF16)]
    outs += [
        (jax.ShapeDtypeStruct((n, MLA_KV_LORA), F32), pl.BlockSpec((tm, MLA_KV_LORA), row)),
        (jax.ShapeDtypeStruct((n, MLA_ROPE), F32), pl.BlockSpec((tm, MLA_ROPE), row)),
        (jax.ShapeDtypeStruct((n, GLA_QK), F32), pl.BlockSpec((tm, GLA_QK), row)),
        (jax.ShapeDtypeStruct((n, GLA_QK), F32), pl.BlockSpec((tm, GLA_QK), row)),
        (jax.ShapeDtypeStruct((n, GLA_WIDTH), BF16), pl.BlockSpec((tm, GLA_WIDTH), row)),
        (jax.ShapeDtypeStruct((n, GLA_QK), F32), pl.BlockSpec((tm, GLA_QK), row)),
        (jax.ShapeDtypeStruct((n, GLA_WIDTH), BF16), pl.BlockSpec((tm, GLA_WIDTH), row)),
    ]
    return pl.pallas_call(
        functools.partial(_in_kernel, with_kv),
        grid=(nt,),
        in_specs=in_specs,
        out_specs=[o[1] for o in outs],
        out_shape=[o[0] for o in outs],
        compiler_params=_cparams(("parallel",)),
        name="inproj_prompt" if with_kv else "inproj_sample",
    )(x, sh, sc, w["g_norm_mix"], w["w_in"], w["g_q_a"], w["wq1"], w["wq2"], w["g_kv_a"],
      w["wk"], w["wv"], w["wgk"], w["b_gk"], qc, qs, kc, ks)


def _prefill_kernel(q_ref, k_ref, v_ref, o_ref):
    qi = pl.program_id(2)
    t = ATT_TILE
    lanes = [slice(g * HEAD_PAD, (g + 1) * HEAD_PAD) for g in range(ATT_HEADS)]
    qs = [q_ref[0, :, sl] for sl in lanes]

    def update_all(carries, j, mask):
        r0 = pl.multiple_of(j * t, t)
        scores = [lax.dot_general(k_ref[0, pl.ds(r0, t), sl], q, (((1,), (1,)), ((), ())),
                                  preferred_element_type=F32) for q, sl in zip(qs, lanes)]
        stats = []
        for (m, l, acc), s in zip(carries, scores):
            if mask is not None:
                s = jnp.where(mask, s, -jnp.inf)
            m_new = jnp.maximum(m, jnp.max(s, axis=0, keepdims=True))
            p = jnp.exp2(s - m_new)
            alpha = jnp.exp2(m - m_new)
            stats.append((m_new, alpha * l + jnp.sum(p, axis=0, keepdims=True), alpha * acc, p.astype(BF16)))
        out = []
        for (m_new, l_new, acc_scaled, p), sl in zip(stats, lanes):
            pv = lax.dot_general(v_ref[0, pl.ds(r0, t), sl], p, (((0,), (0,)), ((), ())),
                                 preferred_element_type=F32)
            out.append((m_new, l_new, acc_scaled + pv))
        return tuple(out)

    def body(j, carries):
        return update_all(carries, j, None)

    init = (jnp.full((1, t), -jnp.inf, F32), jnp.zeros((1, t), F32), jnp.zeros((HEAD_PAD, t), F32))
    carries = lax.fori_loop(0, qi, body, (init,) * ATT_HEADS)
    key_pos = lax.broadcasted_iota(jnp.int32, (t, t), 0)
    qry_pos = lax.broadcasted_iota(jnp.int32, (t, t), 1)
    carries = update_all(carries, qi, key_pos <= qry_pos)
    for (m, l, acc), sl in zip(carries, lanes):
        o_ref[0, :, sl] = (acc / l).T.astype(BF16)


def mla_prefill(q, k, v, batch, seq):
    t = ATT_TILE
    q3 = q.reshape(batch, seq, MLA_PAD)
    k3 = k.reshape(batch, seq, MLA_PAD)
    v3 = v.reshape(batch, seq, MLA_PAD)
    o = pl.pallas_call(
        _prefill_kernel,
        grid=(batch, MLA_HEADS // ATT_HEADS, seq // t),
        in_specs=[pl.BlockSpec((1, t, ATT_HEADS * HEAD_PAD), lambda b, h, i: (b, i, h)),
                  pl.BlockSpec((1, seq, ATT_HEADS * HEAD_PAD), lambda b, h, i: (b, 0, h)),
                  pl.BlockSpec((1, seq, ATT_HEADS * HEAD_PAD), lambda b, h, i: (b, 0, h))],
        out_specs=pl.BlockSpec((1, t, ATT_HEADS * HEAD_PAD), lambda b, h, i: (b, i, h)),
        out_shape=jax.ShapeDtypeStruct((batch, seq, MLA_PAD), BF16),
        compiler_params=_cparams(("parallel", "parallel", "arbitrary")),
        name="mla_prefill",
    )(q3, k3, v3)
    return o.reshape(batch * seq, MLA_PAD)


def _absorb_kernel(q_ref, wuk_ref, qlat_ref, qpe_ref):
    q = q_ref[...]
    qlat_ref[0] = jnp.dot(q, wuk_ref[0], preferred_element_type=F32)
    qf = pltpu.roll(q.astype(F32), HEAD_PAD - MLA_NOPE, axis=1)
    lane = lax.broadcasted_iota(jnp.int32, qf.shape, 1)
    qpe_ref[0] = jnp.where(lane < MLA_ROPE, qf, 0.0)


def absorb_q(q, wuk_t):
    r = q.shape[0]
    return pl.pallas_call(
        _absorb_kernel,
        grid=(MLA_HEADS,),
        in_specs=[pl.BlockSpec((r, HEAD_PAD), lambda h: (0, h)),
                  pl.BlockSpec((1, HEAD_PAD, MLA_KV_LORA), lambda h: (h, 0, 0))],
        out_specs=[pl.BlockSpec((1, r, MLA_KV_LORA), lambda h: (h, 0, 0)),
                   pl.BlockSpec((1, r, HEAD_PAD), lambda h: (h, 0, 0))],
        out_shape=[jax.ShapeDtypeStruct((MLA_HEADS, r, MLA_KV_LORA), F32),
                   jax.ShapeDtypeStruct((MLA_HEADS, r, HEAD_PAD), F32)],
        compiler_params=_cparams(("parallel",)),
        name="absorb_q",
    )(q, wuk_t)


def _decode_kernel(t_new, pt_ref, qlat_ref, qpe_ref, cnew_ref, knew_ref, ckv_hbm, kpe_hbm, o_ref,
                   cbuf, pbuf, sem):
    b = pl.program_id(0)
    nb = pl.num_programs(0)
    n_pages = cbuf.shape[1]
    slot = lax.rem(b, 2)
    rows = MLA_HEADS * t_new

    def page_copies(bb, sl, p):
        pg = pt_ref[bb, p]
        return (pltpu.make_async_copy(ckv_hbm.at[0, pg], cbuf.at[sl, p], sem.at[0, sl]),
                pltpu.make_async_copy(kpe_hbm.at[0, pg], pbuf.at[sl, p], sem.at[1, sl]))

    def fetch(bb, sl):
        def body(p, c):
            for cp in page_copies(bb, sl, p):
                cp.start()
            return c
        lax.fori_loop(0, n_pages, body, 0)

    @pl.when(b == 0)
    def _():
        fetch(0, 0)

    @pl.when(b + 1 < nb)
    def _():
        fetch(b + 1, 1 - slot)

    pltpu.make_async_copy(ckv_hbm.at[0, pl.ds(0, n_pages)], cbuf.at[slot], sem.at[0, slot]).wait()
    pltpu.make_async_copy(kpe_hbm.at[0, pl.ds(0, n_pages)], pbuf.at[slot], sem.at[1, slot]).wait()

    qlat = qlat_ref[...].reshape(rows, MLA_KV_LORA).astype(BF16)
    qpe = qpe_ref[...].reshape(rows, HEAD_PAD)[:, :MLA_ROPE].astype(BF16)
    dn = (((1,), (1,)), ((), ()))

    span_pages = n_pages // DEC_SPANS
    values, scores = [], []
    for c in range(DEC_SPANS):
        pages = range(c * span_pages, (c + 1) * span_pages)
        kb = jnp.concatenate([cbuf[slot, p].astype(BF16) for p in pages], axis=0)
        pb = jnp.concatenate([pbuf[slot, p].astype(BF16) for p in pages], axis=1)
        values.append(kb)
        scores.append(lax.dot_general(qlat, kb, dn, preferred_element_type=F32)
                      + jnp.dot(qpe, pb, preferred_element_type=F32))
    cn = cnew_ref[0].astype(BF16)
    kn = knew_ref[0].astype(BF16)
    sn = (lax.dot_general(qlat, cn, dn, preferred_element_type=F32)
          + lax.dot_general(qpe, kn, dn, preferred_element_type=F32))
    tq = lax.broadcasted_iota(jnp.int32, sn.shape, 0) % t_new
    tk = lax.broadcasted_iota(jnp.int32, sn.shape, 1)
    values.append(cn)
    scores.append(jnp.where(tk <= tq, sn, -jnp.inf))

    maxes = [jnp.max(s, axis=-1, keepdims=True) for s in scores]
    m = maxes[0]
    for pm in maxes[1:]:
        m = jnp.maximum(m, pm)
    probs = [jnp.exp2(s - pm) for s, pm in zip(scores, maxes)]
    l = jnp.zeros_like(m)
    acc = jnp.zeros((rows, MLA_KV_LORA), F32)
    for p, pm, vals in zip(probs, maxes, values):
        scale = jnp.exp2(pm - m)
        l = l + scale * jnp.sum(p, axis=-1, keepdims=True)
        acc = acc + scale * jnp.dot(p.astype(BF16), vals, preferred_element_type=F32)
    o_ref[0] = acc / l


def mla_decode(qlat, qpe, ckv_new, kpe_new, cache_ckv, cache_kpe, page_table, dec_batch, t_new):
    n_pages = page_table.shape[1]
    past_len = n_pages * PAGE_SIZE
    rows = MLA_HEADS * t_new
    kpe_t = jnp.swapaxes(cache_kpe, 2, 3)
    qlat4 = qlat.reshape(MLA_HEADS, dec_batch, t_new, MLA_KV_LORA)
    qpe4 = qpe.reshape(MLA_HEADS, dec_batch, t_new, HEAD_PAD)
    t_pad = NEW_PAD
    pad_new = lambda a: jnp.pad(a.reshape(dec_batch, t_new, a.shape[-1]), ((0, 0), (0, t_pad - t_new), (0, 0)))
    cnew = pad_new(ckv_new)
    knew = pad_new(kpe_new)

    in_specs = [
        pl.BlockSpec((MLA_HEADS, 1, t_new, MLA_KV_LORA), lambda b, pt: (0, b, 0, 0)),
        pl.BlockSpec((MLA_HEADS, 1, t_new, HEAD_PAD), lambda b, pt: (0, b, 0, 0)),
        pl.BlockSpec((1, t_pad, MLA_KV_LORA), lambda b, pt: (b, 0, 0)),
        pl.BlockSpec((1, t_pad, MLA_ROPE), lambda b, pt: (b, 0, 0)),
        pl.BlockSpec(memory_space=pl.ANY),
        pl.BlockSpec(memory_space=pl.ANY),
    ]
    grid_spec = pltpu.PrefetchScalarGridSpec(
        num_scalar_prefetch=1,
        grid=(dec_batch,),
        in_specs=in_specs,
        out_specs=pl.BlockSpec((1, rows, MLA_KV_LORA), lambda b, pt: (b, 0, 0)),
        scratch_shapes=[pltpu.VMEM((2, n_pages, PAGE_SIZE, MLA_KV_LORA), F32),
                        pltpu.VMEM((2, n_pages, MLA_ROPE, PAGE_SIZE), F32),
                        pltpu.SemaphoreType.DMA((2, 2))],
    )
    return pl.pallas_call(
        functools.partial(_decode_kernel, t_new),
        grid_spec=grid_spec,
        out_shape=jax.ShapeDtypeStruct((dec_batch, rows, MLA_KV_LORA), F32),
        compiler_params=_cparams(("arbitrary",)),
        name="mla_decode",
    )(page_table, qlat4, qpe4, cnew, knew, cache_ckv, kpe_t)


def _uv_kernel(o_ref, wuv_ref, out_ref):
    o = o_ref[...]
    o = o.reshape(o.shape[0] * o.shape[2], MLA_KV_LORA).astype(BF16)
    out_ref[...] = jnp.dot(o, wuv_ref[0], preferred_element_type=F32).astype(BF16)


def latent_to_values(o_lat, wuv, dec_batch, t_new):
    o4 = o_lat.reshape(dec_batch, MLA_HEADS, t_new, MLA_KV_LORA)
    return pl.pallas_call(
        _uv_kernel,
        grid=(MLA_HEADS,),
        in_specs=[pl.BlockSpec((dec_batch, 1, t_new, MLA_KV_LORA), lambda h: (0, h, 0, 0)),
                  pl.BlockSpec((1, MLA_KV_LORA, HEAD_PAD), lambda h: (h, 0, 0))],
        out_specs=pl.BlockSpec((dec_batch * t_new, HEAD_PAD), lambda h: (0, h)),
        out_shape=jax.ShapeDtypeStruct((dec_batch * t_new, MLA_PAD), BF16),
        compiler_params=_cparams(("parallel",)),
        name="latent_to_values",
    )(o4, wuv)


def _gla_kernel(nsb, q_ref, k_ref, la_ref, v_ref, s0_ref, o_ref, sfin_ref, st_sc, kv_sc, sall_sc):
    blk = pl.program_id(1)
    nblk = pl.num_programs(1)
    c = GLA_CHUNK
    tb = q_ref.shape[1]
    nc = tb // c
    cps = nc // nsb
    head_k = [slice(hh * GLA_DK, (hh + 1) * GLA_DK) for hh in range(GLA_HEADS)]
    head_v = [slice(hh * GLA_DV, (hh + 1) * GLA_DV) for hh in range(GLA_HEADS)]

    @pl.when(blk == 0)
    def _():
        for sq in range(nsb):
            for hh in range(GLA_HEADS):
                st_sc[sq, hh] = s0_ref[sq, hh].T

    q = q_ref[0]
    k = k_ref[0]
    la = la_ref[0]
    v = v_ref[0]

    r = lax.broadcasted_iota(jnp.int32, (tb, tb), 0)
    cc = lax.broadcasted_iota(jnp.int32, (tb, tb), 1)
    same = (r >> 4) == (cc >> 4)
    tri = jnp.where(same & (cc <= r), 1.0, 0.0)
    ones = jnp.where(same, 1.0, 0.0)
    sel = jnp.concatenate([tri, ones], axis=0).astype(BF16)
    la_hi = la.astype(BF16)
    rest = la - la_hi.astype(F32)
    la_mid = rest.astype(BF16)
    la_lo = (rest - la_mid.astype(F32)).astype(BF16)
    sums = (jnp.dot(sel, la_hi, preferred_element_type=F32)
            + (jnp.dot(sel, la_mid, preferred_element_type=F32)
               + jnp.dot(sel, la_lo, preferred_element_type=F32)))
    b = sums[:tb]
    b_last = sums[tb:]
    qd = (q * jnp.exp(b)).astype(BF16)
    kd = (k * jnp.exp(b_last - b)).astype(BF16)
    dec = jnp.exp(b_last)

    key_head = lax.broadcasted_iota(jnp.int32, (GLA_QK, GLA_HEADS * c), 0) >> 6
    col = lax.broadcasted_iota(jnp.int32, (GLA_QK, GLA_HEADS * c), 1)
    b3 = b.reshape(nc, c, GLA_QK)
    k3 = k.reshape(nc, c, GLA_QK)
    q3 = q.reshape(nc, c, GLA_QK)
    pos = lax.broadcasted_iota(jnp.int32, (nc, c, GLA_QK), 1)
    scores = jnp.zeros((tb, GLA_HEADS * c), F32)
    for j in range(c):
        e = jnp.exp(b3 - b3[:, j:j + 1, :])
        w = jnp.where(pos >= j, q3 * k3[:, j:j + 1, :] * e, 0.0)
        place = jnp.where(col == key_head * c + j, 1.0, 0.0).astype(BF16)
        scores = scores + jnp.dot(w.reshape(tb, GLA_QK).astype(BF16), place, preferred_element_type=F32)
    scores = scores.astype(BF16)

    for ci in range(nc):
        rs = slice(ci * c, (ci + 1) * c)
        for hh in range(GLA_HEADS):
            o_ref[0, rs, head_v[hh]] = jnp.dot(scores[rs, hh * c:(hh + 1) * c], v[rs, head_v[hh]],
                                               preferred_element_type=F32)
            kv_sc[ci, hh] = lax.dot_general(v[rs, head_v[hh]], kd[rs, head_k[hh]], (((0,), (0,)), ((), ())),
                                            preferred_element_type=F32)

    for sq in range(nsb):
        states = [st_sc[sq, hh] for hh in range(GLA_HEADS)]
        for cj in range(cps):
            ci = sq * cps + cj
            dec_row = dec[ci * c:ci * c + 1, :]
            for hh in range(GLA_HEADS):
                sall_sc[ci, hh] = states[hh].astype(BF16)
                states[hh] = states[hh] * dec_row[:, head_k[hh]] + kv_sc[ci, hh]
        for hh in range(GLA_HEADS):
            st_sc[sq, hh] = states[hh]

    for ci in range(nc):
        rs = slice(ci * c, (ci + 1) * c)
        outs = [lax.dot_general(qd[rs, head_k[hh]], sall_sc[ci, hh], (((1,), (1,)), ((), ())),
                                preferred_element_type=F32) for hh in range(GLA_HEADS)]
        o_ref[0, rs, :] += jnp.concatenate(outs, axis=1)

    @pl.when(blk == nblk - 1)
    def _():
        for sq in range(nsb):
            for hh in range(GLA_HEADS):
                sfin_ref[sq, hh] = st_sc[sq, hh].T


def gla(gq, gk, la, gv, s0, n_seq, seq_len):
    tb = GLA_BLOCK
    nsb = max(1, tb // seq_len)
    nblk = max(1, seq_len // tb)
    n_outer = n_seq // nsb
    nc = tb // GLA_CHUNK
    sh3 = lambda a: a.reshape(n_outer, nblk * tb, a.shape[-1])
    row = lambda b, i: (b, i, 0)
    st_spec = pl.BlockSpec((nsb, GLA_HEADS, GLA_DK, GLA_DV), lambda b, i: (b, 0, 0, 0))
    o, s_fin = pl.pallas_call(
        functools.partial(_gla_kernel, nsb),
        grid=(n_outer, nblk),
        in_specs=[pl.BlockSpec((1, tb, GLA_QK), row), pl.BlockSpec((1, tb, GLA_QK), row),
                  pl.BlockSpec((1, tb, GLA_QK), row), pl.BlockSpec((1, tb, GLA_WIDTH), row), st_spec],
        out_specs=[pl.BlockSpec((1, tb, GLA_WIDTH), row), st_spec],
        out_shape=[jax.ShapeDtypeStruct((n_outer, nblk * tb, GLA_WIDTH), F32),
                   jax.ShapeDtypeStruct((n_seq, GLA_HEADS, GLA_DK, GLA_DV), F32)],
        scratch_shapes=[pltpu.VMEM((nsb, GLA_HEADS, GLA_DV, GLA_DK), F32),
                        pltpu.VMEM((nc, GLA_HEADS, GLA_DV, GLA_DK), F32),
                        pltpu.VMEM((nc, GLA_HEADS, GLA_DV, GLA_DK), BF16)],
        compiler_params=_cparams(("parallel", "arbitrary")),
        name="gla_prompt" if nblk > 1 else "gla_sample",
    )(sh3(gq), sh3(gk), sh3(la), sh3(gv), s0)
    return o.reshape(n_seq * seq_len, GLA_WIDTH), s_fin


def _mix_kernel(x_ref, om_ref, og_ref, gg_ref, gta_ref, shf_ref, scf_ref, gm_ref, ggl_ref, wo_ref,
                gn_ref, wr_ref, br_ref, h2a_in, h2b_in, x1_ref, h2a_ref, h2b_ref, ti_ref, tg_ref, hist_ref):
    del h2a_in, h2b_in
    om = om_ref[...].astype(F32)
    ms = jnp.sum(om * om, axis=-1, keepdims=True) * (1.0 / (MLA_HEADS * MLA_V))
    om = om * lax.rsqrt(ms + EPS) * gm_ref[...]
    og = og_ref[...]
    gg = gg_ref[...].astype(F32)
    gate = gg * jax.nn.sigmoid(gg)
    parts = []
    for hh in range(GLA_HEADS):
        sl = slice(hh * GLA_DV, (hh + 1) * GLA_DV)
        parts.append(_rms(og[:, sl], ggl_ref[...]) * gate[:, sl])
    mix = jnp.concatenate([om] + parts, axis=1).astype(BF16)
    x1 = x_ref[...] + gta_ref[0] * jnp.dot(mix, wo_ref[...], preferred_element_type=F32)
    x1_ref[...] = x1
    h2 = _rms(x1, gn_ref[...]) * (1.0 + scf_ref[0]) + shf_ref[0]
    words = _pack_rows(h2)
    h2a_ref[...] = words[:, :PERM_W]
    h2b_ref[...] = words[:, PERM_W:]
    h_hi = h2.astype(BF16)
    h_lo = (h2 - h_hi.astype(F32)).astype(BF16)
    logits = (jnp.dot(h_hi, wr_ref[0], preferred_element_type=F32)
              + (jnp.dot(h_hi, wr_ref[1], preferred_element_type=F32)
                 + jnp.dot(h_lo, wr_ref[0], preferred_element_type=F32))) + br_ref[...]
    lane_i = lax.broadcasted_iota(jnp.int32, logits.shape, 1)
    lane = lane_i.astype(F32)
    vals = []
    idxs = []
    for _ in range(TOP_K):
        mx = jnp.max(logits, axis=-1, keepdims=True)
        ix = jnp.min(jnp.where(logits == mx, lane, float(LANES)), axis=-1, keepdims=True)
        vals.append(mx)
        idxs.append(ix)
        logits = jnp.where(lane == ix, -jnp.inf, logits)
    ex = [jnp.exp(vv - vals[0]) for vv in vals]
    den = ex[0] + ex[1] + ex[2] + ex[3]
    ti = jnp.zeros(logits.shape, F32)
    tg = jnp.zeros(logits.shape, F32)
    onehot = jnp.zeros(logits.shape, F32)
    for kk in range(TOP_K):
        ti = jnp.where(lane_i == kk, idxs[kk], ti)
        tg = jnp.where(lane_i == kk, ex[kk] / den, tg)
        onehot = onehot + jnp.where(lane == idxs[kk], 1.0, 0.0)
    ti_ref[...] = ti.astype(jnp.int32)
    tg_ref[...] = tg
    hist_ref[0] = jnp.sum(onehot, axis=0, keepdims=True)


def mixer_out(x, o_mla, o_gla, gg, gta, shf, scf, mod_per_row, w, n_all, row0, h2_buf):
    n = x.shape[0]
    tm = ROW_TILE
    t0 = row0 // tm
    row = lambda i: (i, 0)
    const = lambda i: (0, 0)
    extra_specs = [pl.BlockSpec(memory_space=pl.ANY)] * 2
    extra_args = list(h2_buf)
    if mod_per_row:
        mod_spec = pl.BlockSpec((1, tm, D_MODEL), lambda i: (0, i, 0))
    else:
        tiles_per_seq = n // gta.shape[0] // tm
        mod_spec = pl.BlockSpec((1, 1, D_MODEL), lambda i: (i // tiles_per_seq, 0, 0))
    d_mix = MLA_PAD + GLA_WIDTH
    n_in = 13
    return pl.pallas_call(
        _mix_kernel,
        grid=(n // tm,),
        in_specs=[pl.BlockSpec((tm, D_MODEL), row), pl.BlockSpec((tm, MLA_PAD), row),
                  pl.BlockSpec((tm, GLA_WIDTH), row), pl.BlockSpec((tm, GLA_WIDTH), row),
                  mod_spec, mod_spec, mod_spec,
                  pl.BlockSpec((1, MLA_PAD), const), pl.BlockSpec((1, GLA_DV), const),
                  pl.BlockSpec((d_mix, D_MODEL), const), pl.BlockSpec((1, D_MODEL), const),
                  pl.BlockSpec((2, D_MODEL, LANES), lambda i: (0, 0, 0)),
                  pl.BlockSpec((1, LANES), const)] + extra_specs,
        out_specs=[pl.BlockSpec((tm, D_MODEL), row),
                   pl.BlockSpec((tm, PERM_W), lambda i: (i + t0, 0)),
                   pl.BlockSpec((tm, PERM_W), lambda i: (i + t0, 0)),
                   pl.BlockSpec((tm, LANES), row), pl.BlockSpec((tm, LANES), row),
                   pl.BlockSpec((1, 1, LANES), lambda i: (i, 0, 0))],
        out_shape=[jax.ShapeDtypeStruct((n, D_MODEL), F32),
                   jax.ShapeDtypeStruct((n_all, PERM_W), jnp.uint32),
                   jax.ShapeDtypeStruct((n_all, PERM_W), jnp.uint32),
                   jax.ShapeDtypeStruct((n, LANES), jnp.int32), jax.ShapeDtypeStruct((n, LANES), F32),
                   jax.ShapeDtypeStruct((n // tm, 1, LANES), F32)],
        input_output_aliases={n_in: 1, n_in + 1: 2},
        compiler_params=_cparams(("parallel",)),
        name="mixer_sample" if mod_per_row else "mixer_prompt",
    )(x, o_mla, o_gla, gg, gta, shf, scf, w["g_mla_pad"], w["g_gla_out"], w["w_o_pad"],
      w["g_norm_ffn"], w["w_router_pad"], w["b_router_pad"], *extra_args)


def _rank_kernel(ti_ref, base_ref, dest_ref):
    ti = ti_ref[...]
    tm = ti.shape[0]
    lane = lax.broadcasted_iota(jnp.int32, ti.shape, 1)
    cols = [ti[:, kk:kk + 1] for kk in range(TOP_K)]
    onehot = jnp.zeros(ti.shape, F32)
    for kk in range(TOP_K):
        onehot = onehot + jnp.where(lane == cols[kk], 1.0, 0.0)
    r = lax.broadcasted_iota(jnp.int32, (tm, tm), 0)
    c = lax.broadcasted_iota(jnp.int32, (tm, tm), 1)
    earlier = jnp.where(c < r, 1.0, 0.0).astype(BF16)
    pos = jnp.dot(earlier, onehot.astype(BF16), preferred_element_type=F32) + base_ref[0]
    out = jnp.zeros(ti.shape, F32)
    for kk in range(TOP_K):
        dk = jnp.sum(jnp.where(lane == cols[kk], pos, 0.0), axis=-1, keepdims=True)
        out = jnp.where(lane == kk, dk, out)
    dest_ref[...] = out.astype(jnp.int32)


def route_rank(ti, base):
    n = ti.shape[0]
    tm = RANK_TILE
    return pl.pallas_call(
        _rank_kernel,
        grid=(n // tm,),
        in_specs=[pl.BlockSpec((tm, LANES), lambda i: (i, 0)),
                  pl.BlockSpec((1, 1, LANES), lambda i: (i, 0, 0))],
        out_specs=pl.BlockSpec((tm, LANES), lambda i: (i, 0)),
        out_shape=jax.ShapeDtypeStruct((n, LANES), jnp.int32),
        compiler_params=_cparams(("parallel",)),
        name="route_rank",
    )(ti, base)


def _route_tables(hist, n_tok):
    tm = MOE_TILE
    h = hist[:, 0, :].astype(jnp.int32)
    h = h.reshape(-1, RANK_TILE // ROW_TILE, LANES).sum(axis=1)
    counts = jnp.sum(h, axis=0)
    padded = (counts + tm - 1) // tm * tm
    pad_ends = jnp.cumsum(padded)
    pad_starts = pad_ends - padded
    base = (pad_starts[None, :] + jnp.cumsum(h, axis=0) - h).astype(F32)[:, None, :]
    n_blocks = pl.cdiv(n_tok * TOP_K, tm) + N_EXPERTS
    n_active = (pad_ends[N_EXPERTS - 1] // tm).astype(jnp.int32)
    blk = jnp.arange(n_blocks, dtype=jnp.int32)
    blk_c = jnp.minimum(blk, n_active - 1)
    ends = pad_ends[:N_EXPERTS]
    block_e = jnp.minimum(jnp.sum((ends[None, :] <= (blk_c * tm)[:, None]).astype(jnp.int32), axis=1),
                          N_EXPERTS - 1).astype(jnp.int32)
    used_end = (pad_starts + counts)[:N_EXPERTS][block_e]
    block_rows = jnp.where(blk < n_active, jnp.clip(used_end - blk * tm, 0, tm), 0).astype(jnp.int32)
    return base, block_e, block_rows, n_blocks


def _sc_mesh():
    return plsc.VectorSubcoreMesh(core_axis_name="c", subcore_axis_name="s")


def sc_dispatch(x_rows, idx, n_out):
    n, wd = x_rows.shape
    win = SC_WIN
    nwin = n // win

    @functools.partial(pl.kernel, out_type=jax.ShapeDtypeStruct((n_out, wd), x_rows.dtype),
                       mesh=_sc_mesh(), scratch_types=[])
    def k(x_hbm, i_hbm, o_hbm):
        def body(x_vmem, i_vmem):
            pltpu.sync_copy(x_vmem, o_hbm.at[i_vmem.at[0]])

        pltpu.emit_pipeline(
            body,
            grid=(idx.shape[1] // win,),
            in_specs=[pl.BlockSpec((win, wd), lambda i: (i % nwin, 0)),
                      pl.BlockSpec((1, win), lambda i: (0, i))],
            out_specs=[],
            core_axis_name=("c", "s"),
            dimension_semantics=(pltpu.PARALLEL,),
        )(x_hbm, i_hbm)

    return k(x_rows, idx)


def sc_combine(y_rows, idx):
    wd = y_rows.shape[1]
    m = idx.shape[1]
    win = SC_WIN

    @functools.partial(pl.kernel, out_type=jax.ShapeDtypeStruct((m, wd), y_rows.dtype),
                       mesh=_sc_mesh(), scratch_types=[])
    def k(y_hbm, i_hbm, o_hbm):
        def body(i_vmem, o_vmem):
            pltpu.sync_copy(y_hbm.at[i_vmem.at[0]], o_vmem)

        pltpu.emit_pipeline(
            body,
            grid=(m // win,),
            in_specs=[pl.BlockSpec((1, win), lambda i: (0, i))],
            out_specs=[pl.BlockSpec((win, wd), lambda i: (i, 0))],
            core_axis_name=("c", "s"),
            dimension_semantics=(pltpu.PARALLEL,),
        )(i_hbm, o_hbm)

    return k(y_rows, idx)


def _moe_kernel(be_ref, nr_ref, xa_ref, xb_ref, wup_ref, bup_ref, wdn_ref, bdn_ref, ya_ref, yb_ref,
                wup_sc, wdn_sc):
    i = pl.program_id(0)
    n_real = nr_ref[i]
    prev = be_ref[jnp.maximum(i - 1, 0)]
    fresh = (i == 0) | (be_ref[i] != prev)

    @pl.when((n_real > 0) & fresh)
    def _():
        wup_sc[...] = wup_ref[0].astype(BF16)
        wdn_sc[...] = wdn_ref[0].astype(BF16)

    n_sub = MOE_TILE // MOE_SUB
    live_subs = (n_real + (MOE_SUB - 1)) // MOE_SUB
    for live in range(n_sub + 1):
        m = live * MOE_SUB

        @pl.when(live_subs == live)
        def _():
            if m > 0:
                xb = _unpack_rows(jnp.concatenate([xa_ref[:m, :], xb_ref[:m, :]], axis=1)).astype(BF16)
                hu = jnp.dot(xb, wup_sc[...], preferred_element_type=F32) + bup_ref[0]
                gate = jnp.minimum(hu[:, :D_FF], SWIGLU_LIMIT)
                lin = jnp.clip(hu[:, D_FF:], -SWIGLU_LIMIT, SWIGLU_LIMIT)
                act = gate * jax.nn.sigmoid(SWIGLU_ALPHA * gate) * (lin + 1.0)
                y = jnp.dot(act.astype(BF16), wdn_sc[...], preferred_element_type=F32) + bdn_ref[0]
                words = _pack_rows(y)
                ya_ref[:m, :] = words[:, :PERM_W]
                yb_ref[:m, :] = words[:, PERM_W:]
            if m < MOE_TILE:
                ya_ref[m:, :] = jnp.zeros((MOE_TILE - m, PERM_W), jnp.uint32)
                yb_ref[m:, :] = jnp.zeros((MOE_TILE - m, PERM_W), jnp.uint32)


def moe_experts(xs_a, xs_b, block_e, block_rows, w_up, b_up, w_down, b_down):
    n_rows = xs_a.shape[0]
    tm = MOE_TILE
    n_blocks = n_rows // tm
    emap3 = lambda i, be, na: (be[i], 0, 0)
    grid_spec = pltpu.PrefetchScalarGridSpec(
        num_scalar_prefetch=2,
        grid=(n_blocks,),
        in_specs=[pl.BlockSpec((tm, PERM_W), lambda i, be, na: (i, 0)),
                  pl.BlockSpec((tm, PERM_W), lambda i, be, na: (i, 0)),
                  pl.BlockSpec((1, D_MODEL, 2 * D_FF), emap3),
                  pl.BlockSpec((1, 1, 2 * D_FF), emap3),
                  pl.BlockSpec((1, D_FF, D_MODEL), emap3),
                  pl.BlockSpec((1, 1, D_MODEL), emap3)],
        out_specs=[pl.BlockSpec((tm, PERM_W), lambda i, be, na: (i, 0)),
                   pl.BlockSpec((tm, PERM_W), lambda i, be, na: (i, 0))],
        scratch_shapes=[pltpu.VMEM((D_MODEL, 2 * D_FF), BF16), pltpu.VMEM((D_FF, D_MODEL), BF16)],
    )
    return pl.pallas_call(
        _moe_kernel,
        grid_spec=grid_spec,
        out_shape=[jax.ShapeDtypeStruct((n_rows, PERM_W), jnp.uint32)] * 2,
        compiler_params=_cparams(("arbitrary",)),
        name="moe_experts",
    )(block_e, block_rows, xs_a, xs_b, w_up, b_up.reshape(N_EXPERTS, 1, 2 * D_FF), w_down,
      b_down.reshape(N_EXPERTS, 1, D_MODEL))


def _final_kernel(x1_ref, yga_ref, ygb_ref, tg_ref, gtf_ref, sh_ref, sc_ref, g_ref, y_ref):
    tg = tg_ref[...]
    moe = jnp.zeros(x1_ref.shape, F32)
    for kk in range(TOP_K):
        moe = moe + tg[:, kk:kk + 1] * _unpack_rows(jnp.concatenate([yga_ref[kk], ygb_ref[kk]], axis=1))
    x2 = x1_ref[...] + gtf_ref[0] * moe
    y_ref[...] = _rms(x2, g_ref[...]) * (1.0 + sc_ref[0]) + sh_ref[0]


def final_out(x1, yg_a, yg_b, tg, row0, gtf, sh, sc, mod_per_row, g_final):
    n = x1.shape[0]
    tm = ROW_TILE
    t0 = row0 // tm
    row = lambda i: (i, 0)
    if mod_per_row:
        mod_spec = pl.BlockSpec((1, tm, D_MODEL), lambda i: (0, i, 0))
    else:
        tiles_per_seq = n // gtf.shape[0] // tm
        mod_spec = pl.BlockSpec((1, 1, D_MODEL), lambda i: (i // tiles_per_seq, 0, 0))
    return pl.pallas_call(
        _final_kernel,
        grid=(n // tm,),
        in_specs=[pl.BlockSpec((tm, D_MODEL), row),
                  pl.BlockSpec((TOP_K, tm, PERM_W), lambda i: (0, i + t0, 0)),
                  pl.BlockSpec((TOP_K, tm, PERM_W), lambda i: (0, i + t0, 0)),
                  pl.BlockSpec((tm, LANES), lambda i: (i + t0, 0)),
                  mod_spec, mod_spec, mod_spec,
                  pl.BlockSpec((1, D_MODEL), lambda i: (0, 0))],
        out_specs=pl.BlockSpec((tm, D_MODEL), row),
        out_shape=jax.ShapeDtypeStruct((n, D_MODEL), F32),
        compiler_params=_cparams(("parallel",)),
        name="final_sample" if mod_per_row else "final_prompt",
    )(x1, yg_a, yg_b, tg, gtf, sh, sc, g_final)


def _prep_weights(w_in, g_q_a, w_q_b, g_kv_a, w_kv_b, w_gk_b, b_gk, g_mla_out, g_gla_out, w_o,
                  g_norm_mix, g_norm_ffn, w_router, b_router):
    sizes = (MLA_Q_LORA, MLA_KV_LORA, MLA_ROPE, GLA_QK, GLA_QK, GLA_WIDTH, GLA_GATE_RANK, GLA_WIDTH)
    offs = np.cumsum((0,) + sizes)
    part = lambda i: w_in[:, offs[i]:offs[i + 1]]
    half = MLA_ROPE // 2
    k_rope = part(2)
    misc = jnp.concatenate([k_rope, part(6), jnp.zeros((D_MODEL, LANES - MLA_ROPE - GLA_GATE_RANK), F32)], 1)
    swap = jnp.concatenate([-k_rope[:, half:], k_rope[:, :half],
                            jnp.zeros((D_MODEL, LANES - MLA_ROPE), F32)], 1)
    w_in_pad = jnp.concatenate([part(0), part(1), part(3), part(4), part(5), part(7), misc, swap], 1)

    pad_q = jnp.zeros((MLA_Q_LORA, MLA_HEADS, HEAD_PAD - MLA_NOPE - MLA_ROPE), F32)
    wq1 = jnp.concatenate([w_q_b, pad_q], axis=2)
    q_lo = w_q_b[:, :, MLA_NOPE:MLA_NOPE + half]
    q_hi = w_q_b[:, :, MLA_NOPE + half:]
    wq2 = jnp.concatenate([jnp.zeros((MLA_Q_LORA, MLA_HEADS, MLA_NOPE), F32), -q_hi, q_lo, pad_q], axis=2)
    pad_kv = jnp.zeros((MLA_KV_LORA, MLA_HEADS, HEAD_PAD - MLA_NOPE), F32)
    w_uk = w_kv_b[:, :, :MLA_NOPE]
    w_uv = w_kv_b[:, :, MLA_NOPE:]
    wk = jnp.concatenate([w_uk, pad_kv], axis=2)
    wv = jnp.concatenate([w_uv, pad_kv], axis=2)
    wuk_t = jnp.concatenate([jnp.transpose(w_uk, (1, 2, 0)),
                             jnp.zeros((MLA_HEADS, HEAD_PAD - MLA_NOPE, MLA_KV_LORA), F32)], axis=1)
    wuv_h = jnp.transpose(wv, (1, 0, 2))
    wgk = jnp.zeros((LANES, GLA_QK), F32).at[MLA_ROPE:MLA_ROPE + GLA_GATE_RANK].set(w_gk_b)

    wo_mla = w_o[:MLA_HEADS * MLA_V].reshape(MLA_HEADS, MLA_V, D_MODEL)
    wo_mla = jnp.concatenate([wo_mla, jnp.zeros((MLA_HEADS, HEAD_PAD - MLA_V, D_MODEL), F32)], axis=1)
    w_o_pad = jnp.concatenate([wo_mla.reshape(MLA_PAD, D_MODEL), w_o[MLA_HEADS * MLA_V:]], axis=0)
    g_mla = g_mla_out.reshape(MLA_HEADS, MLA_V)
    g_mla_pad = jnp.concatenate([g_mla, jnp.zeros((MLA_HEADS, HEAD_PAD - MLA_V), F32)], 1).reshape(1, MLA_PAD)
    w_router_pad = jnp.concatenate([w_router, jnp.zeros((D_MODEL, LANES - N_EXPERTS), F32)], axis=1)
    wr_hi = w_router_pad.astype(BF16)
    w_router_pad = jnp.stack([wr_hi, (w_router_pad - wr_hi.astype(F32)).astype(BF16)])
    b_router_pad = jnp.concatenate([b_router, jnp.full((LANES - N_EXPERTS,), -jnp.inf, F32)]).reshape(1, LANES)
    return dict(
        w_in=w_in_pad.astype(BF16), g_norm_mix=g_norm_mix.reshape(1, D_MODEL),
        g_q_a=g_q_a.reshape(1, MLA_Q_LORA),
        wq1=wq1.reshape(MLA_Q_LORA, MLA_PAD).astype(BF16), wq2=wq2.reshape(MLA_Q_LORA, MLA_PAD).astype(BF16),
        g_kv_a=g_kv_a.reshape(1, MLA_KV_LORA),
        wk=wk.reshape(MLA_KV_LORA, MLA_PAD).astype(BF16), wv=wv.reshape(MLA_KV_LORA, MLA_PAD).astype(BF16),
        wuk_t=wuk_t.astype(BF16), wuv_h=wuv_h.astype(BF16),
        wgk=wgk.astype(BF16), b_gk=b_gk.reshape(1, GLA_QK),
        g_mla_pad=g_mla_pad, g_gla_out=g_gla_out.reshape(1, GLA_DV), w_o_pad=w_o_pad.astype(BF16),
        g_norm_ffn=g_norm_ffn.reshape(1, D_MODEL), w_router_pad=w_router_pad, b_router_pad=b_router_pad,
    )


def _rope_tables(pos, reps):
    half = MLA_ROPE // 2
    inv = ROPE_THETA ** (-jnp.arange(half, dtype=F32) / half)
    ang = pos.astype(F32)[:, None] * inv
    cos, sin = jnp.cos(ang), jnp.sin(ang)
    n = pos.shape[0]
    qc = jnp.concatenate([jnp.full((n, MLA_NOPE), Q_SCALE, F32), Q_SCALE * cos, Q_SCALE * cos,
                          jnp.zeros((n, HEAD_PAD - MLA_NOPE - MLA_ROPE), F32)], axis=1)
    qs = jnp.concatenate([jnp.zeros((n, MLA_NOPE), F32), Q_SCALE * sin, Q_SCALE * sin,
                          jnp.zeros((n, HEAD_PAD - MLA_NOPE - MLA_ROPE), F32)], axis=1)
    kc = jnp.concatenate([cos, cos, jnp.zeros((n, LANES - MLA_ROPE), F32)], axis=1)
    ks = jnp.concatenate([sin, sin, jnp.zeros((n, LANES - MLA_ROPE), F32)], axis=1)
    return tuple(jnp.tile(t, (reps, 1)) for t in (qc, qs, kc, ks))


def kernel(x_prompt, x_sample, cache_ckv, cache_kpe, state_gla, page_table, c_prompt, c_sample, w_ada, b_ada, g_norm_mix, w_in, g_q_a, w_q_b, g_kv_a, w_kv_b, w_gk_b, b_gk, g_mla_out, g_gla_out, w_o, g_norm_ffn, w_router, b_router, w_up, b_up, w_down, b_down, g_norm_final, w_ada_final, b_ada_final):
    B, S, D = x_prompt.shape
    DB, T, _ = x_sample.shape
    depth = w_ada.shape[0]
    assert depth == 1
    past_len = page_table.shape[1] * cache_ckv.shape[2]
    n_p, n_s = B * S, DB * T
    l = 0

    w = _prep_weights(w_in[l], g_q_a[l], w_q_b[l], g_kv_a[l], w_kv_b[l], w_gk_b[l], b_gk[l],
                      g_mla_out[l], g_gla_out[l], w_o[l], g_norm_mix[l], g_norm_ffn[l],
                      w_router[l], b_router[l])

    n_c = B + DB
    n_c_pad = (n_c + 7) // 8 * 8
    c_all = jnp.concatenate([c_prompt, c_sample, jnp.zeros((n_c_pad - n_c, D), F32)], axis=0)
    mod = ada_terms(c_all, w_ada[l], b_ada[l])
    mod_f = ada_terms(c_all, w_ada_final, b_ada_final)
    term = lambda m, i: m[:, i * D:(i + 1) * D]
    p_term = lambda m, i: term(m, i)[:B].reshape(B, 1, D)
    s_term = lambda m, i: jnp.broadcast_to(term(m, i)[B:n_c][:, None, :], (DB, T, D)).reshape(1, n_s, D)

    xp = x_prompt.reshape(n_p, D)
    xs = x_sample.reshape(n_s, D)
    tabs_p = _rope_tables(jnp.arange(S), 1)
    tabs_s = _rope_tables(past_len + jnp.arange(T), ROW_TILE // T)

    (q_p, k_p, v_p, ckv_p, kpe_p, gq_p, gk_p, gv_p, la_p, gg_p) = in_proj(
        xp, p_term(mod, 0), p_term(mod, 1), False, tabs_p, w, True)
    o_mla_p = mla_prefill(q_p, k_p, v_p, B, S)
    s0 = jnp.zeros((B, GLA_HEADS, GLA_DK, GLA_DV), F32)
    o_gla_p, gla_p = gla(gq_p, gk_p, la_p, gv_p, s0, B, S)
    n_all = n_p + n_s
    h2_init = (jnp.zeros((n_all, PERM_W), jnp.uint32), jnp.zeros((n_all, PERM_W), jnp.uint32))
    x1_p, h2a, h2b, ti_p, tg_p, hist_p = mixer_out(xp, o_mla_p, o_gla_p.reshape(n_p, GLA_WIDTH), gg_p,
                                                   p_term(mod, 2), p_term(mod, 3), p_term(mod, 4), False, w,
                                                   n_all, 0, h2_init)

    (q_s, ckv_s, kpe_s, gq_s, gk_s, gv_s, la_s, gg_s) = in_proj(
        xs, s_term(mod, 0), s_term(mod, 1), True, tabs_s, w, False)
    qlat, qpe = absorb_q(q_s, w["wuk_t"])
    o_lat = mla_decode(qlat, qpe, ckv_s, kpe_s, cache_ckv, cache_kpe, page_table, DB, T)
    o_mla_s = latent_to_values(o_lat, w["wuv_h"], DB, T)
    tpad = GLA_CHUNK
    padt = lambda a: jnp.pad(a.reshape(DB, T, a.shape[-1]), ((0, 0), (0, tpad - T), (0, 0))).reshape(
        DB * tpad, a.shape[-1])
    o_gla_s, gla_s = gla(padt(gq_s), padt(gk_s), padt(la_s), padt(gv_s), state_gla[l], DB, tpad)
    o_gla_s = o_gla_s.reshape(DB, tpad, GLA_WIDTH)[:, :T].reshape(n_s, GLA_WIDTH)
    x1_s, h2a, h2b, ti_s, tg_s, hist_s = mixer_out(xs, o_mla_s, o_gla_s, gg_s,
                                                   s_term(mod, 2), s_term(mod, 3), s_term(mod, 4), True, w,
                                                   n_all, n_p, (h2a, h2b))

    ti = jnp.concatenate([ti_p, ti_s], axis=0)
    tg = jnp.concatenate([tg_p, tg_s], axis=0)
    base, block_e, block_rows, n_blocks = _route_tables(jnp.concatenate([hist_p, hist_s], axis=0), n_all)
    dest = route_rank(ti, base)
    idx = dest[:, :TOP_K].T.reshape(1, TOP_K * n_all)
    n_rows = n_blocks * MOE_TILE
    xs_a = sc_dispatch(h2a, idx, n_rows)
    xs_b = sc_dispatch(h2b, idx, n_rows)
    ys_a, ys_b = moe_experts(xs_a, xs_b, block_e, block_rows, w_up[l], b_up[l], w_down[l], b_down[l])
    yg_a = sc_combine(ys_a, idx).reshape(TOP_K, n_all, PERM_W)
    yg_b = sc_combine(ys_b, idx).reshape(TOP_K, n_all, PERM_W)

    g_fin = g_norm_final.reshape(1, D)
    y_p = final_out(x1_p, yg_a, yg_b, tg, 0, p_term(mod, 5), p_term(mod_f, 0), p_term(mod_f, 1), False, g_fin)
    y_s = final_out(x1_s, yg_a, yg_b, tg, n_p, s_term(mod, 5), s_term(mod_f, 0), s_term(mod_f, 1), True, g_fin)

    return (y_p.reshape(B, S, D), y_s.reshape(DB, T, D),
            ckv_p.reshape(1, B, S, MLA_KV_LORA), kpe_p.reshape(1, B, S, MLA_ROPE), gla_p[None],
            ckv_s.reshape(1, DB, T, MLA_KV_LORA), kpe_s.reshape(1, DB, T, MLA_ROPE), gla_s[None])
```

```python
import functools
import math

import jax
import jax.numpy as jnp
import numpy as np
from jax import lax
from jax.experimental import pallas as pl
from jax.experimental.pallas import tpu as pltpu
from jax.experimental.pallas import tpu_sc as plsc

F32 = jnp.float32
BF16 = jnp.bfloat16

D_MODEL = 1024
MLA_HEADS = 8
MLA_NOPE = 64
MLA_ROPE = 32
MLA_V = 64
MLA_Q_LORA = 384
MLA_KV_LORA = 256
MLA_SCALE = (MLA_NOPE + MLA_ROPE) ** -0.5
ROPE_THETA = 10000.0
GLA_HEADS = 4
GLA_DK = 64
GLA_DV = 128
GLA_GATE_RANK = 16
GLA_GATE_NORM = 16.0
GLA_CHUNK = 16
GLA_QK = GLA_HEADS * GLA_DK
GLA_WIDTH = GLA_HEADS * GLA_DV
N_EXPERTS = 32
TOP_K = 4
D_FF = D_MODEL
SWIGLU_LIMIT = 7.0
SWIGLU_ALPHA = 1.702
N_MOD = 6
EPS = 1e-6
PAGE_SIZE = 128

LANES = 128
HEAD_PAD = LANES
MLA_PAD = MLA_HEADS * HEAD_PAD
VMEM_LIMIT = 56 * 1024 * 1024

Q_SCALE = MLA_SCALE * math.log2(math.e)

C_QA = 0
C_KV = C_QA + MLA_Q_LORA
C_GQ = C_KV + MLA_KV_LORA
C_GK = C_GQ + GLA_QK
C_GV = C_GK + GLA_QK
C_GG = C_GV + GLA_WIDTH
C_MISC = C_GG + GLA_WIDTH
C_SWAP = C_MISC + LANES
D_IN_PAD = C_SWAP + LANES

ROW_TILE = 512
RANK_TILE = 1024
ATT_TILE = 512
ATT_HEADS = 4
GLA_BLOCK = 256
DEC_SPANS = 4
SC_WIN = 128
PERM_W = D_MODEL // 4
MOE_TILE = 768
MOE_SUB = 256
NEW_PAD = 16


def _cparams(sem):
    return pltpu.CompilerParams(dimension_semantics=sem, vmem_limit_bytes=VMEM_LIMIT)


def _rms(x, g):
    return x * lax.rsqrt(jnp.mean(x * x, axis=-1, keepdims=True) + EPS) * g


def _mod_spec(term, seq_rows, row_off, tm):
    if seq_rows >= tm:
        tiles_per_seq = seq_rows // tm
        return pl.BlockSpec((1, 1, D_MODEL), lambda i: (row_off + i // tiles_per_seq, 0, term))
    g = tm // seq_rows
    assert row_off % g == 0
    return pl.BlockSpec((g, 1, D_MODEL), lambda i: (row_off // g + i, 0, term))


def _mod_rows(ref, rows):
    m = ref[...]
    g = m.shape[0]
    if g == 1:
        return m[0]
    return jnp.broadcast_to(m, (g, rows // g, m.shape[2])).reshape(rows, m.shape[2])


def _pack_rows(x):
    bits = lax.bitcast_convert_type(x.astype(BF16).astype(F32), jnp.uint32)
    w = x.shape[1] // 2
    return (bits[:, :w] >> 16) | bits[:, w:]


def _unpack_rows(words):
    lo = lax.bitcast_convert_type(words << 16, F32)
    hi = lax.bitcast_convert_type(words & jnp.uint32(0xFFFF0000), F32)
    return jnp.concatenate([lo, hi], axis=1)


def _ada_kernel(c_ref, w_ref, b_ref, o_ref):
    c = c_ref[...]
    a = (c * jax.nn.sigmoid(c)).astype(BF16)
    o_ref[...] = jnp.dot(a, w_ref[...].astype(BF16), preferred_element_type=F32) + b_ref[...]


def ada_terms(c, w, b):
    rows, d = c.shape
    n = w.shape[1]
    tn = 512
    return pl.pallas_call(
        _ada_kernel,
        grid=(n // tn,),
        in_specs=[pl.BlockSpec((rows, d), lambda j: (0, 0)),
                  pl.BlockSpec((d, tn), lambda j: (0, j)),
                  pl.BlockSpec((1, tn), lambda j: (0, j))],
        out_specs=pl.BlockSpec((rows, tn), lambda j: (0, j)),
        out_shape=jax.ShapeDtypeStruct((rows, n), F32),
        compiler_params=_cparams(("arbitrary",)),
        name="ada_terms",
    )(c, w, b.reshape(1, n))


def _in_kernel(with_kv, x_ref, sh_ref, sc_ref, g_ref, win_ref, gqa_ref, wq1_ref, wq2_ref,
               gkv_ref, wk_ref, wv_ref, wgk_ref, bgk_ref, qc_ref, qs_ref, kc_ref, ks_ref, *outs):
    if with_kv:
        q_out, k_out, v_out, ckv_out, kpe_out, gq_out, gk_out, gv_out, la_out, gg_out = outs
    else:
        q_out, ckv_out, kpe_out, gq_out, gk_out, gv_out, la_out, gg_out = outs
    x = x_ref[...]
    h = _rms(x, g_ref[...])
    rows = x.shape[0]
    h = h * (1.0 + _mod_rows(sc_ref, rows)) + _mod_rows(sh_ref, rows)
    proj = jnp.dot(h.astype(BF16), win_ref[...], preferred_element_type=F32)

    qn = _rms(proj[:, C_QA:C_KV], gqa_ref[...]).astype(BF16)
    qa = jnp.dot(qn, wq1_ref[...], preferred_element_type=F32)
    qb = jnp.dot(qn, wq2_ref[...], preferred_element_type=F32)
    qc = qc_ref[...]
    qs = qs_ref[...]
    for hh in range(MLA_HEADS):
        sl = slice(hh * HEAD_PAD, (hh + 1) * HEAD_PAD)
        q_out[:, sl] = (qa[:, sl] * qc + qb[:, sl] * qs).astype(BF16)

    ckv = _rms(proj[:, C_KV:C_GQ], gkv_ref[...])
    ckv_out[...] = ckv
    kpe = proj[:, C_MISC:C_SWAP] * kc_ref[...] + proj[:, C_SWAP:D_IN_PAD] * ks_ref[...]
    kpe_out[...] = kpe[:, :MLA_ROPE]
    if with_kv:
        ckv_b = ckv.astype(BF16)
        kn = jnp.dot(ckv_b, wk_ref[...], preferred_element_type=F32)
        kpe_sh = pltpu.roll(kpe, MLA_NOPE, axis=1)
        for hh in range(MLA_HEADS):
            sl = slice(hh * HEAD_PAD, (hh + 1) * HEAD_PAD)
            k_out[:, sl] = (kn[:, sl] + kpe_sh).astype(BF16)
        v_out[...] = jnp.dot(ckv_b, wv_ref[...], preferred_element_type=F32).astype(BF16)

    gq_out[...] = proj[:, C_GQ:C_GK] * (GLA_DK ** -0.5)
    gk_out[...] = proj[:, C_GK:C_GV]
    gv_out[...] = proj[:, C_GV:C_GG].astype(BF16)
    gg_out[...] = proj[:, C_GG:C_MISC].astype(BF16)
    xg = jnp.dot(proj[:, C_MISC:C_SWAP].astype(BF16), wgk_ref[...],
                 preferred_element_type=F32) + bgk_ref[...]
    la_out[...] = (jnp.minimum(xg, 0.0) - jnp.log(1.0 + jnp.exp(-jnp.abs(xg)))) * (1.0 / GLA_GATE_NORM)


def in_proj(x, mod, seq_rows, row_off, tabs, wts, with_kv):
    n = x.shape[0]
    tm = ROW_TILE
    nt = n // tm
    qc, qs, kc, ks = tabs
    n_tab = qc.shape[0] // tm
    row = lambda i: (i, 0)
    const = lambda i: (0, 0)
    tab_spec = pl.BlockSpec((tm, LANES), lambda i: (i % n_tab, 0))
    w = wts
    in_specs = [
        pl.BlockSpec((tm, D_MODEL), row),
        _mod_spec(0, seq_rows, row_off, tm), _mod_spec(1, seq_rows, row_off, tm),
        pl.BlockSpec((1, D_MODEL), const),
        pl.BlockSpec((D_MODEL, D_IN_PAD), const),
        pl.BlockSpec((1, MLA_Q_LORA), const),
        pl.BlockSpec((MLA_Q_LORA, MLA_PAD), const),
        pl.BlockSpec((MLA_Q_LORA, MLA_PAD), const),
        pl.BlockSpec((1, MLA_KV_LORA), const),
        pl.BlockSpec((MLA_KV_LORA, MLA_PAD), const),
        pl.BlockSpec((MLA_KV_LORA, MLA_PAD), const),
        pl.BlockSpec((LANES, GLA_QK), const),
        pl.BlockSpec((1, GLA_QK), const),
        tab_spec, tab_spec, tab_spec, tab_spec,
    ]
    wide = lambda dt: (jax.ShapeDtypeStruct((n, MLA_PAD), dt), pl.BlockSpec((tm, MLA_PAD), row))
    outs = [wide(BF16)]
    if with_kv:
        outs += [wide(BF16), wide(BF16)]
    outs += [
        (jax.ShapeDtypeStruct((n, MLA_KV_LORA), F32), pl.BlockSpec((tm, MLA_KV_LORA), row)),
        (jax.ShapeDtypeStruct((n, MLA_ROPE), F32), pl.BlockSpec((tm, MLA_ROPE), row)),
        (jax.ShapeDtypeStruct((n, GLA_QK), F32), pl.BlockSpec((tm, GLA_QK), row)),
        (jax.ShapeDtypeStruct((n, GLA_QK), F32), pl.BlockSpec((tm, GLA_QK), row)),
        (jax.ShapeDtypeStruct((n, GLA_WIDTH), BF16), pl.BlockSpec((tm, GLA_WIDTH), row)),
        (jax.ShapeDtypeStruct((n, GLA_QK), F32), pl.BlockSpec((tm, GLA_QK), row)),
        (jax.ShapeDtypeStruct((n, GLA_WIDTH), BF16), pl.BlockSpec((tm, GLA_WIDTH), row)),
    ]
    return pl.pallas_call(
        functools.partial(_in_kernel, with_kv),
        grid=(nt,),
        in_specs=in_specs,
        out_specs=[o[1] for o in outs],
        out_shape=[o[0] for o in outs],
        compiler_params=_cparams(("parallel",)),
        name="inproj_prompt" if with_kv else "inproj_sample",
    )(x, mod, mod, w["g_norm_mix"], w["w_in"], w["g_q_a"], w["wq1"], w["wq2"], w["g_kv_a"],
      w["wk"], w["wv"], w["wgk"], w["b_gk"], qc, qs, kc, ks)


def _prefill_kernel(q_ref, k_ref, v_ref, o_ref):
    qi = pl.program_id(2)
    t = ATT_TILE
    lanes = [slice(g * HEAD_PAD, (g + 1) * HEAD_PAD) for g in range(ATT_HEADS)]
    qs = [q_ref[0, :, sl] for sl in lanes]

    def update_all(carries, j, mask):
        r0 = pl.multiple_of(j * t, t)
        scores = [lax.dot_general(k_ref[0, pl.ds(r0, t), sl], q, (((1,), (1,)), ((), ())),
                                  preferred_element_type=F32) for q, sl in zip(qs, lanes)]
        stats = []
        for (m, l, acc), s in zip(carries, scores):
            if mask is not None:
                s = jnp.where(mask, s, -jnp.inf)
            m_new = jnp.maximum(m, jnp.max(s, axis=0, keepdims=True))
            p = jnp.exp2(s - m_new)
            alpha = jnp.exp2(m - m_new)
            stats.append((m_new, alpha * l + jnp.sum(p, axis=0, keepdims=True), alpha * acc, p.astype(BF16)))
        out = []
        for (m_new, l_new, acc_scaled, p), sl in zip(stats, lanes):
            pv = lax.dot_general(v_ref[0, pl.ds(r0, t), sl], p, (((0,), (0,)), ((), ())),
                                 preferred_element_type=F32)
            out.append((m_new, l_new, acc_scaled + pv))
        return tuple(out)

    def body(j, carries):
        return update_all(carries, j, None)

    init = (jnp.full((1, t), -jnp.inf, F32), jnp.zeros((1, t), F32), jnp.zeros((HEAD_PAD, t), F32))
    carries = lax.fori_loop(0, qi, body, (init,) * ATT_HEADS)
    key_pos = lax.broadcasted_iota(jnp.int32, (t, t), 0)
    qry_pos = lax.broadcasted_iota(jnp.int32, (t, t), 1)
    carries = update_all(carries, qi, key_pos <= qry_pos)
    for (m, l, acc), sl in zip(carries, lanes):
        o_ref[0, :, sl] = (acc / l).T.astype(BF16)


def mla_prefill(q, k, v, batch, seq):
    t = ATT_TILE
    q3 = q.reshape(batch, seq, MLA_PAD)
    k3 = k.reshape(batch, seq, MLA_PAD)
    v3 = v.reshape(batch, seq, MLA_PAD)
    o = pl.pallas_call(
        _prefill_kernel,
        grid=(batch, MLA_HEADS // ATT_HEADS, seq // t),
        in_specs=[pl.BlockSpec((1, t, ATT_HEADS * HEAD_PAD), lambda b, h, i: (b, i, h)),
                  pl.BlockSpec((1, seq, ATT_HEADS * HEAD_PAD), lambda b, h, i: (b, 0, h)),
                  pl.BlockSpec((1, seq, ATT_HEADS * HEAD_PAD), lambda b, h, i: (b, 0, h))],
        out_specs=pl.BlockSpec((1, t, ATT_HEADS * HEAD_PAD), lambda b, h, i: (b, i, h)),
        out_shape=jax.ShapeDtypeStruct((batch, seq, MLA_PAD), BF16),
        compiler_params=_cparams(("parallel", "parallel", "arbitrary")),
        name="mla_prefill",
    )(q3, k3, v3)
    return o.reshape(batch * seq, MLA_PAD)


def _absorb_kernel(q_ref, wuk_ref, qlat_ref, qpe_ref):
    q = q_ref[...]
    qlat_ref[0] = jnp.dot(q, wuk_ref[0], preferred_element_type=F32)
    qf = pltpu.roll(q.astype(F32), HEAD_PAD - MLA_NOPE, axis=1)
    lane = lax.broadcasted_iota(jnp.int32, qf.shape, 1)
    qpe_ref[0] = jnp.where(lane < MLA_ROPE, qf, 0.0)


def absorb_q(q, wuk_t):
    r = q.shape[0]
    return pl.pallas_call(
        _absorb_kernel,
        grid=(MLA_HEADS,),
        in_specs=[pl.BlockSpec((r, HEAD_PAD), lambda h: (0, h)),
                  pl.BlockSpec((1, HEAD_PAD, MLA_KV_LORA), lambda h: (h, 0, 0))],
        out_specs=[pl.BlockSpec((1, r, MLA_KV_LORA), lambda h: (h, 0, 0)),
                   pl.BlockSpec((1, r, HEAD_PAD), lambda h: (h, 0, 0))],
        out_shape=[jax.ShapeDtypeStruct((MLA_HEADS, r, MLA_KV_LORA), F32),
                   jax.ShapeDtypeStruct((MLA_HEADS, r, HEAD_PAD), F32)],
        compiler_params=_cparams(("parallel",)),
        name="absorb_q",
    )(q, wuk_t)


def _decode_kernel(t_new, pt_ref, qlat_ref, qpe_ref, cnew_ref, knew_ref, ckv_hbm, kpe_hbm, o_ref,
                   cbuf, pbuf, sem):
    b = pl.program_id(0)
    nb = pl.num_programs(0)
    n_pages = cbuf.shape[1]
    slot = lax.rem(b, 2)
    rows = MLA_HEADS * t_new

    def page_copies(bb, sl, p):
        pg = pt_ref[bb, p]
        return (pltpu.make_async_copy(ckv_hbm.at[0, pg], cbuf.at[sl, p], sem.at[0, sl]),
                pltpu.make_async_copy(kpe_hbm.at[0, pg], pbuf.at[sl, p], sem.at[1, sl]))

    def fetch(bb, sl):
        def body(p, c):
            for cp in page_copies(bb, sl, p):
                cp.start()
            return c
        lax.fori_loop(0, n_pages, body, 0)

    @pl.when(b == 0)
    def _():
        fetch(0, 0)

    @pl.when(b + 1 < nb)
    def _():
        fetch(b + 1, 1 - slot)

    pltpu.make_async_copy(ckv_hbm.at[0, pl.ds(0, n_pages)], cbuf.at[slot], sem.at[0, slot]).wait()
    pltpu.make_async_copy(kpe_hbm.at[0, pl.ds(0, n_pages)], pbuf.at[slot], sem.at[1, slot]).wait()

    qlat = qlat_ref[...].reshape(rows, MLA_KV_LORA).astype(BF16)
    qpe = qpe_ref[...].reshape(rows, HEAD_PAD)[:, :MLA_ROPE].astype(BF16)
    dn = (((1,), (1,)), ((), ()))

    span_pages = n_pages // DEC_SPANS
    values, scores = [], []
    for c in range(DEC_SPANS):
        pages = range(c * span_pages, (c + 1) * span_pages)
        kb = jnp.concatenate([cbuf[slot, p].astype(BF16) for p in pages], axis=0)
        pb = jnp.concatenate([pbuf[slot, p].astype(BF16) for p in pages], axis=1)
        values.append(kb)
        scores.append(lax.dot_general(qlat, kb, dn, preferred_element_type=F32)
                      + jnp.dot(qpe, pb, preferred_element_type=F32))
    cn = cnew_ref[0].astype(BF16)
    kn = knew_ref[0].astype(BF16)
    sn = (lax.dot_general(qlat, cn, dn, preferred_element_type=F32)
          + lax.dot_general(qpe, kn, dn, preferred_element_type=F32))
    tq = lax.broadcasted_iota(jnp.int32, sn.shape, 0) % t_new
    tk = lax.broadcasted_iota(jnp.int32, sn.shape, 1)
    values.append(cn)
    scores.append(jnp.where(tk <= tq, sn, -jnp.inf))

    maxes = [jnp.max(s, axis=-1, keepdims=True) for s in scores]
    m = maxes[0]
    for pm in maxes[1:]:
        m = jnp.maximum(m, pm)
    probs = [jnp.exp2(s - pm) for s, pm in zip(scores, maxes)]
    l = jnp.zeros_like(m)
    acc = jnp.zeros((rows, MLA_KV_LORA), F32)
    for p, pm, vals in zip(probs, maxes, values):
        scale = jnp.exp2(pm - m)
        l = l + scale * jnp.sum(p, axis=-1, keepdims=True)
        acc = acc + scale * jnp.dot(p.astype(BF16), vals, preferred_element_type=F32)
    o_ref[0] = acc / l


def mla_decode(qlat, qpe, ckv_new, kpe_new, cache_ckv, cache_kpe, page_table, dec_batch, t_new):
    n_pages = page_table.shape[1]
    past_len = n_pages * PAGE_SIZE
    rows = MLA_HEADS * t_new
    kpe_t = jnp.swapaxes(cache_kpe, 2, 3)
    qlat4 = qlat.reshape(MLA_HEADS, dec_batch, t_new, MLA_KV_LORA)
    qpe4 = qpe.reshape(MLA_HEADS, dec_batch, t_new, HEAD_PAD)
    t_pad = NEW_PAD
    pad_new = lambda a: jnp.pad(a.reshape(dec_batch, t_new, a.shape[-1]), ((0, 0), (0, t_pad - t_new), (0, 0)))
    cnew = pad_new(ckv_new)
    knew = pad_new(kpe_new)

    in_specs = [
        pl.BlockSpec((MLA_HEADS, 1, t_new, MLA_KV_LORA), lambda b, pt: (0, b, 0, 0)),
        pl.BlockSpec((MLA_HEADS, 1, t_new, HEAD_PAD), lambda b, pt: (0, b, 0, 0)),
        pl.BlockSpec((1, t_pad, MLA_KV_LORA), lambda b, pt: (b, 0, 0)),
        pl.BlockSpec((1, t_pad, MLA_ROPE), lambda b, pt: (b, 0, 0)),
        pl.BlockSpec(memory_space=pl.ANY),
        pl.BlockSpec(memory_space=pl.ANY),
    ]
    grid_spec = pltpu.PrefetchScalarGridSpec(
        num_scalar_prefetch=1,
        grid=(dec_batch,),
        in_specs=in_specs,
        out_specs=pl.BlockSpec((1, rows, MLA_KV_LORA), lambda b, pt: (b, 0, 0)),
        scratch_shapes=[pltpu.VMEM((2, n_pages, PAGE_SIZE, MLA_KV_LORA), F32),
                        pltpu.VMEM((2, n_pages, MLA_ROPE, PAGE_SIZE), F32),
                        pltpu.SemaphoreType.DMA((2, 2))],
    )
    return pl.pallas_call(
        functools.partial(_decode_kernel, t_new),
        grid_spec=grid_spec,
        out_shape=jax.ShapeDtypeStruct((dec_batch, rows, MLA_KV_LORA), F32),
        compiler_params=_cparams(("arbitrary",)),
        name="mla_decode",
    )(page_table, qlat4, qpe4, cnew, knew, cache_ckv, kpe_t)


def _uv_kernel(o_ref, wuv_ref, out_ref):
    o = o_ref[...]
    o = o.reshape(o.shape[0] * o.shape[2], MLA_KV_LORA).astype(BF16)
    out_ref[...] = jnp.dot(o, wuv_ref[0], preferred_element_type=F32).astype(BF16)


def latent_to_values(o_lat, wuv, dec_batch, t_new):
    o4 = o_lat.reshape(dec_batch, MLA_HEADS, t_new, MLA_KV_LORA)
    return pl.pallas_call(
        _uv_kernel,
        grid=(MLA_HEADS,),
        in_specs=[pl.BlockSpec((dec_batch, 1, t_new, MLA_KV_LORA), lambda h: (0, h, 0, 0)),
                  pl.BlockSpec((1, MLA_KV_LORA, HEAD_PAD), lambda h: (h, 0, 0))],
        out_specs=pl.BlockSpec((dec_batch * t_new, HEAD_PAD), lambda h: (0, h)),
        out_shape=jax.ShapeDtypeStruct((dec_batch * t_new, MLA_PAD), BF16),
        compiler_params=_cparams(("parallel",)),
        name="latent_to_values",
    )(o4, wuv)


def _gla_kernel(nsb, q_ref, k_ref, la_ref, v_ref, s0_ref, o_ref, sfin_ref, st_sc, kv_sc, sall_sc):
    blk = pl.program_id(1)
    nblk = pl.num_programs(1)
    c = GLA_CHUNK
    tb = q_ref.shape[1]
    nc = tb // c
    cps = nc // nsb
    head_k = [slice(hh * GLA_DK, (hh + 1) * GLA_DK) for hh in range(GLA_HEADS)]
    head_v = [slice(hh * GLA_DV, (hh + 1) * GLA_DV) for hh in range(GLA_HEADS)]

    @pl.when(blk == 0)
    def _():
        for sq in range(nsb):
            for hh in range(GLA_HEADS):
                st_sc[sq, hh] = s0_ref[sq, hh].T

    q = q_ref[0]
    k = k_ref[0]
    la = la_ref[0]
    v = v_ref[0]

    r = lax.broadcasted_iota(jnp.int32, (tb, tb), 0)
    cc = lax.broadcasted_iota(jnp.int32, (tb, tb), 1)
    same = (r >> 4) == (cc >> 4)
    tri = jnp.where(same & (cc <= r), 1.0, 0.0)
    ones = jnp.where(same, 1.0, 0.0)
    sel = jnp.concatenate([tri, ones], axis=0).astype(BF16)
    la_hi = la.astype(BF16)
    rest = la - la_hi.astype(F32)
    la_mid = rest.astype(BF16)
    la_lo = (rest - la_mid.astype(F32)).astype(BF16)
    sums = (jnp.dot(sel, la_hi, preferred_element_type=F32)
            + (jnp.dot(sel, la_mid, preferred_element_type=F32)
               + jnp.dot(sel, la_lo, preferred_element_type=F32)))
    b = sums[:tb]
    b_last = sums[tb:]
    qd = (q * jnp.exp(b)).astype(BF16)
    kd = (k * jnp.exp(b_last - b)).astype(BF16)
    dec = jnp.exp(b_last)

    key_head = lax.broadcasted_iota(jnp.int32, (GLA_QK, GLA_HEADS * c), 0) >> 6
    col = lax.broadcasted_iota(jnp.int32, (GLA_QK, GLA_HEADS * c), 1)
    b3 = b.reshape(nc, c, GLA_QK)
    k3 = k.reshape(nc, c, GLA_QK)
    q3 = q.reshape(nc, c, GLA_QK)
    pos = lax.broadcasted_iota(jnp.int32, (nc, c, GLA_QK), 1)
    scores = jnp.zeros((tb, GLA_HEADS * c), F32)
    for j in range(c):
        e = jnp.exp(b3 - b3[:, j:j + 1, :])
        w = jnp.where(pos >= j, q3 * k3[:, j:j + 1, :] * e, 0.0)
        place = jnp.where(col == key_head * c + j, 1.0, 0.0).astype(BF16)
        scores = scores + jnp.dot(w.reshape(tb, GLA_QK).astype(BF16), place, preferred_element_type=F32)
    scores = scores.astype(BF16)

    for ci in range(nc):
        rs = slice(ci * c, (ci + 1) * c)
        for hh in range(GLA_HEADS):
            o_ref[0, rs, head_v[hh]] = jnp.dot(scores[rs, hh * c:(hh + 1) * c], v[rs, head_v[hh]],
                                               preferred_element_type=F32)
            kv_sc[ci, hh] = lax.dot_general(v[rs, head_v[hh]], kd[rs, head_k[hh]], (((0,), (0,)), ((), ())),
                                            preferred_element_type=F32)

    for sq in range(nsb):
        states = [st_sc[sq, hh] for hh in range(GLA_HEADS)]
        for cj in range(cps):
            ci = sq * cps + cj
            dec_row = dec[ci * c:ci * c + 1, :]
            for hh in range(GLA_HEADS):
                sall_sc[ci, hh] = states[hh].astype(BF16)
                states[hh] = states[hh] * dec_row[:, head_k[hh]] + kv_sc[ci, hh]
        for hh in range(GLA_HEADS):
            st_sc[sq, hh] = states[hh]

    for ci in range(nc):
        rs = slice(ci * c, (ci + 1) * c)
        outs = [lax.dot_general(qd[rs, head_k[hh]], sall_sc[ci, hh], (((1,), (1,)), ((), ())),
                                preferred_element_type=F32) for hh in range(GLA_HEADS)]
        o_ref[0, rs, :] += jnp.concatenate(outs, axis=1)

    @pl.when(blk == nblk - 1)
    def _():
        for sq in range(nsb):
            for hh in range(GLA_HEADS):
                sfin_ref[sq, hh] = st_sc[sq, hh].T


def gla(gq, gk, la, gv, s0, n_seq, seq_len):
    tb = GLA_BLOCK
    nsb = max(1, tb // seq_len)
    nblk = max(1, seq_len // tb)
    n_outer = n_seq // nsb
    nc = tb // GLA_CHUNK
    sh3 = lambda a: a.reshape(n_outer, nblk * tb, a.shape[-1])
    row = lambda b, i: (b, i, 0)
    st_spec = pl.BlockSpec((nsb, GLA_HEADS, GLA_DK, GLA_DV), lambda b, i: (b, 0, 0, 0))
    o, s_fin = pl.pallas_call(
        functools.partial(_gla_kernel, nsb),
        grid=(n_outer, nblk),
        in_specs=[pl.BlockSpec((1, tb, GLA_QK), row), pl.BlockSpec((1, tb, GLA_QK), row),
                  pl.BlockSpec((1, tb, GLA_QK), row), pl.BlockSpec((1, tb, GLA_WIDTH), row), st_spec],
        out_specs=[pl.BlockSpec((1, tb, GLA_WIDTH), row), st_spec],
        out_shape=[jax.ShapeDtypeStruct((n_outer, nblk * tb, GLA_WIDTH), F32),
                   jax.ShapeDtypeStruct((n_seq, GLA_HEADS, GLA_DK, GLA_DV), F32)],
        scratch_shapes=[pltpu.VMEM((nsb, GLA_HEADS, GLA_DV, GLA_DK), F32),
                        pltpu.VMEM((nc, GLA_HEADS, GLA_DV, GLA_DK), F32),
                        pltpu.VMEM((nc, GLA_HEADS, GLA_DV, GLA_DK), BF16)],
        compiler_params=_cparams(("parallel", "arbitrary")),
        name="gla_prompt" if nblk > 1 else "gla_sample",
    )(sh3(gq), sh3(gk), sh3(la), sh3(gv), s0)
    return o.reshape(n_seq * seq_len, GLA_WIDTH), s_fin


def _mix_kernel(x_ref, om_ref, og_ref, gg_ref, gta_ref, shf_ref, scf_ref, gm_ref, ggl_ref, wo_ref,
                gn_ref, wr_ref, br_ref, h2a_in, h2b_in, x1_ref, h2a_ref, h2b_ref, ti_ref, tg_ref, hist_ref):
    del h2a_in, h2b_in
    om = om_ref[...].astype(F32)
    ms = jnp.sum(om * om, axis=-1, keepdims=True) * (1.0 / (MLA_HEADS * MLA_V))
    om = om * lax.rsqrt(ms + EPS) * gm_ref[...]
    og = og_ref[...]
    gg = gg_ref[...].astype(F32)
    gate = gg * jax.nn.sigmoid(gg)
    parts = []
    for hh in range(GLA_HEADS):
        sl = slice(hh * GLA_DV, (hh + 1) * GLA_DV)
        parts.append(_rms(og[:, sl], ggl_ref[...]) * gate[:, sl])
    mix = jnp.concatenate([om] + parts, axis=1).astype(BF16)
    rows = x_ref.shape[0]
    x1 = x_ref[...] + _mod_rows(gta_ref, rows) * jnp.dot(mix, wo_ref[...], preferred_element_type=F32)
    x1_ref[...] = x1
    h2 = _rms(x1, gn_ref[...]) * (1.0 + _mod_rows(scf_ref, rows)) + _mod_rows(shf_ref, rows)
    words = _pack_rows(h2)
    h2a_ref[...] = words[:, :PERM_W]
    h2b_ref[...] = words[:, PERM_W:]
    h_hi = h2.astype(BF16)
    h_lo = (h2 - h_hi.astype(F32)).astype(BF16)
    logits = (jnp.dot(h_hi, wr_ref[0], preferred_element_type=F32)
              + (jnp.dot(h_hi, wr_ref[1], preferred_element_type=F32)
                 + jnp.dot(h_lo, wr_ref[0], preferred_element_type=F32))) + br_ref[...]
    lane_i = lax.broadcasted_iota(jnp.int32, logits.shape, 1)
    lane = lane_i.astype(F32)
    vals = []
    idxs = []
    for _ in range(TOP_K):
        mx = jnp.max(logits, axis=-1, keepdims=True)
        ix = jnp.min(jnp.where(logits == mx, lane, float(LANES)), axis=-1, keepdims=True)
        vals.append(mx)
        idxs.append(ix)
        logits = jnp.where(lane == ix, -jnp.inf, logits)
    ex = [jnp.exp(vv - vals[0]) for vv in vals]
    den = ex[0] + ex[1] + ex[2] + ex[3]
    ti = jnp.zeros(logits.shape, F32)
    tg = jnp.zeros(logits.shape, F32)
    onehot = jnp.zeros(logits.shape, F32)
    for kk in range(TOP_K):
        ti = jnp.where(lane_i == kk, idxs[kk], ti)
        tg = jnp.where(lane_i == kk, ex[kk] / den, tg)
        onehot = onehot + jnp.where(lane == idxs[kk], 1.0, 0.0)
    ti_ref[...] = ti.astype(jnp.int32)
    tg_ref[...] = tg
    hist_ref[0] = jnp.sum(onehot, axis=0, keepdims=True)


def mixer_out(x, o_mla, o_gla, gg, mod, seq_rows, row_off, w, n_all, row0, h2_buf):
    n = x.shape[0]
    tm = ROW_TILE
    t0 = row0 // tm
    row = lambda i: (i, 0)
    const = lambda i: (0, 0)
    extra_specs = [pl.BlockSpec(memory_space=pl.ANY)] * 2
    extra_args = list(h2_buf)
    mod_specs = [_mod_spec(term, seq_rows, row_off, tm) for term in (2, 3, 4)]
    d_mix = MLA_PAD + GLA_WIDTH
    n_in = 13
    return pl.pallas_call(
        _mix_kernel,
        grid=(n // tm,),
        in_specs=[pl.BlockSpec((tm, D_MODEL), row), pl.BlockSpec((tm, MLA_PAD), row),
                  pl.BlockSpec((tm, GLA_WIDTH), row), pl.BlockSpec((tm, GLA_WIDTH), row),
                  *mod_specs,
                  pl.BlockSpec((1, MLA_PAD), const), pl.BlockSpec((1, GLA_DV), const),
                  pl.BlockSpec((d_mix, D_MODEL), const), pl.BlockSpec((1, D_MODEL), const),
                  pl.BlockSpec((2, D_MODEL, LANES), lambda i: (0, 0, 0)),
                  pl.BlockSpec((1, LANES), const)] + extra_specs,
        out_specs=[pl.BlockSpec((tm, D_MODEL), row),
                   pl.BlockSpec((tm, PERM_W), lambda i: (i + t0, 0)),
                   pl.BlockSpec((tm, PERM_W), lambda i: (i + t0, 0)),
                   pl.BlockSpec((tm, LANES), row), pl.BlockSpec((tm, LANES), row),
                   pl.BlockSpec((1, 1, LANES), lambda i: (i, 0, 0))],
        out_shape=[jax.ShapeDtypeStruct((n, D_MODEL), F32),
                   jax.ShapeDtypeStruct((n_all, PERM_W), jnp.uint32),
                   jax.ShapeDtypeStruct((n_all, PERM_W), jnp.uint32),
                   jax.ShapeDtypeStruct((n, LANES), jnp.int32), jax.ShapeDtypeStruct((n, LANES), F32),
                   jax.ShapeDtypeStruct((n // tm, 1, LANES), F32)],
        input_output_aliases={n_in: 1, n_in + 1: 2},
        compiler_params=_cparams(("parallel",)),
        name="mixer_sample" if seq_rows < tm else "mixer_prompt",
    )(x, o_mla, o_gla, gg, mod, mod, mod, w["g_mla_pad"], w["g_gla_out"], w["w_o_pad"],
      w["g_norm_ffn"], w["w_router_pad"], w["b_router_pad"], *extra_args)


def _rank_kernel(ti_ref, base_ref, dest_ref):
    ti = ti_ref[...]
    tm = ti.shape[0]
    lane = lax.broadcasted_iota(jnp.int32, ti.shape, 1)
    cols = [ti[:, kk:kk + 1] for kk in range(TOP_K)]
    onehot = jnp.zeros(ti.shape, F32)
    for kk in range(TOP_K):
        onehot = onehot + jnp.where(lane == cols[kk], 1.0, 0.0)
    r = lax.broadcasted_iota(jnp.int32, (tm, tm), 0)
    c = lax.broadcasted_iota(jnp.int32, (tm, tm), 1)
    earlier = jnp.where(c < r, 1.0, 0.0).astype(BF16)
    pos = jnp.dot(earlier, onehot.astype(BF16), preferred_element_type=F32) + base_ref[0]
    out = jnp.zeros(ti.shape, F32)
    for kk in range(TOP_K):
        dk = jnp.sum(jnp.where(lane == cols[kk], pos, 0.0), axis=-1, keepdims=True)
        out = jnp.where(lane == kk, dk, out)
    dest_ref[...] = out.astype(jnp.int32)


def route_rank(ti, base):
    n = ti.shape[0]
    tm = RANK_TILE
    return pl.pallas_call(
        _rank_kernel,
        grid=(n // tm,),
        in_specs=[pl.BlockSpec((tm, LANES), lambda i: (i, 0)),
                  pl.BlockSpec((1, 1, LANES), lambda i: (i, 0, 0))],
        out_specs=pl.BlockSpec((tm, LANES), lambda i: (i, 0)),
        out_shape=jax.ShapeDtypeStruct((n, LANES), jnp.int32),
        compiler_params=_cparams(("parallel",)),
        name="route_rank",
    )(ti, base)


def _route_tables(hist, n_tok):
    tm = MOE_TILE
    h = hist[:, 0, :].astype(jnp.int32)
    h = h.reshape(-1, RANK_TILE // ROW_TILE, LANES).sum(axis=1)
    counts = jnp.sum(h, axis=0)
    padded = (counts + tm - 1) // tm * tm
    pad_ends = jnp.cumsum(padded)
    pad_starts = pad_ends - padded
    base = (pad_starts[None, :] + jnp.cumsum(h, axis=0) - h).astype(F32)[:, None, :]
    n_blocks = pl.cdiv(n_tok * TOP_K, tm) + N_EXPERTS
    n_active = (pad_ends[N_EXPERTS - 1] // tm).astype(jnp.int32)
    blk = jnp.arange(n_blocks, dtype=jnp.int32)
    blk_c = jnp.minimum(blk, n_active - 1)
    ends = pad_ends[:N_EXPERTS]
    block_e = jnp.minimum(jnp.sum((ends[None, :] <= (blk_c * tm)[:, None]).astype(jnp.int32), axis=1),
                          N_EXPERTS - 1).astype(jnp.int32)
    used_end = (pad_starts + counts)[:N_EXPERTS][block_e]
    block_rows = jnp.where(blk < n_active, jnp.clip(used_end - blk * tm, 0, tm), 0).astype(jnp.int32)
    return base, block_e, block_rows, n_blocks


def _sc_mesh():
    return plsc.VectorSubcoreMesh(core_axis_name="c", subcore_axis_name="s")


def sc_dispatch(x_rows, idx, n_out):
    n, wd = x_rows.shape
    win = SC_WIN
    nwin = n // win

    @functools.partial(pl.kernel, out_type=jax.ShapeDtypeStruct((n_out, wd), x_rows.dtype),
                       mesh=_sc_mesh(), scratch_types=[])
    def k(x_hbm, i_hbm, o_hbm):
        def body(x_vmem, i_vmem):
            pltpu.sync_copy(x_vmem, o_hbm.at[i_vmem.at[0]])

        pltpu.emit_pipeline(
            body,
            grid=(idx.shape[1] // win,),
            in_specs=[pl.BlockSpec((win, wd), lambda i: (i % nwin, 0)),
                      pl.BlockSpec((1, win), lambda i: (0, i))],
            out_specs=[],
            core_axis_name=("c", "s"),
            dimension_semantics=(pltpu.PARALLEL,),
        )(x_hbm, i_hbm)

    return k(x_rows, idx)


def sc_combine(y_rows, idx):
    wd = y_rows.shape[1]
    m = idx.shape[1]
    win = SC_WIN

    @functools.partial(pl.kernel, out_type=jax.ShapeDtypeStruct((m, wd), y_rows.dtype),
                       mesh=_sc_mesh(), scratch_types=[])
    def k(y_hbm, i_hbm, o_hbm):
        def body(i_vmem, o_vmem):
            pltpu.sync_copy(y_hbm.at[i_vmem.at[0]], o_vmem)

        pltpu.emit_pipeline(
            body,
            grid=(m // win,),
            in_specs=[pl.BlockSpec((1, win), lambda i: (0, i))],
            out_specs=[pl.BlockSpec((win, wd), lambda i: (i, 0))],
            core_axis_name=("c", "s"),
            dimension_semantics=(pltpu.PARALLEL,),
        )(i_hbm, o_hbm)

    return k(y_rows, idx)


def _moe_kernel(be_ref, nr_ref, xa_ref, xb_ref, wup_ref, bup_ref, wdn_ref, bdn_ref, ya_ref, yb_ref,
                wup_sc, wdn_sc):
    i = pl.program_id(0)
    n_real = nr_ref[i]
    prev = be_ref[jnp.maximum(i - 1, 0)]
    fresh = (i == 0) | (be_ref[i] != prev)

    @pl.when((n_real > 0) & fresh)
    def _():
        wup_sc[...] = wup_ref[0].astype(BF16)
        wdn_sc[...] = wdn_ref[0].astype(BF16)

    n_sub = MOE_TILE // MOE_SUB
    live_subs = (n_real + (MOE_SUB - 1)) // MOE_SUB
    for live in range(n_sub + 1):
        m = live * MOE_SUB

        @pl.when(live_subs == live)
        def _():
            if m > 0:
                xb = _unpack_rows(jnp.concatenate([xa_ref[:m, :], xb_ref[:m, :]], axis=1)).astype(BF16)
                hu = jnp.dot(xb, wup_sc[...], preferred_element_type=F32) + bup_ref[0]
                gate = jnp.minimum(hu[:, :D_FF], SWIGLU_LIMIT)
                lin = jnp.clip(hu[:, D_FF:], -SWIGLU_LIMIT, SWIGLU_LIMIT)
                act = gate * jax.nn.sigmoid(SWIGLU_ALPHA * gate) * (lin + 1.0)
                y = jnp.dot(act.astype(BF16), wdn_sc[...], preferred_element_type=F32) + bdn_ref[0]
                words = _pack_rows(y)
                ya_ref[:m, :] = words[:, :PERM_W]
                yb_ref[:m, :] = words[:, PERM_W:]
            if m < MOE_TILE:
                ya_ref[m:, :] = jnp.zeros((MOE_TILE - m, PERM_W), jnp.uint32)
                yb_ref[m:, :] = jnp.zeros((MOE_TILE - m, PERM_W), jnp.uint32)


def moe_experts(xs_a, xs_b, block_e, block_rows, w_up, b_up, w_down, b_down):
    n_rows = xs_a.shape[0]
    tm = MOE_TILE
    n_blocks = n_rows // tm
    emap3 = lambda i, be, na: (be[i], 0, 0)
    grid_spec = pltpu.PrefetchScalarGridSpec(
        num_scalar_prefetch=2,
        grid=(n_blocks,),
        in_specs=[pl.BlockSpec((tm, PERM_W), lambda i, be, na: (i, 0)),
                  pl.BlockSpec((tm, PERM_W), lambda i, be, na: (i, 0)),
                  pl.BlockSpec((1, D_MODEL, 2 * D_FF), emap3),
                  pl.BlockSpec((1, 1, 2 * D_FF), emap3),
                  pl.BlockSpec((1, D_FF, D_MODEL), emap3),
                  pl.BlockSpec((1, 1, D_MODEL), emap3)],
        out_specs=[pl.BlockSpec((tm, PERM_W), lambda i, be, na: (i, 0)),
                   pl.BlockSpec((tm, PERM_W), lambda i, be, na: (i, 0))],
        scratch_shapes=[pltpu.VMEM((D_MODEL, 2 * D_FF), BF16), pltpu.VMEM((D_FF, D_MODEL), BF16)],
    )
    return pl.pallas_call(
        _moe_kernel,
        grid_spec=grid_spec,
        out_shape=[jax.ShapeDtypeStruct((n_rows, PERM_W), jnp.uint32)] * 2,
        compiler_params=_cparams(("arbitrary",)),
        name="moe_experts",
    )(block_e, block_rows, xs_a, xs_b, w_up, b_up.reshape(N_EXPERTS, 1, 2 * D_FF), w_down,
      b_down.reshape(N_EXPERTS, 1, D_MODEL))


def _final_kernel(x1_ref, yga_ref, ygb_ref, tg_ref, gtf_ref, sh_ref, sc_ref, g_ref, y_ref):
    tg = tg_ref[...]
    moe = jnp.zeros(x1_ref.shape, F32)
    for kk in range(TOP_K):
        moe = moe + tg[:, kk:kk + 1] * _unpack_rows(jnp.concatenate([yga_ref[kk], ygb_ref[kk]], axis=1))
    rows = x1_ref.shape[0]
    x2 = x1_ref[...] + _mod_rows(gtf_ref, rows) * moe
    y_ref[...] = _rms(x2, g_ref[...]) * (1.0 + _mod_rows(sc_ref, rows)) + _mod_rows(sh_ref, rows)


def final_out(x1, yg_a, yg_b, tg, row0, mod, mod_f, seq_rows, row_off, g_final):
    n = x1.shape[0]
    tm = ROW_TILE
    t0 = row0 // tm
    row = lambda i: (i, 0)
    mod_specs = [_mod_spec(5, seq_rows, row_off, tm),
                 _mod_spec(0, seq_rows, row_off, tm), _mod_spec(1, seq_rows, row_off, tm)]
    return pl.pallas_call(
        _final_kernel,
        grid=(n // tm,),
        in_specs=[pl.BlockSpec((tm, D_MODEL), row),
                  pl.BlockSpec((TOP_K, tm, PERM_W), lambda i: (0, i + t0, 0)),
                  pl.BlockSpec((TOP_K, tm, PERM_W), lambda i: (0, i + t0, 0)),
                  pl.BlockSpec((tm, LANES), lambda i: (i + t0, 0)),
                  *mod_specs,
                  pl.BlockSpec((1, D_MODEL), lambda i: (0, 0))],
        out_specs=pl.BlockSpec((tm, D_MODEL), row),
        out_shape=jax.ShapeDtypeStruct((n, D_MODEL), F32),
        compiler_params=_cparams(("parallel",)),
        name="final_sample" if seq_rows < tm else "final_prompt",
    )(x1, yg_a, yg_b, tg, mod, mod_f, mod_f, g_final)


def _prep_weights(w_in, g_q_a, w_q_b, g_kv_a, w_kv_b, w_gk_b, b_gk, g_mla_out, g_gla_out, w_o,
                  g_norm_mix, g_norm_ffn, w_router, b_router):
    sizes = (MLA_Q_LORA, MLA_KV_LORA, MLA_ROPE, GLA_QK, GLA_QK, GLA_WIDTH, GLA_GATE_RANK, GLA_WIDTH)
    offs = np.cumsum((0,) + sizes)
    part = lambda i: w_in[:, offs[i]:offs[i + 1]]
    half = MLA_ROPE // 2
    k_rope = part(2)
    misc = jnp.concatenate([k_rope, part(6), jnp.zeros((D_MODEL, LANES - MLA_ROPE - GLA_GATE_RANK), F32)], 1)
    swap = jnp.concatenate([-k_rope[:, half:], k_rope[:, :half],
                            jnp.zeros((D_MODEL, LANES - MLA_ROPE), F32)], 1)
    w_in_pad = jnp.concatenate([part(0), part(1), part(3), part(4), part(5), part(7), misc, swap], 1)

    pad_q = jnp.zeros((MLA_Q_LORA, MLA_HEADS, HEAD_PAD - MLA_NOPE - MLA_ROPE), F32)
    wq1 = jnp.concatenate([w_q_b, pad_q], axis=2)
    q_lo = w_q_b[:, :, MLA_NOPE:MLA_NOPE + half]
    q_hi = w_q_b[:, :, MLA_NOPE + half:]
    wq2 = jnp.concatenate([jnp.zeros((MLA_Q_LORA, MLA_HEADS, MLA_NOPE), F32), -q_hi, q_lo, pad_q], axis=2)
    pad_kv = jnp.zeros((MLA_KV_LORA, MLA_HEADS, HEAD_PAD - MLA_NOPE), F32)
    w_uk = w_kv_b[:, :, :MLA_NOPE]
    w_uv = w_kv_b[:, :, MLA_NOPE:]
    wk = jnp.concatenate([w_uk, pad_kv], axis=2)
    wv = jnp.concatenate([w_uv, pad_kv], axis=2)
    wuk_t = jnp.concatenate([jnp.transpose(w_uk, (1, 2, 0)),
                             jnp.zeros((MLA_HEADS, HEAD_PAD - MLA_NOPE, MLA_KV_LORA), F32)], axis=1)
    wuv_h = jnp.transpose(wv, (1, 0, 2))
    wgk = jnp.zeros((LANES, GLA_QK), F32).at[MLA_ROPE:MLA_ROPE + GLA_GATE_RANK].set(w_gk_b)

    wo_mla = w_o[:MLA_HEADS * MLA_V].reshape(MLA_HEADS, MLA_V, D_MODEL)
    wo_mla = jnp.concatenate([wo_mla, jnp.zeros((MLA_HEADS, HEAD_PAD - MLA_V, D_MODEL), F32)], axis=1)
    w_o_pad = jnp.concatenate([wo_mla.reshape(MLA_PAD, D_MODEL), w_o[MLA_HEADS * MLA_V:]], axis=0)
    g_mla = g_mla_out.reshape(MLA_HEADS, MLA_V)
    g_mla_pad = jnp.concatenate([g_mla, jnp.zeros((MLA_HEADS, HEAD_PAD - MLA_V), F32)], 1).reshape(1, MLA_PAD)
    w_router_pad = jnp.concatenate([w_router, jnp.zeros((D_MODEL, LANES - N_EXPERTS), F32)], axis=1)
    wr_hi = w_router_pad.astype(BF16)
    w_router_pad = jnp.stack([wr_hi, (w_router_pad - wr_hi.astype(F32)).astype(BF16)])
    b_router_pad = jnp.concatenate([b_router, jnp.full((LANES - N_EXPERTS,), -jnp.inf, F32)]).reshape(1, LANES)
    return dict(
        w_in=w_in_pad.astype(BF16), g_norm_mix=g_norm_mix.reshape(1, D_MODEL),
        g_q_a=g_q_a.reshape(1, MLA_Q_LORA),
        wq1=wq1.reshape(MLA_Q_LORA, MLA_PAD).astype(BF16), wq2=wq2.reshape(MLA_Q_LORA, MLA_PAD).astype(BF16),
        g_kv_a=g_kv_a.reshape(1, MLA_KV_LORA),
        wk=wk.reshape(MLA_KV_LORA, MLA_PAD).astype(BF16), wv=wv.reshape(MLA_KV_LORA, MLA_PAD).astype(BF16),
        wuk_t=wuk_t.astype(BF16), wuv_h=wuv_h.astype(BF16),
        wgk=wgk.astype(BF16), b_gk=b_gk.reshape(1, GLA_QK),
        g_mla_pad=g_mla_pad, g_gla_out=g_gla_out.reshape(1, GLA_DV), w_o_pad=w_o_pad.astype(BF16),
        g_norm_ffn=g_norm_ffn.reshape(1, D_MODEL), w_router_pad=w_router_pad, b_router_pad=b_router_pad,
    )


def _rope_tables(pos, reps):
    half = MLA_ROPE // 2
    inv = ROPE_THETA ** (-jnp.arange(half, dtype=F32) / half)
    ang = pos.astype(F32)[:, None] * inv
    cos, sin = jnp.cos(ang), jnp.sin(ang)
    n = pos.shape[0]
    qc = jnp.concatenate([jnp.full((n, MLA_NOPE), Q_SCALE, F32), Q_SCALE * cos, Q_SCALE * cos,
                          jnp.zeros((n, HEAD_PAD - MLA_NOPE - MLA_ROPE), F32)], axis=1)
    qs = jnp.concatenate([jnp.zeros((n, MLA_NOPE), F32), Q_SCALE * sin, Q_SCALE * sin,
                          jnp.zeros((n, HEAD_PAD - MLA_NOPE - MLA_ROPE), F32)], axis=1)
    kc = jnp.concatenate([cos, cos, jnp.zeros((n, LANES - MLA_ROPE), F32)], axis=1)
    ks = jnp.concatenate([sin, sin, jnp.zeros((n, LANES - MLA_ROPE), F32)], axis=1)
    return tuple(jnp.tile(t, (reps, 1)) for t in (qc, qs, kc, ks))


def kernel(x_prompt, x_sample, cache_ckv, cache_kpe, state_gla, page_table, c_prompt, c_sample, w_ada, b_ada, g_norm_mix, w_in, g_q_a, w_q_b, g_kv_a, w_kv_b, w_gk_b, b_gk, g_mla_out, g_gla_out, w_o, g_norm_ffn, w_router, b_router, w_up, b_up, w_down, b_down, g_norm_final, w_ada_final, b_ada_final):
    B, S, D = x_prompt.shape
    DB, T, _ = x_sample.shape
    depth = w_ada.shape[0]
    assert depth == 1
    past_len = page_table.shape[1] * cache_ckv.shape[2]
    n_p, n_s = B * S, DB * T
    l = 0

    w = _prep_weights(w_in[l], g_q_a[l], w_q_b[l], g_kv_a[l], w_kv_b[l], w_gk_b[l], b_gk[l],
                      g_mla_out[l], g_gla_out[l], w_o[l], g_norm_mix[l], g_norm_ffn[l],
                      w_router[l], b_router[l])

    n_c = B + DB
    n_c_pad = (n_c + 7) // 8 * 8
    c_all = jnp.concatenate([c_sample, c_prompt, jnp.zeros((n_c_pad - n_c, D), F32)], axis=0)
    mod = ada_terms(c_all, w_ada[l], b_ada[l]).reshape(n_c_pad, 1, N_MOD * D)
    mod_f = ada_terms(c_all, w_ada_final, b_ada_final).reshape(n_c_pad, 1, 2 * D)
    off_s, off_p = 0, DB

    xp = x_prompt.reshape(n_p, D)
    xs = x_sample.reshape(n_s, D)
    tabs_p = _rope_tables(jnp.arange(S), 1)
    tabs_s = _rope_tables(past_len + jnp.arange(T), ROW_TILE // T)

    (q_p, k_p, v_p, ckv_p, kpe_p, gq_p, gk_p, gv_p, la_p, gg_p) = in_proj(
        xp, mod, S, off_p, tabs_p, w, True)
    o_mla_p = mla_prefill(q_p, k_p, v_p, B, S)
    s0 = jnp.zeros((B, GLA_HEADS, GLA_DK, GLA_DV), F32)
    o_gla_p, gla_p = gla(gq_p, gk_p, la_p, gv_p, s0, B, S)
    n_all = n_p + n_s
    h2_init = (jnp.zeros((n_all, PERM_W), jnp.uint32), jnp.zeros((n_all, PERM_W), jnp.uint32))
    x1_p, h2a, h2b, ti_p, tg_p, hist_p = mixer_out(xp, o_mla_p, o_gla_p.reshape(n_p, GLA_WIDTH), gg_p,
                                                   mod, S, off_p, w, n_all, 0, h2_init)

    (q_s, ckv_s, kpe_s, gq_s, gk_s, gv_s, la_s, gg_s) = in_proj(
        xs, mod, T, off_s, tabs_s, w, False)
    qlat, qpe = absorb_q(q_s, w["wuk_t"])
    o_lat = mla_decode(qlat, qpe, ckv_s, kpe_s, cache_ckv, cache_kpe, page_table, DB, T)
    o_mla_s = latent_to_values(o_lat, w["wuv_h"], DB, T)
    tpad = GLA_CHUNK
    padt = lambda a: jnp.pad(a.reshape(DB, T, a.shape[-1]), ((0, 0), (0, tpad - T), (0, 0))).reshape(
        DB * tpad, a.shape[-1])
    o_gla_s, gla_s = gla(padt(gq_s), padt(gk_s), padt(la_s), padt(gv_s), state_gla[l], DB, tpad)
    o_gla_s = o_gla_s.reshape(DB, tpad, GLA_WIDTH)[:, :T].reshape(n_s, GLA_WIDTH)
    x1_s, h2a, h2b, ti_s, tg_s, hist_s = mixer_out(xs, o_mla_s, o_gla_s, gg_s,
                                                   mod, T, off_s, w, n_all, n_p, (h2a, h2b))

    ti = jnp.concatenate([ti_p, ti_s], axis=0)
    tg = jnp.concatenate([tg_p, tg_s], axis=0)
    base, block_e, block_rows, n_blocks = _route_tables(jnp.concatenate([hist_p, hist_s], axis=0), n_all)
    dest = route_rank(ti, base)
    idx = dest[:, :TOP_K].T.reshape(1, TOP_K * n_all)
    n_rows = n_blocks * MOE_TILE
    xs_a = sc_dispatch(h2a, idx, n_rows)
    xs_b = sc_dispatch(h2b, idx, n_rows)
    ys_a, ys_b = moe_experts(xs_a, xs_b, block_e, block_rows, w_up[l], b_up[l], w_down[l], b_down[l])
    yg_a = sc_combine(ys_a, idx).reshape(TOP_K, n_all, PERM_W)
    yg_b = sc_combine(ys_b, idx).reshape(TOP_K, n_all, PERM_W)

    g_fin = g_norm_final.reshape(1, D)
    y_p = final_out(x1_p, yg_a, yg_b, tg, 0, mod, mod_f, S, off_p, g_fin)
    y_s = final_out(x1_s, yg_a, yg_b, tg, n_p, mod, mod_f, T, off_s, g_fin)

    return (y_p.reshape(B, S, D), y_s.reshape(DB, T, D),
            ckv_p.reshape(1, B, S, MLA_KV_LORA), kpe_p.reshape(1, B, S, MLA_ROPE), gla_p[None],
            ckv_s.reshape(1, DB, T, MLA_KV_LORA), kpe_s.reshape(1, DB, T, MLA_ROPE), gla_s[None])
```

```python
import functools
import math

import jax
import jax.numpy as jnp
import numpy as np
from jax import lax
from jax.experimental import pallas as pl
from jax.experimental.pallas import tpu as pltpu
from jax.experimental.pallas import tpu_sc as plsc

F32 = jnp.float32
BF16 = jnp.bfloat16

D_MODEL = 1024
MLA_HEADS = 8
MLA_NOPE = 64
MLA_ROPE = 32
MLA_V = 64
MLA_Q_LORA = 384
MLA_KV_LORA = 256
MLA_SCALE = (MLA_NOPE + MLA_ROPE) ** -0.5
ROPE_THETA = 10000.0
GLA_HEADS = 4
GLA_DK = 64
GLA_DV = 128
GLA_GATE_RANK = 16
GLA_GATE_NORM = 16.0
GLA_CHUNK = 16
GLA_QK = GLA_HEADS * GLA_DK
GLA_WIDTH = GLA_HEADS * GLA_DV
N_EXPERTS = 32
TOP_K = 4
D_FF = D_MODEL
SWIGLU_LIMIT = 7.0
SWIGLU_ALPHA = 1.702
N_MOD = 6
EPS = 1e-6
PAGE_SIZE = 128

LANES = 128
HEAD_PAD = LANES
MLA_PAD = MLA_HEADS * HEAD_PAD
VMEM_LIMIT = 56 * 1024 * 1024

Q_SCALE = MLA_SCALE * math.log2(math.e)

C_QA = 0
C_KV = C_QA + MLA_Q_LORA
C_GQ = C_KV + MLA_KV_LORA
C_GK = C_GQ + GLA_QK
C_GV = C_GK + GLA_QK
C_GG = C_GV + GLA_WIDTH
C_MISC = C_GG + GLA_WIDTH
C_SWAP = C_MISC + LANES
D_IN_PAD = C_SWAP + LANES

ROW_TILE = 512
RANK_TILE = 1024
ATT_TILE = 512
ATT_HEADS = 4
GLA_BLOCK = 256
DEC_SPANS = 4
SC_WIN = 128
PERM_W = D_MODEL // 4
MOE_TILE = 768
MOE_SUB = 256
NEW_PAD = 16


def _cparams(sem):
    return pltpu.CompilerParams(dimension_semantics=sem, vmem_limit_bytes=VMEM_LIMIT)


def _rms(x, g):
    return x * lax.rsqrt(jnp.mean(x * x, axis=-1, keepdims=True) + EPS) * g


def _mod_spec(term, seq_rows, row_off, tm):
    if seq_rows >= tm:
        tiles_per_seq = seq_rows // tm
        return pl.BlockSpec((1, 1, D_MODEL), lambda i: (row_off + i // tiles_per_seq, 0, term))
    g = tm // seq_rows
    assert row_off % g == 0
    return pl.BlockSpec((g, 1, D_MODEL), lambda i: (row_off // g + i, 0, term))


def _mod_rows(ref, rows):
    m = ref[...]
    g = m.shape[0]
    if g == 1:
        return m[0]
    return jnp.broadcast_to(m, (g, rows // g, m.shape[2])).reshape(rows, m.shape[2])


def _pack_rows(x):
    bits = lax.bitcast_convert_type(x.astype(BF16).astype(F32), jnp.uint32)
    w = x.shape[1] // 2
    return (bits[:, :w] >> 16) | bits[:, w:]


def _unpack_rows(words):
    lo = lax.bitcast_convert_type(words << 16, F32)
    hi = lax.bitcast_convert_type(words & jnp.uint32(0xFFFF0000), F32)
    return jnp.concatenate([lo, hi], axis=1)


def _ada_kernel(c_ref, w_ref, b_ref, o_ref):
    c = c_ref[...]
    a = (c * jax.nn.sigmoid(c)).astype(BF16)
    o_ref[...] = jnp.dot(a, w_ref[...].astype(BF16), preferred_element_type=F32) + b_ref[...]


def ada_terms(c, w, b):
    rows, d = c.shape
    n = w.shape[1]
    tn = 512
    return pl.pallas_call(
        _ada_kernel,
        grid=(n // tn,),
        in_specs=[pl.BlockSpec((rows, d), lambda j: (0, 0)),
                  pl.BlockSpec((d, tn), lambda j: (0, j)),
                  pl.BlockSpec((1, tn), lambda j: (0, j))],
        out_specs=pl.BlockSpec((rows, tn), lambda j: (0, j)),
        out_shape=jax.ShapeDtypeStruct((rows, n), F32),
        compiler_params=_cparams(("arbitrary",)),
        name="ada_terms",
    )(c, w, b.reshape(1, n))


def _in_kernel(with_kv, x_ref, sh_ref, sc_ref, g_ref, win_ref, gqa_ref, wq1_ref, wq2_ref,
               gkv_ref, wk_ref, wv_ref, wgk_ref, bgk_ref, qc_ref, qs_ref, kc_ref, ks_ref, *outs):
    if with_kv:
        q_out, k_out, v_out, ckv_out, kpe_out, gq_out, gk_out, gv_out, la_out, gg_out = outs
    else:
        q_out, ckv_out, kpe_out, gq_out, gk_out, gv_out, la_out, gg_out = outs
    x = x_ref[...]
    h = _rms(x, g_ref[...])
    rows = x.shape[0]
    h = h * (1.0 + _mod_rows(sc_ref, rows)) + _mod_rows(sh_ref, rows)
    proj = jnp.dot(h.astype(BF16), win_ref[...], preferred_element_type=F32)

    qn = _rms(proj[:, C_QA:C_KV], gqa_ref[...]).astype(BF16)
    qa = jnp.dot(qn, wq1_ref[...], preferred_element_type=F32)
    qb = jnp.dot(qn, wq2_ref[...], preferred_element_type=F32)
    qc = qc_ref[...]
    qs = qs_ref[...]
    for hh in range(MLA_HEADS):
        sl = slice(hh * HEAD_PAD, (hh + 1) * HEAD_PAD)
        q_out[:, sl] = (qa[:, sl] * qc + qb[:, sl] * qs).astype(BF16)

    ckv = _rms(proj[:, C_KV:C_GQ], gkv_ref[...])
    ckv_out[...] = ckv
    kpe = proj[:, C_MISC:C_SWAP] * kc_ref[...] + proj[:, C_SWAP:D_IN_PAD] * ks_ref[...]
    kpe_out[...] = kpe[:, :MLA_ROPE]
    if with_kv:
        ckv_b = ckv.astype(BF16)
        kn = jnp.dot(ckv_b, wk_ref[...], preferred_element_type=F32)
        kpe_sh = pltpu.roll(kpe, MLA_NOPE, axis=1)
        for hh in range(MLA_HEADS):
            sl = slice(hh * HEAD_PAD, (hh + 1) * HEAD_PAD)
            k_out[:, sl] = (kn[:, sl] + kpe_sh).astype(BF16)
        v_out[...] = jnp.dot(ckv_b, wv_ref[...], preferred_element_type=F32).astype(BF16)

    gq_out[...] = proj[:, C_GQ:C_GK] * (GLA_DK ** -0.5)
    gk_out[...] = proj[:, C_GK:C_GV]
    gv_out[...] = proj[:, C_GV:C_GG].astype(BF16)
    gg_out[...] = proj[:, C_GG:C_MISC].astype(BF16)
    xg = jnp.dot(proj[:, C_MISC:C_SWAP].astype(BF16), wgk_ref[...],
                 preferred_element_type=F32) + bgk_ref[...]
    la_out[...] = (jnp.minimum(xg, 0.0) - jnp.log(1.0 + jnp.exp(-jnp.abs(xg)))) * (1.0 / GLA_GATE_NORM)


def in_proj(x, mod, seq_rows, row_off, tabs, wts, with_kv):
    n = x.shape[0]
    tm = ROW_TILE
    nt = n // tm
    qc, qs, kc, ks = tabs
    n_tab = qc.shape[0] // tm
    row = lambda i: (i, 0)
    const = lambda i: (0, 0)
    tab_spec = pl.BlockSpec((tm, LANES), lambda i: (i % n_tab, 0))
    w = wts
    in_specs = [
        pl.BlockSpec((tm, D_MODEL), row),
        _mod_spec(0, seq_rows, row_off, tm), _mod_spec(1, seq_rows, row_off, tm),
        pl.BlockSpec((1, D_MODEL), const),
        pl.BlockSpec((D_MODEL, D_IN_PAD), const),
        pl.BlockSpec((1, MLA_Q_LORA), const),
        pl.BlockSpec((MLA_Q_LORA, MLA_PAD), const),
        pl.BlockSpec((MLA_Q_LORA, MLA_PAD), const),
        pl.BlockSpec((1, MLA_KV_LORA), const),
        pl.BlockSpec((MLA_KV_LORA, MLA_PAD), const),
        pl.BlockSpec((MLA_KV_LORA, MLA_PAD), const),
        pl.BlockSpec((LANES, GLA_QK), const),
        pl.BlockSpec((1, GLA_QK), const),
        tab_spec, tab_spec, tab_spec, tab_spec,
    ]
    wide = lambda dt: (jax.ShapeDtypeStruct((n, MLA_PAD), dt), pl.BlockSpec((tm, MLA_PAD), row))
    outs = [wide(BF16)]
    if with_kv:
        outs += [wide(BF16), wide(BF16)]
    outs += [
        (jax.ShapeDtypeStruct((n, MLA_KV_LORA), F32), pl.BlockSpec((tm, MLA_KV_LORA), row)),
        (jax.ShapeDtypeStruct((n, MLA_ROPE), F32), pl.BlockSpec((tm, MLA_ROPE), row)),
        (jax.ShapeDtypeStruct((n, GLA_QK), F32), pl.BlockSpec((tm, GLA_QK), row)),
        (jax.ShapeDtypeStruct((n, GLA_QK), F32), pl.BlockSpec((tm, GLA_QK), row)),
        (jax.ShapeDtypeStruct((n, GLA_WIDTH), BF16), pl.BlockSpec((tm, GLA_WIDTH), row)),
        (jax.ShapeDtypeStruct((n, GLA_QK), F32), pl.BlockSpec((tm, GLA_QK), row)),
        (jax.ShapeDtypeStruct((n, GLA_WIDTH), BF16), pl.BlockSpec((tm, GLA_WIDTH), row)),
    ]
    return pl.pallas_call(
        functools.partial(_in_kernel, with_kv),
        grid=(nt,),
        in_specs=in_specs,
        out_specs=[o[1] for o in outs],
        out_shape=[o[0] for o in outs],
        compiler_params=_cparams(("parallel",)),
        name="inproj_prompt" if with_kv else "inproj_sample",
    )(x, mod, mod, w["g_norm_mix"], w["w_in"], w["g_q_a"], w["wq1"], w["wq2"], w["g_kv_a"],
      w["wk"], w["wv"], w["wgk"], w["b_gk"], qc, qs, kc, ks)


def _prefill_kernel(q_ref, k_ref, v_ref, o_ref):
    qi = pl.program_id(2)
    t = ATT_TILE
    lanes = [slice(g * HEAD_PAD, (g + 1) * HEAD_PAD) for g in range(ATT_HEADS)]
    qs = [q_ref[0, :, sl] for sl in lanes]

    def update_all(carries, r0, n_keys, mask):
        r0 = pl.multiple_of(r0, t)
        scores = [lax.dot_general(k_ref[0, pl.ds(r0, n_keys), sl], q, (((1,), (1,)), ((), ())),
                                  preferred_element_type=F32) for q, sl in zip(qs, lanes)]
        stats = []
        for (m, l, acc), s in zip(carries, scores):
            if mask is not None:
                s = jnp.where(mask, s, -jnp.inf)
            m_new = jnp.maximum(m, jnp.max(s, axis=0, keepdims=True))
            p = jnp.exp2(s - m_new)
            alpha = jnp.exp2(m - m_new)
            stats.append((m_new, alpha * l + jnp.sum(p, axis=0, keepdims=True), alpha * acc, p.astype(BF16)))
        out = []
        for (m_new, l_new, acc_scaled, p), sl in zip(stats, lanes):
            pv = lax.dot_general(v_ref[0, pl.ds(r0, n_keys), sl], p, (((0,), (0,)), ((), ())),
                                 preferred_element_type=F32)
            out.append((m_new, l_new, acc_scaled + pv))
        return tuple(out)

    def body(jj, carries):
        return update_all(carries, jj * (2 * t), 2 * t, None)

    init = (jnp.full((1, t), -jnp.inf, F32), jnp.zeros((1, t), F32), jnp.zeros((HEAD_PAD, t), F32))
    carries = lax.fori_loop(0, qi // 2, body, (init,) * ATT_HEADS)

    def causal(n_keys):
        key_pos = lax.broadcasted_iota(jnp.int32, (n_keys, t), 0)
        qry_pos = lax.broadcasted_iota(jnp.int32, (n_keys, t), 1) + (n_keys - t)
        return key_pos <= qry_pos

    carries = lax.cond(qi % 2 == 1,
                       lambda c: update_all(c, (qi - 1) * t, 2 * t, causal(2 * t)),
                       lambda c: update_all(c, qi * t, t, causal(t)),
                       carries)
    for (m, l, acc), sl in zip(carries, lanes):
        o_ref[0, :, sl] = (acc / l).T.astype(BF16)


def mla_prefill(q, k, v, batch, seq):
    t = ATT_TILE
    q3 = q.reshape(batch, seq, MLA_PAD)
    k3 = k.reshape(batch, seq, MLA_PAD)
    v3 = v.reshape(batch, seq, MLA_PAD)
    o = pl.pallas_call(
        _prefill_kernel,
        grid=(batch, MLA_HEADS // ATT_HEADS, seq // t),
        in_specs=[pl.BlockSpec((1, t, ATT_HEADS * HEAD_PAD), lambda b, h, i: (b, i, h)),
                  pl.BlockSpec((1, seq, ATT_HEADS * HEAD_PAD), lambda b, h, i: (b, 0, h)),
                  pl.BlockSpec((1, seq, ATT_HEADS * HEAD_PAD), lambda b, h, i: (b, 0, h))],
        out_specs=pl.BlockSpec((1, t, ATT_HEADS * HEAD_PAD), lambda b, h, i: (b, i, h)),
        out_shape=jax.ShapeDtypeStruct((batch, seq, MLA_PAD), BF16),
        compiler_params=_cparams(("parallel", "parallel", "arbitrary")),
        name="mla_prefill",
    )(q3, k3, v3)
    return o.reshape(batch * seq, MLA_PAD)


def _absorb_kernel(q_ref, wuk_ref, qlat_ref, qpe_ref):
    q = q_ref[...]
    qlat_ref[0] = jnp.dot(q, wuk_ref[0], preferred_element_type=F32)
    qf = pltpu.roll(q.astype(F32), HEAD_PAD - MLA_NOPE, axis=1)
    lane = lax.broadcasted_iota(jnp.int32, qf.shape, 1)
    qpe_ref[0] = jnp.where(lane < MLA_ROPE, qf, 0.0)


def absorb_q(q, wuk_t):
    r = q.shape[0]
    return pl.pallas_call(
        _absorb_kernel,
        grid=(MLA_HEADS,),
        in_specs=[pl.BlockSpec((r, HEAD_PAD), lambda h: (0, h)),
                  pl.BlockSpec((1, HEAD_PAD, MLA_KV_LORA), lambda h: (h, 0, 0))],
        out_specs=[pl.BlockSpec((1, r, MLA_KV_LORA), lambda h: (h, 0, 0)),
                   pl.BlockSpec((1, r, HEAD_PAD), lambda h: (h, 0, 0))],
        out_shape=[jax.ShapeDtypeStruct((MLA_HEADS, r, MLA_KV_LORA), F32),
                   jax.ShapeDtypeStruct((MLA_HEADS, r, HEAD_PAD), F32)],
        compiler_params=_cparams(("parallel",)),
        name="absorb_q",
    )(q, wuk_t)


def _decode_kernel(t_new, pt_ref, qlat_ref, qpe_ref, cnew_ref, knew_ref, ckv_hbm, kpe_hbm, o_ref,
                   cbuf, pbuf, sem):
    b = pl.program_id(0)
    nb = pl.num_programs(0)
    n_pages = cbuf.shape[1]
    slot = lax.rem(b, 2)
    rows = MLA_HEADS * t_new

    def page_copies(bb, sl, p):
        pg = pt_ref[bb, p]
        return (pltpu.make_async_copy(ckv_hbm.at[0, pg], cbuf.at[sl, p], sem.at[0, sl]),
                pltpu.make_async_copy(kpe_hbm.at[0, pg], pbuf.at[sl, p], sem.at[1, sl]))

    def fetch(bb, sl):
        def body(p, c):
            latent_cp, rope_cp = page_copies(bb, sl, p)
            latent_cp.start()
            rope_cp.start(priority=1)
            return c
        lax.fori_loop(0, n_pages, body, 0)

    @pl.when(b == 0)
    def _():
        fetch(0, 0)

    @pl.when(b + 1 < nb)
    def _():
        fetch(b + 1, 1 - slot)

    pltpu.make_async_copy(ckv_hbm.at[0, pl.ds(0, n_pages)], cbuf.at[slot], sem.at[0, slot]).wait()
    pltpu.make_async_copy(kpe_hbm.at[0, pl.ds(0, n_pages)], pbuf.at[slot], sem.at[1, slot]).wait()

    qlat = qlat_ref[...].reshape(rows, MLA_KV_LORA).astype(BF16)
    qpe = qpe_ref[...].reshape(rows, HEAD_PAD)[:, :MLA_ROPE].astype(BF16)
    dn = (((1,), (1,)), ((), ()))

    span_pages = n_pages // DEC_SPANS
    values, scores = [], []
    for c in range(DEC_SPANS):
        pages = range(c * span_pages, (c + 1) * span_pages)
        kb = jnp.concatenate([cbuf[slot, p].astype(BF16) for p in pages], axis=0)
        pb = jnp.concatenate([pbuf[slot, p].astype(BF16) for p in pages], axis=1)
        values.append(kb)
        scores.append(lax.dot_general(qlat, kb, dn, preferred_element_type=F32)
                      + jnp.dot(qpe, pb, preferred_element_type=F32))
    cn = cnew_ref[0].astype(BF16)
    kn = knew_ref[0].astype(BF16)
    sn = (lax.dot_general(qlat, cn, dn, preferred_element_type=F32)
          + lax.dot_general(qpe, kn, dn, preferred_element_type=F32))
    tq = lax.broadcasted_iota(jnp.int32, sn.shape, 0) % t_new
    tk = lax.broadcasted_iota(jnp.int32, sn.shape, 1)
    values.append(cn)
    scores.append(jnp.where(tk <= tq, sn, -jnp.inf))

    maxes = [jnp.max(s, axis=-1, keepdims=True) for s in scores]
    m = maxes[0]
    for pm in maxes[1:]:
        m = jnp.maximum(m, pm)
    probs = [jnp.exp2(s - pm) for s, pm in zip(scores, maxes)]
    l = jnp.zeros_like(m)
    acc = jnp.zeros((rows, MLA_KV_LORA), F32)
    for p, pm, vals in zip(probs, maxes, values):
        scale = jnp.exp2(pm - m)
        l = l + scale * jnp.sum(p, axis=-1, keepdims=True)
        acc = acc + scale * jnp.dot(p.astype(BF16), vals, preferred_element_type=F32)
    o_ref[0] = acc / l


def mla_decode(qlat, qpe, ckv_new, kpe_new, cache_ckv, cache_kpe, page_table, dec_batch, t_new):
    n_pages = page_table.shape[1]
    past_len = n_pages * PAGE_SIZE
    rows = MLA_HEADS * t_new
    kpe_t = jnp.swapaxes(cache_kpe, 2, 3)
    qlat4 = qlat.reshape(MLA_HEADS, dec_batch, t_new, MLA_KV_LORA)
    qpe4 = qpe.reshape(MLA_HEADS, dec_batch, t_new, HEAD_PAD)
    t_pad = NEW_PAD
    pad_new = lambda a: jnp.pad(a.reshape(dec_batch, t_new, a.shape[-1]), ((0, 0), (0, t_pad - t_new), (0, 0)))
    cnew = pad_new(ckv_new)
    knew = pad_new(kpe_new)

    in_specs = [
        pl.BlockSpec((MLA_HEADS, 1, t_new, MLA_KV_LORA), lambda b, pt: (0, b, 0, 0)),
        pl.BlockSpec((MLA_HEADS, 1, t_new, HEAD_PAD), lambda b, pt: (0, b, 0, 0)),
        pl.BlockSpec((1, t_pad, MLA_KV_LORA), lambda b, pt: (b, 0, 0)),
        pl.BlockSpec((1, t_pad, MLA_ROPE), lambda b, pt: (b, 0, 0)),
        pl.BlockSpec(memory_space=pl.ANY),
        pl.BlockSpec(memory_space=pl.ANY),
    ]
    grid_spec = pltpu.PrefetchScalarGridSpec(
        num_scalar_prefetch=1,
        grid=(dec_batch,),
        in_specs=in_specs,
        out_specs=pl.BlockSpec((1, rows, MLA_KV_LORA), lambda b, pt: (b, 0, 0)),
        scratch_shapes=[pltpu.VMEM((2, n_pages, PAGE_SIZE, MLA_KV_LORA), F32),
                        pltpu.VMEM((2, n_pages, MLA_ROPE, PAGE_SIZE), F32),
                        pltpu.SemaphoreType.DMA((2, 2))],
    )
    return pl.pallas_call(
        functools.partial(_decode_kernel, t_new),
        grid_spec=grid_spec,
        out_shape=jax.ShapeDtypeStruct((dec_batch, rows, MLA_KV_LORA), F32),
        compiler_params=_cparams(("arbitrary",)),
        name="mla_decode",
    )(page_table, qlat4, qpe4, cnew, knew, cache_ckv, kpe_t)


def _uv_kernel(o_ref, wuv_ref, out_ref):
    o = o_ref[...]
    o = o.reshape(o.shape[0] * o.shape[2], MLA_KV_LORA).astype(BF16)
    out_ref[...] = jnp.dot(o, wuv_ref[0], preferred_element_type=F32).astype(BF16)


def latent_to_values(o_lat, wuv, dec_batch, t_new):
    o4 = o_lat.reshape(dec_batch, MLA_HEADS, t_new, MLA_KV_LORA)
    return pl.pallas_call(
        _uv_kernel,
        grid=(MLA_HEADS,),
        in_specs=[pl.BlockSpec((dec_batch, 1, t_new, MLA_KV_LORA), lambda h: (0, h, 0, 0)),
                  pl.BlockSpec((1, MLA_KV_LORA, HEAD_PAD), lambda h: (h, 0, 0))],
        out_specs=pl.BlockSpec((dec_batch * t_new, HEAD_PAD), lambda h: (0, h)),
        out_shape=jax.ShapeDtypeStruct((dec_batch * t_new, MLA_PAD), BF16),
        compiler_params=_cparams(("parallel",)),
        name="latent_to_values",
    )(o4, wuv)


def _gla_kernel(nsb, q_ref, k_ref, la_ref, v_ref, s0_ref, o_ref, sfin_ref, st_sc, kv_sc, sall_sc):
    blk = pl.program_id(1)
    nblk = pl.num_programs(1)
    c = GLA_CHUNK
    tb = q_ref.shape[1]
    nc = tb // c
    cps = nc // nsb
    head_k = [slice(hh * GLA_DK, (hh + 1) * GLA_DK) for hh in range(GLA_HEADS)]
    head_v = [slice(hh * GLA_DV, (hh + 1) * GLA_DV) for hh in range(GLA_HEADS)]

    @pl.when(blk == 0)
    def _():
        for sq in range(nsb):
            for hh in range(GLA_HEADS):
                st_sc[sq, hh] = s0_ref[sq, hh].T

    q = q_ref[0]
    k = k_ref[0]
    la = la_ref[0]
    v = v_ref[0]

    r = lax.broadcasted_iota(jnp.int32, (tb, tb), 0)
    cc = lax.broadcasted_iota(jnp.int32, (tb, tb), 1)
    same = (r >> 4) == (cc >> 4)
    tri = jnp.where(same & (cc <= r), 1.0, 0.0)
    ones = jnp.where(same, 1.0, 0.0)
    sel = jnp.concatenate([tri, ones], axis=0).astype(BF16)
    la_hi = la.astype(BF16)
    rest = la - la_hi.astype(F32)
    la_mid = rest.astype(BF16)
    la_lo = (rest - la_mid.astype(F32)).astype(BF16)
    sums = (jnp.dot(sel, la_hi, preferred_element_type=F32)
            + (jnp.dot(sel, la_mid, preferred_element_type=F32)
               + jnp.dot(sel, la_lo, preferred_element_type=F32)))
    b = sums[:tb]
    b_last = sums[tb:]
    qd = (q * jnp.exp(b)).astype(BF16)
    kd = (k * jnp.exp(b_last - b)).astype(BF16)
    dec = jnp.exp(b_last)

    key_head = lax.broadcasted_iota(jnp.int32, (GLA_QK, GLA_HEADS * c), 0) >> 6
    col = lax.broadcasted_iota(jnp.int32, (GLA_QK, GLA_HEADS * c), 1)
    b3 = b.reshape(nc, c, GLA_QK)
    k3 = k.reshape(nc, c, GLA_QK)
    q3 = q.reshape(nc, c, GLA_QK)
    pos = lax.broadcasted_iota(jnp.int32, (nc, c, GLA_QK), 1)
    scores = jnp.zeros((tb, GLA_HEADS * c), F32)
    for j in range(c):
        e = jnp.exp(b3 - b3[:, j:j + 1, :])
        w = jnp.where(pos >= j, q3 * k3[:, j:j + 1, :] * e, 0.0)
        place = jnp.where(col == key_head * c + j, 1.0, 0.0).astype(BF16)
        scores = scores + jnp.dot(w.reshape(tb, GLA_QK).astype(BF16), place, preferred_element_type=F32)
    scores = scores.astype(BF16)

    for ci in range(nc):
        rs = slice(ci * c, (ci + 1) * c)
        for hh in range(GLA_HEADS):
            o_ref[0, rs, head_v[hh]] = jnp.dot(scores[rs, hh * c:(hh + 1) * c], v[rs, head_v[hh]],
                                               preferred_element_type=F32)
            kv_sc[ci, hh] = lax.dot_general(v[rs, head_v[hh]], kd[rs, head_k[hh]], (((0,), (0,)), ((), ())),
                                            preferred_element_type=F32)

    for sq in range(nsb):
        states = [st_sc[sq, hh] for hh in range(GLA_HEADS)]
        for cj in range(cps):
            ci = sq * cps + cj
            dec_row = dec[ci * c:ci * c + 1, :]
            for hh in range(GLA_HEADS):
                sall_sc[ci, hh] = states[hh].astype(BF16)
                states[hh] = states[hh] * dec_row[:, head_k[hh]] + kv_sc[ci, hh]
        for hh in range(GLA_HEADS):
            st_sc[sq, hh] = states[hh]

    for ci in range(nc):
        rs = slice(ci * c, (ci + 1) * c)
        outs = [lax.dot_general(qd[rs, head_k[hh]], sall_sc[ci, hh], (((1,), (1,)), ((), ())),
                                preferred_element_type=F32) for hh in range(GLA_HEADS)]
        o_ref[0, rs, :] += jnp.concatenate(outs, axis=1)

    @pl.when(blk == nblk - 1)
    def _():
        for sq in range(nsb):
            for hh in range(GLA_HEADS):
                sfin_ref[sq, hh] = st_sc[sq, hh].T


def gla(gq, gk, la, gv, s0, n_seq, seq_len):
    tb = GLA_BLOCK
    nsb = max(1, tb // seq_len)
    nblk = max(1, seq_len // tb)
    n_outer = n_seq // nsb
    nc = tb // GLA_CHUNK
    sh3 = lambda a: a.reshape(n_outer, nblk * tb, a.shape[-1])
    row = lambda b, i: (b, i, 0)
    st_spec = pl.BlockSpec((nsb, GLA_HEADS, GLA_DK, GLA_DV), lambda b, i: (b, 0, 0, 0))
    o, s_fin = pl.pallas_call(
        functools.partial(_gla_kernel, nsb),
        grid=(n_outer, nblk),
        in_specs=[pl.BlockSpec((1, tb, GLA_QK), row), pl.BlockSpec((1, tb, GLA_QK), row),
                  pl.BlockSpec((1, tb, GLA_QK), row), pl.BlockSpec((1, tb, GLA_WIDTH), row), st_spec],
        out_specs=[pl.BlockSpec((1, tb, GLA_WIDTH), row), st_spec],
        out_shape=[jax.ShapeDtypeStruct((n_outer, nblk * tb, GLA_WIDTH), F32),
                   jax.ShapeDtypeStruct((n_seq, GLA_HEADS, GLA_DK, GLA_DV), F32)],
        scratch_shapes=[pltpu.VMEM((nsb, GLA_HEADS, GLA_DV, GLA_DK), F32),
                        pltpu.VMEM((nc, GLA_HEADS, GLA_DV, GLA_DK), F32),
                        pltpu.VMEM((nc, GLA_HEADS, GLA_DV, GLA_DK), BF16)],
        compiler_params=_cparams(("parallel", "arbitrary")),
        name="gla_prompt" if nblk > 1 else "gla_sample",
    )(sh3(gq), sh3(gk), sh3(la), sh3(gv), s0)
    return o.reshape(n_seq * seq_len, GLA_WIDTH), s_fin


def _mix_kernel(x_ref, om_ref, og_ref, gg_ref, gta_ref, shf_ref, scf_ref, gm_ref, ggl_ref, wo_ref,
                gn_ref, wr_ref, br_ref, h2a_in, h2b_in, x1_ref, h2a_ref, h2b_ref, ti_ref, tg_ref, hist_ref):
    del h2a_in, h2b_in
    om = om_ref[...].astype(F32)
    ms = jnp.sum(om * om, axis=-1, keepdims=True) * (1.0 / (MLA_HEADS * MLA_V))
    om = om * lax.rsqrt(ms + EPS) * gm_ref[...]
    og = og_ref[...]
    gg = gg_ref[...].astype(F32)
    gate = gg * jax.nn.sigmoid(gg)
    parts = []
    for hh in range(GLA_HEADS):
        sl = slice(hh * GLA_DV, (hh + 1) * GLA_DV)
        parts.append(_rms(og[:, sl], ggl_ref[...]) * gate[:, sl])
    mix = jnp.concatenate([om] + parts, axis=1).astype(BF16)
    rows = x_ref.shape[0]
    x1 = x_ref[...] + _mod_rows(gta_ref, rows) * jnp.dot(mix, wo_ref[...], preferred_element_type=F32)
    x1_ref[...] = x1
    h2 = _rms(x1, gn_ref[...]) * (1.0 + _mod_rows(scf_ref, rows)) + _mod_rows(shf_ref, rows)
    words = _pack_rows(h2)
    h2a_ref[...] = words[:, :PERM_W]
    h2b_ref[...] = words[:, PERM_W:]
    h_hi = h2.astype(BF16)
    h_lo = (h2 - h_hi.astype(F32)).astype(BF16)
    logits = (jnp.dot(h_hi, wr_ref[0], preferred_element_type=F32)
              + (jnp.dot(h_hi, wr_ref[1], preferred_element_type=F32)
                 + jnp.dot(h_lo, wr_ref[0], preferred_element_type=F32))) + br_ref[...]
    lane_i = lax.broadcasted_iota(jnp.int32, logits.shape, 1)
    lane = lane_i.astype(F32)
    vals = []
    idxs = []
    for _ in range(TOP_K):
        mx = jnp.max(logits, axis=-1, keepdims=True)
        ix = jnp.min(jnp.where(logits == mx, lane, float(LANES)), axis=-1, keepdims=True)
        vals.append(mx)
        idxs.append(ix)
        logits = jnp.where(lane == ix, -jnp.inf, logits)
    ex = [jnp.exp(vv - vals[0]) for vv in vals]
    den = ex[0] + ex[1] + ex[2] + ex[3]
    ti = jnp.zeros(logits.shape, F32)
    tg = jnp.zeros(logits.shape, F32)
    onehot = jnp.zeros(logits.shape, F32)
    for kk in range(TOP_K):
        ti = jnp.where(lane_i == kk, idxs[kk], ti)
        tg = jnp.where(lane_i == kk, ex[kk] / den, tg)
        onehot = onehot + jnp.where(lane == idxs[kk], 1.0, 0.0)
    ti_ref[...] = ti.astype(jnp.int32)
    tg_ref[...] = tg
    hist_ref[0] = jnp.sum(onehot, axis=0, keepdims=True)


def mixer_out(x, o_mla, o_gla, gg, mod, seq_rows, row_off, w, n_all, row0, h2_buf):
    n = x.shape[0]
    tm = ROW_TILE
    t0 = row0 // tm
    row = lambda i: (i, 0)
    const = lambda i: (0, 0)
    extra_specs = [pl.BlockSpec(memory_space=pl.ANY)] * 2
    extra_args = list(h2_buf)
    mod_specs = [_mod_spec(term, seq_rows, row_off, tm) for term in (2, 3, 4)]
    d_mix = MLA_PAD + GLA_WIDTH
    n_in = 13
    return pl.pallas_call(
        _mix_kernel,
        grid=(n // tm,),
        in_specs=[pl.BlockSpec((tm, D_MODEL), row), pl.BlockSpec((tm, MLA_PAD), row),
                  pl.BlockSpec((tm, GLA_WIDTH), row), pl.BlockSpec((tm, GLA_WIDTH), row),
                  *mod_specs,
                  pl.BlockSpec((1, MLA_PAD), const), pl.BlockSpec((1, GLA_DV), const),
                  pl.BlockSpec((d_mix, D_MODEL), const), pl.BlockSpec((1, D_MODEL), const),
                  pl.BlockSpec((2, D_MODEL, LANES), lambda i: (0, 0, 0)),
                  pl.BlockSpec((1, LANES), const)] + extra_specs,
        out_specs=[pl.BlockSpec((tm, D_MODEL), row),
                   pl.BlockSpec((tm, PERM_W), lambda i: (i + t0, 0)),
                   pl.BlockSpec((tm, PERM_W), lambda i: (i + t0, 0)),
                   pl.BlockSpec((tm, LANES), row), pl.BlockSpec((tm, LANES), row),
                   pl.BlockSpec((1, 1, LANES), lambda i: (i, 0, 0))],
        out_shape=[jax.ShapeDtypeStruct((n, D_MODEL), F32),
                   jax.ShapeDtypeStruct((n_all, PERM_W), jnp.uint32),
                   jax.ShapeDtypeStruct((n_all, PERM_W), jnp.uint32),
                   jax.ShapeDtypeStruct((n, LANES), jnp.int32), jax.ShapeDtypeStruct((n, LANES), F32),
                   jax.ShapeDtypeStruct((n // tm, 1, LANES), F32)],
        input_output_aliases={n_in: 1, n_in + 1: 2},
        compiler_params=_cparams(("parallel",)),
        name="mixer_sample" if seq_rows < tm else "mixer_prompt",
    )(x, o_mla, o_gla, gg, mod, mod, mod, w["g_mla_pad"], w["g_gla_out"], w["w_o_pad"],
      w["g_norm_ffn"], w["w_router_pad"], w["b_router_pad"], *extra_args)


def _rank_kernel(ti_ref, base_ref, dest_ref):
    ti = ti_ref[...]
    tm = ti.shape[0]
    lane = lax.broadcasted_iota(jnp.int32, ti.shape, 1)
    cols = [ti[:, kk:kk + 1] for kk in range(TOP_K)]
    onehot = jnp.zeros(ti.shape, F32)
    for kk in range(TOP_K):
        onehot = onehot + jnp.where(lane == cols[kk], 1.0, 0.0)
    r = lax.broadcasted_iota(jnp.int32, (tm, tm), 0)
    c = lax.broadcasted_iota(jnp.int32, (tm, tm), 1)
    earlier = jnp.where(c < r, 1.0, 0.0).astype(BF16)
    pos = jnp.dot(earlier, onehot.astype(BF16), preferred_element_type=F32) + base_ref[0]
    out = jnp.zeros(ti.shape, F32)
    for kk in range(TOP_K):
        dk = jnp.sum(jnp.where(lane == cols[kk], pos, 0.0), axis=-1, keepdims=True)
        out = jnp.where(lane == kk, dk, out)
    dest_ref[...] = out.astype(jnp.int32)


def route_rank(ti, base):
    n = ti.shape[0]
    tm = RANK_TILE
    return pl.pallas_call(
        _rank_kernel,
        grid=(n // tm,),
        in_specs=[pl.BlockSpec((tm, LANES), lambda i: (i, 0)),
                  pl.BlockSpec((1, 1, LANES), lambda i: (i, 0, 0))],
        out_specs=pl.BlockSpec((tm, LANES), lambda i: (i, 0)),
        out_shape=jax.ShapeDtypeStruct((n, LANES), jnp.int32),
        compiler_params=_cparams(("parallel",)),
        name="route_rank",
    )(ti, base)


def _route_tables(hist, n_tok):
    tm = MOE_TILE
    h = hist[:, 0, :].astype(jnp.int32)
    h = h.reshape(-1, RANK_TILE // ROW_TILE, LANES).sum(axis=1)
    counts = jnp.sum(h, axis=0)
    padded = (counts + tm - 1) // tm * tm
    pad_ends = jnp.cumsum(padded)
    pad_starts = pad_ends - padded
    base = (pad_starts[None, :] + jnp.cumsum(h, axis=0) - h).astype(F32)[:, None, :]
    n_blocks = pl.cdiv(n_tok * TOP_K, tm) + N_EXPERTS
    n_active = (pad_ends[N_EXPERTS - 1] // tm).astype(jnp.int32)
    blk = jnp.arange(n_blocks, dtype=jnp.int32)
    blk_c = jnp.minimum(blk, n_active - 1)
    ends = pad_ends[:N_EXPERTS]
    block_e = jnp.minimum(jnp.sum((ends[None, :] <= (blk_c * tm)[:, None]).astype(jnp.int32), axis=1),
                          N_EXPERTS - 1).astype(jnp.int32)
    used_end = (pad_starts + counts)[:N_EXPERTS][block_e]
    block_rows = jnp.where(blk < n_active, jnp.clip(used_end - blk * tm, 0, tm), 0).astype(jnp.int32)
    return base, block_e, block_rows, n_blocks


def _sc_mesh():
    return plsc.VectorSubcoreMesh(core_axis_name="c", subcore_axis_name="s")


def sc_dispatch(x_rows, idx, n_out):
    n, wd = x_rows.shape
    win = SC_WIN
    nwin = n // win

    @functools.partial(pl.kernel, out_type=jax.ShapeDtypeStruct((n_out, wd), x_rows.dtype),
                       mesh=_sc_mesh(), scratch_types=[])
    def k(x_hbm, i_hbm, o_hbm):
        def body(x_vmem, i_vmem):
            pltpu.sync_copy(x_vmem, o_hbm.at[i_vmem.at[0]])

        pltpu.emit_pipeline(
            body,
            grid=(idx.shape[1] // win,),
            in_specs=[pl.BlockSpec((win, wd), lambda i: (i % nwin, 0)),
                      pl.BlockSpec((1, win), lambda i: (0, i))],
            out_specs=[],
            core_axis_name=("c", "s"),
            dimension_semantics=(pltpu.PARALLEL,),
        )(x_hbm, i_hbm)

    return k(x_rows, idx)


def sc_combine(y_rows, idx):
    wd = y_rows.shape[1]
    m = idx.shape[1]
    win = SC_WIN

    @functools.partial(pl.kernel, out_type=jax.ShapeDtypeStruct((m, wd), y_rows.dtype),
                       mesh=_sc_mesh(), scratch_types=[])
    def k(y_hbm, i_hbm, o_hbm):
        def body(i_vmem, o_vmem):
            pltpu.sync_copy(y_hbm.at[i_vmem.at[0]], o_vmem)

        pltpu.emit_pipeline(
            body,
            grid=(m // win,),
            in_specs=[pl.BlockSpec((1, win), lambda i: (0, i))],
            out_specs=[pl.BlockSpec((win, wd), lambda i: (i, 0))],
            core_axis_name=("c", "s"),
            dimension_semantics=(pltpu.PARALLEL,),
        )(i_hbm, o_hbm)

    return k(y_rows, idx)


def _moe_kernel(be_ref, nr_ref, xa_ref, xb_ref, wup_ref, bup_ref, wdn_ref, bdn_ref, ya_ref, yb_ref,
                wup_sc, wdn_sc):
    i = pl.program_id(0)
    n_real = nr_ref[i]
    prev = be_ref[jnp.maximum(i - 1, 0)]
    fresh = (i == 0) | (be_ref[i] != prev)

    @pl.when((n_real > 0) & fresh)
    def _():
        wup_sc[...] = wup_ref[0].astype(BF16)
        wdn_sc[...] = wdn_ref[0].astype(BF16)

    n_sub = MOE_TILE // MOE_SUB
    live_subs = (n_real + (MOE_SUB - 1)) // MOE_SUB
    for live in range(n_sub + 1):
        m = live * MOE_SUB

        @pl.when(live_subs == live)
        def _():
            if m > 0:
                xb = _unpack_rows(jnp.concatenate([xa_ref[:m, :], xb_ref[:m, :]], axis=1)).astype(BF16)
                hu = jnp.dot(xb, wup_sc[...], preferred_element_type=F32) + bup_ref[0]
                gate = jnp.minimum(hu[:, :D_FF], SWIGLU_LIMIT)
                lin = jnp.clip(hu[:, D_FF:], -SWIGLU_LIMIT, SWIGLU_LIMIT)
                act = gate * jax.nn.sigmoid(SWIGLU_ALPHA * gate) * (lin + 1.0)
                y = jnp.dot(act.astype(BF16), wdn_sc[...], preferred_element_type=F32) + bdn_ref[0]
                words = _pack_rows(y)
                ya_ref[:m, :] = words[:, :PERM_W]
                yb_ref[:m, :] = words[:, PERM_W:]
            if m < MOE_TILE:
                ya_ref[m:, :] = jnp.zeros((MOE_TILE - m, PERM_W), jnp.uint32)
                yb_ref[m:, :] = jnp.zeros((MOE_TILE - m, PERM_W), jnp.uint32)


def moe_experts(xs_a, xs_b, block_e, block_rows, w_up, b_up, w_down, b_down):
    n_rows = xs_a.shape[0]
    tm = MOE_TILE
    n_blocks = n_rows // tm
    emap3 = lambda i, be, na: (be[i], 0, 0)
    grid_spec = pltpu.PrefetchScalarGridSpec(
        num_scalar_prefetch=2,
        grid=(n_blocks,),
        in_specs=[pl.BlockSpec((tm, PERM_W), lambda i, be, na: (i, 0)),
                  pl.BlockSpec((tm, PERM_W), lambda i, be, na: (i, 0)),
                  pl.BlockSpec((1, D_MODEL, 2 * D_FF), emap3),
                  pl.BlockSpec((1, 1, 2 * D_FF), emap3),
                  pl.BlockSpec((1, D_FF, D_MODEL), emap3),
                  pl.BlockSpec((1, 1, D_MODEL), emap3)],
        out_specs=[pl.BlockSpec((tm, PERM_W), lambda i, be, na: (i, 0)),
                   pl.BlockSpec((tm, PERM_W), lambda i, be, na: (i, 0))],
        scratch_shapes=[pltpu.VMEM((D_MODEL, 2 * D_FF), BF16), pltpu.VMEM((D_FF, D_MODEL), BF16)],
    )
    return pl.pallas_call(
        _moe_kernel,
        grid_spec=grid_spec,
        out_shape=[jax.ShapeDtypeStruct((n_rows, PERM_W), jnp.uint32)] * 2,
        compiler_params=_cparams(("arbitrary",)),
        name="moe_experts",
    )(block_e, block_rows, xs_a, xs_b, w_up, b_up.reshape(N_EXPERTS, 1, 2 * D_FF), w_down,
      b_down.reshape(N_EXPERTS, 1, D_MODEL))


def _final_kernel(x1_ref, yga_ref, ygb_ref, tg_ref, gtf_ref, sh_ref, sc_ref, g_ref, y_ref):
    tg = tg_ref[...]
    moe = jnp.zeros(x1_ref.shape, F32)
    for kk in range(TOP_K):
        moe = moe + tg[:, kk:kk + 1] * _unpack_rows(jnp.concatenate([yga_ref[kk], ygb_ref[kk]], axis=1))
    rows = x1_ref.shape[0]
    x2 = x1_ref[...] + _mod_rows(gtf_ref, rows) * moe
    y_ref[...] = _rms(x2, g_ref[...]) * (1.0 + _mod_rows(sc_ref, rows)) + _mod_rows(sh_ref, rows)


def final_out(x1, yg_a, yg_b, tg, row0, mod, mod_f, seq_rows, row_off, g_final):
    n = x1.shape[0]
    tm = ROW_TILE
    t0 = row0 // tm
    row = lambda i: (i, 0)
    mod_specs = [_mod_spec(5, seq_rows, row_off, tm),
                 _mod_spec(0, seq_rows, row_off, tm), _mod_spec(1, seq_rows, row_off, tm)]
    return pl.pallas_call(
        _final_kernel,
        grid=(n // tm,),
        in_specs=[pl.BlockSpec((tm, D_MODEL), row),
                  pl.BlockSpec((TOP_K, tm, PERM_W), lambda i: (0, i + t0, 0)),
                  pl.BlockSpec((TOP_K, tm, PERM_W), lambda i: (0, i + t0, 0)),
                  pl.BlockSpec((tm, LANES), row),
                  *mod_specs,
                  pl.BlockSpec((1, D_MODEL), lambda i: (0, 0))],
        out_specs=pl.BlockSpec((tm, D_MODEL), row),
        out_shape=jax.ShapeDtypeStruct((n, D_MODEL), F32),
        compiler_params=_cparams(("parallel",)),
        name="final_sample" if seq_rows < tm else "final_prompt",
    )(x1, yg_a, yg_b, tg, mod, mod_f, mod_f, g_final)


def _prep_weights(w_in, g_q_a, w_q_b, g_kv_a, w_kv_b, w_gk_b, b_gk, g_mla_out, g_gla_out, w_o,
                  g_norm_mix, g_norm_ffn, w_router, b_router):
    sizes = (MLA_Q_LORA, MLA_KV_LORA, MLA_ROPE, GLA_QK, GLA_QK, GLA_WIDTH, GLA_GATE_RANK, GLA_WIDTH)
    offs = np.cumsum((0,) + sizes)
    part = lambda i: w_in[:, offs[i]:offs[i + 1]]
    half = MLA_ROPE // 2
    k_rope = part(2)
    misc = jnp.concatenate([k_rope, part(6), jnp.zeros((D_MODEL, LANES - MLA_ROPE - GLA_GATE_RANK), F32)], 1)
    swap = jnp.concatenate([-k_rope[:, half:], k_rope[:, :half],
                            jnp.zeros((D_MODEL, LANES - MLA_ROPE), F32)], 1)
    w_in_pad = jnp.concatenate([part(0), part(1), part(3), part(4), part(5), part(7), misc, swap], 1)

    pad_q = jnp.zeros((MLA_Q_LORA, MLA_HEADS, HEAD_PAD - MLA_NOPE - MLA_ROPE), F32)
    wq1 = jnp.concatenate([w_q_b, pad_q], axis=2)
    q_lo = w_q_b[:, :, MLA_NOPE:MLA_NOPE + half]
    q_hi = w_q_b[:, :, MLA_NOPE + half:]
    wq2 = jnp.concatenate([jnp.zeros((MLA_Q_LORA, MLA_HEADS, MLA_NOPE), F32), -q_hi, q_lo, pad_q], axis=2)
    pad_kv = jnp.zeros((MLA_KV_LORA, MLA_HEADS, HEAD_PAD - MLA_NOPE), F32)
    w_uk = w_kv_b[:, :, :MLA_NOPE]
    w_uv = w_kv_b[:, :, MLA_NOPE:]
    wk = jnp.concatenate([w_uk, pad_kv], axis=2)
    wv = jnp.concatenate([w_uv, pad_kv], axis=2)
    wuk_t = jnp.concatenate([jnp.transpose(w_uk, (1, 2, 0)),
                             jnp.zeros((MLA_HEADS, HEAD_PAD - MLA_NOPE, MLA_KV_LORA), F32)], axis=1)
    wuv_h = jnp.transpose(wv, (1, 0, 2))
    wgk = jnp.zeros((LANES, GLA_QK), F32).at[MLA_ROPE:MLA_ROPE + GLA_GATE_RANK].set(w_gk_b)

    wo_mla = w_o[:MLA_HEADS * MLA_V].reshape(MLA_HEADS, MLA_V, D_MODEL)
    wo_mla = jnp.concatenate([wo_mla, jnp.zeros((MLA_HEADS, HEAD_PAD - MLA_V, D_MODEL), F32)], axis=1)
    w_o_pad = jnp.concatenate([wo_mla.reshape(MLA_PAD, D_MODEL), w_o[MLA_HEADS * MLA_V:]], axis=0)
    g_mla = g_mla_out.reshape(MLA_HEADS, MLA_V)
    g_mla_pad = jnp.concatenate([g_mla, jnp.zeros((MLA_HEADS, HEAD_PAD - MLA_V), F32)], 1).reshape(1, MLA_PAD)
    w_router_pad = jnp.concatenate([w_router, jnp.zeros((D_MODEL, LANES - N_EXPERTS), F32)], axis=1)
    wr_hi = w_router_pad.astype(BF16)
    w_router_pad = jnp.stack([wr_hi, (w_router_pad - wr_hi.astype(F32)).astype(BF16)])
    b_router_pad = jnp.concatenate([b_router, jnp.full((LANES - N_EXPERTS,), -jnp.inf, F32)]).reshape(1, LANES)
    return dict(
        w_in=w_in_pad.astype(BF16), g_norm_mix=g_norm_mix.reshape(1, D_MODEL),
        g_q_a=g_q_a.reshape(1, MLA_Q_LORA),
        wq1=wq1.reshape(MLA_Q_LORA, MLA_PAD).astype(BF16), wq2=wq2.reshape(MLA_Q_LORA, MLA_PAD).astype(BF16),
        g_kv_a=g_kv_a.reshape(1, MLA_KV_LORA),
        wk=wk.reshape(MLA_KV_LORA, MLA_PAD).astype(BF16), wv=wv.reshape(MLA_KV_LORA, MLA_PAD).astype(BF16),
        wuk_t=wuk_t.astype(BF16), wuv_h=wuv_h.astype(BF16),
        wgk=wgk.astype(BF16), b_gk=b_gk.reshape(1, GLA_QK),
        g_mla_pad=g_mla_pad, g_gla_out=g_gla_out.reshape(1, GLA_DV), w_o_pad=w_o_pad.astype(BF16),
        g_norm_ffn=g_norm_ffn.reshape(1, D_MODEL), w_router_pad=w_router_pad, b_router_pad=b_router_pad,
    )


def _rope_tables(pos, reps):
    half = MLA_ROPE // 2
    inv = ROPE_THETA ** (-jnp.arange(half, dtype=F32) / half)
    ang = pos.astype(F32)[:, None] * inv
    cos, sin = jnp.cos(ang), jnp.sin(ang)
    n = pos.shape[0]
    qc = jnp.concatenate([jnp.full((n, MLA_NOPE), Q_SCALE, F32), Q_SCALE * cos, Q_SCALE * cos,
                          jnp.zeros((n, HEAD_PAD - MLA_NOPE - MLA_ROPE), F32)], axis=1)
    qs = jnp.concatenate([jnp.zeros((n, MLA_NOPE), F32), Q_SCALE * sin, Q_SCALE * sin,
                          jnp.zeros((n, HEAD_PAD - MLA_NOPE - MLA_ROPE), F32)], axis=1)
    kc = jnp.concatenate([cos, cos, jnp.zeros((n, LANES - MLA_ROPE), F32)], axis=1)
    ks = jnp.concatenate([sin, sin, jnp.zeros((n, LANES - MLA_ROPE), F32)], axis=1)
    return tuple(jnp.tile(t, (reps, 1)) for t in (qc, qs, kc, ks))


def kernel(x_prompt, x_sample, cache_ckv, cache_kpe, state_gla, page_table, c_prompt, c_sample, w_ada, b_ada, g_norm_mix, w_in, g_q_a, w_q_b, g_kv_a, w_kv_b, w_gk_b, b_gk, g_mla_out, g_gla_out, w_o, g_norm_ffn, w_router, b_router, w_up, b_up, w_down, b_down, g_norm_final, w_ada_final, b_ada_final):
    B, S, D = x_prompt.shape
    DB, T, _ = x_sample.shape
    depth = w_ada.shape[0]
    assert depth == 1
    past_len = page_table.shape[1] * cache_ckv.shape[2]
    n_p, n_s = B * S, DB * T
    l = 0

    w = _prep_weights(w_in[l], g_q_a[l], w_q_b[l], g_kv_a[l], w_kv_b[l], w_gk_b[l], b_gk[l],
                      g_mla_out[l], g_gla_out[l], w_o[l], g_norm_mix[l], g_norm_ffn[l],
                      w_router[l], b_router[l])

    n_c = B + DB
    n_c_pad = (n_c + 7) // 8 * 8
    c_all = jnp.concatenate([c_sample, c_prompt, jnp.zeros((n_c_pad - n_c, D), F32)], axis=0)
    mod = ada_terms(c_all, w_ada[l], b_ada[l]).reshape(n_c_pad, 1, N_MOD * D)
    mod_f = ada_terms(c_all, w_ada_final, b_ada_final).reshape(n_c_pad, 1, 2 * D)
    off_s, off_p = 0, DB

    xp = x_prompt.reshape(n_p, D)
    xs = x_sample.reshape(n_s, D)
    tabs_p = _rope_tables(jnp.arange(S), 1)
    tabs_s = _rope_tables(past_len + jnp.arange(T), ROW_TILE // T)

    (q_p, k_p, v_p, ckv_p, kpe_p, gq_p, gk_p, gv_p, la_p, gg_p) = in_proj(
        xp, mod, S, off_p, tabs_p, w, True)
    o_mla_p = mla_prefill(q_p, k_p, v_p, B, S)
    s0 = jnp.zeros((B, GLA_HEADS, GLA_DK, GLA_DV), F32)
    o_gla_p, gla_p = gla(gq_p, gk_p, la_p, gv_p, s0, B, S)
    n_all = n_p + n_s
    h2_init = (jnp.zeros((n_all, PERM_W), jnp.uint32), jnp.zeros((n_all, PERM_W), jnp.uint32))
    x1_p, h2a, h2b, ti_p, tg_p, hist_p = mixer_out(xp, o_mla_p, o_gla_p.reshape(n_p, GLA_WIDTH), gg_p,
                                                   mod, S, off_p, w, n_all, 0, h2_init)

    (q_s, ckv_s, kpe_s, gq_s, gk_s, gv_s, la_s, gg_s) = in_proj(
        xs, mod, T, off_s, tabs_s, w, False)
    qlat, qpe = absorb_q(q_s, w["wuk_t"])
    o_lat = mla_decode(qlat, qpe, ckv_s, kpe_s, cache_ckv, cache_kpe, page_table, DB, T)
    o_mla_s = latent_to_values(o_lat, w["wuv_h"], DB, T)
    tpad = GLA_CHUNK
    padt = lambda a: jnp.pad(a.reshape(DB, T, a.shape[-1]), ((0, 0), (0, tpad - T), (0, 0))).reshape(
        DB * tpad, a.shape[-1])
    o_gla_s, gla_s = gla(padt(gq_s), padt(gk_s), padt(la_s), padt(gv_s), state_gla[l], DB, tpad)
    o_gla_s = o_gla_s.reshape(DB, tpad, GLA_WIDTH)[:, :T].reshape(n_s, GLA_WIDTH)
    x1_s, h2a, h2b, ti_s, tg_s, hist_s = mixer_out(xs, o_mla_s, o_gla_s, gg_s,
                                                   mod, T, off_s, w, n_all, n_p, (h2a, h2b))

    ti = jnp.concatenate([ti_p, ti_s], axis=0)
    base, block_e, block_rows, n_blocks = _route_tables(jnp.concatenate([hist_p, hist_s], axis=0), n_all)
    dest = route_rank(ti, base)
    idx = dest[:, :TOP_K].T.reshape(1, TOP_K * n_all)
    n_rows = n_blocks * MOE_TILE
    xs_a = sc_dispatch(h2a, idx, n_rows)
    xs_b = sc_dispatch(h2b, idx, n_rows)
    ys_a, ys_b = moe_experts(xs_a, xs_b, block_e, block_rows, w_up[l], b_up[l], w_down[l], b_down[l])
    yg_a = sc_combine(ys_a, idx).reshape(TOP_K, n_all, PERM_W)
    yg_b = sc_combine(ys_b, idx).reshape(TOP_K, n_all, PERM_W)

    g_fin = g_norm_final.reshape(1, D)
    y_p = final_out(x1_p, yg_a, yg_b, tg_p, 0, mod, mod_f, S, off_p, g_fin)
    y_s = final_out(x1_s, yg_a, yg_b, tg_s, n_p, mod, mod_f, T, off_s, g_fin)

    return (y_p.reshape(B, S, D), y_s.reshape(DB, T, D),
            ckv_p.reshape(1, B, S, MLA_KV_LORA), kpe_p.reshape(1, B, S, MLA_ROPE), gla_p[None],
            ckv_s.reshape(1, DB, T, MLA_KV_LORA), kpe_s.reshape(1, DB, T, MLA_ROPE), gla_s[None])
```

```python
import functools
import math

import jax
import jax.numpy as jnp
import numpy as np
from jax import lax
from jax.experimental import pallas as pl
from jax.experimental.pallas import tpu as pltpu
from jax.experimental.pallas import tpu_sc as plsc

F32 = jnp.float32
BF16 = jnp.bfloat16

D_MODEL = 1024
MLA_HEADS = 8
MLA_NOPE = 64
MLA_ROPE = 32
MLA_V = 64
MLA_Q_LORA = 384
MLA_KV_LORA = 256
MLA_SCALE = (MLA_NOPE + MLA_ROPE) ** -0.5
ROPE_THETA = 10000.0
GLA_HEADS = 4
GLA_DK = 64
GLA_DV = 128
GLA_GATE_RANK = 16
GLA_GATE_NORM = 16.0
GLA_CHUNK = 16
GLA_QK = GLA_HEADS * GLA_DK
GLA_WIDTH = GLA_HEADS * GLA_DV
N_EXPERTS = 32
TOP_K = 4
D_FF = D_MODEL
SWIGLU_LIMIT = 7.0
SWIGLU_ALPHA = 1.702
N_MOD = 6
EPS = 1e-6
PAGE_SIZE = 128

LANES = 128
HEAD_PAD = LANES
MLA_PAD = MLA_HEADS * HEAD_PAD
MLA_WIDTH = MLA_HEADS * MLA_V
VMEM_LIMIT = 56 * 1024 * 1024

Q_SCALE = MLA_SCALE * math.log2(math.e)

C_QA = 0
C_KV = C_QA + MLA_Q_LORA
C_GQ = C_KV + MLA_KV_LORA
C_GK = C_GQ + GLA_QK
C_GV = C_GK + GLA_QK
C_GG = C_GV + GLA_WIDTH
C_MISC = C_GG + GLA_WIDTH
C_SWAP = C_MISC + LANES
D_IN_PAD = C_SWAP + LANES

ROW_TILE = 512
RANK_TILE = 1024
ATT_TILE = 512
ATT_HEADS = 4
GLA_BLOCK = 256
DEC_SPANS = 4
SC_WIN = 128
PERM_W = D_MODEL // 4
MOE_TILE = 768
MOE_SUB = 256
NEW_PAD = 16


def _cparams(sem):
    return pltpu.CompilerParams(dimension_semantics=sem, vmem_limit_bytes=VMEM_LIMIT)


def _rms(x, g):
    return x * lax.rsqrt(jnp.mean(x * x, axis=-1, keepdims=True) + EPS) * g


def _mod_spec(term, seq_rows, row_off, tm):
    if seq_rows >= tm:
        tiles_per_seq = seq_rows // tm
        return pl.BlockSpec((1, 1, D_MODEL), lambda i: (row_off + i // tiles_per_seq, 0, term))
    g = tm // seq_rows
    assert row_off % g == 0
    return pl.BlockSpec((g, 1, D_MODEL), lambda i: (row_off // g + i, 0, term))


def _mod_rows(ref, rows):
    m = ref[...]
    g = m.shape[0]
    if g == 1:
        return m[0]
    return jnp.broadcast_to(m, (g, rows // g, m.shape[2])).reshape(rows, m.shape[2])


def _pack_rows(x):
    bits = lax.bitcast_convert_type(x.astype(BF16).astype(F32), jnp.uint32)
    w = x.shape[1] // 2
    return (bits[:, :w] >> 16) | bits[:, w:]


def _unpack_rows(words):
    lo = lax.bitcast_convert_type(words << 16, F32)
    hi = lax.bitcast_convert_type(words & jnp.uint32(0xFFFF0000), F32)
    return jnp.concatenate([lo, hi], axis=1)


def _ada_kernel(c_ref, w_ref, b_ref, o_ref):
    c = c_ref[...]
    a = (c * jax.nn.sigmoid(c)).astype(BF16)
    o_ref[...] = jnp.dot(a, w_ref[...].astype(BF16), preferred_element_type=F32) + b_ref[...]


def ada_terms(c, w, b):
    rows, d = c.shape
    n = w.shape[1]
    tn = 512
    return pl.pallas_call(
        _ada_kernel,
        grid=(n // tn,),
        in_specs=[pl.BlockSpec((rows, d), lambda j: (0, 0)),
                  pl.BlockSpec((d, tn), lambda j: (0, j)),
                  pl.BlockSpec((1, tn), lambda j: (0, j))],
        out_specs=pl.BlockSpec((rows, tn), lambda j: (0, j)),
        out_shape=jax.ShapeDtypeStruct((rows, n), F32),
        compiler_params=_cparams(("arbitrary",)),
        name="ada_terms",
    )(c, w, b.reshape(1, n))


def _in_kernel(with_kv, x_ref, sh_ref, sc_ref, g_ref, win_ref, gqa_ref, wq1_ref, wq2_ref,
               gkv_ref, wk_ref, wv_ref, wgk_ref, bgk_ref, qc_ref, qs_ref, kc_ref, ks_ref, *outs):
    if with_kv:
        q_out, k_out, v_out, ckv_out, kpe_out, gq_out, gk_out, gv_out, la_out, gg_out = outs
    else:
        q_out, ckv_out, kpe_out, gq_out, gk_out, gv_out, la_out, gg_out = outs
    x = x_ref[...]
    h = _rms(x, g_ref[...])
    rows = x.shape[0]
    h = h * (1.0 + _mod_rows(sc_ref, rows)) + _mod_rows(sh_ref, rows)
    proj = jnp.dot(h.astype(BF16), win_ref[...], preferred_element_type=F32)

    qn = _rms(proj[:, C_QA:C_KV], gqa_ref[...]).astype(BF16)
    qa = jnp.dot(qn, wq1_ref[...], preferred_element_type=F32)
    qb = jnp.dot(qn, wq2_ref[...], preferred_element_type=F32)
    qc = qc_ref[...]
    qs = qs_ref[...]
    for hh in range(MLA_HEADS):
        sl = slice(hh * HEAD_PAD, (hh + 1) * HEAD_PAD)
        q_out[:, sl] = (qa[:, sl] * qc + qb[:, sl] * qs).astype(BF16)

    ckv = _rms(proj[:, C_KV:C_GQ], gkv_ref[...])
    ckv_out[...] = ckv
    kpe = proj[:, C_MISC:C_SWAP] * kc_ref[...] + proj[:, C_SWAP:D_IN_PAD] * ks_ref[...]
    kpe_out[...] = kpe[:, :MLA_ROPE]
    if with_kv:
        ckv_b = ckv.astype(BF16)
        kn = jnp.dot(ckv_b, wk_ref[...], preferred_element_type=F32)
        kpe_sh = pltpu.roll(kpe, MLA_NOPE, axis=1)
        for hh in range(MLA_HEADS):
            sl = slice(hh * HEAD_PAD, (hh + 1) * HEAD_PAD)
            k_out[:, sl] = (kn[:, sl] + kpe_sh).astype(BF16)
        v_out[...] = jnp.dot(ckv_b, wv_ref[...], preferred_element_type=F32).astype(BF16)

    gq_out[...] = proj[:, C_GQ:C_GK] * (GLA_DK ** -0.5)
    gk_out[...] = proj[:, C_GK:C_GV]
    gv_out[...] = proj[:, C_GV:C_GG].astype(BF16)
    gg_out[...] = proj[:, C_GG:C_MISC].astype(BF16)
    xg = jnp.dot(proj[:, C_MISC:C_SWAP].astype(BF16), wgk_ref[...],
                 preferred_element_type=F32) + bgk_ref[...]
    la_out[...] = (jnp.minimum(xg, 0.0) - jnp.log(1.0 + jnp.exp(-jnp.abs(xg)))) * (1.0 / GLA_GATE_NORM)


def in_proj(x, mod, seq_rows, row_off, tabs, wts, with_kv):
    n = x.shape[0]
    tm = ROW_TILE
    nt = n // tm
    qc, qs, kc, ks = tabs
    n_tab = qc.shape[0] // tm
    row = lambda i: (i, 0)
    const = lambda i: (0, 0)
    tab_spec = pl.BlockSpec((tm, LANES), lambda i: (i % n_tab, 0))
    w = wts
    in_specs = [
        pl.BlockSpec((tm, D_MODEL), row),
        _mod_spec(0, seq_rows, row_off, tm), _mod_spec(1, seq_rows, row_off, tm),
        pl.BlockSpec((1, D_MODEL), const),
        pl.BlockSpec((D_MODEL, D_IN_PAD), const),
        pl.BlockSpec((1, MLA_Q_LORA), const),
        pl.BlockSpec((MLA_Q_LORA, MLA_PAD), const),
        pl.BlockSpec((MLA_Q_LORA, MLA_PAD), const),
        pl.BlockSpec((1, MLA_KV_LORA), const),
        pl.BlockSpec((MLA_KV_LORA, MLA_PAD), const),
        pl.BlockSpec((MLA_KV_LORA, MLA_WIDTH), const),
        pl.BlockSpec((LANES, GLA_QK), const),
        pl.BlockSpec((1, GLA_QK), const),
        tab_spec, tab_spec, tab_spec, tab_spec,
    ]
    wide = lambda dt: (jax.ShapeDtypeStruct((n, MLA_PAD), dt), pl.BlockSpec((tm, MLA_PAD), row))
    outs = [wide(BF16)]
    if with_kv:
        outs += [wide(BF16),
                 (jax.ShapeDtypeStruct((n, MLA_WIDTH), BF16), pl.BlockSpec((tm, MLA_WIDTH), row))]
    outs += [
        (jax.ShapeDtypeStruct((n, MLA_KV_LORA), F32), pl.BlockSpec((tm, MLA_KV_LORA), row)),
        (jax.ShapeDtypeStruct((n, MLA_ROPE), F32), pl.BlockSpec((tm, MLA_ROPE), row)),
        (jax.ShapeDtypeStruct((n, GLA_QK), F32), pl.BlockSpec((tm, GLA_QK), row)),
        (jax.ShapeDtypeStruct((n, GLA_QK), F32), pl.BlockSpec((tm, GLA_QK), row)),
        (jax.ShapeDtypeStruct((n, GLA_WIDTH), BF16), pl.BlockSpec((tm, GLA_WIDTH), row)),
        (jax.ShapeDtypeStruct((n, GLA_QK), F32), pl.BlockSpec((tm, GLA_QK), row)),
        (jax.ShapeDtypeStruct((n, GLA_WIDTH), BF16), pl.BlockSpec((tm, GLA_WIDTH), row)),
    ]
    return pl.pallas_call(
        functools.partial(_in_kernel, with_kv),
        grid=(nt,),
        in_specs=in_specs,
        out_specs=[o[1] for o in outs],
        out_shape=[o[0] for o in outs],
        compiler_params=_cparams(("parallel",)),
        name="inproj_prompt" if with_kv else "inproj_sample",
    )(x, mod, mod, w["g_norm_mix"], w["w_in"], w["g_q_a"], w["wq1"], w["wq2"], w["g_kv_a"],
      w["wk"], w["wv"], w["wgk"], w["b_gk"], qc, qs, kc, ks)


def _prefill_kernel(q_ref, k_ref, v_ref, o_ref):
    qi = pl.program_id(2)
    t = ATT_TILE
    lanes = [slice(g * HEAD_PAD, (g + 1) * HEAD_PAD) for g in range(ATT_HEADS)]
    v_lanes = [slice(g * MLA_V, (g + 1) * MLA_V) for g in range(ATT_HEADS)]
    qs = [q_ref[0, :, sl] for sl in lanes]

    def update_all(carries, r0, n_keys, mask):
        r0 = pl.multiple_of(r0, t)
        scores = [lax.dot_general(k_ref[0, pl.ds(r0, n_keys), sl], q, (((1,), (1,)), ((), ())),
                                  preferred_element_type=F32) for q, sl in zip(qs, lanes)]
        stats = []
        for (m, l, acc), s in zip(carries, scores):
            if mask is not None:
                s = jnp.where(mask, s, -jnp.inf)
            m_new = jnp.maximum(m, jnp.max(s, axis=0, keepdims=True))
            p = jnp.exp2(s - m_new)
            alpha = jnp.exp2(m - m_new)
            stats.append((m_new, alpha * l + jnp.sum(p, axis=0, keepdims=True), alpha * acc, p.astype(BF16)))
        out = []
        for (m_new, l_new, acc_scaled, p), sl in zip(stats, v_lanes):
            pv = lax.dot_general(v_ref[0, pl.ds(r0, n_keys), sl], p, (((0,), (0,)), ((), ())),
                                 preferred_element_type=F32)
            out.append((m_new, l_new, acc_scaled + pv))
        return tuple(out)

    def body(jj, carries):
        return update_all(carries, jj * (2 * t), 2 * t, None)

    init = (jnp.full((1, t), -jnp.inf, F32), jnp.zeros((1, t), F32), jnp.zeros((MLA_V, t), F32))
    carries = lax.fori_loop(0, qi // 2, body, (init,) * ATT_HEADS)

    def causal(n_keys):
        key_pos = lax.broadcasted_iota(jnp.int32, (n_keys, t), 0)
        qry_pos = lax.broadcasted_iota(jnp.int32, (n_keys, t), 1) + (n_keys - t)
        return key_pos <= qry_pos

    carries = lax.cond(qi % 2 == 1,
                       lambda c: update_all(c, (qi - 1) * t, 2 * t, causal(2 * t)),
                       lambda c: update_all(c, qi * t, t, causal(t)),
                       carries)
    o_ref[0] = jnp.concatenate([(acc / l).T for _, l, acc in carries], axis=1).astype(BF16)


def mla_prefill(q, k, v, batch, seq):
    t = ATT_TILE
    q3 = q.reshape(batch, seq, MLA_PAD)
    k3 = k.reshape(batch, seq, MLA_PAD)
    v3 = v.reshape(batch, seq, MLA_WIDTH)
    o = pl.pallas_call(
        _prefill_kernel,
        grid=(batch, MLA_HEADS // ATT_HEADS, seq // t),
        in_specs=[pl.BlockSpec((1, t, ATT_HEADS * HEAD_PAD), lambda b, h, i: (b, i, h)),
                  pl.BlockSpec((1, seq, ATT_HEADS * HEAD_PAD), lambda b, h, i: (b, 0, h)),
                  pl.BlockSpec((1, seq, ATT_HEADS * MLA_V), lambda b, h, i: (b, 0, h))],
        out_specs=pl.BlockSpec((1, t, ATT_HEADS * MLA_V), lambda b, h, i: (b, i, h)),
        out_shape=jax.ShapeDtypeStruct((batch, seq, MLA_WIDTH), BF16),
        compiler_params=_cparams(("parallel", "parallel", "arbitrary")),
        name="mla_prefill",
    )(q3, k3, v3)
    return o.reshape(batch * seq, MLA_WIDTH)


def _absorb_kernel(q_ref, wuk_ref, qlat_ref, qpe_ref):
    q = q_ref[...]
    qlat_ref[0] = jnp.dot(q, wuk_ref[0], preferred_element_type=F32)
    qf = pltpu.roll(q.astype(F32), HEAD_PAD - MLA_NOPE, axis=1)
    lane = lax.broadcasted_iota(jnp.int32, qf.shape, 1)
    qpe_ref[0] = jnp.where(lane < MLA_ROPE, qf, 0.0)


def absorb_q(q, wuk_t):
    r = q.shape[0]
    return pl.pallas_call(
        _absorb_kernel,
        grid=(MLA_HEADS,),
        in_specs=[pl.BlockSpec((r, HEAD_PAD), lambda h: (0, h)),
                  pl.BlockSpec((1, HEAD_PAD, MLA_KV_LORA), lambda h: (h, 0, 0))],
        out_specs=[pl.BlockSpec((1, r, MLA_KV_LORA), lambda h: (h, 0, 0)),
                   pl.BlockSpec((1, r, HEAD_PAD), lambda h: (h, 0, 0))],
        out_shape=[jax.ShapeDtypeStruct((MLA_HEADS, r, MLA_KV_LORA), F32),
                   jax.ShapeDtypeStruct((MLA_HEADS, r, HEAD_PAD), F32)],
        compiler_params=_cparams(("parallel",)),
        name="absorb_q",
    )(q, wuk_t)


def _decode_kernel(t_new, pt_ref, qlat_ref, qpe_ref, cnew_ref, knew_ref, ckv_hbm, kpe_hbm, o_ref,
                   cbuf, pbuf, sem):
    b = pl.program_id(0)
    nb = pl.num_programs(0)
    n_pages = cbuf.shape[1]
    slot = lax.rem(b, 2)
    rows = MLA_HEADS * t_new

    def page_copies(bb, sl, p):
        pg = pt_ref[bb, p]
        return (pltpu.make_async_copy(ckv_hbm.at[0, pg], cbuf.at[sl, p], sem.at[0, sl]),
                pltpu.make_async_copy(kpe_hbm.at[0, pg], pbuf.at[sl, p], sem.at[1, sl]))

    def fetch(bb, sl):
        def body(p, c):
            latent_cp, rope_cp = page_copies(bb, sl, p)
            latent_cp.start()
            rope_cp.start(priority=1)
            return c
        lax.fori_loop(0, n_pages, body, 0)

    @pl.when(b == 0)
    def _():
        fetch(0, 0)

    @pl.when(b + 1 < nb)
    def _():
        fetch(b + 1, 1 - slot)

    pltpu.make_async_copy(ckv_hbm.at[0, pl.ds(0, n_pages)], cbuf.at[slot], sem.at[0, slot]).wait()
    pltpu.make_async_copy(kpe_hbm.at[0, pl.ds(0, n_pages)], pbuf.at[slot], sem.at[1, slot]).wait()

    qlat = qlat_ref[...].reshape(rows, MLA_KV_LORA).astype(BF16)
    qpe = qpe_ref[...].reshape(rows, HEAD_PAD)[:, :MLA_ROPE].astype(BF16)
    dn = (((1,), (1,)), ((), ()))

    span_pages = n_pages // DEC_SPANS
    values, scores = [], []
    for c in range(DEC_SPANS):
        pages = range(c * span_pages, (c + 1) * span_pages)
        kb = jnp.concatenate([cbuf[slot, p].astype(BF16) for p in pages], axis=0)
        pb = jnp.concatenate([pbuf[slot, p].astype(BF16) for p in pages], axis=1)
        values.append(kb)
        scores.append(lax.dot_general(qlat, kb, dn, preferred_element_type=F32)
                      + jnp.dot(qpe, pb, preferred_element_type=F32))
    cn = cnew_ref[0].astype(BF16)
    kn = knew_ref[0].astype(BF16)
    sn = (lax.dot_general(qlat, cn, dn, preferred_element_type=F32)
          + lax.dot_general(qpe, kn, dn, preferred_element_type=F32))
    tq = lax.broadcasted_iota(jnp.int32, sn.shape, 0) % t_new
    tk = lax.broadcasted_iota(jnp.int32, sn.shape, 1)
    values.append(cn)
    scores.append(jnp.where(tk <= tq, sn, -jnp.inf))

    maxes = [jnp.max(s, axis=-1, keepdims=True) for s in scores]
    m = maxes[0]
    for pm in maxes[1:]:
        m = jnp.maximum(m, pm)
    probs = [jnp.exp2(s - pm) for s, pm in zip(scores, maxes)]
    l = jnp.zeros_like(m)
    acc = jnp.zeros((rows, MLA_KV_LORA), F32)
    for p, pm, vals in zip(probs, maxes, values):
        scale = jnp.exp2(pm - m)
        l = l + scale * jnp.sum(p, axis=-1, keepdims=True)
        acc = acc + scale * jnp.dot(p.astype(BF16), vals, preferred_element_type=F32)
    o_ref[0] = acc / l


def mla_decode(qlat, qpe, ckv_new, kpe_new, cache_ckv, cache_kpe, page_table, dec_batch, t_new):
    n_pages = page_table.shape[1]
    past_len = n_pages * PAGE_SIZE
    rows = MLA_HEADS * t_new
    kpe_t = jnp.swapaxes(cache_kpe, 2, 3)
    qlat4 = qlat.reshape(MLA_HEADS, dec_batch, t_new, MLA_KV_LORA)
    qpe4 = qpe.reshape(MLA_HEADS, dec_batch, t_new, HEAD_PAD)
    t_pad = NEW_PAD
    pad_new = lambda a: jnp.pad(a.reshape(dec_batch, t_new, a.shape[-1]), ((0, 0), (0, t_pad - t_new), (0, 0)))
    cnew = pad_new(ckv_new)
    knew = pad_new(kpe_new)

    in_specs = [
        pl.BlockSpec((MLA_HEADS, 1, t_new, MLA_KV_LORA), lambda b, pt: (0, b, 0, 0)),
        pl.BlockSpec((MLA_HEADS, 1, t_new, HEAD_PAD), lambda b, pt: (0, b, 0, 0)),
        pl.BlockSpec((1, t_pad, MLA_KV_LORA), lambda b, pt: (b, 0, 0)),
        pl.BlockSpec((1, t_pad, MLA_ROPE), lambda b, pt: (b, 0, 0)),
        pl.BlockSpec(memory_space=pl.ANY),
        pl.BlockSpec(memory_space=pl.ANY),
    ]
    grid_spec = pltpu.PrefetchScalarGridSpec(
        num_scalar_prefetch=1,
        grid=(dec_batch,),
        in_specs=in_specs,
        out_specs=pl.BlockSpec((1, rows, MLA_KV_LORA), lambda b, pt: (b, 0, 0)),
        scratch_shapes=[pltpu.VMEM((2, n_pages, PAGE_SIZE, MLA_KV_LORA), F32),
                        pltpu.VMEM((2, n_pages, MLA_ROPE, PAGE_SIZE), F32),
                        pltpu.SemaphoreType.DMA((2, 2))],
    )
    return pl.pallas_call(
        functools.partial(_decode_kernel, t_new),
        grid_spec=grid_spec,
        out_shape=jax.ShapeDtypeStruct((dec_batch, rows, MLA_KV_LORA), F32),
        compiler_params=_cparams(("arbitrary",)),
        name="mla_decode",
    )(page_table, qlat4, qpe4, cnew, knew, cache_ckv, kpe_t)


def _uv_kernel(o_ref, wuv_ref, out_ref):
    outs = []
    for hh in range(2):
        o = o_ref[:, hh]
        o = o.reshape(o.shape[0] * o.shape[1], MLA_KV_LORA).astype(BF16)
        outs.append(jnp.dot(o, wuv_ref[hh], preferred_element_type=F32))
    out_ref[...] = jnp.concatenate(outs, axis=1).astype(BF16)


def latent_to_values(o_lat, wuv, dec_batch, t_new):
    o4 = o_lat.reshape(dec_batch, MLA_HEADS, t_new, MLA_KV_LORA)
    return pl.pallas_call(
        _uv_kernel,
        grid=(MLA_HEADS // 2,),
        in_specs=[pl.BlockSpec((dec_batch, 2, t_new, MLA_KV_LORA), lambda h: (0, h, 0, 0)),
                  pl.BlockSpec((2, MLA_KV_LORA, MLA_V), lambda h: (h, 0, 0))],
        out_specs=pl.BlockSpec((dec_batch * t_new, 2 * MLA_V), lambda h: (0, h)),
        out_shape=jax.ShapeDtypeStruct((dec_batch * t_new, MLA_WIDTH), BF16),
        compiler_params=_cparams(("parallel",)),
        name="latent_to_values",
    )(o4, wuv)


def _gla_kernel(nsb, q_ref, k_ref, la_ref, v_ref, s0_ref, o_ref, sfin_ref, st_sc, kv_sc, sall_sc):
    blk = pl.program_id(1)
    nblk = pl.num_programs(1)
    c = GLA_CHUNK
    tb = q_ref.shape[1]
    nc = tb // c
    cps = nc // nsb
    head_k = [slice(hh * GLA_DK, (hh + 1) * GLA_DK) for hh in range(GLA_HEADS)]
    head_v = [slice(hh * GLA_DV, (hh + 1) * GLA_DV) for hh in range(GLA_HEADS)]

    @pl.when(blk == 0)
    def _():
        for sq in range(nsb):
            for hh in range(GLA_HEADS):
                st_sc[sq, hh] = s0_ref[sq, hh].T

    q = q_ref[0]
    k = k_ref[0]
    la = la_ref[0]
    v = v_ref[0]

    r = lax.broadcasted_iota(jnp.int32, (tb, tb), 0)
    cc = lax.broadcasted_iota(jnp.int32, (tb, tb), 1)
    same = (r >> 4) == (cc >> 4)
    tri = jnp.where(same & (cc <= r), 1.0, 0.0)
    ones = jnp.where(same, 1.0, 0.0)
    sel = jnp.concatenate([tri, ones], axis=0).astype(BF16)
    la_hi = la.astype(BF16)
    rest = la - la_hi.astype(F32)
    la_mid = rest.astype(BF16)
    la_lo = (rest - la_mid.astype(F32)).astype(BF16)
    sums = (jnp.dot(sel, la_hi, preferred_element_type=F32)
            + (jnp.dot(sel, la_mid, preferred_element_type=F32)
               + jnp.dot(sel, la_lo, preferred_element_type=F32)))
    b = sums[:tb]
    b_last = sums[tb:]
    qd = (q * jnp.exp(b)).astype(BF16)
    kd = (k * jnp.exp(b_last - b)).astype(BF16)
    dec = jnp.exp(b_last)

    key_head = lax.broadcasted_iota(jnp.int32, (GLA_QK, GLA_HEADS * c), 0) >> 6
    col = lax.broadcasted_iota(jnp.int32, (GLA_QK, GLA_HEADS * c), 1)
    b3 = b.reshape(nc, c, GLA_QK)
    k3 = k.reshape(nc, c, GLA_QK)
    q3 = q.reshape(nc, c, GLA_QK)
    pos = lax.broadcasted_iota(jnp.int32, (nc, c, GLA_QK), 1)
    scores = jnp.zeros((tb, GLA_HEADS * c), F32)
    for j in range(c):
        e = jnp.exp(b3 - b3[:, j:j + 1, :])
        w = jnp.where(pos >= j, q3 * k3[:, j:j + 1, :] * e, 0.0)
        place = jnp.where(col == key_head * c + j, 1.0, 0.0).astype(BF16)
        scores = scores + jnp.dot(w.reshape(tb, GLA_QK).astype(BF16), place, preferred_element_type=F32)
    scores = scores.astype(BF16)

    for ci in range(nc):
        rs = slice(ci * c, (ci + 1) * c)
        for hh in range(GLA_HEADS):
            o_ref[0, rs, head_v[hh]] = jnp.dot(scores[rs, hh * c:(hh + 1) * c], v[rs, head_v[hh]],
                                               preferred_element_type=F32)
            kv_sc[ci, hh] = lax.dot_general(v[rs, head_v[hh]], kd[rs, head_k[hh]], (((0,), (0,)), ((), ())),
                                            preferred_element_type=F32)

    for sq in range(nsb):
        states = [st_sc[sq, hh] for hh in range(GLA_HEADS)]
        for cj in range(cps):
            ci = sq * cps + cj
            dec_row = dec[ci * c:ci * c + 1, :]
            for hh in range(GLA_HEADS):
                sall_sc[ci, hh] = states[hh].astype(BF16)
                states[hh] = states[hh] * dec_row[:, head_k[hh]] + kv_sc[ci, hh]
        for hh in range(GLA_HEADS):
            st_sc[sq, hh] = states[hh]

    for ci in range(nc):
        rs = slice(ci * c, (ci + 1) * c)
        outs = [lax.dot_general(qd[rs, head_k[hh]], sall_sc[ci, hh], (((1,), (1,)), ((), ())),
                                preferred_element_type=F32) for hh in range(GLA_HEADS)]
        o_ref[0, rs, :] += jnp.concatenate(outs, axis=1)

    @pl.when(blk == nblk - 1)
    def _():
        for sq in range(nsb):
            for hh in range(GLA_HEADS):
                sfin_ref[sq, hh] = st_sc[sq, hh].T


def gla(gq, gk, la, gv, s0, n_seq, seq_len):
    tb = GLA_BLOCK
    nsb = max(1, tb // seq_len)
    nblk = max(1, seq_len // tb)
    n_outer = n_seq // nsb
    nc = tb // GLA_CHUNK
    sh3 = lambda a: a.reshape(n_outer, nblk * tb, a.shape[-1])
    row = lambda b, i: (b, i, 0)
    st_spec = pl.BlockSpec((nsb, GLA_HEADS, GLA_DK, GLA_DV), lambda b, i: (b, 0, 0, 0))
    o, s_fin = pl.pallas_call(
        functools.partial(_gla_kernel, nsb),
        grid=(n_outer, nblk),
        in_specs=[pl.BlockSpec((1, tb, GLA_QK), row), pl.BlockSpec((1, tb, GLA_QK), row),
                  pl.BlockSpec((1, tb, GLA_QK), row), pl.BlockSpec((1, tb, GLA_WIDTH), row), st_spec],
        out_specs=[pl.BlockSpec((1, tb, GLA_WIDTH), row), st_spec],
        out_shape=[jax.ShapeDtypeStruct((n_outer, nblk * tb, GLA_WIDTH), F32),
                   jax.ShapeDtypeStruct((n_seq, GLA_HEADS, GLA_DK, GLA_DV), F32)],
        scratch_shapes=[pltpu.VMEM((nsb, GLA_HEADS, GLA_DV, GLA_DK), F32),
                        pltpu.VMEM((nc, GLA_HEADS, GLA_DV, GLA_DK), F32),
                        pltpu.VMEM((nc, GLA_HEADS, GLA_DV, GLA_DK), BF16)],
        compiler_params=_cparams(("parallel", "arbitrary")),
        name="gla_prompt" if nblk > 1 else "gla_sample",
    )(sh3(gq), sh3(gk), sh3(la), sh3(gv), s0)
    return o.reshape(n_seq * seq_len, GLA_WIDTH), s_fin


def _mix_kernel(x_ref, om_ref, og_ref, gg_ref, gta_ref, shf_ref, scf_ref, gm_ref, ggl_ref, wo_ref,
                gn_ref, wr_ref, br_ref, h2a_in, h2b_in, x1_ref, h2a_ref, h2b_ref, ti_ref, tg_ref, hist_ref):
    del h2a_in, h2b_in
    om = _rms(om_ref[...].astype(F32), gm_ref[...])
    og = og_ref[...]
    gg = gg_ref[...].astype(F32)
    gate = gg * jax.nn.sigmoid(gg)
    parts = []
    for hh in range(GLA_HEADS):
        sl = slice(hh * GLA_DV, (hh + 1) * GLA_DV)
        parts.append(_rms(og[:, sl], ggl_ref[...]) * gate[:, sl])
    mix = jnp.concatenate([om] + parts, axis=1).astype(BF16)
    rows = x_ref.shape[0]
    x1 = x_ref[...] + _mod_rows(gta_ref, rows) * jnp.dot(mix, wo_ref[...], preferred_element_type=F32)
    x1_ref[...] = x1
    h2 = _rms(x1, gn_ref[...]) * (1.0 + _mod_rows(scf_ref, rows)) + _mod_rows(shf_ref, rows)
    words = _pack_rows(h2)
    h2a_ref[...] = words[:, :PERM_W]
    h2b_ref[...] = words[:, PERM_W:]
    h_hi = h2.astype(BF16)
    h_lo = (h2 - h_hi.astype(F32)).astype(BF16)
    logits = (jnp.dot(h_hi, wr_ref[0], preferred_element_type=F32)
              + (jnp.dot(h_hi, wr_ref[1], preferred_element_type=F32)
                 + jnp.dot(h_lo, wr_ref[0], preferred_element_type=F32))) + br_ref[...]
    lane_i = lax.broadcasted_iota(jnp.int32, logits.shape, 1)
    lane = lane_i.astype(F32)
    vals = []
    idxs = []
    for _ in range(TOP_K):
        mx = jnp.max(logits, axis=-1, keepdims=True)
        ix = jnp.min(jnp.where(logits == mx, lane, float(LANES)), axis=-1, keepdims=True)
        vals.append(mx)
        idxs.append(ix)
        logits = jnp.where(lane == ix, -jnp.inf, logits)
    ex = [jnp.exp(vv - vals[0]) for vv in vals]
    den = ex[0] + ex[1] + ex[2] + ex[3]
    ti = jnp.zeros(logits.shape, F32)
    tg = jnp.zeros(logits.shape, F32)
    onehot = jnp.zeros(logits.shape, F32)
    for kk in range(TOP_K):
        ti = jnp.where(lane_i == kk, idxs[kk], ti)
        tg = jnp.where(lane_i == kk, ex[kk] / den, tg)
        onehot = onehot + jnp.where(lane == idxs[kk], 1.0, 0.0)
    ti_ref[...] = ti.astype(jnp.int32)
    tg_ref[...] = tg
    hist_ref[0] = jnp.sum(onehot, axis=0, keepdims=True)


def mixer_out(x, o_mla, o_gla, gg, mod, seq_rows, row_off, w, n_all, row0, h2_buf):
    n = x.shape[0]
    tm = ROW_TILE
    t0 = row0 // tm
    row = lambda i: (i, 0)
    const = lambda i: (0, 0)
    extra_specs = [pl.BlockSpec(memory_space=pl.ANY)] * 2
    extra_args = list(h2_buf)
    mod_specs = [_mod_spec(term, seq_rows, row_off, tm) for term in (2, 3, 4)]
    d_mix = MLA_WIDTH + GLA_WIDTH
    n_in = 13
    return pl.pallas_call(
        _mix_kernel,
        grid=(n // tm,),
        in_specs=[pl.BlockSpec((tm, D_MODEL), row), pl.BlockSpec((tm, MLA_WIDTH), row),
                  pl.BlockSpec((tm, GLA_WIDTH), row), pl.BlockSpec((tm, GLA_WIDTH), row),
                  *mod_specs,
                  pl.BlockSpec((1, MLA_WIDTH), const), pl.BlockSpec((1, GLA_DV), const),
                  pl.BlockSpec((d_mix, D_MODEL), const), pl.BlockSpec((1, D_MODEL), const),
                  pl.BlockSpec((2, D_MODEL, LANES), lambda i: (0, 0, 0)),
                  pl.BlockSpec((1, LANES), const)] + extra_specs,
        out_specs=[pl.BlockSpec((tm, D_MODEL), row),
                   pl.BlockSpec((tm, PERM_W), lambda i: (i + t0, 0)),
                   pl.BlockSpec((tm, PERM_W), lambda i: (i + t0, 0)),
                   pl.BlockSpec((tm, LANES), row), pl.BlockSpec((tm, LANES), row),
                   pl.BlockSpec((1, 1, LANES), lambda i: (i, 0, 0))],
        out_shape=[jax.ShapeDtypeStruct((n, D_MODEL), F32),
                   jax.ShapeDtypeStruct((n_all, PERM_W), jnp.uint32),
                   jax.ShapeDtypeStruct((n_all, PERM_W), jnp.uint32),
                   jax.ShapeDtypeStruct((n, LANES), jnp.int32), jax.ShapeDtypeStruct((n, LANES), F32),
                   jax.ShapeDtypeStruct((n // tm, 1, LANES), F32)],
        input_output_aliases={n_in: 1, n_in + 1: 2},
        compiler_params=_cparams(("parallel",)),
        name="mixer_sample" if seq_rows < tm else "mixer_prompt",
    )(x, o_mla, o_gla, gg, mod, mod, mod, w["g_mla_out"], w["g_gla_out"], w["w_o"],
      w["g_norm_ffn"], w["w_router_pad"], w["b_router_pad"], *extra_args)


def _rank_kernel(ti_ref, base_ref, dest_ref):
    ti = ti_ref[...]
    tm = ti.shape[0]
    lane = lax.broadcasted_iota(jnp.int32, ti.shape, 1)
    cols = [ti[:, kk:kk + 1] for kk in range(TOP_K)]
    onehot = jnp.zeros(ti.shape, F32)
    for kk in range(TOP_K):
        onehot = onehot + jnp.where(lane == cols[kk], 1.0, 0.0)
    r = lax.broadcasted_iota(jnp.int32, (tm, tm), 0)
    c = lax.broadcasted_iota(jnp.int32, (tm, tm), 1)
    earlier = jnp.where(c < r, 1.0, 0.0).astype(BF16)
    pos = jnp.dot(earlier, onehot.astype(BF16), preferred_element_type=F32) + base_ref[0]
    out = jnp.zeros(ti.shape, F32)
    for kk in range(TOP_K):
        dk = jnp.sum(jnp.where(lane == cols[kk], pos, 0.0), axis=-1, keepdims=True)
        out = jnp.where(lane == kk, dk, out)
    dest_ref[...] = out.astype(jnp.int32)


def route_rank(ti, base):
    n = ti.shape[0]
    tm = RANK_TILE
    return pl.pallas_call(
        _rank_kernel,
        grid=(n // tm,),
        in_specs=[pl.BlockSpec((tm, LANES), lambda i: (i, 0)),
                  pl.BlockSpec((1, 1, LANES), lambda i: (i, 0, 0))],
        out_specs=pl.BlockSpec((tm, LANES), lambda i: (i, 0)),
        out_shape=jax.ShapeDtypeStruct((n, LANES), jnp.int32),
        compiler_params=_cparams(("parallel",)),
        name="route_rank",
    )(ti, base)


def _route_tables(hist, n_tok):
    tm = MOE_TILE
    h = hist[:, 0, :].astype(jnp.int32)
    h = h.reshape(-1, RANK_TILE // ROW_TILE, LANES).sum(axis=1)
    counts = jnp.sum(h, axis=0)
    padded = (counts + tm - 1) // tm * tm
    pad_ends = jnp.cumsum(padded)
    pad_starts = pad_ends - padded
    base = (pad_starts[None, :] + jnp.cumsum(h, axis=0) - h).astype(F32)[:, None, :]
    n_blocks = pl.cdiv(n_tok * TOP_K, tm) + N_EXPERTS
    n_active = (pad_ends[N_EXPERTS - 1] // tm).astype(jnp.int32)
    blk = jnp.arange(n_blocks, dtype=jnp.int32)
    blk_c = jnp.minimum(blk, n_active - 1)
    ends = pad_ends[:N_EXPERTS]
    block_e = jnp.minimum(jnp.sum((ends[None, :] <= (blk_c * tm)[:, None]).astype(jnp.int32), axis=1),
                          N_EXPERTS - 1).astype(jnp.int32)
    used_end = (pad_starts + counts)[:N_EXPERTS][block_e]
    block_rows = jnp.where(blk < n_active, jnp.clip(used_end - blk * tm, 0, tm), 0).astype(jnp.int32)
    return base, block_e, block_rows, n_blocks


def _sc_mesh():
    return plsc.VectorSubcoreMesh(core_axis_name="c", subcore_axis_name="s")


def sc_dispatch(x_rows, idx, n_out):
    n, wd = x_rows.shape
    win = SC_WIN
    nwin = n // win

    @functools.partial(pl.kernel, out_type=jax.ShapeDtypeStruct((n_out, wd), x_rows.dtype),
                       mesh=_sc_mesh(), scratch_types=[])
    def k(x_hbm, i_hbm, o_hbm):
        def body(x_vmem, i_vmem):
            pltpu.sync_copy(x_vmem, o_hbm.at[i_vmem.at[0]])

        pltpu.emit_pipeline(
            body,
            grid=(idx.shape[1] // win,),
            in_specs=[pl.BlockSpec((win, wd), lambda i: (i % nwin, 0)),
                      pl.BlockSpec((1, win), lambda i: (0, i))],
            out_specs=[],
            core_axis_name=("c", "s"),
            dimension_semantics=(pltpu.PARALLEL,),
        )(x_hbm, i_hbm)

    return k(x_rows, idx)


def sc_combine(y_rows, idx):
    wd = y_rows.shape[1]
    m = idx.shape[1]
    win = SC_WIN

    @functools.partial(pl.kernel, out_type=jax.ShapeDtypeStruct((m, wd), y_rows.dtype),
                       mesh=_sc_mesh(), scratch_types=[])
    def k(y_hbm, i_hbm, o_hbm):
        def body(i_vmem, o_vmem):
            pltpu.sync_copy(y_hbm.at[i_vmem.at[0]], o_vmem)

        pltpu.emit_pipeline(
            body,
            grid=(m // win,),
            in_specs=[pl.BlockSpec((1, win), lambda i: (0, i))],
            out_specs=[pl.BlockSpec((win, wd), lambda i: (i, 0))],
            core_axis_name=("c", "s"),
            dimension_semantics=(pltpu.PARALLEL,),
        )(i_hbm, o_hbm)

    return k(y_rows, idx)


def _moe_kernel(be_ref, nr_ref, xa_ref, xb_ref, wup_ref, bup_ref, wdn_ref, bdn_ref, ya_ref, yb_ref,
                wup_sc, wdn_sc):
    i = pl.program_id(0)
    n_real = nr_ref[i]
    prev = be_ref[jnp.maximum(i - 1, 0)]
    fresh = (i == 0) | (be_ref[i] != prev)

    @pl.when((n_real > 0) & fresh)
    def _():
        wup_sc[...] = wup_ref[0].astype(BF16)
        wdn_sc[...] = wdn_ref[0].astype(BF16)

    n_sub = MOE_TILE // MOE_SUB
    live_subs = (n_real + (MOE_SUB - 1)) // MOE_SUB
    for live in range(n_sub + 1):
        m = live * MOE_SUB

        @pl.when(live_subs == live)
        def _():
            if m > 0:
                xb = _unpack_rows(jnp.concatenate([xa_ref[:m, :], xb_ref[:m, :]], axis=1)).astype(BF16)
                hu = jnp.dot(xb, wup_sc[...], preferred_element_type=F32) + bup_ref[0]
                gate = jnp.minimum(hu[:, :D_FF], SWIGLU_LIMIT)
                lin = jnp.clip(hu[:, D_FF:], -SWIGLU_LIMIT, SWIGLU_LIMIT)
                act = gate * jax.nn.sigmoid(SWIGLU_ALPHA * gate) * (lin + 1.0)
                y = jnp.dot(act.astype(BF16), wdn_sc[...], preferred_element_type=F32) + bdn_ref[0]
                words = _pack_rows(y)
                ya_ref[:m, :] = words[:, :PERM_W]
                yb_ref[:m, :] = words[:, PERM_W:]
            if m < MOE_TILE:
                ya_ref[m:, :] = jnp.zeros((MOE_TILE - m, PERM_W), jnp.uint32)
                yb_ref[m:, :] = jnp.zeros((MOE_TILE - m, PERM_W), jnp.uint32)


def moe_experts(xs_a, xs_b, block_e, block_rows, w_up, b_up, w_down, b_down):
    n_rows = xs_a.shape[0]
    tm = MOE_TILE
    n_blocks = n_rows // tm
    emap3 = lambda i, be, na: (be[i], 0, 0)
    grid_spec = pltpu.PrefetchScalarGridSpec(
        num_scalar_prefetch=2,
        grid=(n_blocks,),
        in_specs=[pl.BlockSpec((tm, PERM_W), lambda i, be, na: (i, 0)),
                  pl.BlockSpec((tm, PERM_W), lambda i, be, na: (i, 0)),
                  pl.BlockSpec((1, D_MODEL, 2 * D_FF), emap3),
                  pl.BlockSpec((1, 1, 2 * D_FF), emap3),
                  pl.BlockSpec((1, D_FF, D_MODEL), emap3),
                  pl.BlockSpec((1, 1, D_MODEL), emap3)],
        out_specs=[pl.BlockSpec((tm, PERM_W), lambda i, be, na: (i, 0)),
                   pl.BlockSpec((tm, PERM_W), lambda i, be, na: (i, 0))],
        scratch_shapes=[pltpu.VMEM((D_MODEL, 2 * D_FF), BF16), pltpu.VMEM((D_FF, D_MODEL), BF16)],
    )
    return pl.pallas_call(
        _moe_kernel,
        grid_spec=grid_spec,
        out_shape=[jax.ShapeDtypeStruct((n_rows, PERM_W), jnp.uint32)] * 2,
        compiler_params=_cparams(("arbitrary",)),
        name="moe_experts",
    )(block_e, block_rows, xs_a, xs_b, w_up, b_up.reshape(N_EXPERTS, 1, 2 * D_FF), w_down,
      b_down.reshape(N_EXPERTS, 1, D_MODEL))


def _final_kernel(x1_ref, yga_ref, ygb_ref, tg_ref, gtf_ref, sh_ref, sc_ref, g_ref, y_ref):
    tg = tg_ref[...]
    moe = jnp.zeros(x1_ref.shape, F32)
    for kk in range(TOP_K):
        moe = moe + tg[:, kk:kk + 1] * _unpack_rows(jnp.concatenate([yga_ref[kk], ygb_ref[kk]], axis=1))
    rows = x1_ref.shape[0]
    x2 = x1_ref[...] + _mod_rows(gtf_ref, rows) * moe
    y_ref[...] = _rms(x2, g_ref[...]) * (1.0 + _mod_rows(sc_ref, rows)) + _mod_rows(sh_ref, rows)


def final_out(x1, yg_a, yg_b, tg, row0, mod, mod_f, seq_rows, row_off, g_final):
    n = x1.shape[0]
    tm = ROW_TILE
    t0 = row0 // tm
    row = lambda i: (i, 0)
    mod_specs = [_mod_spec(5, seq_rows, row_off, tm),
                 _mod_spec(0, seq_rows, row_off, tm), _mod_spec(1, seq_rows, row_off, tm)]
    return pl.pallas_call(
        _final_kernel,
        grid=(n // tm,),
        in_specs=[pl.BlockSpec((tm, D_MODEL), row),
                  pl.BlockSpec((TOP_K, tm, PERM_W), lambda i: (0, i + t0, 0)),
                  pl.BlockSpec((TOP_K, tm, PERM_W), lambda i: (0, i + t0, 0)),
                  pl.BlockSpec((tm, LANES), row),
                  *mod_specs,
                  pl.BlockSpec((1, D_MODEL), lambda i: (0, 0))],
        out_specs=pl.BlockSpec((tm, D_MODEL), row),
        out_shape=jax.ShapeDtypeStruct((n, D_MODEL), F32),
        compiler_params=_cparams(("parallel",)),
        name="final_sample" if seq_rows < tm else "final_prompt",
    )(x1, yg_a, yg_b, tg, mod, mod_f, mod_f, g_final)


def _prep_weights(w_in, g_q_a, w_q_b, g_kv_a, w_kv_b, w_gk_b, b_gk, g_mla_out, g_gla_out, w_o,
                  g_norm_mix, g_norm_ffn, w_router, b_router):
    sizes = (MLA_Q_LORA, MLA_KV_LORA, MLA_ROPE, GLA_QK, GLA_QK, GLA_WIDTH, GLA_GATE_RANK, GLA_WIDTH)
    offs = np.cumsum((0,) + sizes)
    part = lambda i: w_in[:, offs[i]:offs[i + 1]]
    half = MLA_ROPE // 2
    k_rope = part(2)
    misc = jnp.concatenate([k_rope, part(6), jnp.zeros((D_MODEL, LANES - MLA_ROPE - GLA_GATE_RANK), F32)], 1)
    swap = jnp.concatenate([-k_rope[:, half:], k_rope[:, :half],
                            jnp.zeros((D_MODEL, LANES - MLA_ROPE), F32)], 1)
    w_in_pad = jnp.concatenate([part(0), part(1), part(3), part(4), part(5), part(7), misc, swap], 1)

    pad_q = jnp.zeros((MLA_Q_LORA, MLA_HEADS, HEAD_PAD - MLA_NOPE - MLA_ROPE), F32)
    wq1 = jnp.concatenate([w_q_b, pad_q], axis=2)
    q_lo = w_q_b[:, :, MLA_NOPE:MLA_NOPE + half]
    q_hi = w_q_b[:, :, MLA_NOPE + half:]
    wq2 = jnp.concatenate([jnp.zeros((MLA_Q_LORA, MLA_HEADS, MLA_NOPE), F32), -q_hi, q_lo, pad_q], axis=2)
    pad_kv = jnp.zeros((MLA_KV_LORA, MLA_HEADS, HEAD_PAD - MLA_NOPE), F32)
    w_uk = w_kv_b[:, :, :MLA_NOPE]
    w_uv = w_kv_b[:, :, MLA_NOPE:]
    wk = jnp.concatenate([w_uk, pad_kv], axis=2)
    wuk_t = jnp.concatenate([jnp.transpose(w_uk, (1, 2, 0)),
                             jnp.zeros((MLA_HEADS, HEAD_PAD - MLA_NOPE, MLA_KV_LORA), F32)], axis=1)
    wuv_h = jnp.transpose(w_uv, (1, 0, 2))
    wgk = jnp.zeros((LANES, GLA_QK), F32).at[MLA_ROPE:MLA_ROPE + GLA_GATE_RANK].set(w_gk_b)
    w_router_pad = jnp.concatenate([w_router, jnp.zeros((D_MODEL, LANES - N_EXPERTS), F32)], axis=1)
    wr_hi = w_router_pad.astype(BF16)
    w_router_pad = jnp.stack([wr_hi, (w_router_pad - wr_hi.astype(F32)).astype(BF16)])
    b_router_pad = jnp.concatenate([b_router, jnp.full((LANES - N_EXPERTS,), -jnp.inf, F32)]).reshape(1, LANES)
    return dict(
        w_in=w_in_pad.astype(BF16), g_norm_mix=g_norm_mix.reshape(1, D_MODEL),
        g_q_a=g_q_a.reshape(1, MLA_Q_LORA),
        wq1=wq1.reshape(MLA_Q_LORA, MLA_PAD).astype(BF16), wq2=wq2.reshape(MLA_Q_LORA, MLA_PAD).astype(BF16),
        g_kv_a=g_kv_a.reshape(1, MLA_KV_LORA),
        wk=wk.reshape(MLA_KV_LORA, MLA_PAD).astype(BF16),
        wv=w_uv.reshape(MLA_KV_LORA, MLA_WIDTH).astype(BF16),
        wuk_t=wuk_t.astype(BF16), wuv_h=wuv_h.astype(BF16),
        wgk=wgk.astype(BF16), b_gk=b_gk.reshape(1, GLA_QK),
        g_mla_out=g_mla_out.reshape(1, MLA_WIDTH), g_gla_out=g_gla_out.reshape(1, GLA_DV),
        w_o=w_o.astype(BF16),
        g_norm_ffn=g_norm_ffn.reshape(1, D_MODEL), w_router_pad=w_router_pad, b_router_pad=b_router_pad,
    )


def _rope_tables(pos, reps):
    half = MLA_ROPE // 2
    inv = ROPE_THETA ** (-jnp.arange(half, dtype=F32) / half)
    ang = pos.astype(F32)[:, None] * inv
    cos, sin = jnp.cos(ang), jnp.sin(ang)
    n = pos.shape[0]
    qc = jnp.concatenate([jnp.full((n, MLA_NOPE), Q_SCALE, F32), Q_SCALE * cos, Q_SCALE * cos,
                          jnp.zeros((n, HEAD_PAD - MLA_NOPE - MLA_ROPE), F32)], axis=1)
    qs = jnp.concatenate([jnp.zeros((n, MLA_NOPE), F32), Q_SCALE * sin, Q_SCALE * sin,
                          jnp.zeros((n, HEAD_PAD - MLA_NOPE - MLA_ROPE), F32)], axis=1)
    kc = jnp.concatenate([cos, cos, jnp.zeros((n, LANES - MLA_ROPE), F32)], axis=1)
    ks = jnp.concatenate([sin, sin, jnp.zeros((n, LANES - MLA_ROPE), F32)], axis=1)
    return tuple(jnp.tile(t, (reps, 1)) for t in (qc, qs, kc, ks))


def kernel(x_prompt, x_sample, cache_ckv, cache_kpe, state_gla, page_table, c_prompt, c_sample, w_ada, b_ada, g_norm_mix, w_in, g_q_a, w_q_b, g_kv_a, w_kv_b, w_gk_b, b_gk, g_mla_out, g_gla_out, w_o, g_norm_ffn, w_router, b_router, w_up, b_up, w_down, b_down, g_norm_final, w_ada_final, b_ada_final):
    B, S, D = x_prompt.shape
    DB, T, _ = x_sample.shape
    depth = w_ada.shape[0]
    assert depth == 1
    past_len = page_table.shape[1] * cache_ckv.shape[2]
    n_p, n_s = B * S, DB * T
    l = 0

    w = _prep_weights(w_in[l], g_q_a[l], w_q_b[l], g_kv_a[l], w_kv_b[l], w_gk_b[l], b_gk[l],
                      g_mla_out[l], g_gla_out[l], w_o[l], g_norm_mix[l], g_norm_ffn[l],
                      w_router[l], b_router[l])

    n_c = B + DB
    n_c_pad = (n_c + 7) // 8 * 8
    c_all = jnp.concatenate([c_sample, c_prompt, jnp.zeros((n_c_pad - n_c, D), F32)], axis=0)
    mod = ada_terms(c_all, w_ada[l], b_ada[l]).reshape(n_c_pad, 1, N_MOD * D)
    mod_f = ada_terms(c_all, w_ada_final, b_ada_final).reshape(n_c_pad, 1, 2 * D)
    off_s, off_p = 0, DB

    xp = x_prompt.reshape(n_p, D)
    xs = x_sample.reshape(n_s, D)
    tabs_p = _rope_tables(jnp.arange(S), 1)
    tabs_s = _rope_tables(past_len + jnp.arange(T), ROW_TILE // T)

    (q_p, k_p, v_p, ckv_p, kpe_p, gq_p, gk_p, gv_p, la_p, gg_p) = in_proj(
        xp, mod, S, off_p, tabs_p, w, True)
    o_mla_p = mla_prefill(q_p, k_p, v_p, B, S)
    s0 = jnp.zeros((B, GLA_HEADS, GLA_DK, GLA_DV), F32)
    o_gla_p, gla_p = gla(gq_p, gk_p, la_p, gv_p, s0, B, S)
    n_all = n_p + n_s
    h2_init = (jnp.zeros((n_all, PERM_W), jnp.uint32), jnp.zeros((n_all, PERM_W), jnp.uint32))
    x1_p, h2a, h2b, ti_p, tg_p, hist_p = mixer_out(xp, o_mla_p, o_gla_p.reshape(n_p, GLA_WIDTH), gg_p,
                                                   mod, S, off_p, w, n_all, 0, h2_init)

    (q_s, ckv_s, kpe_s, gq_s, gk_s, gv_s, la_s, gg_s) = in_proj(
        xs, mod, T, off_s, tabs_s, w, False)
    qlat, qpe = absorb_q(q_s, w["wuk_t"])
    o_lat = mla_decode(qlat, qpe, ckv_s, kpe_s, cache_ckv, cache_kpe, page_table, DB, T)
    o_mla_s = latent_to_values(o_lat, w["wuv_h"], DB, T)
    tpad = GLA_CHUNK
    padt = lambda a: jnp.pad(a.reshape(DB, T, a.shape[-1]), ((0, 0), (0, tpad - T), (0, 0))).reshape(
        DB * tpad, a.shape[-1])
    o_gla_s, gla_s = gla(padt(gq_s), padt(gk_s), padt(la_s), padt(gv_s), state_gla[l], DB, tpad)
    o_gla_s = o_gla_s.reshape(DB, tpad, GLA_WIDTH)[:, :T].reshape(n_s, GLA_WIDTH)
    x1_s, h2a, h2b, ti_s, tg_s, hist_s = mixer_out(xs, o_mla_s, o_gla_s, gg_s,
                                                   mod, T, off_s, w, n_all, n_p, (h2a, h2b))

    ti = jnp.concatenate([ti_p, ti_s], axis=0)
    base, block_e, block_rows, n_blocks = _route_tables(jnp.concatenate([hist_p, hist_s], axis=0), n_all)
    dest = route_rank(ti, base)
    idx = dest[:, :TOP_K].T.reshape(1, TOP_K * n_all)
    n_rows = n_blocks * MOE_TILE
    xs_a = sc_dispatch(h2a, idx, n_rows)
    xs_b = sc_dispatch(h2b, idx, n_rows)
    ys_a, ys_b = moe_experts(xs_a, xs_b, block_e, block_rows, w_up[l], b_up[l], w_down[l], b_down[l])
    yg_a = sc_combine(ys_a, idx).reshape(TOP_K, n_all, PERM_W)
    yg_b = sc_combine(ys_b, idx).reshape(TOP_K, n_all, PERM_W)

    g_fin = g_norm_final.reshape(1, D)
    y_p = final_out(x1_p, yg_a, yg_b, tg_p, 0, mod, mod_f, S, off_p, g_fin)
    y_s = final_out(x1_s, yg_a, yg_b, tg_s, n_p, mod, mod_f, T, off_s, g_fin)

    return (y_p.reshape(B, S, D), y_s.reshape(DB, T, D),
            ckv_p.reshape(1, B, S, MLA_KV_LORA), kpe_p.reshape(1, B, S, MLA_ROPE), gla_p[None],
            ckv_s.reshape(1, DB, T, MLA_KV_LORA), kpe_s.reshape(1, DB, T, MLA_ROPE), gla_s[None])
```

```python
import functools
import math

import jax
import jax.numpy as jnp
import numpy as np
from jax import lax
from jax.experimental import pallas as pl
from jax.experimental.pallas import tpu as pltpu
from jax.experimental.pallas import tpu_sc as plsc

F32 = jnp.float32
BF16 = jnp.bfloat16

D_MODEL = 1024
MLA_HEADS = 8
MLA_NOPE = 64
MLA_ROPE = 32
MLA_V = 64
MLA_Q_LORA = 384
MLA_KV_LORA = 256
MLA_SCALE = (MLA_NOPE + MLA_ROPE) ** -0.5
ROPE_THETA = 10000.0
GLA_HEADS = 4
GLA_DK = 64
GLA_DV = 128
GLA_GATE_RANK = 16
GLA_GATE_NORM = 16.0
GLA_CHUNK = 16
GLA_QK = GLA_HEADS * GLA_DK
GLA_WIDTH = GLA_HEADS * GLA_DV
N_EXPERTS = 32
TOP_K = 4
D_FF = D_MODEL
SWIGLU_LIMIT = 7.0
SWIGLU_ALPHA = 1.702
N_MOD = 6
EPS = 1e-6
PAGE_SIZE = 128

LANES = 128
HEAD_PAD = LANES
MLA_PAD = MLA_HEADS * HEAD_PAD
MLA_WIDTH = MLA_HEADS * MLA_V
VMEM_LIMIT = 56 * 1024 * 1024

Q_SCALE = MLA_SCALE * math.log2(math.e)

C_QA = 0
C_KV = C_QA + MLA_Q_LORA
C_GQ = C_KV + MLA_KV_LORA
C_GK = C_GQ + GLA_QK
C_GV = C_GK + GLA_QK
C_GG = C_GV + GLA_WIDTH
C_MISC = C_GG + GLA_WIDTH
C_SWAP = C_MISC + LANES
D_IN_PAD = C_SWAP + LANES

ROW_TILE = 512
RANK_TILE = 1024
ATT_TILE = 512
ATT_HEADS = 4
GLA_BLOCK = 256
DEC_SPANS = 4
SC_WIN = 128
PERM_W = D_MODEL // 4
MOE_TILE = 768
MOE_SUB = 128
NEW_PAD = 16


def _cparams(sem):
    return pltpu.CompilerParams(dimension_semantics=sem, vmem_limit_bytes=VMEM_LIMIT)


def _rms(x, g):
    return x * lax.rsqrt(jnp.mean(x * x, axis=-1, keepdims=True) + EPS) * g


def _mod_spec(term, seq_rows, row_off, tm):
    if seq_rows >= tm:
        tiles_per_seq = seq_rows // tm
        return pl.BlockSpec((1, 1, D_MODEL), lambda i: (row_off + i // tiles_per_seq, 0, term))
    g = tm // seq_rows
    assert row_off % g == 0
    return pl.BlockSpec((g, 1, D_MODEL), lambda i: (row_off // g + i, 0, term))


def _mod_rows(ref, rows):
    m = ref[...]
    g = m.shape[0]
    if g == 1:
        return m[0]
    return jnp.broadcast_to(m, (g, rows // g, m.shape[2])).reshape(rows, m.shape[2])


def _pack_rows(x):
    bits = lax.bitcast_convert_type(x.astype(BF16).astype(F32), jnp.uint32)
    w = x.shape[1] // 2
    return (bits[:, :w] >> 16) | bits[:, w:]


def _unpack_rows(words):
    lo = lax.bitcast_convert_type(words << 16, F32)
    hi = lax.bitcast_convert_type(words & jnp.uint32(0xFFFF0000), F32)
    return jnp.concatenate([lo, hi], axis=1)


def _ada_kernel(c_ref, w_ref, b_ref, o_ref):
    c = c_ref[...]
    a = (c * jax.nn.sigmoid(c)).astype(BF16)
    o_ref[...] = jnp.dot(a, w_ref[...].astype(BF16), preferred_element_type=F32) + b_ref[...]


def ada_terms(c, w, b):
    rows, d = c.shape
    n = w.shape[1]
    tn = 512
    return pl.pallas_call(
        _ada_kernel,
        grid=(n // tn,),
        in_specs=[pl.BlockSpec((rows, d), lambda j: (0, 0)),
                  pl.BlockSpec((d, tn), lambda j: (0, j)),
                  pl.BlockSpec((1, tn), lambda j: (0, j))],
        out_specs=pl.BlockSpec((rows, tn), lambda j: (0, j)),
        out_shape=jax.ShapeDtypeStruct((rows, n), F32),
        compiler_params=_cparams(("arbitrary",)),
        name="ada_terms",
    )(c, w, b.reshape(1, n))


def _in_kernel(with_kv, x_ref, sh_ref, sc_ref, g_ref, win_ref, gqa_ref, wq1_ref, wq2_ref,
               gkv_ref, wk_ref, wv_ref, wgk_ref, bgk_ref, qc_ref, qs_ref, kc_ref, ks_ref, *outs):
    if with_kv:
        q_out, k_out, v_out, ckv_out, kpe_out, gq_out, gk_out, gv_out, la_out, gg_out = outs
    else:
        q_out, ckv_out, kpe_out, gq_out, gk_out, gv_out, la_out, gg_out = outs
    x = x_ref[...]
    h = _rms(x, g_ref[...])
    rows = x.shape[0]
    h = h * (1.0 + _mod_rows(sc_ref, rows)) + _mod_rows(sh_ref, rows)
    proj = jnp.dot(h.astype(BF16), win_ref[...], preferred_element_type=F32)

    qn = _rms(proj[:, C_QA:C_KV], gqa_ref[...]).astype(BF16)
    qa = jnp.dot(qn, wq1_ref[...], preferred_element_type=F32)
    qb = jnp.dot(qn, wq2_ref[...], preferred_element_type=F32)
    qc = qc_ref[...]
    qs = qs_ref[...]
    for hh in range(MLA_HEADS):
        sl = slice(hh * HEAD_PAD, (hh + 1) * HEAD_PAD)
        q_out[:, sl] = (qa[:, sl] * qc + qb[:, sl] * qs).astype(BF16)

    ckv = _rms(proj[:, C_KV:C_GQ], gkv_ref[...])
    ckv_out[...] = ckv
    kpe = proj[:, C_MISC:C_SWAP] * kc_ref[...] + proj[:, C_SWAP:D_IN_PAD] * ks_ref[...]
    if with_kv:
        kpe_out[0] = kpe.T[:MLA_ROPE, :]
    else:
        kpe_out[...] = kpe[:, :MLA_ROPE]
    if with_kv:
        ckv_b = ckv.astype(BF16)
        kn = jnp.dot(ckv_b, wk_ref[...], preferred_element_type=F32)
        kpe_sh = pltpu.roll(kpe, MLA_NOPE, axis=1)
        for hh in range(MLA_HEADS):
            sl = slice(hh * HEAD_PAD, (hh + 1) * HEAD_PAD)
            k_out[:, sl] = (kn[:, sl] + kpe_sh).astype(BF16)
        v_out[...] = jnp.dot(ckv_b, wv_ref[...], preferred_element_type=F32).astype(BF16)

    gq_out[...] = proj[:, C_GQ:C_GK] * (GLA_DK ** -0.5)
    gk_out[...] = proj[:, C_GK:C_GV]
    gv_out[...] = proj[:, C_GV:C_GG].astype(BF16)
    gg_out[...] = proj[:, C_GG:C_MISC].astype(BF16)
    xg = jnp.dot(proj[:, C_MISC:C_SWAP].astype(BF16), wgk_ref[...],
                 preferred_element_type=F32) + bgk_ref[...]
    la_out[...] = (jnp.minimum(xg, 0.0) - jnp.log(1.0 + jnp.exp(-jnp.abs(xg)))) * (1.0 / GLA_GATE_NORM)


def in_proj(x, mod, seq_rows, row_off, tabs, wts, with_kv):
    n = x.shape[0]
    tm = ROW_TILE
    nt = n // tm
    qc, qs, kc, ks = tabs
    n_tab = qc.shape[0] // tm
    row = lambda i: (i, 0)
    const = lambda i: (0, 0)
    tab_spec = pl.BlockSpec((tm, LANES), lambda i: (i % n_tab, 0))
    w = wts
    in_specs = [
        pl.BlockSpec((tm, D_MODEL), row),
        _mod_spec(0, seq_rows, row_off, tm), _mod_spec(1, seq_rows, row_off, tm),
        pl.BlockSpec((1, D_MODEL), const),
        pl.BlockSpec((D_MODEL, D_IN_PAD), const),
        pl.BlockSpec((1, MLA_Q_LORA), const),
        pl.BlockSpec((MLA_Q_LORA, MLA_PAD), const),
        pl.BlockSpec((MLA_Q_LORA, MLA_PAD), const),
        pl.BlockSpec((1, MLA_KV_LORA), const),
        pl.BlockSpec((MLA_KV_LORA, MLA_PAD), const),
        pl.BlockSpec((MLA_KV_LORA, MLA_WIDTH), const),
        pl.BlockSpec((LANES, GLA_QK), const),
        pl.BlockSpec((1, GLA_QK), const),
        tab_spec, tab_spec, tab_spec, tab_spec,
    ]
    wide = lambda dt: (jax.ShapeDtypeStruct((n, MLA_PAD), dt), pl.BlockSpec((tm, MLA_PAD), row))
    outs = [wide(BF16)]
    if with_kv:
        outs += [wide(BF16),
                 (jax.ShapeDtypeStruct((n, MLA_WIDTH), BF16), pl.BlockSpec((tm, MLA_WIDTH), row))]
    if with_kv:
        tps = seq_rows // tm
        kpe_out = (jax.ShapeDtypeStruct((n // seq_rows, MLA_ROPE, seq_rows), F32),
                   pl.BlockSpec((1, MLA_ROPE, tm), lambda i: (i // tps, 0, i % tps)))
    else:
        kpe_out = (jax.ShapeDtypeStruct((n, MLA_ROPE), F32), pl.BlockSpec((tm, MLA_ROPE), row))
    outs += [
        (jax.ShapeDtypeStruct((n, MLA_KV_LORA), F32), pl.BlockSpec((tm, MLA_KV_LORA), row)),
        kpe_out,
        (jax.ShapeDtypeStruct((n, GLA_QK), F32), pl.BlockSpec((tm, GLA_QK), row)),
        (jax.ShapeDtypeStruct((n, GLA_QK), F32), pl.BlockSpec((tm, GLA_QK), row)),
        (jax.ShapeDtypeStruct((n, GLA_WIDTH), BF16), pl.BlockSpec((tm, GLA_WIDTH), row)),
        (jax.ShapeDtypeStruct((n, GLA_QK), F32), pl.BlockSpec((tm, GLA_QK), row)),
        (jax.ShapeDtypeStruct((n, GLA_WIDTH), BF16), pl.BlockSpec((tm, GLA_WIDTH), row)),
    ]
    return pl.pallas_call(
        functools.partial(_in_kernel, with_kv),
        grid=(nt,),
        in_specs=in_specs,
        out_specs=[o[1] for o in outs],
        out_shape=[o[0] for o in outs],
        compiler_params=_cparams(("parallel",)),
        name="inproj_prompt" if with_kv else "inproj_sample",
    )(x, mod, mod, w["g_norm_mix"], w["w_in"], w["g_q_a"], w["wq1"], w["wq2"], w["g_kv_a"],
      w["wk"], w["wv"], w["wgk"], w["b_gk"], qc, qs, kc, ks)


def _prefill_kernel(q_ref, k_ref, v_ref, o_ref):
    qi = pl.program_id(2)
    t = ATT_TILE
    lanes = [slice(g * HEAD_PAD, (g + 1) * HEAD_PAD) for g in range(ATT_HEADS)]
    v_lanes = [slice(g * MLA_V, (g + 1) * MLA_V) for g in range(ATT_HEADS)]
    qs = [q_ref[0, :, sl] for sl in lanes]

    def update_all(carries, r0, n_keys, mask):
        r0 = pl.multiple_of(r0, t)
        scores = [lax.dot_general(k_ref[0, pl.ds(r0, n_keys), sl], q, (((1,), (1,)), ((), ())),
                                  preferred_element_type=F32) for q, sl in zip(qs, lanes)]
        stats = []
        for (m, l, acc), s in zip(carries, scores):
            if mask is not None:
                s = jnp.where(mask, s, -jnp.inf)
            m_new = jnp.maximum(m, jnp.max(s, axis=0, keepdims=True))
            p = jnp.exp2(s - m_new)
            alpha = jnp.exp2(m - m_new)
            stats.append((m_new, alpha * l + jnp.sum(p, axis=0, keepdims=True), alpha * acc, p.astype(BF16)))
        out = []
        for (m_new, l_new, acc_scaled, p), sl in zip(stats, v_lanes):
            pv = lax.dot_general(v_ref[0, pl.ds(r0, n_keys), sl], p, (((0,), (0,)), ((), ())),
                                 preferred_element_type=F32)
            out.append((m_new, l_new, acc_scaled + pv))
        return tuple(out)

    def body(jj, carries):
        return update_all(carries, jj * (2 * t), 2 * t, None)

    init = (jnp.full((1, t), -jnp.inf, F32), jnp.zeros((1, t), F32), jnp.zeros((MLA_V, t), F32))
    carries = lax.fori_loop(0, qi // 2, body, (init,) * ATT_HEADS)

    def causal(n_keys):
        key_pos = lax.broadcasted_iota(jnp.int32, (n_keys, t), 0)
        qry_pos = lax.broadcasted_iota(jnp.int32, (n_keys, t), 1) + (n_keys - t)
        return key_pos <= qry_pos

    carries = lax.cond(qi % 2 == 1,
                       lambda c: update_all(c, (qi - 1) * t, 2 * t, causal(2 * t)),
                       lambda c: update_all(c, qi * t, t, causal(t)),
                       carries)
    o_ref[0] = jnp.concatenate([(acc / l).T for _, l, acc in carries], axis=1).astype(BF16)


def mla_prefill(q, k, v, batch, seq):
    t = ATT_TILE
    q3 = q.reshape(batch, seq, MLA_PAD)
    k3 = k.reshape(batch, seq, MLA_PAD)
    v3 = v.reshape(batch, seq, MLA_WIDTH)
    o = pl.pallas_call(
        _prefill_kernel,
        grid=(batch, MLA_HEADS // ATT_HEADS, seq // t),
        in_specs=[pl.BlockSpec((1, t, ATT_HEADS * HEAD_PAD), lambda b, h, i: (b, i, h)),
                  pl.BlockSpec((1, seq, ATT_HEADS * HEAD_PAD), lambda b, h, i: (b, 0, h)),
                  pl.BlockSpec((1, seq, ATT_HEADS * MLA_V), lambda b, h, i: (b, 0, h))],
        out_specs=pl.BlockSpec((1, t, ATT_HEADS * MLA_V), lambda b, h, i: (b, i, h)),
        out_shape=jax.ShapeDtypeStruct((batch, seq, MLA_WIDTH), BF16),
        compiler_params=_cparams(("parallel", "parallel", "arbitrary")),
        name="mla_prefill",
    )(q3, k3, v3)
    return o.reshape(batch * seq, MLA_WIDTH)


def _absorb_kernel(q_ref, wuk_ref, qlat_ref, qpe_ref):
    q = q_ref[...]
    qlat_ref[0] = jnp.dot(q, wuk_ref[0], preferred_element_type=F32)
    qf = pltpu.roll(q.astype(F32), HEAD_PAD - MLA_NOPE, axis=1)
    lane = lax.broadcasted_iota(jnp.int32, qf.shape, 1)
    qpe_ref[0] = jnp.where(lane < MLA_ROPE, qf, 0.0)


def absorb_q(q, wuk_t):
    r = q.shape[0]
    return pl.pallas_call(
        _absorb_kernel,
        grid=(MLA_HEADS,),
        in_specs=[pl.BlockSpec((r, HEAD_PAD), lambda h: (0, h)),
                  pl.BlockSpec((1, HEAD_PAD, MLA_KV_LORA), lambda h: (h, 0, 0))],
        out_specs=[pl.BlockSpec((1, r, MLA_KV_LORA), lambda h: (h, 0, 0)),
                   pl.BlockSpec((1, r, HEAD_PAD), lambda h: (h, 0, 0))],
        out_shape=[jax.ShapeDtypeStruct((MLA_HEADS, r, MLA_KV_LORA), F32),
                   jax.ShapeDtypeStruct((MLA_HEADS, r, HEAD_PAD), F32)],
        compiler_params=_cparams(("parallel",)),
        name="absorb_q",
    )(q, wuk_t)


def _decode_kernel(t_new, pt_ref, qlat_ref, qpe_ref, cnew_ref, knew_ref, ckv_hbm, kpe_hbm, o_ref,
                   cbuf, pbuf, sem):
    b = pl.program_id(0)
    nb = pl.num_programs(0)
    n_pages = cbuf.shape[1]
    slot = lax.rem(b, 2)
    rows = MLA_HEADS * t_new

    def page_copies(bb, sl, p):
        pg = pt_ref[bb, p]
        return (pltpu.make_async_copy(ckv_hbm.at[0, pg], cbuf.at[sl, p], sem.at[0, sl]),
                pltpu.make_async_copy(kpe_hbm.at[0, pg], pbuf.at[sl, p], sem.at[1, sl]))

    def fetch(bb, sl):
        def body(p, c):
            latent_cp, rope_cp = page_copies(bb, sl, p)
            latent_cp.start()
            rope_cp.start(priority=1)
            return c
        lax.fori_loop(0, n_pages, body, 0)

    @pl.when(b == 0)
    def _():
        fetch(0, 0)

    @pl.when(b + 1 < nb)
    def _():
        fetch(b + 1, 1 - slot)

    pltpu.make_async_copy(ckv_hbm.at[0, pl.ds(0, n_pages)], cbuf.at[slot], sem.at[0, slot]).wait()
    pltpu.make_async_copy(kpe_hbm.at[0, pl.ds(0, n_pages)], pbuf.at[slot], sem.at[1, slot]).wait()

    qlat = qlat_ref[...].reshape(rows, MLA_KV_LORA).astype(BF16)
    qpe = qpe_ref[...].reshape(rows, HEAD_PAD)[:, :MLA_ROPE].astype(BF16)
    dn = (((1,), (1,)), ((), ()))

    span_pages = n_pages // DEC_SPANS
    values, scores = [], []
    for c in range(DEC_SPANS):
        pages = range(c * span_pages, (c + 1) * span_pages)
        kb = jnp.concatenate([cbuf[slot, p].astype(BF16) for p in pages], axis=0)
        pb = jnp.concatenate([pbuf[slot, p].astype(BF16) for p in pages], axis=1)
        values.append(kb)
        scores.append(lax.dot_general(qlat, kb, dn, preferred_element_type=F32)
                      + jnp.dot(qpe, pb, preferred_element_type=F32))
    cn = cnew_ref[0].astype(BF16)
    kn = knew_ref[0].astype(BF16)
    sn = (lax.dot_general(qlat, cn, dn, preferred_element_type=F32)
          + lax.dot_general(qpe, kn, dn, preferred_element_type=F32))
    tq = lax.broadcasted_iota(jnp.int32, sn.shape, 0) % t_new
    tk = lax.broadcasted_iota(jnp.int32, sn.shape, 1)
    values.append(cn)
    scores.append(jnp.where(tk <= tq, sn, -jnp.inf))

    maxes = [jnp.max(s, axis=-1, keepdims=True) for s in scores]
    m = maxes[0]
    for pm in maxes[1:]:
        m = jnp.maximum(m, pm)
    probs = [jnp.exp2(s - pm) for s, pm in zip(scores, maxes)]
    l = jnp.zeros_like(m)
    acc = jnp.zeros((rows, MLA_KV_LORA), F32)
    for p, pm, vals in zip(probs, maxes, values):
        scale = jnp.exp2(pm - m)
        l = l + scale * jnp.sum(p, axis=-1, keepdims=True)
        acc = acc + scale * jnp.dot(p.astype(BF16), vals, preferred_element_type=F32)
    o_ref[0] = acc / l


def mla_decode(qlat, qpe, ckv_new, kpe_new, cache_ckv, cache_kpe, page_table, dec_batch, t_new):
    n_pages = page_table.shape[1]
    past_len = n_pages * PAGE_SIZE
    rows = MLA_HEADS * t_new
    kpe_t = jnp.swapaxes(cache_kpe, 2, 3)
    qlat4 = qlat.reshape(MLA_HEADS, dec_batch, t_new, MLA_KV_LORA)
    qpe4 = qpe.reshape(MLA_HEADS, dec_batch, t_new, HEAD_PAD)
    t_pad = NEW_PAD
    pad_new = lambda a: jnp.pad(a.reshape(dec_batch, t_new, a.shape[-1]), ((0, 0), (0, t_pad - t_new), (0, 0)))
    cnew = pad_new(ckv_new)
    knew = pad_new(kpe_new)

    in_specs = [
        pl.BlockSpec((MLA_HEADS, 1, t_new, MLA_KV_LORA), lambda b, pt: (0, b, 0, 0)),
        pl.BlockSpec((MLA_HEADS, 1, t_new, HEAD_PAD), lambda b, pt: (0, b, 0, 0)),
        pl.BlockSpec((1, t_pad, MLA_KV_LORA), lambda b, pt: (b, 0, 0)),
        pl.BlockSpec((1, t_pad, MLA_ROPE), lambda b, pt: (b, 0, 0)),
        pl.BlockSpec(memory_space=pl.ANY),
        pl.BlockSpec(memory_space=pl.ANY),
    ]
    grid_spec = pltpu.PrefetchScalarGridSpec(
        num_scalar_prefetch=1,
        grid=(dec_batch,),
        in_specs=in_specs,
        out_specs=pl.BlockSpec((1, rows, MLA_KV_LORA), lambda b, pt: (b, 0, 0)),
        scratch_shapes=[pltpu.VMEM((2, n_pages, PAGE_SIZE, MLA_KV_LORA), F32),
                        pltpu.VMEM((2, n_pages, MLA_ROPE, PAGE_SIZE), F32),
                        pltpu.SemaphoreType.DMA((2, 2))],
    )
    return pl.pallas_call(
        functools.partial(_decode_kernel, t_new),
        grid_spec=grid_spec,
        out_shape=jax.ShapeDtypeStruct((dec_batch, rows, MLA_KV_LORA), F32),
        compiler_params=_cparams(("arbitrary",)),
        name="mla_decode",
    )(page_table, qlat4, qpe4, cnew, knew, cache_ckv, kpe_t)


def _uv_kernel(o_ref, wuv_ref, out_ref):
    outs = []
    for hh in range(2):
        o = o_ref[:, hh]
        o = o.reshape(o.shape[0] * o.shape[1], MLA_KV_LORA).astype(BF16)
        outs.append(jnp.dot(o, wuv_ref[hh], preferred_element_type=F32))
    out_ref[...] = jnp.concatenate(outs, axis=1).astype(BF16)


def latent_to_values(o_lat, wuv, dec_batch, t_new):
    o4 = o_lat.reshape(dec_batch, MLA_HEADS, t_new, MLA_KV_LORA)
    return pl.pallas_call(
        _uv_kernel,
        grid=(MLA_HEADS // 2,),
        in_specs=[pl.BlockSpec((dec_batch, 2, t_new, MLA_KV_LORA), lambda h: (0, h, 0, 0)),
                  pl.BlockSpec((2, MLA_KV_LORA, MLA_V), lambda h: (h, 0, 0))],
        out_specs=pl.BlockSpec((dec_batch * t_new, 2 * MLA_V), lambda h: (0, h)),
        out_shape=jax.ShapeDtypeStruct((dec_batch * t_new, MLA_WIDTH), BF16),
        compiler_params=_cparams(("parallel",)),
        name="latent_to_values",
    )(o4, wuv)


def _gla_kernel(nsb, q_ref, k_ref, la_ref, v_ref, s0_ref, o_ref, sfin_ref, st_sc, kv_sc, sall_sc):
    blk = pl.program_id(1)
    nblk = pl.num_programs(1)
    c = GLA_CHUNK
    tb = q_ref.shape[1]
    nc = tb // c
    cps = nc // nsb
    head_k = [slice(hh * GLA_DK, (hh + 1) * GLA_DK) for hh in range(GLA_HEADS)]
    head_v = [slice(hh * GLA_DV, (hh + 1) * GLA_DV) for hh in range(GLA_HEADS)]

    @pl.when(blk == 0)
    def _():
        for sq in range(nsb):
            for hh in range(GLA_HEADS):
                st_sc[sq, hh] = s0_ref[sq, hh].T

    q = q_ref[0]
    k = k_ref[0]
    la = la_ref[0]
    v = v_ref[0]

    r = lax.broadcasted_iota(jnp.int32, (tb, tb), 0)
    cc = lax.broadcasted_iota(jnp.int32, (tb, tb), 1)
    same = (r >> 4) == (cc >> 4)
    tri = jnp.where(same & (cc <= r), 1.0, 0.0)
    ones = jnp.where(same, 1.0, 0.0)
    sel = jnp.concatenate([tri, ones], axis=0).astype(BF16)
    la_hi = la.astype(BF16)
    rest = la - la_hi.astype(F32)
    la_mid = rest.astype(BF16)
    la_lo = (rest - la_mid.astype(F32)).astype(BF16)
    sums = (jnp.dot(sel, la_hi, preferred_element_type=F32)
            + (jnp.dot(sel, la_mid, preferred_element_type=F32)
               + jnp.dot(sel, la_lo, preferred_element_type=F32)))
    b = sums[:tb]
    b_last = sums[tb:]
    qd = (q * jnp.exp(b)).astype(BF16)
    kd = (k * jnp.exp(b_last - b)).astype(BF16)
    dec = jnp.exp(b_last)

    key_head = lax.broadcasted_iota(jnp.int32, (GLA_QK, GLA_HEADS * c), 0) >> 6
    col = lax.broadcasted_iota(jnp.int32, (GLA_QK, GLA_HEADS * c), 1)
    b3 = b.reshape(nc, c, GLA_QK)
    k3 = k.reshape(nc, c, GLA_QK)
    q3 = q.reshape(nc, c, GLA_QK)
    pos = lax.broadcasted_iota(jnp.int32, (nc, c, GLA_QK), 1)
    scores = jnp.zeros((tb, GLA_HEADS * c), F32)
    for j in range(c):
        e = jnp.exp(b3 - b3[:, j:j + 1, :])
        w = jnp.where(pos >= j, q3 * k3[:, j:j + 1, :] * e, 0.0)
        place = jnp.where(col == key_head * c + j, 1.0, 0.0).astype(BF16)
        scores = scores + jnp.dot(w.reshape(tb, GLA_QK).astype(BF16), place, preferred_element_type=F32)
    scores = scores.astype(BF16)

    for ci in range(nc):
        rs = slice(ci * c, (ci + 1) * c)
        for hh in range(GLA_HEADS):
            o_ref[0, rs, head_v[hh]] = jnp.dot(scores[rs, hh * c:(hh + 1) * c], v[rs, head_v[hh]],
                                               preferred_element_type=F32)
            kv_sc[ci, hh] = lax.dot_general(v[rs, head_v[hh]], kd[rs, head_k[hh]], (((0,), (0,)), ((), ())),
                                            preferred_element_type=F32)

    for sq in range(nsb):
        states = [st_sc[sq, hh] for hh in range(GLA_HEADS)]
        for cj in range(cps):
            ci = sq * cps + cj
            dec_row = dec[ci * c:ci * c + 1, :]
            for hh in range(GLA_HEADS):
                sall_sc[ci, hh] = states[hh].astype(BF16)
                states[hh] = states[hh] * dec_row[:, head_k[hh]] + kv_sc[ci, hh]
        for hh in range(GLA_HEADS):
            st_sc[sq, hh] = states[hh]

    for ci in range(nc):
        rs = slice(ci * c, (ci + 1) * c)
        outs = [lax.dot_general(qd[rs, head_k[hh]], sall_sc[ci, hh], (((1,), (1,)), ((), ())),
                                preferred_element_type=F32) for hh in range(GLA_HEADS)]
        o_ref[0, rs, :] += jnp.concatenate(outs, axis=1)

    @pl.when(blk == nblk - 1)
    def _():
        for sq in range(nsb):
            for hh in range(GLA_HEADS):
                sfin_ref[sq, hh] = st_sc[sq, hh].T


def gla(gq, gk, la, gv, s0, n_seq, seq_len):
    tb = GLA_BLOCK
    nsb = max(1, tb // seq_len)
    nblk = max(1, seq_len // tb)
    n_outer = n_seq // nsb
    nc = tb // GLA_CHUNK
    sh3 = lambda a: a.reshape(n_outer, nblk * tb, a.shape[-1])
    row = lambda b, i: (b, i, 0)
    st_spec = pl.BlockSpec((nsb, GLA_HEADS, GLA_DK, GLA_DV), lambda b, i: (b, 0, 0, 0))
    o, s_fin = pl.pallas_call(
        functools.partial(_gla_kernel, nsb),
        grid=(n_outer, nblk),
        in_specs=[pl.BlockSpec((1, tb, GLA_QK), row), pl.BlockSpec((1, tb, GLA_QK), row),
                  pl.BlockSpec((1, tb, GLA_QK), row), pl.BlockSpec((1, tb, GLA_WIDTH), row), st_spec],
        out_specs=[pl.BlockSpec((1, tb, GLA_WIDTH), row), st_spec],
        out_shape=[jax.ShapeDtypeStruct((n_outer, nblk * tb, GLA_WIDTH), F32),
                   jax.ShapeDtypeStruct((n_seq, GLA_HEADS, GLA_DK, GLA_DV), F32)],
        scratch_shapes=[pltpu.VMEM((nsb, GLA_HEADS, GLA_DV, GLA_DK), F32),
                        pltpu.VMEM((nc, GLA_HEADS, GLA_DV, GLA_DK), F32),
                        pltpu.VMEM((nc, GLA_HEADS, GLA_DV, GLA_DK), BF16)],
        compiler_params=_cparams(("parallel", "arbitrary")),
        name="gla_prompt" if nblk > 1 else "gla_sample",
    )(sh3(gq), sh3(gk), sh3(la), sh3(gv), s0)
    return o.reshape(n_seq * seq_len, GLA_WIDTH), s_fin


def _mix_kernel(x_ref, om_ref, og_ref, gg_ref, gta_ref, shf_ref, scf_ref, gm_ref, ggl_ref, wo_ref,
                gn_ref, wr_ref, br_ref, h2a_in, h2b_in, x1_ref, h2a_ref, h2b_ref, ti_ref, tg_ref, hist_ref):
    del h2a_in, h2b_in
    om = _rms(om_ref[...].astype(F32), gm_ref[...])
    og = og_ref[...]
    gg = gg_ref[...].astype(F32)
    gate = gg * jax.nn.sigmoid(gg)
    parts = []
    for hh in range(GLA_HEADS):
        sl = slice(hh * GLA_DV, (hh + 1) * GLA_DV)
        parts.append(_rms(og[:, sl], ggl_ref[...]) * gate[:, sl])
    mix = jnp.concatenate([om] + parts, axis=1).astype(BF16)
    rows = x_ref.shape[0]
    x1 = x_ref[...] + _mod_rows(gta_ref, rows) * jnp.dot(mix, wo_ref[...], preferred_element_type=F32)
    x1_ref[...] = x1
    h2 = _rms(x1, gn_ref[...]) * (1.0 + _mod_rows(scf_ref, rows)) + _mod_rows(shf_ref, rows)
    words = _pack_rows(h2)
    h2a_ref[...] = words[:, :PERM_W]
    h2b_ref[...] = words[:, PERM_W:]
    h_hi = h2.astype(BF16)
    h_lo = (h2 - h_hi.astype(F32)).astype(BF16)
    logits = (jnp.dot(h_hi, wr_ref[0], preferred_element_type=F32)
              + (jnp.dot(h_hi, wr_ref[1], preferred_element_type=F32)
                 + jnp.dot(h_lo, wr_ref[0], preferred_element_type=F32))) + br_ref[...]
    lane_i = lax.broadcasted_iota(jnp.int32, logits.shape, 1)
    lane = lane_i.astype(F32)
    vals = []
    idxs = []
    for _ in range(TOP_K):
        mx = jnp.max(logits, axis=-1, keepdims=True)
        ix = jnp.min(jnp.where(logits == mx, lane, float(LANES)), axis=-1, keepdims=True)
        vals.append(mx)
        idxs.append(ix)
        logits = jnp.where(lane == ix, -jnp.inf, logits)
    ex = [jnp.exp(vv - vals[0]) for vv in vals]
    den = ex[0] + ex[1] + ex[2] + ex[3]
    ti = jnp.zeros(logits.shape, F32)
    tg = jnp.zeros(logits.shape, F32)
    onehot = jnp.zeros(logits.shape, F32)
    for kk in range(TOP_K):
        ti = jnp.where(lane_i == kk, idxs[kk], ti)
        tg = jnp.where(lane_i == kk, ex[kk] / den, tg)
        onehot = onehot + jnp.where(lane == idxs[kk], 1.0, 0.0)
    ti_ref[...] = ti.astype(jnp.int32)
    tg_ref[...] = tg
    hist_ref[0] = jnp.sum(onehot, axis=0, keepdims=True)


def mixer_out(x, o_mla, o_gla, gg, mod, seq_rows, row_off, w, n_all, row0, h2_buf):
    n = x.shape[0]
    tm = ROW_TILE
    t0 = row0 // tm
    row = lambda i: (i, 0)
    const = lambda i: (0, 0)
    extra_specs = [pl.BlockSpec(memory_space=pl.ANY)] * 2
    extra_args = list(h2_buf)
    mod_specs = [_mod_spec(term, seq_rows, row_off, tm) for term in (2, 3, 4)]
    d_mix = MLA_WIDTH + GLA_WIDTH
    n_in = 13
    return pl.pallas_call(
        _mix_kernel,
        grid=(n // tm,),
        in_specs=[pl.BlockSpec((tm, D_MODEL), row), pl.BlockSpec((tm, MLA_WIDTH), row),
                  pl.BlockSpec((tm, GLA_WIDTH), row), pl.BlockSpec((tm, GLA_WIDTH), row),
                  *mod_specs,
                  pl.BlockSpec((1, MLA_WIDTH), const), pl.BlockSpec((1, GLA_DV), const),
                  pl.BlockSpec((d_mix, D_MODEL), const), pl.BlockSpec((1, D_MODEL), const),
                  pl.BlockSpec((2, D_MODEL, LANES), lambda i: (0, 0, 0)),
                  pl.BlockSpec((1, LANES), const)] + extra_specs,
        out_specs=[pl.BlockSpec((tm, D_MODEL), row),
                   pl.BlockSpec((tm, PERM_W), lambda i: (i + t0, 0)),
                   pl.BlockSpec((tm, PERM_W), lambda i: (i + t0, 0)),
                   pl.BlockSpec((tm, LANES), row), pl.BlockSpec((tm, LANES), row),
                   pl.BlockSpec((1, 1, LANES), lambda i: (i, 0, 0))],
        out_shape=[jax.ShapeDtypeStruct((n, D_MODEL), F32),
                   jax.ShapeDtypeStruct((n_all, PERM_W), jnp.uint32),
                   jax.ShapeDtypeStruct((n_all, PERM_W), jnp.uint32),
                   jax.ShapeDtypeStruct((n, LANES), jnp.int32), jax.ShapeDtypeStruct((n, LANES), F32),
                   jax.ShapeDtypeStruct((n // tm, 1, LANES), F32)],
        input_output_aliases={n_in: 1, n_in + 1: 2},
        compiler_params=_cparams(("parallel",)),
        name="mixer_sample" if seq_rows < tm else "mixer_prompt",
    )(x, o_mla, o_gla, gg, mod, mod, mod, w["g_mla_out"], w["g_gla_out"], w["w_o"],
      w["g_norm_ffn"], w["w_router_pad"], w["b_router_pad"], *extra_args)


def _rank_kernel(ti_ref, base_ref, dest_ref):
    ti = ti_ref[...]
    tm = ti.shape[0]
    lane = lax.broadcasted_iota(jnp.int32, ti.shape, 1)
    cols = [ti[:, kk:kk + 1] for kk in range(TOP_K)]
    onehot = jnp.zeros(ti.shape, F32)
    for kk in range(TOP_K):
        onehot = onehot + jnp.where(lane == cols[kk], 1.0, 0.0)
    r = lax.broadcasted_iota(jnp.int32, (tm, tm), 0)
    c = lax.broadcasted_iota(jnp.int32, (tm, tm), 1)
    earlier = jnp.where(c < r, 1.0, 0.0).astype(BF16)
    pos = jnp.dot(earlier, onehot.astype(BF16), preferred_element_type=F32) + base_ref[0]
    out = jnp.zeros(ti.shape, F32)
    for kk in range(TOP_K):
        dk = jnp.sum(jnp.where(lane == cols[kk], pos, 0.0), axis=-1, keepdims=True)
        out = jnp.where(lane == kk, dk, out)
    dest_ref[...] = out.astype(jnp.int32)


def route_rank(ti, base):
    n = ti.shape[0]
    tm = RANK_TILE
    return pl.pallas_call(
        _rank_kernel,
        grid=(n // tm,),
        in_specs=[pl.BlockSpec((tm, LANES), lambda i: (i, 0)),
                  pl.BlockSpec((1, 1, LANES), lambda i: (i, 0, 0))],
        out_specs=pl.BlockSpec((tm, LANES), lambda i: (i, 0)),
        out_shape=jax.ShapeDtypeStruct((n, LANES), jnp.int32),
        compiler_params=_cparams(("parallel",)),
        name="route_rank",
    )(ti, base)


def _route_tables(hist, n_tok):
    tm = MOE_TILE
    h = hist[:, 0, :].astype(jnp.int32)
    h = h.reshape(-1, RANK_TILE // ROW_TILE, LANES).sum(axis=1)
    counts = jnp.sum(h, axis=0)
    padded = (counts + tm - 1) // tm * tm
    pad_ends = jnp.cumsum(padded)
    pad_starts = pad_ends - padded
    base = (pad_starts[None, :] + jnp.cumsum(h, axis=0) - h).astype(F32)[:, None, :]
    n_blocks = pl.cdiv(n_tok * TOP_K, tm) + N_EXPERTS
    n_active = (pad_ends[N_EXPERTS - 1] // tm).astype(jnp.int32)
    blk = jnp.arange(n_blocks, dtype=jnp.int32)
    blk_c = jnp.minimum(blk, n_active - 1)
    ends = pad_ends[:N_EXPERTS]
    block_e = jnp.minimum(jnp.sum((ends[None, :] <= (blk_c * tm)[:, None]).astype(jnp.int32), axis=1),
                          N_EXPERTS - 1).astype(jnp.int32)
    used_end = (pad_starts + counts)[:N_EXPERTS][block_e]
    block_rows = jnp.where(blk < n_active, jnp.clip(used_end - blk * tm, 0, tm), 0).astype(jnp.int32)
    return base, block_e, block_rows, n_blocks


def _sc_mesh():
    return plsc.VectorSubcoreMesh(core_axis_name="c", subcore_axis_name="s")


def sc_dispatch(x_rows, idx, n_out):
    n, wd = x_rows.shape
    win = SC_WIN
    nwin = n // win

    @functools.partial(pl.kernel, out_type=jax.ShapeDtypeStruct((n_out, wd), x_rows.dtype),
                       mesh=_sc_mesh(), scratch_types=[])
    def k(x_hbm, i_hbm, o_hbm):
        def body(x_vmem, i_vmem):
            pltpu.sync_copy(x_vmem, o_hbm.at[i_vmem.at[0]])

        pltpu.emit_pipeline(
            body,
            grid=(idx.shape[1] // win,),
            in_specs=[pl.BlockSpec((win, wd), lambda i: (i % nwin, 0)),
                      pl.BlockSpec((1, win), lambda i: (0, i))],
            out_specs=[],
            core_axis_name=("c", "s"),
            dimension_semantics=(pltpu.PARALLEL,),
        )(x_hbm, i_hbm)

    return k(x_rows, idx)


def sc_combine(y_rows, idx):
    wd = y_rows.shape[1]
    m = idx.shape[1]
    win = SC_WIN

    @functools.partial(pl.kernel, out_type=jax.ShapeDtypeStruct((m, wd), y_rows.dtype),
                       mesh=_sc_mesh(), scratch_types=[])
    def k(y_hbm, i_hbm, o_hbm):
        def body(i_vmem, o_vmem):
            pltpu.sync_copy(y_hbm.at[i_vmem.at[0]], o_vmem)

        pltpu.emit_pipeline(
            body,
            grid=(m // win,),
            in_specs=[pl.BlockSpec((1, win), lambda i: (0, i))],
            out_specs=[pl.BlockSpec((win, wd), lambda i: (i, 0))],
            core_axis_name=("c", "s"),
            dimension_semantics=(pltpu.PARALLEL,),
        )(i_hbm, o_hbm)

    return k(y_rows, idx)


def _moe_kernel(be_ref, nr_ref, xa_ref, xb_ref, wup_ref, bup_ref, wdn_ref, bdn_ref, ya_ref, yb_ref,
                wup_sc, wdn_sc):
    i = pl.program_id(0)
    n_real = nr_ref[i]
    prev = be_ref[jnp.maximum(i - 1, 0)]
    fresh = (i == 0) | (be_ref[i] != prev)

    @pl.when((n_real > 0) & fresh)
    def _():
        wup_sc[...] = wup_ref[0].astype(BF16)
        wdn_sc[...] = wdn_ref[0].astype(BF16)

    n_sub = MOE_TILE // MOE_SUB
    live_subs = (n_real + (MOE_SUB - 1)) // MOE_SUB
    for live in range(n_sub + 1):
        m = live * MOE_SUB

        @pl.when(live_subs == live)
        def _():
            if m > 0:
                xb = _unpack_rows(jnp.concatenate([xa_ref[:m, :], xb_ref[:m, :]], axis=1)).astype(BF16)
                hu = jnp.dot(xb, wup_sc[...], preferred_element_type=F32) + bup_ref[0]
                gate = jnp.minimum(hu[:, :D_FF], SWIGLU_LIMIT)
                lin = jnp.clip(hu[:, D_FF:], -SWIGLU_LIMIT, SWIGLU_LIMIT)
                act = gate * jax.nn.sigmoid(SWIGLU_ALPHA * gate) * (lin + 1.0)
                y = jnp.dot(act.astype(BF16), wdn_sc[...], preferred_element_type=F32) + bdn_ref[0]
                words = _pack_rows(y)
                ya_ref[:m, :] = words[:, :PERM_W]
                yb_ref[:m, :] = words[:, PERM_W:]
            if m < MOE_TILE:
                ya_ref[m:, :] = jnp.zeros((MOE_TILE - m, PERM_W), jnp.uint32)
                yb_ref[m:, :] = jnp.zeros((MOE_TILE - m, PERM_W), jnp.uint32)


def moe_experts(xs_a, xs_b, block_e, block_rows, w_up, b_up, w_down, b_down):
    n_rows = xs_a.shape[0]
    tm = MOE_TILE
    n_blocks = n_rows // tm
    emap3 = lambda i, be, na: (be[i], 0, 0)
    grid_spec = pltpu.PrefetchScalarGridSpec(
        num_scalar_prefetch=2,
        grid=(n_blocks,),
        in_specs=[pl.BlockSpec((tm, PERM_W), lambda i, be, na: (i, 0)),
                  pl.BlockSpec((tm, PERM_W), lambda i, be, na: (i, 0)),
                  pl.BlockSpec((1, D_MODEL, 2 * D_FF), emap3),
                  pl.BlockSpec((1, 1, 2 * D_FF), emap3),
                  pl.BlockSpec((1, D_FF, D_MODEL), emap3),
                  pl.BlockSpec((1, 1, D_MODEL), emap3)],
        out_specs=[pl.BlockSpec((tm, PERM_W), lambda i, be, na: (i, 0)),
                   pl.BlockSpec((tm, PERM_W), lambda i, be, na: (i, 0))],
        scratch_shapes=[pltpu.VMEM((D_MODEL, 2 * D_FF), BF16), pltpu.VMEM((D_FF, D_MODEL), BF16)],
    )
    return pl.pallas_call(
        _moe_kernel,
        grid_spec=grid_spec,
        out_shape=[jax.ShapeDtypeStruct((n_rows, PERM_W), jnp.uint32)] * 2,
        compiler_params=_cparams(("arbitrary",)),
        name="moe_experts",
    )(block_e, block_rows, xs_a, xs_b, w_up, b_up.reshape(N_EXPERTS, 1, 2 * D_FF), w_down,
      b_down.reshape(N_EXPERTS, 1, D_MODEL))


def _final_kernel(x1_ref, yga_ref, ygb_ref, tg_ref, gtf_ref, sh_ref, sc_ref, g_ref, y_ref):
    tg = tg_ref[...]
    moe = jnp.zeros(x1_ref.shape, F32)
    for kk in range(TOP_K):
        moe = moe + tg[:, kk:kk + 1] * _unpack_rows(jnp.concatenate([yga_ref[kk], ygb_ref[kk]], axis=1))
    rows = x1_ref.shape[0]
    x2 = x1_ref[...] + _mod_rows(gtf_ref, rows) * moe
    y_ref[...] = _rms(x2, g_ref[...]) * (1.0 + _mod_rows(sc_ref, rows)) + _mod_rows(sh_ref, rows)


def final_out(x1, yg_a, yg_b, tg, row0, mod, mod_f, seq_rows, row_off, g_final):
    n = x1.shape[0]
    tm = ROW_TILE
    t0 = row0 // tm
    row = lambda i: (i, 0)
    mod_specs = [_mod_spec(5, seq_rows, row_off, tm),
                 _mod_spec(0, seq_rows, row_off, tm), _mod_spec(1, seq_rows, row_off, tm)]
    return pl.pallas_call(
        _final_kernel,
        grid=(n // tm,),
        in_specs=[pl.BlockSpec((tm, D_MODEL), row),
                  pl.BlockSpec((TOP_K, tm, PERM_W), lambda i: (0, i + t0, 0)),
                  pl.BlockSpec((TOP_K, tm, PERM_W), lambda i: (0, i + t0, 0)),
                  pl.BlockSpec((tm, LANES), row),
                  *mod_specs,
                  pl.BlockSpec((1, D_MODEL), lambda i: (0, 0))],
        out_specs=pl.BlockSpec((tm, D_MODEL), row),
        out_shape=jax.ShapeDtypeStruct((n, D_MODEL), F32),
        compiler_params=_cparams(("parallel",)),
        name="final_sample" if seq_rows < tm else "final_prompt",
    )(x1, yg_a, yg_b, tg, mod, mod_f, mod_f, g_final)


def _prep_weights(w_in, g_q_a, w_q_b, g_kv_a, w_kv_b, w_gk_b, b_gk, g_mla_out, g_gla_out, w_o,
                  g_norm_mix, g_norm_ffn, w_router, b_router):
    sizes = (MLA_Q_LORA, MLA_KV_LORA, MLA_ROPE, GLA_QK, GLA_QK, GLA_WIDTH, GLA_GATE_RANK, GLA_WIDTH)
    offs = np.cumsum((0,) + sizes)
    part = lambda i: w_in[:, offs[i]:offs[i + 1]]
    half = MLA_ROPE // 2
    k_rope = part(2)
    misc = jnp.concatenate([k_rope, part(6), jnp.zeros((D_MODEL, LANES - MLA_ROPE - GLA_GATE_RANK), F32)], 1)
    swap = jnp.concatenate([-k_rope[:, half:], k_rope[:, :half],
                            jnp.zeros((D_MODEL, LANES - MLA_ROPE), F32)], 1)
    w_in_pad = jnp.concatenate([part(0), part(1), part(3), part(4), part(5), part(7), misc, swap], 1)

    pad_q = jnp.zeros((MLA_Q_LORA, MLA_HEADS, HEAD_PAD - MLA_NOPE - MLA_ROPE), F32)
    wq1 = jnp.concatenate([w_q_b, pad_q], axis=2)
    q_lo = w_q_b[:, :, MLA_NOPE:MLA_NOPE + half]
    q_hi = w_q_b[:, :, MLA_NOPE + half:]
    wq2 = jnp.concatenate([jnp.zeros((MLA_Q_LORA, MLA_HEADS, MLA_NOPE), F32), -q_hi, q_lo, pad_q], axis=2)
    pad_kv = jnp.zeros((MLA_KV_LORA, MLA_HEADS, HEAD_PAD - MLA_NOPE), F32)
    w_uk = w_kv_b[:, :, :MLA_NOPE]
    w_uv = w_kv_b[:, :, MLA_NOPE:]
    wk = jnp.concatenate([w_uk, pad_kv], axis=2)
    wuk_t = jnp.concatenate([jnp.transpose(w_uk, (1, 2, 0)),
                             jnp.zeros((MLA_HEADS, HEAD_PAD - MLA_NOPE, MLA_KV_LORA), F32)], axis=1)
    wuv_h = jnp.transpose(w_uv, (1, 0, 2))
    wgk = jnp.zeros((LANES, GLA_QK), F32).at[MLA_ROPE:MLA_ROPE + GLA_GATE_RANK].set(w_gk_b)
    w_router_pad = jnp.concatenate([w_router, jnp.zeros((D_MODEL, LANES - N_EXPERTS), F32)], axis=1)
    wr_hi = w_router_pad.astype(BF16)
    w_router_pad = jnp.stack([wr_hi, (w_router_pad - wr_hi.astype(F32)).astype(BF16)])
    b_router_pad = jnp.concatenate([b_router, jnp.full((LANES - N_EXPERTS,), -jnp.inf, F32)]).reshape(1, LANES)
    return dict(
        w_in=w_in_pad.astype(BF16), g_norm_mix=g_norm_mix.reshape(1, D_MODEL),
        g_q_a=g_q_a.reshape(1, MLA_Q_LORA),
        wq1=wq1.reshape(MLA_Q_LORA, MLA_PAD).astype(BF16), wq2=wq2.reshape(MLA_Q_LORA, MLA_PAD).astype(BF16),
        g_kv_a=g_kv_a.reshape(1, MLA_KV_LORA),
        wk=wk.reshape(MLA_KV_LORA, MLA_PAD).astype(BF16),
        wv=w_uv.reshape(MLA_KV_LORA, MLA_WIDTH).astype(BF16),
        wuk_t=wuk_t.astype(BF16), wuv_h=wuv_h.astype(BF16),
        wgk=wgk.astype(BF16), b_gk=b_gk.reshape(1, GLA_QK),
        g_mla_out=g_mla_out.reshape(1, MLA_WIDTH), g_gla_out=g_gla_out.reshape(1, GLA_DV),
        w_o=w_o.astype(BF16),
        g_norm_ffn=g_norm_ffn.reshape(1, D_MODEL), w_router_pad=w_router_pad, b_router_pad=b_router_pad,
    )


def _rope_tables(pos, reps):
    half = MLA_ROPE // 2
    inv = ROPE_THETA ** (-jnp.arange(half, dtype=F32) / half)
    ang = pos.astype(F32)[:, None] * inv
    cos, sin = jnp.cos(ang), jnp.sin(ang)
    n = pos.shape[0]
    qc = jnp.concatenate([jnp.full((n, MLA_NOPE), Q_SCALE, F32), Q_SCALE * cos, Q_SCALE * cos,
                          jnp.zeros((n, HEAD_PAD - MLA_NOPE - MLA_ROPE), F32)], axis=1)
    qs = jnp.concatenate([jnp.zeros((n, MLA_NOPE), F32), Q_SCALE * sin, Q_SCALE * sin,
                          jnp.zeros((n, HEAD_PAD - MLA_NOPE - MLA_ROPE), F32)], axis=1)
    kc = jnp.concatenate([cos, cos, jnp.zeros((n, LANES - MLA_ROPE), F32)], axis=1)
    ks = jnp.concatenate([sin, sin, jnp.zeros((n, LANES - MLA_ROPE), F32)], axis=1)
    return tuple(jnp.tile(t, (reps, 1)) for t in (qc, qs, kc, ks))


def kernel(x_prompt, x_sample, cache_ckv, cache_kpe, state_gla, page_table, c_prompt, c_sample, w_ada, b_ada, g_norm_mix, w_in, g_q_a, w_q_b, g_kv_a, w_kv_b, w_gk_b, b_gk, g_mla_out, g_gla_out, w_o, g_norm_ffn, w_router, b_router, w_up, b_up, w_down, b_down, g_norm_final, w_ada_final, b_ada_final):
    B, S, D = x_prompt.shape
    DB, T, _ = x_sample.shape
    depth = w_ada.shape[0]
    assert depth == 1
    past_len = page_table.shape[1] * cache_ckv.shape[2]
    n_p, n_s = B * S, DB * T
    l = 0

    w = _prep_weights(w_in[l], g_q_a[l], w_q_b[l], g_kv_a[l], w_kv_b[l], w_gk_b[l], b_gk[l],
                      g_mla_out[l], g_gla_out[l], w_o[l], g_norm_mix[l], g_norm_ffn[l],
                      w_router[l], b_router[l])

    n_c = B + DB
    n_c_pad = (n_c + 7) // 8 * 8
    c_all = jnp.concatenate([c_sample, c_prompt, jnp.zeros((n_c_pad - n_c, D), F32)], axis=0)
    mod = ada_terms(c_all, w_ada[l], b_ada[l]).reshape(n_c_pad, 1, N_MOD * D)
    mod_f = ada_terms(c_all, w_ada_final, b_ada_final).reshape(n_c_pad, 1, 2 * D)
    off_s, off_p = 0, DB

    xp = x_prompt.reshape(n_p, D)
    xs = x_sample.reshape(n_s, D)
    tabs_p = _rope_tables(jnp.arange(S), 1)
    tabs_s = _rope_tables(past_len + jnp.arange(T), ROW_TILE // T)

    (q_p, k_p, v_p, ckv_p, kpe_p, gq_p, gk_p, gv_p, la_p, gg_p) = in_proj(
        xp, mod, S, off_p, tabs_p, w, True)
    o_mla_p = mla_prefill(q_p, k_p, v_p, B, S)
    s0 = jnp.zeros((B, GLA_HEADS, GLA_DK, GLA_DV), F32)
    o_gla_p, gla_p = gla(gq_p, gk_p, la_p, gv_p, s0, B, S)
    n_all = n_p + n_s
    h2_init = (jnp.zeros((n_all, PERM_W), jnp.uint32), jnp.zeros((n_all, PERM_W), jnp.uint32))
    x1_p, h2a, h2b, ti_p, tg_p, hist_p = mixer_out(xp, o_mla_p, o_gla_p.reshape(n_p, GLA_WIDTH), gg_p,
                                                   mod, S, off_p, w, n_all, 0, h2_init)

    (q_s, ckv_s, kpe_s, gq_s, gk_s, gv_s, la_s, gg_s) = in_proj(
        xs, mod, T, off_s, tabs_s, w, False)
    qlat, qpe = absorb_q(q_s, w["wuk_t"])
    o_lat = mla_decode(qlat, qpe, ckv_s, kpe_s, cache_ckv, cache_kpe, page_table, DB, T)
    o_mla_s = latent_to_values(o_lat, w["wuv_h"], DB, T)
    tpad = GLA_CHUNK
    padt = lambda a: jnp.pad(a.reshape(DB, T, a.shape[-1]), ((0, 0), (0, tpad - T), (0, 0))).reshape(
        DB * tpad, a.shape[-1])
    o_gla_s, gla_s = gla(padt(gq_s), padt(gk_s), padt(la_s), padt(gv_s), state_gla[l], DB, tpad)
    o_gla_s = o_gla_s.reshape(DB, tpad, GLA_WIDTH)[:, :T].reshape(n_s, GLA_WIDTH)
    x1_s, h2a, h2b, ti_s, tg_s, hist_s = mixer_out(xs, o_mla_s, o_gla_s, gg_s,
                                                   mod, T, off_s, w, n_all, n_p, (h2a, h2b))

    ti = jnp.concatenate([ti_p, ti_s], axis=0)
    base, block_e, block_rows, n_blocks = _route_tables(jnp.concatenate([hist_p, hist_s], axis=0), n_all)
    dest = route_rank(ti, base)
    idx = dest[:, :TOP_K].T.reshape(1, TOP_K * n_all)
    n_rows = n_blocks * MOE_TILE
    xs_a = sc_dispatch(h2a, idx, n_rows)
    xs_b = sc_dispatch(h2b, idx, n_rows)
    ys_a, ys_b = moe_experts(xs_a, xs_b, block_e, block_rows, w_up[l], b_up[l], w_down[l], b_down[l])
    yg_a = sc_combine(ys_a, idx).reshape(TOP_K, n_all, PERM_W)
    yg_b = sc_combine(ys_b, idx).reshape(TOP_K, n_all, PERM_W)

    g_fin = g_norm_final.reshape(1, D)
    y_p = final_out(x1_p, yg_a, yg_b, tg_p, 0, mod, mod_f, S, off_p, g_fin)
    y_s = final_out(x1_s, yg_a, yg_b, tg_s, n_p, mod, mod_f, T, off_s, g_fin)

    return (y_p.reshape(B, S, D), y_s.reshape(DB, T, D),
            ckv_p.reshape(1, B, S, MLA_KV_LORA), jnp.swapaxes(kpe_p, 1, 2)[None], gla_p[None],
            ckv_s.reshape(1, DB, T, MLA_KV_LORA), kpe_s.reshape(1, DB, T, MLA_ROPE), gla_s[None])
```

```python
import functools
import math

import jax
import jax.numpy as jnp
import numpy as np
from jax import lax
from jax.experimental import pallas as pl
from jax.experimental.pallas import tpu as pltpu
from jax.experimental.pallas import tpu_sc as plsc

F32 = jnp.float32
BF16 = jnp.bfloat16

D_MODEL = 1024
MLA_HEADS = 8
MLA_NOPE = 64
MLA_ROPE = 32
MLA_V = 64
MLA_Q_LORA = 384
MLA_KV_LORA = 256
MLA_SCALE = (MLA_NOPE + MLA_ROPE) ** -0.5
ROPE_THETA = 10000.0
GLA_HEADS = 4
GLA_DK = 64
GLA_DV = 128
GLA_GATE_RANK = 16
GLA_GATE_NORM = 16.0
GLA_CHUNK = 16
GLA_QK = GLA_HEADS * GLA_DK
GLA_WIDTH = GLA_HEADS * GLA_DV
N_EXPERTS = 32
TOP_K = 4
D_FF = D_MODEL
SWIGLU_LIMIT = 7.0
SWIGLU_ALPHA = 1.702
N_MOD = 6
EPS = 1e-6
PAGE_SIZE = 128

LANES = 128
HEAD_PAD = LANES
MLA_PAD = MLA_HEADS * HEAD_PAD
MLA_WIDTH = MLA_HEADS * MLA_V
VMEM_LIMIT = 56 * 1024 * 1024

Q_SCALE = MLA_SCALE * math.log2(math.e)

C_QA = 0
C_KV = C_QA + MLA_Q_LORA
C_GQ = C_KV + MLA_KV_LORA
C_GK = C_GQ + GLA_QK
C_GV = C_GK + GLA_QK
C_GG = C_GV + GLA_WIDTH
C_MISC = C_GG + GLA_WIDTH
C_SWAP = C_MISC + LANES
D_IN_PAD = C_SWAP + LANES

ROW_TILE = 512
RANK_TILE = 1024
ATT_TILE = 512
ATT_HEADS = 4
GLA_BLOCK = 256
DEC_SPANS = 4
SC_WIN = 128
PERM_W = D_MODEL // 4
MOE_TILE = 768
MOE_SUB = 256
NEW_PAD = 16


def _cparams(sem):
    return pltpu.CompilerParams(dimension_semantics=sem, vmem_limit_bytes=VMEM_LIMIT)


def _rms(x, g):
    return x * lax.rsqrt(jnp.mean(x * x, axis=-1, keepdims=True) + EPS) * g


def _mod_spec(term, seq_rows, row_off, tm):
    if seq_rows >= tm:
        tiles_per_seq = seq_rows // tm
        return pl.BlockSpec((1, 1, D_MODEL), lambda i: (row_off + i // tiles_per_seq, 0, term))
    g = tm // seq_rows
    assert row_off % g == 0
    return pl.BlockSpec((g, 1, D_MODEL), lambda i: (row_off // g + i, 0, term))


def _mod_rows(ref, rows):
    m = ref[...]
    g = m.shape[0]
    if g == 1:
        return m[0]
    return jnp.broadcast_to(m, (g, rows // g, m.shape[2])).reshape(rows, m.shape[2])


def _pack_rows(x):
    bits = lax.bitcast_convert_type(x.astype(BF16).astype(F32), jnp.uint32)
    w = x.shape[1] // 2
    return (bits[:, :w] >> 16) | bits[:, w:]


def _unpack_rows(words):
    lo = lax.bitcast_convert_type(words << 16, F32)
    hi = lax.bitcast_convert_type(words & jnp.uint32(0xFFFF0000), F32)
    return jnp.concatenate([lo, hi], axis=1)


def _ada_kernel(c_ref, w_ref, b_ref, o_ref):
    c = c_ref[...]
    a = (c * jax.nn.sigmoid(c)).astype(BF16)
    o_ref[...] = jnp.dot(a, w_ref[...].astype(BF16), preferred_element_type=F32) + b_ref[...]


def ada_terms(c, w, b):
    rows, d = c.shape
    n = w.shape[1]
    tn = 512
    return pl.pallas_call(
        _ada_kernel,
        grid=(n // tn,),
        in_specs=[pl.BlockSpec((rows, d), lambda j: (0, 0)),
                  pl.BlockSpec((d, tn), lambda j: (0, j)),
                  pl.BlockSpec((1, tn), lambda j: (0, j))],
        out_specs=pl.BlockSpec((rows, tn), lambda j: (0, j)),
        out_shape=jax.ShapeDtypeStruct((rows, n), F32),
        compiler_params=_cparams(("arbitrary",)),
        name="ada_terms",
    )(c, w, b.reshape(1, n))


def _in_kernel(with_kv, x_ref, sh_ref, sc_ref, g_ref, win_ref, gqa_ref, wq1_ref, wq2_ref,
               gkv_ref, wk_ref, wv_ref, wgk_ref, bgk_ref, qc_ref, qs_ref, kc_ref, ks_ref, *outs):
    if with_kv:
        q_out, k_out, v_out, ckv_out, kpe_out, gq_out, gk_out, gv_out, la_out, gg_out = outs
    else:
        q_out, ckv_out, kpe_out, gq_out, gk_out, gv_out, la_out, gg_out = outs
    x = x_ref[...]
    h = _rms(x, g_ref[...])
    rows = x.shape[0]
    h = h * (1.0 + _mod_rows(sc_ref, rows)) + _mod_rows(sh_ref, rows)
    proj = jnp.dot(h.astype(BF16), win_ref[...], preferred_element_type=F32)

    qn = _rms(proj[:, C_QA:C_KV], gqa_ref[...]).astype(BF16)
    qa = jnp.dot(qn, wq1_ref[...], preferred_element_type=F32)
    qb = jnp.dot(qn, wq2_ref[...], preferred_element_type=F32)
    qc = qc_ref[...]
    qs = qs_ref[...]
    for hh in range(MLA_HEADS):
        sl = slice(hh * HEAD_PAD, (hh + 1) * HEAD_PAD)
        q_out[:, sl] = (qa[:, sl] * qc + qb[:, sl] * qs).astype(BF16)

    ckv = _rms(proj[:, C_KV:C_GQ], gkv_ref[...])
    ckv_out[...] = ckv
    kpe = proj[:, C_MISC:C_SWAP] * kc_ref[...] + proj[:, C_SWAP:D_IN_PAD] * ks_ref[...]
    kpe_out[...] = kpe[:, :MLA_ROPE]
    if with_kv:
        ckv_b = ckv.astype(BF16)
        kn = jnp.dot(ckv_b, wk_ref[...], preferred_element_type=F32)
        kpe_sh = pltpu.roll(kpe, MLA_NOPE, axis=1)
        for hh in range(MLA_HEADS):
            sl = slice(hh * HEAD_PAD, (hh + 1) * HEAD_PAD)
            k_out[:, sl] = (kn[:, sl] + kpe_sh).astype(BF16)
        v_out[...] = jnp.dot(ckv_b, wv_ref[...], preferred_element_type=F32).astype(BF16)

    gq_out[...] = proj[:, C_GQ:C_GK] * (GLA_DK ** -0.5)
    gk_out[...] = proj[:, C_GK:C_GV]
    gv_out[...] = proj[:, C_GV:C_GG].astype(BF16)
    gg_out[...] = proj[:, C_GG:C_MISC].astype(BF16)
    xg = jnp.dot(proj[:, C_MISC:C_SWAP].astype(BF16), wgk_ref[...],
                 preferred_element_type=F32) + bgk_ref[...]
    la_out[...] = (jnp.minimum(xg, 0.0) - jnp.log(1.0 + jnp.exp(-jnp.abs(xg)))) * (1.0 / GLA_GATE_NORM)


def in_proj(x, mod, seq_rows, row_off, tabs, wts, with_kv):
    n = x.shape[0]
    tm = ROW_TILE
    nt = n // tm
    qc, qs, kc, ks = tabs
    n_tab = qc.shape[0] // tm
    row = lambda i: (i, 0)
    const = lambda i: (0, 0)
    tab_spec = pl.BlockSpec((tm, LANES), lambda i: (i % n_tab, 0))
    w = wts
    in_specs = [
        pl.BlockSpec((tm, D_MODEL), row),
        _mod_spec(0, seq_rows, row_off, tm), _mod_spec(1, seq_rows, row_off, tm),
        pl.BlockSpec((1, D_MODEL), const),
        pl.BlockSpec((D_MODEL, D_IN_PAD), const),
        pl.BlockSpec((1, MLA_Q_LORA), const),
        pl.BlockSpec((MLA_Q_LORA, MLA_PAD), const),
        pl.BlockSpec((MLA_Q_LORA, MLA_PAD), const),
        pl.BlockSpec((1, MLA_KV_LORA), const),
        pl.BlockSpec((MLA_KV_LORA, MLA_PAD), const),
        pl.BlockSpec((MLA_KV_LORA, MLA_WIDTH), const),
        pl.BlockSpec((LANES, GLA_QK), const),
        pl.BlockSpec((1, GLA_QK), const),
        tab_spec, tab_spec, tab_spec, tab_spec,
    ]
    wide = lambda dt: (jax.ShapeDtypeStruct((n, MLA_PAD), dt), pl.BlockSpec((tm, MLA_PAD), row))
    outs = [wide(BF16)]
    if with_kv:
        outs += [wide(BF16),
                 (jax.ShapeDtypeStruct((n, MLA_WIDTH), BF16), pl.BlockSpec((tm, MLA_WIDTH), row))]
    outs += [
        (jax.ShapeDtypeStruct((n, MLA_KV_LORA), F32), pl.BlockSpec((tm, MLA_KV_LORA), row)),
        (jax.ShapeDtypeStruct((n, MLA_ROPE), F32), pl.BlockSpec((tm, MLA_ROPE), row)),
        (jax.ShapeDtypeStruct((n, GLA_QK), F32), pl.BlockSpec((tm, GLA_QK), row)),
        (jax.ShapeDtypeStruct((n, GLA_QK), F32), pl.BlockSpec((tm, GLA_QK), row)),
        (jax.ShapeDtypeStruct((n, GLA_WIDTH), BF16), pl.BlockSpec((tm, GLA_WIDTH), row)),
        (jax.ShapeDtypeStruct((n, GLA_QK), F32), pl.BlockSpec((tm, GLA_QK), row)),
        (jax.ShapeDtypeStruct((n, GLA_WIDTH), BF16), pl.BlockSpec((tm, GLA_WIDTH), row)),
    ]
    return pl.pallas_call(
        functools.partial(_in_kernel, with_kv),
        grid=(nt,),
        in_specs=in_specs,
        out_specs=[o[1] for o in outs],
        out_shape=[o[0] for o in outs],
        compiler_params=_cparams(("parallel",)),
        name="inproj_prompt" if with_kv else "inproj_sample",
    )(x, mod, mod, w["g_norm_mix"], w["w_in"], w["g_q_a"], w["wq1"], w["wq2"], w["g_kv_a"],
      w["wk"], w["wv"], w["wgk"], w["b_gk"], qc, qs, kc, ks)


def _prefill_kernel(q_ref, k_ref, v_ref, o_ref):
    qi = pl.program_id(2)
    t = ATT_TILE
    lanes = [slice(g * HEAD_PAD, (g + 1) * HEAD_PAD) for g in range(ATT_HEADS)]
    v_lanes = [slice(g * MLA_V, (g + 1) * MLA_V) for g in range(ATT_HEADS)]
    qs = [q_ref[0, :, sl] for sl in lanes]

    def update_all(carries, r0, n_keys, mask):
        r0 = pl.multiple_of(r0, t)
        scores = [lax.dot_general(k_ref[0, pl.ds(r0, n_keys), sl], q, (((1,), (1,)), ((), ())),
                                  preferred_element_type=F32) for q, sl in zip(qs, lanes)]
        stats = []
        for (m, l, acc), s in zip(carries, scores):
            if mask is not None:
                s = jnp.where(mask, s, -jnp.inf)
            m_new = jnp.maximum(m, jnp.max(s, axis=0, keepdims=True))
            p = jnp.exp2(s - m_new)
            alpha = jnp.exp2(m - m_new)
            stats.append((m_new, alpha * l + jnp.sum(p, axis=0, keepdims=True), alpha * acc, p.astype(BF16)))
        out = []
        for (m_new, l_new, acc_scaled, p), sl in zip(stats, v_lanes):
            pv = lax.dot_general(v_ref[0, pl.ds(r0, n_keys), sl], p, (((0,), (0,)), ((), ())),
                                 preferred_element_type=F32)
            out.append((m_new, l_new, acc_scaled + pv))
        return tuple(out)

    def body(jj, carries):
        return update_all(carries, jj * (2 * t), 2 * t, None)

    init = (jnp.full((1, t), -jnp.inf, F32), jnp.zeros((1, t), F32), jnp.zeros((MLA_V, t), F32))
    carries = lax.fori_loop(0, qi // 2, body, (init,) * ATT_HEADS)

    def causal(n_keys):
        key_pos = lax.broadcasted_iota(jnp.int32, (n_keys, t), 0)
        qry_pos = lax.broadcasted_iota(jnp.int32, (n_keys, t), 1) + (n_keys - t)
        return key_pos <= qry_pos

    carries = lax.cond(qi % 2 == 1,
                       lambda c: update_all(c, (qi - 1) * t, 2 * t, causal(2 * t)),
                       lambda c: update_all(c, qi * t, t, causal(t)),
                       carries)
    o_ref[0] = jnp.concatenate([(acc / l).T for _, l, acc in carries], axis=1).astype(BF16)


def mla_prefill(q, k, v, batch, seq):
    t = ATT_TILE
    q3 = q.reshape(batch, seq, MLA_PAD)
    k3 = k.reshape(batch, seq, MLA_PAD)
    v3 = v.reshape(batch, seq, MLA_WIDTH)
    o = pl.pallas_call(
        _prefill_kernel,
        grid=(batch, MLA_HEADS // ATT_HEADS, seq // t),
        in_specs=[pl.BlockSpec((1, t, ATT_HEADS * HEAD_PAD), lambda b, h, i: (b, i, h)),
                  pl.BlockSpec((1, seq, ATT_HEADS * HEAD_PAD), lambda b, h, i: (b, 0, h)),
                  pl.BlockSpec((1, seq, ATT_HEADS * MLA_V), lambda b, h, i: (b, 0, h))],
        out_specs=pl.BlockSpec((1, t, ATT_HEADS * MLA_V), lambda b, h, i: (b, i, h)),
        out_shape=jax.ShapeDtypeStruct((batch, seq, MLA_WIDTH), BF16),
        compiler_params=_cparams(("parallel", "parallel", "arbitrary")),
        name="mla_prefill",
    )(q3, k3, v3)
    return o.reshape(batch * seq, MLA_WIDTH)


def _absorb_kernel(q_ref, wuk_ref, qlat_ref, qpe_ref):
    q = q_ref[...]
    qlat_ref[0] = jnp.dot(q, wuk_ref[0], preferred_element_type=F32)
    qf = pltpu.roll(q.astype(F32), HEAD_PAD - MLA_NOPE, axis=1)
    lane = lax.broadcasted_iota(jnp.int32, qf.shape, 1)
    qpe_ref[0] = jnp.where(lane < MLA_ROPE, qf, 0.0)


def absorb_q(q, wuk_t):
    r = q.shape[0]
    return pl.pallas_call(
        _absorb_kernel,
        grid=(MLA_HEADS,),
        in_specs=[pl.BlockSpec((r, HEAD_PAD), lambda h: (0, h)),
                  pl.BlockSpec((1, HEAD_PAD, MLA_KV_LORA), lambda h: (h, 0, 0))],
        out_specs=[pl.BlockSpec((1, r, MLA_KV_LORA), lambda h: (h, 0, 0)),
                   pl.BlockSpec((1, r, HEAD_PAD), lambda h: (h, 0, 0))],
        out_shape=[jax.ShapeDtypeStruct((MLA_HEADS, r, MLA_KV_LORA), F32),
                   jax.ShapeDtypeStruct((MLA_HEADS, r, HEAD_PAD), F32)],
        compiler_params=_cparams(("parallel",)),
        name="absorb_q",
    )(q, wuk_t)


def _decode_kernel(t_new, pt_ref, qlat_ref, qpe_ref, cnew_ref, knew_ref, ckv_hbm, kpe_hbm, o_ref,
                   cbuf, pbuf, sem):
    b = pl.program_id(0)
    nb = pl.num_programs(0)
    n_pages = cbuf.shape[1]
    slot = lax.rem(b, 2)
    rows = MLA_HEADS * t_new

    def page_copies(bb, sl, p):
        pg = pt_ref[bb, p]
        return (pltpu.make_async_copy(ckv_hbm.at[0, pg], cbuf.at[sl, p], sem.at[0, sl]),
                pltpu.make_async_copy(kpe_hbm.at[0, pg], pbuf.at[sl, p], sem.at[1, sl]))

    def fetch(bb, sl):
        def body(p, c):
            latent_cp, rope_cp = page_copies(bb, sl, p)
            latent_cp.start()
            rope_cp.start(priority=1)
            return c
        lax.fori_loop(0, n_pages, body, 0)

    @pl.when(b == 0)
    def _():
        fetch(0, 0)

    @pl.when(b + 1 < nb)
    def _():
        fetch(b + 1, 1 - slot)

    pltpu.make_async_copy(ckv_hbm.at[0, pl.ds(0, n_pages)], cbuf.at[slot], sem.at[0, slot]).wait()
    pltpu.make_async_copy(kpe_hbm.at[0, pl.ds(0, n_pages)], pbuf.at[slot], sem.at[1, slot]).wait()

    qlat = qlat_ref[...].reshape(rows, MLA_KV_LORA).astype(BF16)
    qpe = qpe_ref[...].reshape(rows, HEAD_PAD)[:, :MLA_ROPE].astype(BF16)
    dn = (((1,), (1,)), ((), ()))

    span_pages = n_pages // DEC_SPANS
    values, scores = [], []
    for c in range(DEC_SPANS):
        pages = range(c * span_pages, (c + 1) * span_pages)
        kb = jnp.concatenate([cbuf[slot, p].astype(BF16) for p in pages], axis=0)
        pb = jnp.concatenate([pbuf[slot, p].astype(BF16) for p in pages], axis=1)
        values.append(kb)
        scores.append(lax.dot_general(qlat, kb, dn, preferred_element_type=F32)
                      + jnp.dot(qpe, pb, preferred_element_type=F32))
    cn = cnew_ref[0].astype(BF16)
    kn = knew_ref[0].astype(BF16)
    sn = (lax.dot_general(qlat, cn, dn, preferred_element_type=F32)
          + lax.dot_general(qpe, kn, dn, preferred_element_type=F32))
    tq = lax.broadcasted_iota(jnp.int32, sn.shape, 0) % t_new
    tk = lax.broadcasted_iota(jnp.int32, sn.shape, 1)
    values.append(cn)
    scores.append(jnp.where(tk <= tq, sn, -jnp.inf))

    maxes = [jnp.max(s, axis=-1, keepdims=True) for s in scores]
    m = maxes[0]
    for pm in maxes[1:]:
        m = jnp.maximum(m, pm)
    probs = [jnp.exp2(s - pm) for s, pm in zip(scores, maxes)]
    l = jnp.zeros_like(m)
    acc = jnp.zeros((rows, MLA_KV_LORA), F32)
    for p, pm, vals in zip(probs, maxes, values):
        scale = jnp.exp2(pm - m)
        l = l + scale * jnp.sum(p, axis=-1, keepdims=True)
        acc = acc + scale * jnp.dot(p.astype(BF16), vals, preferred_element_type=F32)
    o_ref[0] = acc / l


def mla_decode(qlat, qpe, ckv_new, kpe_new, cache_ckv, cache_kpe, page_table, dec_batch, t_new):
    n_pages = page_table.shape[1]
    past_len = n_pages * PAGE_SIZE
    rows = MLA_HEADS * t_new
    kpe_t = jnp.swapaxes(cache_kpe, 2, 3)
    qlat4 = qlat.reshape(MLA_HEADS, dec_batch, t_new, MLA_KV_LORA)
    qpe4 = qpe.reshape(MLA_HEADS, dec_batch, t_new, HEAD_PAD)
    t_pad = NEW_PAD
    pad_new = lambda a: jnp.pad(a.reshape(dec_batch, t_new, a.shape[-1]), ((0, 0), (0, t_pad - t_new), (0, 0)))
    cnew = pad_new(ckv_new)
    knew = pad_new(kpe_new)

    in_specs = [
        pl.BlockSpec((MLA_HEADS, 1, t_new, MLA_KV_LORA), lambda b, pt: (0, b, 0, 0)),
        pl.BlockSpec((MLA_HEADS, 1, t_new, HEAD_PAD), lambda b, pt: (0, b, 0, 0)),
        pl.BlockSpec((1, t_pad, MLA_KV_LORA), lambda b, pt: (b, 0, 0)),
        pl.BlockSpec((1, t_pad, MLA_ROPE), lambda b, pt: (b, 0, 0)),
        pl.BlockSpec(memory_space=pl.ANY),
        pl.BlockSpec(memory_space=pl.ANY),
    ]
    grid_spec = pltpu.PrefetchScalarGridSpec(
        num_scalar_prefetch=1,
        grid=(dec_batch,),
        in_specs=in_specs,
        out_specs=pl.BlockSpec((1, rows, MLA_KV_LORA), lambda b, pt: (b, 0, 0)),
        scratch_shapes=[pltpu.VMEM((2, n_pages, PAGE_SIZE, MLA_KV_LORA), F32),
                        pltpu.VMEM((2, n_pages, MLA_ROPE, PAGE_SIZE), F32),
                        pltpu.SemaphoreType.DMA((2, 2))],
    )
    return pl.pallas_call(
        functools.partial(_decode_kernel, t_new),
        grid_spec=grid_spec,
        out_shape=jax.ShapeDtypeStruct((dec_batch, rows, MLA_KV_LORA), F32),
        compiler_params=_cparams(("arbitrary",)),
        name="mla_decode",
    )(page_table, qlat4, qpe4, cnew, knew, cache_ckv, kpe_t)


def _uv_kernel(o_ref, wuv_ref, out_ref):
    outs = []
    for hh in range(2):
        o = o_ref[:, hh]
        o = o.reshape(o.shape[0] * o.shape[1], MLA_KV_LORA).astype(BF16)
        outs.append(jnp.dot(o, wuv_ref[hh], preferred_element_type=F32))
    out_ref[...] = jnp.concatenate(outs, axis=1).astype(BF16)


def latent_to_values(o_lat, wuv, dec_batch, t_new):
    o4 = o_lat.reshape(dec_batch, MLA_HEADS, t_new, MLA_KV_LORA)
    return pl.pallas_call(
        _uv_kernel,
        grid=(MLA_HEADS // 2,),
        in_specs=[pl.BlockSpec((dec_batch, 2, t_new, MLA_KV_LORA), lambda h: (0, h, 0, 0)),
                  pl.BlockSpec((2, MLA_KV_LORA, MLA_V), lambda h: (h, 0, 0))],
        out_specs=pl.BlockSpec((dec_batch * t_new, 2 * MLA_V), lambda h: (0, h)),
        out_shape=jax.ShapeDtypeStruct((dec_batch * t_new, MLA_WIDTH), BF16),
        compiler_params=_cparams(("parallel",)),
        name="latent_to_values",
    )(o4, wuv)


def _gla_kernel(nsb, q_ref, k_ref, la_ref, v_ref, s0_ref, o_ref, sfin_ref, st_sc, kv_sc, sall_sc):
    blk = pl.program_id(1)
    nblk = pl.num_programs(1)
    c = GLA_CHUNK
    tb = q_ref.shape[1]
    nc = tb // c
    cps = nc // nsb
    head_k = [slice(hh * GLA_DK, (hh + 1) * GLA_DK) for hh in range(GLA_HEADS)]
    head_v = [slice(hh * GLA_DV, (hh + 1) * GLA_DV) for hh in range(GLA_HEADS)]

    @pl.when(blk == 0)
    def _():
        for sq in range(nsb):
            for hh in range(GLA_HEADS):
                st_sc[sq, hh] = s0_ref[sq, hh].T

    q = q_ref[0]
    k = k_ref[0]
    la = la_ref[0]
    v = v_ref[0]

    r = lax.broadcasted_iota(jnp.int32, (tb, tb), 0)
    cc = lax.broadcasted_iota(jnp.int32, (tb, tb), 1)
    same = (r >> 4) == (cc >> 4)
    tri = jnp.where(same & (cc <= r), 1.0, 0.0)
    ones = jnp.where(same, 1.0, 0.0)
    sel = jnp.concatenate([tri, ones], axis=0).astype(BF16)
    la_hi = la.astype(BF16)
    rest = la - la_hi.astype(F32)
    la_mid = rest.astype(BF16)
    la_lo = (rest - la_mid.astype(F32)).astype(BF16)
    sums = (jnp.dot(sel, la_hi, preferred_element_type=F32)
            + (jnp.dot(sel, la_mid, preferred_element_type=F32)
               + jnp.dot(sel, la_lo, preferred_element_type=F32)))
    b = sums[:tb]
    b_last = sums[tb:]
    qd = (q * jnp.exp(b)).astype(BF16)
    kd = (k * jnp.exp(b_last - b)).astype(BF16)
    dec = jnp.exp(b_last)

    key_head = lax.broadcasted_iota(jnp.int32, (GLA_QK, GLA_HEADS * c), 0) >> 6
    col = lax.broadcasted_iota(jnp.int32, (GLA_QK, GLA_HEADS * c), 1)
    b3 = b.reshape(nc, c, GLA_QK)
    k3 = k.reshape(nc, c, GLA_QK)
    q3 = q.reshape(nc, c, GLA_QK)
    pos = lax.broadcasted_iota(jnp.int32, (nc, c, GLA_QK), 1)
    scores = jnp.zeros((tb, GLA_HEADS * c), F32)
    for j in range(c):
        e = jnp.exp(b3 - b3[:, j:j + 1, :])
        w = jnp.where(pos >= j, q3 * k3[:, j:j + 1, :] * e, 0.0)
        place = jnp.where(col == key_head * c + j, 1.0, 0.0).astype(BF16)
        scores = scores + jnp.dot(w.reshape(tb, GLA_QK).astype(BF16), place, preferred_element_type=F32)
    scores = scores.astype(BF16)

    for ci in range(nc):
        rs = slice(ci * c, (ci + 1) * c)
        for hh in range(GLA_HEADS):
            o_ref[0, rs, head_v[hh]] = jnp.dot(scores[rs, hh * c:(hh + 1) * c], v[rs, head_v[hh]],
                                               preferred_element_type=F32)
            kv_sc[ci, hh] = lax.dot_general(v[rs, head_v[hh]], kd[rs, head_k[hh]], (((0,), (0,)), ((), ())),
                                            preferred_element_type=F32)

    for sq in range(nsb):
        states = [st_sc[sq, hh] for hh in range(GLA_HEADS)]
        for cj in range(cps):
            ci = sq * cps + cj
            dec_row = dec[ci * c:ci * c + 1, :]
            for hh in range(GLA_HEADS):
                sall_sc[ci, hh] = states[hh].astype(BF16)
                states[hh] = states[hh] * dec_row[:, head_k[hh]] + kv_sc[ci, hh]
        for hh in range(GLA_HEADS):
            st_sc[sq, hh] = states[hh]

    for ci in range(nc):
        rs = slice(ci * c, (ci + 1) * c)
        outs = [lax.dot_general(qd[rs, head_k[hh]], sall_sc[ci, hh], (((1,), (1,)), ((), ())),
                                preferred_element_type=F32) for hh in range(GLA_HEADS)]
        o_ref[0, rs, :] += jnp.concatenate(outs, axis=1)

    @pl.when(blk == nblk - 1)
    def _():
        for sq in range(nsb):
            for hh in range(GLA_HEADS):
                sfin_ref[sq, hh] = st_sc[sq, hh].T


def gla(gq, gk, la, gv, s0, n_seq, seq_len):
    tb = GLA_BLOCK
    nsb = max(1, tb // seq_len)
    nblk = max(1, seq_len // tb)
    n_outer = n_seq // nsb
    nc = tb // GLA_CHUNK
    sh3 = lambda a: a.reshape(n_outer, nblk * tb, a.shape[-1])
    row = lambda b, i: (b, i, 0)
    st_spec = pl.BlockSpec((nsb, GLA_HEADS, GLA_DK, GLA_DV), lambda b, i: (b, 0, 0, 0))
    o, s_fin = pl.pallas_call(
        functools.partial(_gla_kernel, nsb),
        grid=(n_outer, nblk),
        in_specs=[pl.BlockSpec((1, tb, GLA_QK), row), pl.BlockSpec((1, tb, GLA_QK), row),
                  pl.BlockSpec((1, tb, GLA_QK), row), pl.BlockSpec((1, tb, GLA_WIDTH), row), st_spec],
        out_specs=[pl.BlockSpec((1, tb, GLA_WIDTH), row), st_spec],
        out_shape=[jax.ShapeDtypeStruct((n_outer, nblk * tb, GLA_WIDTH), F32),
                   jax.ShapeDtypeStruct((n_seq, GLA_HEADS, GLA_DK, GLA_DV), F32)],
        scratch_shapes=[pltpu.VMEM((nsb, GLA_HEADS, GLA_DV, GLA_DK), F32),
                        pltpu.VMEM((nc, GLA_HEADS, GLA_DV, GLA_DK), F32),
                        pltpu.VMEM((nc, GLA_HEADS, GLA_DV, GLA_DK), BF16)],
        compiler_params=_cparams(("parallel", "arbitrary")),
        name="gla_prompt" if nblk > 1 else "gla_sample",
    )(sh3(gq), sh3(gk), sh3(la), sh3(gv), s0)
    return o.reshape(n_seq * seq_len, GLA_WIDTH), s_fin


def _mix_kernel(x_ref, om_ref, og_ref, gg_ref, gta_ref, shf_ref, scf_ref, gm_ref, ggl_ref, wo_ref,
                gn_ref, wr_ref, br_ref, h2a_in, h2b_in, x1_ref, h2a_ref, h2b_ref, ti_ref, tg_ref, hist_ref):
    del h2a_in, h2b_in
    om = _rms(om_ref[...].astype(F32), gm_ref[...])
    og = og_ref[...]
    gg = gg_ref[...].astype(F32)
    gate = gg * jax.nn.sigmoid(gg)
    parts = []
    for hh in range(GLA_HEADS):
        sl = slice(hh * GLA_DV, (hh + 1) * GLA_DV)
        parts.append(_rms(og[:, sl], ggl_ref[...]) * gate[:, sl])
    mix = jnp.concatenate([om] + parts, axis=1).astype(BF16)
    rows = x_ref.shape[0]
    x1 = x_ref[...] + _mod_rows(gta_ref, rows) * jnp.dot(mix, wo_ref[...], preferred_element_type=F32)
    x1_ref[...] = x1
    h2 = _rms(x1, gn_ref[...]) * (1.0 + _mod_rows(scf_ref, rows)) + _mod_rows(shf_ref, rows)
    words = _pack_rows(h2)
    h2a_ref[...] = words[:, :PERM_W]
    h2b_ref[...] = words[:, PERM_W:]
    h_hi = h2.astype(BF16)
    h_lo = (h2 - h_hi.astype(F32)).astype(BF16)
    logits = (jnp.dot(h_hi, wr_ref[0], preferred_element_type=F32)
              + (jnp.dot(h_hi, wr_ref[1], preferred_element_type=F32)
                 + jnp.dot(h_lo, wr_ref[0], preferred_element_type=F32))) + br_ref[...]
    lane_i = lax.broadcasted_iota(jnp.int32, logits.shape, 1)
    lane = lane_i.astype(F32)
    vals = []
    idxs = []
    for _ in range(TOP_K):
        mx = jnp.max(logits, axis=-1, keepdims=True)
        ix = jnp.min(jnp.where(logits == mx, lane, float(LANES)), axis=-1, keepdims=True)
        vals.append(mx)
        idxs.append(ix)
        logits = jnp.where(lane == ix, -jnp.inf, logits)
    ex = [jnp.exp(vv - vals[0]) for vv in vals]
    den = ex[0] + ex[1] + ex[2] + ex[3]
    ti = jnp.zeros(logits.shape, F32)
    tg = jnp.zeros(logits.shape, F32)
    onehot = jnp.zeros(logits.shape, F32)
    for kk in range(TOP_K):
        ti = jnp.where(lane_i == kk, idxs[kk], ti)
        tg = jnp.where(lane_i == kk, ex[kk] / den, tg)
        onehot = onehot + jnp.where(lane == idxs[kk], 1.0, 0.0)
    ti_ref[...] = ti.astype(jnp.int32)
    tg_ref[...] = tg
    hist_ref[0] = jnp.sum(onehot, axis=0, keepdims=True)


def mixer_out(x, o_mla, o_gla, gg, mod, seq_rows, row_off, w, n_all, row0, h2_buf):
    n = x.shape[0]
    tm = ROW_TILE
    t0 = row0 // tm
    row = lambda i: (i, 0)
    const = lambda i: (0, 0)
    extra_specs = [pl.BlockSpec(memory_space=pl.ANY)] * 2
    extra_args = list(h2_buf)
    mod_specs = [_mod_spec(term, seq_rows, row_off, tm) for term in (2, 3, 4)]
    d_mix = MLA_WIDTH + GLA_WIDTH
    n_in = 13
    return pl.pallas_call(
        _mix_kernel,
        grid=(n // tm,),
        in_specs=[pl.BlockSpec((tm, D_MODEL), row), pl.BlockSpec((tm, MLA_WIDTH), row),
                  pl.BlockSpec((tm, GLA_WIDTH), row), pl.BlockSpec((tm, GLA_WIDTH), row),
                  *mod_specs,
                  pl.BlockSpec((1, MLA_WIDTH), const), pl.BlockSpec((1, GLA_DV), const),
                  pl.BlockSpec((d_mix, D_MODEL), const), pl.BlockSpec((1, D_MODEL), const),
                  pl.BlockSpec((2, D_MODEL, LANES), lambda i: (0, 0, 0)),
                  pl.BlockSpec((1, LANES), const)] + extra_specs,
        out_specs=[pl.BlockSpec((tm, D_MODEL), row),
                   pl.BlockSpec((tm, PERM_W), lambda i: (i + t0, 0)),
                   pl.BlockSpec((tm, PERM_W), lambda i: (i + t0, 0)),
                   pl.BlockSpec((tm, LANES), row), pl.BlockSpec((tm, LANES), row),
                   pl.BlockSpec((1, 1, LANES), lambda i: (i, 0, 0))],
        out_shape=[jax.ShapeDtypeStruct((n, D_MODEL), F32),
                   jax.ShapeDtypeStruct((n_all, PERM_W), jnp.uint32),
                   jax.ShapeDtypeStruct((n_all, PERM_W), jnp.uint32),
                   jax.ShapeDtypeStruct((n, LANES), jnp.int32), jax.ShapeDtypeStruct((n, LANES), F32),
                   jax.ShapeDtypeStruct((n // tm, 1, LANES), F32)],
        input_output_aliases={n_in: 1, n_in + 1: 2},
        compiler_params=_cparams(("parallel",)),
        name="mixer_sample" if seq_rows < tm else "mixer_prompt",
    )(x, o_mla, o_gla, gg, mod, mod, mod, w["g_mla_out"], w["g_gla_out"], w["w_o"],
      w["g_norm_ffn"], w["w_router_pad"], w["b_router_pad"], *extra_args)


def _rank_kernel(ti_ref, base_ref, dest_ref):
    ti = ti_ref[...]
    tm = ti.shape[0]
    lane = lax.broadcasted_iota(jnp.int32, ti.shape, 1)
    cols = [ti[:, kk:kk + 1] for kk in range(TOP_K)]
    onehot = jnp.zeros(ti.shape, F32)
    for kk in range(TOP_K):
        onehot = onehot + jnp.where(lane == cols[kk], 1.0, 0.0)
    r = lax.broadcasted_iota(jnp.int32, (tm, tm), 0)
    c = lax.broadcasted_iota(jnp.int32, (tm, tm), 1)
    earlier = jnp.where(c < r, 1.0, 0.0).astype(BF16)
    pos = jnp.dot(earlier, onehot.astype(BF16), preferred_element_type=F32) + base_ref[0]
    out = jnp.zeros(ti.shape, F32)
    for kk in range(TOP_K):
        dk = jnp.sum(jnp.where(lane == cols[kk], pos, 0.0), axis=-1, keepdims=True)
        out = jnp.where(lane == kk, dk, out)
    dest_ref[...] = out.astype(jnp.int32)


def route_rank(ti, base):
    n = ti.shape[0]
    tm = RANK_TILE
    return pl.pallas_call(
        _rank_kernel,
        grid=(n // tm,),
        in_specs=[pl.BlockSpec((tm, LANES), lambda i: (i, 0)),
                  pl.BlockSpec((1, 1, LANES), lambda i: (i, 0, 0))],
        out_specs=pl.BlockSpec((tm, LANES), lambda i: (i, 0)),
        out_shape=jax.ShapeDtypeStruct((n, LANES), jnp.int32),
        compiler_params=_cparams(("parallel",)),
        name="route_rank",
    )(ti, base)


def _route_tables(hist, n_tok):
    tm = MOE_TILE
    h = hist[:, 0, :].astype(jnp.int32)
    h = h.reshape(-1, RANK_TILE // ROW_TILE, LANES).sum(axis=1)
    counts = jnp.sum(h, axis=0)
    padded = (counts + tm - 1) // tm * tm
    pad_ends = jnp.cumsum(padded)
    pad_starts = pad_ends - padded
    base = (pad_starts[None, :] + jnp.cumsum(h, axis=0) - h).astype(F32)[:, None, :]
    n_blocks = pl.cdiv(n_tok * TOP_K, tm) + N_EXPERTS
    n_active = (pad_ends[N_EXPERTS - 1] // tm).astype(jnp.int32)
    blk = jnp.arange(n_blocks, dtype=jnp.int32)
    blk_c = jnp.minimum(blk, n_active - 1)
    ends = pad_ends[:N_EXPERTS]
    block_e = jnp.minimum(jnp.sum((ends[None, :] <= (blk_c * tm)[:, None]).astype(jnp.int32), axis=1),
                          N_EXPERTS - 1).astype(jnp.int32)
    used_end = (pad_starts + counts)[:N_EXPERTS][block_e]
    block_rows = jnp.where(blk < n_active, jnp.clip(used_end - blk * tm, 0, tm), 0).astype(jnp.int32)
    experts = jnp.arange(N_EXPERTS, dtype=jnp.int32)
    later = (experts[None, :] > block_e[:, None]) & (counts[None, :N_EXPERTS] > 0)
    next_e = jnp.min(jnp.where(later, experts[None, :], N_EXPERTS), axis=1).astype(jnp.int32)
    return base, block_e, block_rows, next_e, n_blocks


def _sc_mesh():
    return plsc.VectorSubcoreMesh(core_axis_name="c", subcore_axis_name="s")


def sc_dispatch(x_rows, idx, n_out):
    n, wd = x_rows.shape
    win = SC_WIN
    nwin = n // win

    @functools.partial(pl.kernel, out_type=jax.ShapeDtypeStruct((n_out, wd), x_rows.dtype),
                       mesh=_sc_mesh(), scratch_types=[])
    def k(x_hbm, i_hbm, o_hbm):
        def body(x_vmem, i_vmem):
            pltpu.sync_copy(x_vmem, o_hbm.at[i_vmem.at[0]])

        pltpu.emit_pipeline(
            body,
            grid=(idx.shape[1] // win,),
            in_specs=[pl.BlockSpec((win, wd), lambda i: (i % nwin, 0)),
                      pl.BlockSpec((1, win), lambda i: (0, i))],
            out_specs=[],
            core_axis_name=("c", "s"),
            dimension_semantics=(pltpu.PARALLEL,),
        )(x_hbm, i_hbm)

    return k(x_rows, idx)


def sc_combine(y_rows, idx):
    wd = y_rows.shape[1]
    m = idx.shape[1]
    win = SC_WIN

    @functools.partial(pl.kernel, out_type=jax.ShapeDtypeStruct((m, wd), y_rows.dtype),
                       mesh=_sc_mesh(), scratch_types=[])
    def k(y_hbm, i_hbm, o_hbm):
        def body(i_vmem, o_vmem):
            pltpu.sync_copy(y_hbm.at[i_vmem.at[0]], o_vmem)

        pltpu.emit_pipeline(
            body,
            grid=(m // win,),
            in_specs=[pl.BlockSpec((1, win), lambda i: (0, i))],
            out_specs=[pl.BlockSpec((win, wd), lambda i: (i, 0))],
            core_axis_name=("c", "s"),
            dimension_semantics=(pltpu.PARALLEL,),
        )(i_hbm, o_hbm)

    return k(y_rows, idx)


def _moe_kernel(be_ref, nr_ref, ne_ref, xa_ref, xb_ref, wup_hbm, bup_ref, wdn_hbm, bdn_ref, ya_ref, yb_ref,
                wup_f32, wdn_f32, wup_sc, wdn_sc, sem):
    i = pl.program_id(0)
    n_real = nr_ref[i]
    expert = be_ref[i]
    prev = be_ref[jnp.maximum(i - 1, 0)]
    fresh = (i == 0) | (expert != prev)

    def weight_copies(ex):
        return (pltpu.make_async_copy(wup_hbm.at[ex], wup_f32, sem.at[0]),
                pltpu.make_async_copy(wdn_hbm.at[ex], wdn_f32, sem.at[1]))

    @pl.when(i == 0)
    def _():
        for cp in weight_copies(expert):
            cp.start()

    @pl.when((n_real > 0) & fresh)
    def _():
        for cp in weight_copies(expert):
            cp.wait()
        wup_sc[...] = wup_f32[...].astype(BF16)
        wdn_sc[...] = wdn_f32[...].astype(BF16)
        nxt = ne_ref[i]

        @pl.when(nxt < N_EXPERTS)
        def _():
            for cp in weight_copies(nxt):
                cp.start()

    n_sub = MOE_TILE // MOE_SUB
    live_subs = (n_real + (MOE_SUB - 1)) // MOE_SUB
    for live in range(n_sub + 1):
        m = live * MOE_SUB

        @pl.when(live_subs == live)
        def _():
            if m > 0:
                xb = _unpack_rows(jnp.concatenate([xa_ref[:m, :], xb_ref[:m, :]], axis=1)).astype(BF16)
                hu = jnp.dot(xb, wup_sc[...], preferred_element_type=F32) + bup_ref[0]
                gate = jnp.minimum(hu[:, :D_FF], SWIGLU_LIMIT)
                lin = jnp.clip(hu[:, D_FF:], -SWIGLU_LIMIT, SWIGLU_LIMIT)
                act = gate * jax.nn.sigmoid(SWIGLU_ALPHA * gate) * (lin + 1.0)
                y = jnp.dot(act.astype(BF16), wdn_sc[...], preferred_element_type=F32) + bdn_ref[0]
                words = _pack_rows(y)
                ya_ref[:m, :] = words[:, :PERM_W]
                yb_ref[:m, :] = words[:, PERM_W:]
            if m < MOE_TILE:
                ya_ref[m:, :] = jnp.zeros((MOE_TILE - m, PERM_W), jnp.uint32)
                yb_ref[m:, :] = jnp.zeros((MOE_TILE - m, PERM_W), jnp.uint32)


def moe_experts(xs_a, xs_b, block_e, block_rows, next_e, w_up, b_up, w_down, b_down):
    n_rows = xs_a.shape[0]
    tm = MOE_TILE
    n_blocks = n_rows // tm
    emap3 = lambda i, be, nr, ne: (be[i], 0, 0)
    rows = lambda i, be, nr, ne: (i, 0)
    grid_spec = pltpu.PrefetchScalarGridSpec(
        num_scalar_prefetch=3,
        grid=(n_blocks,),
        in_specs=[pl.BlockSpec((tm, PERM_W), rows),
                  pl.BlockSpec((tm, PERM_W), rows),
                  pl.BlockSpec(memory_space=pl.ANY),
                  pl.BlockSpec((1, 1, 2 * D_FF), emap3),
                  pl.BlockSpec(memory_space=pl.ANY),
                  pl.BlockSpec((1, 1, D_MODEL), emap3)],
        out_specs=[pl.BlockSpec((tm, PERM_W), rows),
                   pl.BlockSpec((tm, PERM_W), rows)],
        scratch_shapes=[pltpu.VMEM((D_MODEL, 2 * D_FF), F32), pltpu.VMEM((D_FF, D_MODEL), F32),
                        pltpu.VMEM((D_MODEL, 2 * D_FF), BF16), pltpu.VMEM((D_FF, D_MODEL), BF16),
                        pltpu.SemaphoreType.DMA((2,))],
    )
    return pl.pallas_call(
        _moe_kernel,
        grid_spec=grid_spec,
        out_shape=[jax.ShapeDtypeStruct((n_rows, PERM_W), jnp.uint32)] * 2,
        compiler_params=_cparams(("arbitrary",)),
        name="moe_experts",
    )(block_e, block_rows, next_e, xs_a, xs_b, w_up, b_up.reshape(N_EXPERTS, 1, 2 * D_FF), w_down,
      b_down.reshape(N_EXPERTS, 1, D_MODEL))


def _final_kernel(x1_ref, yga_ref, ygb_ref, tg_ref, gtf_ref, sh_ref, sc_ref, g_ref, y_ref):
    tg = tg_ref[...]
    moe = jnp.zeros(x1_ref.shape, F32)
    for kk in range(TOP_K):
        moe = moe + tg[:, kk:kk + 1] * _unpack_rows(jnp.concatenate([yga_ref[kk], ygb_ref[kk]], axis=1))
    rows = x1_ref.shape[0]
    x2 = x1_ref[...] + _mod_rows(gtf_ref, rows) * moe
    y_ref[...] = _rms(x2, g_ref[...]) * (1.0 + _mod_rows(sc_ref, rows)) + _mod_rows(sh_ref, rows)


def final_out(x1, yg_a, yg_b, tg, row0, mod, mod_f, seq_rows, row_off, g_final):
    n = x1.shape[0]
    tm = ROW_TILE
    t0 = row0 // tm
    row = lambda i: (i, 0)
    mod_specs = [_mod_spec(5, seq_rows, row_off, tm),
                 _mod_spec(0, seq_rows, row_off, tm), _mod_spec(1, seq_rows, row_off, tm)]
    return pl.pallas_call(
        _final_kernel,
        grid=(n // tm,),
        in_specs=[pl.BlockSpec((tm, D_MODEL), row),
                  pl.BlockSpec((TOP_K, tm, PERM_W), lambda i: (0, i + t0, 0)),
                  pl.BlockSpec((TOP_K, tm, PERM_W), lambda i: (0, i + t0, 0)),
                  pl.BlockSpec((tm, LANES), row),
                  *mod_specs,
                  pl.BlockSpec((1, D_MODEL), lambda i: (0, 0))],
        out_specs=pl.BlockSpec((tm, D_MODEL), row),
        out_shape=jax.ShapeDtypeStruct((n, D_MODEL), F32),
        compiler_params=_cparams(("parallel",)),
        name="final_sample" if seq_rows < tm else "final_prompt",
    )(x1, yg_a, yg_b, tg, mod, mod_f, mod_f, g_final)


def _prep_weights(w_in, g_q_a, w_q_b, g_kv_a, w_kv_b, w_gk_b, b_gk, g_mla_out, g_gla_out, w_o,
                  g_norm_mix, g_norm_ffn, w_router, b_router):
    sizes = (MLA_Q_LORA, MLA_KV_LORA, MLA_ROPE, GLA_QK, GLA_QK, GLA_WIDTH, GLA_GATE_RANK, GLA_WIDTH)
    offs = np.cumsum((0,) + sizes)
    part = lambda i: w_in[:, offs[i]:offs[i + 1]]
    half = MLA_ROPE // 2
    k_rope = part(2)
    misc = jnp.concatenate([k_rope, part(6), jnp.zeros((D_MODEL, LANES - MLA_ROPE - GLA_GATE_RANK), F32)], 1)
    swap = jnp.concatenate([-k_rope[:, half:], k_rope[:, :half],
                            jnp.zeros((D_MODEL, LANES - MLA_ROPE), F32)], 1)
    w_in_pad = jnp.concatenate([part(0), part(1), part(3), part(4), part(5), part(7), misc, swap], 1)

    pad_q = jnp.zeros((MLA_Q_LORA, MLA_HEADS, HEAD_PAD - MLA_NOPE - MLA_ROPE), F32)
    wq1 = jnp.concatenate([w_q_b, pad_q], axis=2)
    q_lo = w_q_b[:, :, MLA_NOPE:MLA_NOPE + half]
    q_hi = w_q_b[:, :, MLA_NOPE + half:]
    wq2 = jnp.concatenate([jnp.zeros((MLA_Q_LORA, MLA_HEADS, MLA_NOPE), F32), -q_hi, q_lo, pad_q], axis=2)
    pad_kv = jnp.zeros((MLA_KV_LORA, MLA_HEADS, HEAD_PAD - MLA_NOPE), F32)
    w_uk = w_kv_b[:, :, :MLA_NOPE]
    w_uv = w_kv_b[:, :, MLA_NOPE:]
    wk = jnp.concatenate([w_uk, pad_kv], axis=2)
    wuk_t = jnp.concatenate([jnp.transpose(w_uk, (1, 2, 0)),
                             jnp.zeros((MLA_HEADS, HEAD_PAD - MLA_NOPE, MLA_KV_LORA), F32)], axis=1)
    wuv_h = jnp.transpose(w_uv, (1, 0, 2))
    wgk = jnp.zeros((LANES, GLA_QK), F32).at[MLA_ROPE:MLA_ROPE + GLA_GATE_RANK].set(w_gk_b)
    w_router_pad = jnp.concatenate([w_router, jnp.zeros((D_MODEL, LANES - N_EXPERTS), F32)], axis=1)
    wr_hi = w_router_pad.astype(BF16)
    w_router_pad = jnp.stack([wr_hi, (w_router_pad - wr_hi.astype(F32)).astype(BF16)])
    b_router_pad = jnp.concatenate([b_router, jnp.full((LANES - N_EXPERTS,), -jnp.inf, F32)]).reshape(1, LANES)
    return dict(
        w_in=w_in_pad.astype(BF16), g_norm_mix=g_norm_mix.reshape(1, D_MODEL),
        g_q_a=g_q_a.reshape(1, MLA_Q_LORA),
        wq1=wq1.reshape(MLA_Q_LORA, MLA_PAD).astype(BF16), wq2=wq2.reshape(MLA_Q_LORA, MLA_PAD).astype(BF16),
        g_kv_a=g_kv_a.reshape(1, MLA_KV_LORA),
        wk=wk.reshape(MLA_KV_LORA, MLA_PAD).astype(BF16),
        wv=w_uv.reshape(MLA_KV_LORA, MLA_WIDTH).astype(BF16),
        wuk_t=wuk_t.astype(BF16), wuv_h=wuv_h.astype(BF16),
        wgk=wgk.astype(BF16), b_gk=b_gk.reshape(1, GLA_QK),
        g_mla_out=g_mla_out.reshape(1, MLA_WIDTH), g_gla_out=g_gla_out.reshape(1, GLA_DV),
        w_o=w_o.astype(BF16),
        g_norm_ffn=g_norm_ffn.reshape(1, D_MODEL), w_router_pad=w_router_pad, b_router_pad=b_router_pad,
    )


def _rope_tables(pos, reps):
    half = MLA_ROPE // 2
    inv = ROPE_THETA ** (-jnp.arange(half, dtype=F32) / half)
    ang = pos.astype(F32)[:, None] * inv
    cos, sin = jnp.cos(ang), jnp.sin(ang)
    n = pos.shape[0]
    qc = jnp.concatenate([jnp.full((n, MLA_NOPE), Q_SCALE, F32), Q_SCALE * cos, Q_SCALE * cos,
                          jnp.zeros((n, HEAD_PAD - MLA_NOPE - MLA_ROPE), F32)], axis=1)
    qs = jnp.concatenate([jnp.zeros((n, MLA_NOPE), F32), Q_SCALE * sin, Q_SCALE * sin,
                          jnp.zeros((n, HEAD_PAD - MLA_NOPE - MLA_ROPE), F32)], axis=1)
    kc = jnp.concatenate([cos, cos, jnp.zeros((n, LANES - MLA_ROPE), F32)], axis=1)
    ks = jnp.concatenate([sin, sin, jnp.zeros((n, LANES - MLA_ROPE), F32)], axis=1)
    return tuple(jnp.tile(t, (reps, 1)) for t in (qc, qs, kc, ks))


def kernel(x_prompt, x_sample, cache_ckv, cache_kpe, state_gla, page_table, c_prompt, c_sample, w_ada, b_ada, g_norm_mix, w_in, g_q_a, w_q_b, g_kv_a, w_kv_b, w_gk_b, b_gk, g_mla_out, g_gla_out, w_o, g_norm_ffn, w_router, b_router, w_up, b_up, w_down, b_down, g_norm_final, w_ada_final, b_ada_final):
    B, S, D = x_prompt.shape
    DB, T, _ = x_sample.shape
    depth = w_ada.shape[0]
    assert depth == 1
    past_len = page_table.shape[1] * cache_ckv.shape[2]
    n_p, n_s = B * S, DB * T
    l = 0

    w = _prep_weights(w_in[l], g_q_a[l], w_q_b[l], g_kv_a[l], w_kv_b[l], w_gk_b[l], b_gk[l],
                      g_mla_out[l], g_gla_out[l], w_o[l], g_norm_mix[l], g_norm_ffn[l],
                      w_router[l], b_router[l])

    n_c = B + DB
    n_c_pad = (n_c + 7) // 8 * 8
    c_all = jnp.concatenate([c_sample, c_prompt, jnp.zeros((n_c_pad - n_c, D), F32)], axis=0)
    mod = ada_terms(c_all, w_ada[l], b_ada[l]).reshape(n_c_pad, 1, N_MOD * D)
    mod_f = ada_terms(c_all, w_ada_final, b_ada_final).reshape(n_c_pad, 1, 2 * D)
    off_s, off_p = 0, DB

    xp = x_prompt.reshape(n_p, D)
    xs = x_sample.reshape(n_s, D)
    tabs_p = _rope_tables(jnp.arange(S), 1)
    tabs_s = _rope_tables(past_len + jnp.arange(T), ROW_TILE // T)

    (q_p, k_p, v_p, ckv_p, kpe_p, gq_p, gk_p, gv_p, la_p, gg_p) = in_proj(
        xp, mod, S, off_p, tabs_p, w, True)
    o_mla_p = mla_prefill(q_p, k_p, v_p, B, S)
    s0 = jnp.zeros((B, GLA_HEADS, GLA_DK, GLA_DV), F32)
    o_gla_p, gla_p = gla(gq_p, gk_p, la_p, gv_p, s0, B, S)
    n_all = n_p + n_s
    h2_init = (jnp.zeros((n_all, PERM_W), jnp.uint32), jnp.zeros((n_all, PERM_W), jnp.uint32))
    x1_p, h2a, h2b, ti_p, tg_p, hist_p = mixer_out(xp, o_mla_p, o_gla_p.reshape(n_p, GLA_WIDTH), gg_p,
                                                   mod, S, off_p, w, n_all, 0, h2_init)

    (q_s, ckv_s, kpe_s, gq_s, gk_s, gv_s, la_s, gg_s) = in_proj(
        xs, mod, T, off_s, tabs_s, w, False)
    qlat, qpe = absorb_q(q_s, w["wuk_t"])
    o_lat = mla_decode(qlat, qpe, ckv_s, kpe_s, cache_ckv, cache_kpe, page_table, DB, T)
    o_mla_s = latent_to_values(o_lat, w["wuv_h"], DB, T)
    tpad = GLA_CHUNK
    padt = lambda a: jnp.pad(a.reshape(DB, T, a.shape[-1]), ((0, 0), (0, tpad - T), (0, 0))).reshape(
        DB * tpad, a.shape[-1])
    o_gla_s, gla_s = gla(padt(gq_s), padt(gk_s), padt(la_s), padt(gv_s), state_gla[l], DB, tpad)
    o_gla_s = o_gla_s.reshape(DB, tpad, GLA_WIDTH)[:, :T].reshape(n_s, GLA_WIDTH)
    x1_s, h2a, h2b, ti_s, tg_s, hist_s = mixer_out(xs, o_mla_s, o_gla_s, gg_s,
                                                   mod, T, off_s, w, n_all, n_p, (h2a, h2b))

    ti = jnp.concatenate([ti_p, ti_s], axis=0)
    base, block_e, block_rows, next_e, n_blocks = _route_tables(jnp.concatenate([hist_p, hist_s], axis=0), n_all)
    dest = route_rank(ti, base)
    idx = dest[:, :TOP_K].T.reshape(1, TOP_K * n_all)
    n_rows = n_blocks * MOE_TILE
    xs_a = sc_dispatch(h2a, idx, n_rows)
    xs_b = sc_dispatch(h2b, idx, n_rows)
    ys_a, ys_b = moe_experts(xs_a, xs_b, block_e, block_rows, next_e, w_up[l], b_up[l], w_down[l], b_down[l])
    yg_a = sc_combine(ys_a, idx).reshape(TOP_K, n_all, PERM_W)
    yg_b = sc_combine(ys_b, idx).reshape(TOP_K, n_all, PERM_W)

    g_fin = g_norm_final.reshape(1, D)
    y_p = final_out(x1_p, yg_a, yg_b, tg_p, 0, mod, mod_f, S, off_p, g_fin)
    y_s = final_out(x1_s, yg_a, yg_b, tg_s, n_p, mod, mod_f, T, off_s, g_fin)

    return (y_p.reshape(B, S, D), y_s.reshape(DB, T, D),
            ckv_p.reshape(1, B, S, MLA_KV_LORA), kpe_p.reshape(1, B, S, MLA_ROPE), gla_p[None],
            ckv_s.reshape(1, DB, T, MLA_KV_LORA), kpe_s.reshape(1, DB, T, MLA_ROPE), gla_s[None])
```

```python
import functools
import math

import jax
import jax.numpy as jnp
import numpy as np
from jax import lax
from jax.experimental import pallas as pl
from jax.experimental.pallas import tpu as pltpu
from jax.experimental.pallas import tpu_sc as plsc

F32 = jnp.float32
BF16 = jnp.bfloat16

D_MODEL = 1024
MLA_HEADS = 8
MLA_NOPE = 64
MLA_ROPE = 32
MLA_V = 64
MLA_Q_LORA = 384
MLA_KV_LORA = 256
MLA_SCALE = (MLA_NOPE + MLA_ROPE) ** -0.5
ROPE_THETA = 10000.0
GLA_HEADS = 4
GLA_DK = 64
GLA_DV = 128
GLA_GATE_RANK = 16
GLA_GATE_NORM = 16.0
GLA_CHUNK = 16
GLA_QK = GLA_HEADS * GLA_DK
GLA_WIDTH = GLA_HEADS * GLA_DV
N_EXPERTS = 32
TOP_K = 4
D_FF = D_MODEL
SWIGLU_LIMIT = 7.0
SWIGLU_ALPHA = 1.702
N_MOD = 6
EPS = 1e-6
PAGE_SIZE = 128

LANES = 128
HEAD_PAD = LANES
MLA_PAD = MLA_HEADS * HEAD_PAD
MLA_WIDTH = MLA_HEADS * MLA_V
VMEM_LIMIT = 56 * 1024 * 1024

Q_SCALE = MLA_SCALE * math.log2(math.e)

C_QA = 0
C_KV = C_QA + MLA_Q_LORA
C_GQ = C_KV + MLA_KV_LORA
C_GK = C_GQ + GLA_QK
C_GV = C_GK + GLA_QK
C_GG = C_GV + GLA_WIDTH
C_MISC = C_GG + GLA_WIDTH
C_SWAP = C_MISC + LANES
D_IN_PAD = C_SWAP + LANES

ROW_TILE = 512
RANK_TILE = 1024
ATT_TILE = 512
ATT_HEADS = 4
GLA_BLOCK = 256
DEC_SPANS = 4
SC_WIN = 128
PERM_W = D_MODEL // 4
MOE_TILE = 768
MOE_SUB = 128
NEW_PAD = 16


def _cparams(sem):
    return pltpu.CompilerParams(dimension_semantics=sem, vmem_limit_bytes=VMEM_LIMIT)


def _rms(x, g):
    return x * lax.rsqrt(jnp.mean(x * x, axis=-1, keepdims=True) + EPS) * g


def _mod_spec(term, seq_rows, row_off, tm):
    if seq_rows >= tm:
        tiles_per_seq = seq_rows // tm
        return pl.BlockSpec((1, 1, D_MODEL), lambda i: (row_off + i // tiles_per_seq, 0, term))
    g = tm // seq_rows
    assert row_off % g == 0
    return pl.BlockSpec((g, 1, D_MODEL), lambda i: (row_off // g + i, 0, term))


def _mod_rows(ref, rows):
    m = ref[...]
    g = m.shape[0]
    if g == 1:
        return m[0]
    return jnp.broadcast_to(m, (g, rows // g, m.shape[2])).reshape(rows, m.shape[2])


def _pack_rows(x):
    bits = lax.bitcast_convert_type(x.astype(BF16).astype(F32), jnp.uint32)
    w = x.shape[1] // 2
    return (bits[:, :w] >> 16) | bits[:, w:]


def _unpack_rows(words):
    lo = lax.bitcast_convert_type(words << 16, F32)
    hi = lax.bitcast_convert_type(words & jnp.uint32(0xFFFF0000), F32)
    return jnp.concatenate([lo, hi], axis=1)


def _ada_kernel(c_ref, w_ref, b_ref, o_ref):
    c = c_ref[...]
    a = (c * jax.nn.sigmoid(c)).astype(BF16)
    o_ref[...] = jnp.dot(a, w_ref[...].astype(BF16), preferred_element_type=F32) + b_ref[...]


def ada_terms(c, w, b):
    rows, d = c.shape
    n = w.shape[1]
    tn = 512
    return pl.pallas_call(
        _ada_kernel,
        grid=(n // tn,),
        in_specs=[pl.BlockSpec((rows, d), lambda j: (0, 0)),
                  pl.BlockSpec((d, tn), lambda j: (0, j)),
                  pl.BlockSpec((1, tn), lambda j: (0, j))],
        out_specs=pl.BlockSpec((rows, tn), lambda j: (0, j)),
        out_shape=jax.ShapeDtypeStruct((rows, n), F32),
        compiler_params=_cparams(("arbitrary",)),
        name="ada_terms",
    )(c, w, b.reshape(1, n))


def _in_kernel(with_kv, x_ref, sh_ref, sc_ref, g_ref, win_ref, gqa_ref, wq1_ref, wq2_ref,
               gkv_ref, wk_ref, wv_ref, wgk_ref, bgk_ref, qc_ref, qs_ref, kc_ref, ks_ref, *outs):
    if with_kv:
        q_out, k_out, v_out, ckv_out, kpe_out, gq_out, gk_out, gv_out, la_out, gg_out = outs
    else:
        q_out, ckv_out, kpe_out, gq_out, gk_out, gv_out, la_out, gg_out = outs
    x = x_ref[...]
    h = _rms(x, g_ref[...])
    rows = x.shape[0]
    h = h * (1.0 + _mod_rows(sc_ref, rows)) + _mod_rows(sh_ref, rows)
    proj = jnp.dot(h.astype(BF16), win_ref[...], preferred_element_type=F32)

    qn = _rms(proj[:, C_QA:C_KV], gqa_ref[...]).astype(BF16)
    qa = jnp.dot(qn, wq1_ref[...], preferred_element_type=F32)
    qb = jnp.dot(qn, wq2_ref[...], preferred_element_type=F32)
    qc = qc_ref[...]
    qs = qs_ref[...]
    for hh in range(MLA_HEADS):
        sl = slice(hh * HEAD_PAD, (hh + 1) * HEAD_PAD)
        q_out[:, sl] = (qa[:, sl] * qc + qb[:, sl] * qs).astype(BF16)

    ckv = _rms(proj[:, C_KV:C_GQ], gkv_ref[...])
    ckv_out[...] = ckv
    kpe = proj[:, C_MISC:C_SWAP] * kc_ref[...] + proj[:, C_SWAP:D_IN_PAD] * ks_ref[...]
    kpe_out[...] = kpe[:, :MLA_ROPE]
    if with_kv:
        ckv_b = ckv.astype(BF16)
        kn = jnp.dot(ckv_b, wk_ref[...], preferred_element_type=F32)
        kpe_sh = pltpu.roll(kpe, MLA_NOPE, axis=1)
        for hh in range(MLA_HEADS):
            sl = slice(hh * HEAD_PAD, (hh + 1) * HEAD_PAD)
            k_out[:, sl] = (kn[:, sl] + kpe_sh).astype(BF16)
        v_out[...] = jnp.dot(ckv_b, wv_ref[...], preferred_element_type=F32).astype(BF16)

    gq_out[...] = proj[:, C_GQ:C_GK] * (GLA_DK ** -0.5)
    gk_out[...] = proj[:, C_GK:C_GV]
    gv_out[...] = proj[:, C_GV:C_GG].astype(BF16)
    gg_out[...] = proj[:, C_GG:C_MISC].astype(BF16)
    xg = jnp.dot(proj[:, C_MISC:C_SWAP].astype(BF16), wgk_ref[...],
                 preferred_element_type=F32) + bgk_ref[...]
    la_out[...] = (jnp.minimum(xg, 0.0) - jnp.log(1.0 + jnp.exp(-jnp.abs(xg)))) * (1.0 / GLA_GATE_NORM)


def in_proj(x, mod, seq_rows, row_off, tabs, wts, with_kv):
    n = x.shape[0]
    tm = ROW_TILE
    nt = n // tm
    qc, qs, kc, ks = tabs
    n_tab = qc.shape[0] // tm
    row = lambda i: (i, 0)
    const = lambda i: (0, 0)
    tab_spec = pl.BlockSpec((tm, LANES), lambda i: (i % n_tab, 0))
    w = wts
    in_specs = [
        pl.BlockSpec((tm, D_MODEL), row),
        _mod_spec(0, seq_rows, row_off, tm), _mod_spec(1, seq_rows, row_off, tm),
        pl.BlockSpec((1, D_MODEL), const),
        pl.BlockSpec((D_MODEL, D_IN_PAD), const),
        pl.BlockSpec((1, MLA_Q_LORA), const),
        pl.BlockSpec((MLA_Q_LORA, MLA_PAD), const),
        pl.BlockSpec((MLA_Q_LORA, MLA_PAD), const),
        pl.BlockSpec((1, MLA_KV_LORA), const),
        pl.BlockSpec((MLA_KV_LORA, MLA_PAD), const),
        pl.BlockSpec((MLA_KV_LORA, MLA_WIDTH), const),
        pl.BlockSpec((LANES, GLA_QK), const),
        pl.BlockSpec((1, GLA_QK), const),
        tab_spec, tab_spec, tab_spec, tab_spec,
    ]
    wide = lambda dt: (jax.ShapeDtypeStruct((n, MLA_PAD), dt), pl.BlockSpec((tm, MLA_PAD), row))
    outs = [wide(BF16)]
    if with_kv:
        outs += [wide(BF16),
                 (jax.ShapeDtypeStruct((n, MLA_WIDTH), BF16), pl.BlockSpec((tm, MLA_WIDTH), row))]
    outs += [
        (jax.ShapeDtypeStruct((n, MLA_KV_LORA), F32), pl.BlockSpec((tm, MLA_KV_LORA), row)),
        (jax.ShapeDtypeStruct((n, MLA_ROPE), F32), pl.BlockSpec((tm, MLA_ROPE), row)),
        (jax.ShapeDtypeStruct((n, GLA_QK), F32), pl.BlockSpec((tm, GLA_QK), row)),
        (jax.ShapeDtypeStruct((n, GLA_QK), F32), pl.BlockSpec((tm, GLA_QK), row)),
        (jax.ShapeDtypeStruct((n, GLA_WIDTH), BF16), pl.BlockSpec((tm, GLA_WIDTH), row)),
        (jax.ShapeDtypeStruct((n, GLA_QK), F32), pl.BlockSpec((tm, GLA_QK), row)),
        (jax.ShapeDtypeStruct((n, GLA_WIDTH), BF16), pl.BlockSpec((tm, GLA_WIDTH), row)),
    ]
    return pl.pallas_call(
        functools.partial(_in_kernel, with_kv),
        grid=(nt,),
        in_specs=in_specs,
        out_specs=[o[1] for o in outs],
        out_shape=[o[0] for o in outs],
        compiler_params=_cparams(("parallel",)),
        name="inproj_prompt" if with_kv else "inproj_sample",
    )(x, mod, mod, w["g_norm_mix"], w["w_in"], w["g_q_a"], w["wq1"], w["wq2"], w["g_kv_a"],
      w["wk"], w["wv"], w["wgk"], w["b_gk"], qc, qs, kc, ks)


def _prefill_kernel(q_ref, k_ref, v_ref, o_ref):
    qi = pl.program_id(2)
    t = ATT_TILE
    lanes = [slice(g * HEAD_PAD, (g + 1) * HEAD_PAD) for g in range(ATT_HEADS)]
    v_lanes = [slice(g * MLA_V, (g + 1) * MLA_V) for g in range(ATT_HEADS)]
    qs = [q_ref[0, :, sl] for sl in lanes]

    def update_all(carries, r0, n_keys, mask):
        r0 = pl.multiple_of(r0, t)
        scores = [lax.dot_general(k_ref[0, pl.ds(r0, n_keys), sl], q, (((1,), (1,)), ((), ())),
                                  preferred_element_type=F32) for q, sl in zip(qs, lanes)]
        stats = []
        for (m, l, acc), s in zip(carries, scores):
            if mask is not None:
                s = jnp.where(mask, s, -jnp.inf)
            m_new = jnp.maximum(m, jnp.max(s, axis=0, keepdims=True))
            p = jnp.exp2(s - m_new)
            alpha = jnp.exp2(m - m_new)
            stats.append((m_new, alpha * l + jnp.sum(p, axis=0, keepdims=True), alpha * acc, p.astype(BF16)))
        out = []
        for (m_new, l_new, acc_scaled, p), sl in zip(stats, v_lanes):
            pv = lax.dot_general(v_ref[0, pl.ds(r0, n_keys), sl], p, (((0,), (0,)), ((), ())),
                                 preferred_element_type=F32)
            out.append((m_new, l_new, acc_scaled + pv))
        return tuple(out)

    def body(jj, carries):
        return update_all(carries, jj * (2 * t), 2 * t, None)

    init = (jnp.full((1, t), -jnp.inf, F32), jnp.zeros((1, t), F32), jnp.zeros((MLA_V, t), F32))
    carries = lax.fori_loop(0, qi // 2, body, (init,) * ATT_HEADS)

    def causal(n_keys):
        key_pos = lax.broadcasted_iota(jnp.int32, (n_keys, t), 0)
        qry_pos = lax.broadcasted_iota(jnp.int32, (n_keys, t), 1) + (n_keys - t)
        return key_pos <= qry_pos

    carries = lax.cond(qi % 2 == 1,
                       lambda c: update_all(c, (qi - 1) * t, 2 * t, causal(2 * t)),
                       lambda c: update_all(c, qi * t, t, causal(t)),
                       carries)
    o_ref[0] = jnp.concatenate([(acc / l).T for _, l, acc in carries], axis=1).astype(BF16)


def mla_prefill(q, k, v, batch, seq):
    t = ATT_TILE
    q3 = q.reshape(batch, seq, MLA_PAD)
    k3 = k.reshape(batch, seq, MLA_PAD)
    v3 = v.reshape(batch, seq, MLA_WIDTH)
    o = pl.pallas_call(
        _prefill_kernel,
        grid=(batch, MLA_HEADS // ATT_HEADS, seq // t),
        in_specs=[pl.BlockSpec((1, t, ATT_HEADS * HEAD_PAD), lambda b, h, i: (b, i, h)),
                  pl.BlockSpec((1, seq, ATT_HEADS * HEAD_PAD), lambda b, h, i: (b, 0, h)),
                  pl.BlockSpec((1, seq, ATT_HEADS * MLA_V), lambda b, h, i: (b, 0, h))],
        out_specs=pl.BlockSpec((1, t, ATT_HEADS * MLA_V), lambda b, h, i: (b, i, h)),
        out_shape=jax.ShapeDtypeStruct((batch, seq, MLA_WIDTH), BF16),
        compiler_params=_cparams(("parallel", "parallel", "arbitrary")),
        name="mla_prefill",
    )(q3, k3, v3)
    return o.reshape(batch * seq, MLA_WIDTH)


def _absorb_kernel(q_ref, wuk_ref, qlat_ref, qpe_ref):
    q = q_ref[...]
    qlat_ref[0] = jnp.dot(q, wuk_ref[0], preferred_element_type=F32)
    qf = pltpu.roll(q.astype(F32), HEAD_PAD - MLA_NOPE, axis=1)
    lane = lax.broadcasted_iota(jnp.int32, qf.shape, 1)
    qpe_ref[0] = jnp.where(lane < MLA_ROPE, qf, 0.0)


def absorb_q(q, wuk_t):
    r = q.shape[0]
    return pl.pallas_call(
        _absorb_kernel,
        grid=(MLA_HEADS,),
        in_specs=[pl.BlockSpec((r, HEAD_PAD), lambda h: (0, h)),
                  pl.BlockSpec((1, HEAD_PAD, MLA_KV_LORA), lambda h: (h, 0, 0))],
        out_specs=[pl.BlockSpec((1, r, MLA_KV_LORA), lambda h: (h, 0, 0)),
                   pl.BlockSpec((1, r, HEAD_PAD), lambda h: (h, 0, 0))],
        out_shape=[jax.ShapeDtypeStruct((MLA_HEADS, r, MLA_KV_LORA), F32),
                   jax.ShapeDtypeStruct((MLA_HEADS, r, HEAD_PAD), F32)],
        compiler_params=_cparams(("parallel",)),
        name="absorb_q",
    )(q, wuk_t)


def _decode_kernel(t_new, pt_ref, qlat_ref, qpe_ref, cnew_ref, knew_ref, ckv_hbm, kpe_hbm, o_ref,
                   cbuf, pbuf, sem):
    b = pl.program_id(0)
    nb = pl.num_programs(0)
    n_pages = cbuf.shape[1]
    slot = lax.rem(b, 2)
    rows = MLA_HEADS * t_new

    def page_copies(bb, sl, p):
        pg = pt_ref[bb, p]
        return (pltpu.make_async_copy(ckv_hbm.at[0, pg], cbuf.at[sl, p], sem.at[0, sl]),
                pltpu.make_async_copy(kpe_hbm.at[0, pg], pbuf.at[sl, p], sem.at[1, sl]))

    def fetch(bb, sl):
        def body(p, c):
            latent_cp, rope_cp = page_copies(bb, sl, p)
            latent_cp.start()
            rope_cp.start(priority=1)
            return c
        lax.fori_loop(0, n_pages, body, 0)

    @pl.when(b == 0)
    def _():
        fetch(0, 0)

    @pl.when(b + 1 < nb)
    def _():
        fetch(b + 1, 1 - slot)

    pltpu.make_async_copy(ckv_hbm.at[0, pl.ds(0, n_pages)], cbuf.at[slot], sem.at[0, slot]).wait()
    pltpu.make_async_copy(kpe_hbm.at[0, pl.ds(0, n_pages)], pbuf.at[slot], sem.at[1, slot]).wait()

    qlat = qlat_ref[...].reshape(rows, MLA_KV_LORA).astype(BF16)
    qpe = qpe_ref[...].reshape(rows, HEAD_PAD)[:, :MLA_ROPE].astype(BF16)
    dn = (((1,), (1,)), ((), ()))

    span_pages = n_pages // DEC_SPANS
    values, scores = [], []
    for c in range(DEC_SPANS):
        pages = range(c * span_pages, (c + 1) * span_pages)
        kb = jnp.concatenate([cbuf[slot, p].astype(BF16) for p in pages], axis=0)
        pb = jnp.concatenate([pbuf[slot, p].astype(BF16) for p in pages], axis=1)
        values.append(kb)
        scores.append(lax.dot_general(qlat, kb, dn, preferred_element_type=F32)
                      + jnp.dot(qpe, pb, preferred_element_type=F32))
    cn = cnew_ref[0].astype(BF16)
    kn = knew_ref[0].astype(BF16)
    sn = (lax.dot_general(qlat, cn, dn, preferred_element_type=F32)
          + lax.dot_general(qpe, kn, dn, preferred_element_type=F32))
    tq = lax.broadcasted_iota(jnp.int32, sn.shape, 0) % t_new
    tk = lax.broadcasted_iota(jnp.int32, sn.shape, 1)
    values.append(cn)
    scores.append(jnp.where(tk <= tq, sn, -jnp.inf))

    maxes = [jnp.max(s, axis=-1, keepdims=True) for s in scores]
    m = maxes[0]
    for pm in maxes[1:]:
        m = jnp.maximum(m, pm)
    probs = [jnp.exp2(s - pm) for s, pm in zip(scores, maxes)]
    l = jnp.zeros_like(m)
    acc = jnp.zeros((rows, MLA_KV_LORA), F32)
    for p, pm, vals in zip(probs, maxes, values):
        scale = jnp.exp2(pm - m)
        l = l + scale * jnp.sum(p, axis=-1, keepdims=True)
        acc = acc + scale * jnp.dot(p.astype(BF16), vals, preferred_element_type=F32)
    o_ref[0] = acc / l


def mla_decode(qlat, qpe, ckv_new, kpe_new, cache_ckv, cache_kpe, page_table, dec_batch, t_new):
    n_pages = page_table.shape[1]
    past_len = n_pages * PAGE_SIZE
    rows = MLA_HEADS * t_new
    kpe_t = jnp.swapaxes(cache_kpe, 2, 3)
    qlat4 = qlat.reshape(MLA_HEADS, dec_batch, t_new, MLA_KV_LORA)
    qpe4 = qpe.reshape(MLA_HEADS, dec_batch, t_new, HEAD_PAD)
    t_pad = NEW_PAD
    pad_new = lambda a: jnp.pad(a.reshape(dec_batch, t_new, a.shape[-1]), ((0, 0), (0, t_pad - t_new), (0, 0)))
    cnew = pad_new(ckv_new)
    knew = pad_new(kpe_new)

    in_specs = [
        pl.BlockSpec((MLA_HEADS, 1, t_new, MLA_KV_LORA), lambda b, pt: (0, b, 0, 0)),
        pl.BlockSpec((MLA_HEADS, 1, t_new, HEAD_PAD), lambda b, pt: (0, b, 0, 0)),
        pl.BlockSpec((1, t_pad, MLA_KV_LORA), lambda b, pt: (b, 0, 0)),
        pl.BlockSpec((1, t_pad, MLA_ROPE), lambda b, pt: (b, 0, 0)),
        pl.BlockSpec(memory_space=pl.ANY),
        pl.BlockSpec(memory_space=pl.ANY),
    ]
    grid_spec = pltpu.PrefetchScalarGridSpec(
        num_scalar_prefetch=1,
        grid=(dec_batch,),
        in_specs=in_specs,
        out_specs=pl.BlockSpec((1, rows, MLA_KV_LORA), lambda b, pt: (b, 0, 0)),
        scratch_shapes=[pltpu.VMEM((2, n_pages, PAGE_SIZE, MLA_KV_LORA), F32),
                        pltpu.VMEM((2, n_pages, MLA_ROPE, PAGE_SIZE), F32),
                        pltpu.SemaphoreType.DMA((2, 2))],
    )
    return pl.pallas_call(
        functools.partial(_decode_kernel, t_new),
        grid_spec=grid_spec,
        out_shape=jax.ShapeDtypeStruct((dec_batch, rows, MLA_KV_LORA), F32),
        compiler_params=_cparams(("arbitrary",)),
        name="mla_decode",
    )(page_table, qlat4, qpe4, cnew, knew, cache_ckv, kpe_t)


def _uv_kernel(o_ref, wuv_ref, out_ref):
    outs = []
    for hh in range(2):
        o = o_ref[:, hh]
        o = o.reshape(o.shape[0] * o.shape[1], MLA_KV_LORA).astype(BF16)
        outs.append(jnp.dot(o, wuv_ref[hh], preferred_element_type=F32))
    out_ref[...] = jnp.concatenate(outs, axis=1).astype(BF16)


def latent_to_values(o_lat, wuv, dec_batch, t_new):
    o4 = o_lat.reshape(dec_batch, MLA_HEADS, t_new, MLA_KV_LORA)
    return pl.pallas_call(
        _uv_kernel,
        grid=(MLA_HEADS // 2,),
        in_specs=[pl.BlockSpec((dec_batch, 2, t_new, MLA_KV_LORA), lambda h: (0, h, 0, 0)),
                  pl.BlockSpec((2, MLA_KV_LORA, MLA_V), lambda h: (h, 0, 0))],
        out_specs=pl.BlockSpec((dec_batch * t_new, 2 * MLA_V), lambda h: (0, h)),
        out_shape=jax.ShapeDtypeStruct((dec_batch * t_new, MLA_WIDTH), BF16),
        compiler_params=_cparams(("parallel",)),
        name="latent_to_values",
    )(o4, wuv)


def _gla_kernel(nsb, q_ref, k_ref, la_ref, v_ref, s0_ref, o_ref, sfin_ref, st_sc, kv_sc, sall_sc):
    blk = pl.program_id(1)
    nblk = pl.num_programs(1)
    c = GLA_CHUNK
    tb = q_ref.shape[1]
    nc = tb // c
    cps = nc // nsb
    head_k = [slice(hh * GLA_DK, (hh + 1) * GLA_DK) for hh in range(GLA_HEADS)]
    head_v = [slice(hh * GLA_DV, (hh + 1) * GLA_DV) for hh in range(GLA_HEADS)]

    @pl.when(blk == 0)
    def _():
        for sq in range(nsb):
            for hh in range(GLA_HEADS):
                st_sc[sq, hh] = s0_ref[sq, hh].T

    q = q_ref[0]
    k = k_ref[0]
    la = la_ref[0]
    v = v_ref[0]

    r = lax.broadcasted_iota(jnp.int32, (tb, tb), 0)
    cc = lax.broadcasted_iota(jnp.int32, (tb, tb), 1)
    same = (r >> 4) == (cc >> 4)
    tri = jnp.where(same & (cc <= r), 1.0, 0.0)
    ones = jnp.where(same, 1.0, 0.0)
    sel = jnp.concatenate([tri, ones], axis=0).astype(BF16)
    la_hi = la.astype(BF16)
    rest = la - la_hi.astype(F32)
    la_mid = rest.astype(BF16)
    la_lo = (rest - la_mid.astype(F32)).astype(BF16)
    sums = (jnp.dot(sel, la_hi, preferred_element_type=F32)
            + (jnp.dot(sel, la_mid, preferred_element_type=F32)
               + jnp.dot(sel, la_lo, preferred_element_type=F32)))
    b = sums[:tb]
    b_last = sums[tb:]
    qd = (q * jnp.exp(b)).astype(BF16)
    kd = (k * jnp.exp(b_last - b)).astype(BF16)
    dec = jnp.exp(b_last)

    key_head = lax.broadcasted_iota(jnp.int32, (GLA_QK, GLA_HEADS * c), 0) >> 6
    col = lax.broadcasted_iota(jnp.int32, (GLA_QK, GLA_HEADS * c), 1)
    b3 = b.reshape(nc, c, GLA_QK)
    k3 = k.reshape(nc, c, GLA_QK)
    q3 = q.reshape(nc, c, GLA_QK)
    pos = lax.broadcasted_iota(jnp.int32, (nc, c, GLA_QK), 1)
    scores = jnp.zeros((tb, GLA_HEADS * c), F32)
    for j in range(c):
        e = jnp.exp(b3 - b3[:, j:j + 1, :])
        w = jnp.where(pos >= j, q3 * k3[:, j:j + 1, :] * e, 0.0)
        place = jnp.where(col == key_head * c + j, 1.0, 0.0).astype(BF16)
        scores = scores + jnp.dot(w.reshape(tb, GLA_QK).astype(BF16), place, preferred_element_type=F32)
    scores = scores.astype(BF16)

    for ci in range(nc):
        rs = slice(ci * c, (ci + 1) * c)
        for hh in range(GLA_HEADS):
            o_ref[0, rs, head_v[hh]] = jnp.dot(scores[rs, hh * c:(hh + 1) * c], v[rs, head_v[hh]],
                                               preferred_element_type=F32)
            kv_sc[ci, hh] = lax.dot_general(v[rs, head_v[hh]], kd[rs, head_k[hh]], (((0,), (0,)), ((), ())),
                                            preferred_element_type=F32)

    for sq in range(nsb):
        states = [st_sc[sq, hh] for hh in range(GLA_HEADS)]
        for cj in range(cps):
            ci = sq * cps + cj
            dec_row = dec[ci * c:ci * c + 1, :]
            for hh in range(GLA_HEADS):
                sall_sc[ci, hh] = states[hh].astype(BF16)
                states[hh] = states[hh] * dec_row[:, head_k[hh]] + kv_sc[ci, hh]
        for hh in range(GLA_HEADS):
            st_sc[sq, hh] = states[hh]

    for ci in range(nc):
        rs = slice(ci * c, (ci + 1) * c)
        outs = [lax.dot_general(qd[rs, head_k[hh]], sall_sc[ci, hh], (((1,), (1,)), ((), ())),
                                preferred_element_type=F32) for hh in range(GLA_HEADS)]
        o_ref[0, rs, :] += jnp.concatenate(outs, axis=1)

    @pl.when(blk == nblk - 1)
    def _():
        for sq in range(nsb):
            for hh in range(GLA_HEADS):
                sfin_ref[sq, hh] = st_sc[sq, hh].T


def gla(gq, gk, la, gv, s0, n_seq, seq_len):
    tb = GLA_BLOCK
    nsb = max(1, tb // seq_len)
    nblk = max(1, seq_len // tb)
    n_outer = n_seq // nsb
    nc = tb // GLA_CHUNK
    sh3 = lambda a: a.reshape(n_outer, nblk * tb, a.shape[-1])
    row = lambda b, i: (b, i, 0)
    st_spec = pl.BlockSpec((nsb, GLA_HEADS, GLA_DK, GLA_DV), lambda b, i: (b, 0, 0, 0))
    o, s_fin = pl.pallas_call(
        functools.partial(_gla_kernel, nsb),
        grid=(n_outer, nblk),
        in_specs=[pl.BlockSpec((1, tb, GLA_QK), row), pl.BlockSpec((1, tb, GLA_QK), row),
                  pl.BlockSpec((1, tb, GLA_QK), row), pl.BlockSpec((1, tb, GLA_WIDTH), row), st_spec],
        out_specs=[pl.BlockSpec((1, tb, GLA_WIDTH), row), st_spec],
        out_shape=[jax.ShapeDtypeStruct((n_outer, nblk * tb, GLA_WIDTH), F32),
                   jax.ShapeDtypeStruct((n_seq, GLA_HEADS, GLA_DK, GLA_DV), F32)],
        scratch_shapes=[pltpu.VMEM((nsb, GLA_HEADS, GLA_DV, GLA_DK), F32),
                        pltpu.VMEM((nc, GLA_HEADS, GLA_DV, GLA_DK), F32),
                        pltpu.VMEM((nc, GLA_HEADS, GLA_DV, GLA_DK), BF16)],
        compiler_params=_cparams(("parallel", "arbitrary")),
        name="gla_prompt" if nblk > 1 else "gla_sample",
    )(sh3(gq), sh3(gk), sh3(la), sh3(gv), s0)
    return o.reshape(n_seq * seq_len, GLA_WIDTH), s_fin


def _mix_kernel(x_ref, om_ref, og_ref, gg_ref, gta_ref, shf_ref, scf_ref, gm_ref, ggl_ref, wo_ref,
                gn_ref, wr_ref, br_ref, h2a_in, h2b_in, x1_ref, h2a_ref, h2b_ref, ti_ref, tg_ref, hist_ref):
    del h2a_in, h2b_in
    om = _rms(om_ref[...].astype(F32), gm_ref[...])
    og = og_ref[...]
    gg = gg_ref[...].astype(F32)
    gate = gg * jax.nn.sigmoid(gg)
    parts = []
    for hh in range(GLA_HEADS):
        sl = slice(hh * GLA_DV, (hh + 1) * GLA_DV)
        parts.append(_rms(og[:, sl], ggl_ref[...]) * gate[:, sl])
    mix = jnp.concatenate([om] + parts, axis=1).astype(BF16)
    rows = x_ref.shape[0]
    x1 = x_ref[...] + _mod_rows(gta_ref, rows) * jnp.dot(mix, wo_ref[...], preferred_element_type=F32)
    x1_ref[...] = x1
    h2 = _rms(x1, gn_ref[...]) * (1.0 + _mod_rows(scf_ref, rows)) + _mod_rows(shf_ref, rows)
    words = _pack_rows(h2)
    h2a_ref[...] = words[:, :PERM_W]
    h2b_ref[...] = words[:, PERM_W:]
    h_hi = h2.astype(BF16)
    h_lo = (h2 - h_hi.astype(F32)).astype(BF16)
    logits = (jnp.dot(h_hi, wr_ref[0], preferred_element_type=F32)
              + (jnp.dot(h_hi, wr_ref[1], preferred_element_type=F32)
                 + jnp.dot(h_lo, wr_ref[0], preferred_element_type=F32))) + br_ref[...]
    lane_i = lax.broadcasted_iota(jnp.int32, logits.shape, 1)
    lane = lane_i.astype(F32)
    vals = []
    idxs = []
    for _ in range(TOP_K):
        mx = jnp.max(logits, axis=-1, keepdims=True)
        ix = jnp.min(jnp.where(logits == mx, lane, float(LANES)), axis=-1, keepdims=True)
        vals.append(mx)
        idxs.append(ix)
        logits = jnp.where(lane == ix, -jnp.inf, logits)
    ex = [jnp.exp(vv - vals[0]) for vv in vals]
    den = ex[0] + ex[1] + ex[2] + ex[3]
    ti = jnp.zeros(logits.shape, F32)
    tg = jnp.zeros(logits.shape, F32)
    onehot = jnp.zeros(logits.shape, F32)
    for kk in range(TOP_K):
        ti = jnp.where(lane_i == kk, idxs[kk], ti)
        tg = jnp.where(lane_i == kk, ex[kk] / den, tg)
        onehot = onehot + jnp.where(lane == idxs[kk], 1.0, 0.0)
    ti_ref[...] = ti.astype(jnp.int32)
    tg_ref[...] = tg
    hist_ref[0] = jnp.sum(onehot, axis=0, keepdims=True)


def mixer_out(x, o_mla, o_gla, gg, mod, seq_rows, row_off, w, n_all, row0, h2_buf):
    n = x.shape[0]
    tm = ROW_TILE
    t0 = row0 // tm
    row = lambda i: (i, 0)
    const = lambda i: (0, 0)
    extra_specs = [pl.BlockSpec(memory_space=pl.ANY)] * 2
    extra_args = list(h2_buf)
    mod_specs = [_mod_spec(term, seq_rows, row_off, tm) for term in (2, 3, 4)]
    d_mix = MLA_WIDTH + GLA_WIDTH
    n_in = 13
    return pl.pallas_call(
        _mix_kernel,
        grid=(n // tm,),
        in_specs=[pl.BlockSpec((tm, D_MODEL), row), pl.BlockSpec((tm, MLA_WIDTH), row),
                  pl.BlockSpec((tm, GLA_WIDTH), row), pl.BlockSpec((tm, GLA_WIDTH), row),
                  *mod_specs,
                  pl.BlockSpec((1, MLA_WIDTH), const), pl.BlockSpec((1, GLA_DV), const),
                  pl.BlockSpec((d_mix, D_MODEL), const), pl.BlockSpec((1, D_MODEL), const),
                  pl.BlockSpec((2, D_MODEL, LANES), lambda i: (0, 0, 0)),
                  pl.BlockSpec((1, LANES), const)] + extra_specs,
        out_specs=[pl.BlockSpec((tm, D_MODEL), row),
                   pl.BlockSpec((tm, PERM_W), lambda i: (i + t0, 0)),
                   pl.BlockSpec((tm, PERM_W), lambda i: (i + t0, 0)),
                   pl.BlockSpec((tm, LANES), row), pl.BlockSpec((tm, LANES), row),
                   pl.BlockSpec((1, 1, LANES), lambda i: (i, 0, 0))],
        out_shape=[jax.ShapeDtypeStruct((n, D_MODEL), F32),
                   jax.ShapeDtypeStruct((n_all, PERM_W), jnp.uint32),
                   jax.ShapeDtypeStruct((n_all, PERM_W), jnp.uint32),
                   jax.ShapeDtypeStruct((n, LANES), jnp.int32), jax.ShapeDtypeStruct((n, LANES), F32),
                   jax.ShapeDtypeStruct((n // tm, 1, LANES), F32)],
        input_output_aliases={n_in: 1, n_in + 1: 2},
        compiler_params=_cparams(("parallel",)),
        name="mixer_sample" if seq_rows < tm else "mixer_prompt",
    )(x, o_mla, o_gla, gg, mod, mod, mod, w["g_mla_out"], w["g_gla_out"], w["w_o"],
      w["g_norm_ffn"], w["w_router_pad"], w["b_router_pad"], *extra_args)


def _rank_kernel(ti_ref, base_ref, dest_ref):
    ti = ti_ref[...]
    tm = ti.shape[0]
    lane = lax.broadcasted_iota(jnp.int32, ti.shape, 1)
    cols = [ti[:, kk:kk + 1] for kk in range(TOP_K)]
    onehot = jnp.zeros(ti.shape, F32)
    for kk in range(TOP_K):
        onehot = onehot + jnp.where(lane == cols[kk], 1.0, 0.0)
    r = lax.broadcasted_iota(jnp.int32, (tm, tm), 0)
    c = lax.broadcasted_iota(jnp.int32, (tm, tm), 1)
    earlier = jnp.where(c < r, 1.0, 0.0).astype(BF16)
    pos = jnp.dot(earlier, onehot.astype(BF16), preferred_element_type=F32) + base_ref[0]
    out = jnp.zeros(ti.shape, F32)
    for kk in range(TOP_K):
        dk = jnp.sum(jnp.where(lane == cols[kk], pos, 0.0), axis=-1, keepdims=True)
        out = jnp.where(lane == kk, dk, out)
    dest_ref[...] = out.astype(jnp.int32)


def route_rank(ti, base):
    n = ti.shape[0]
    tm = RANK_TILE
    return pl.pallas_call(
        _rank_kernel,
        grid=(n // tm,),
        in_specs=[pl.BlockSpec((tm, LANES), lambda i: (i, 0)),
                  pl.BlockSpec((1, 1, LANES), lambda i: (i, 0, 0))],
        out_specs=pl.BlockSpec((tm, LANES), lambda i: (i, 0)),
        out_shape=jax.ShapeDtypeStruct((n, LANES), jnp.int32),
        compiler_params=_cparams(("parallel",)),
        name="route_rank",
    )(ti, base)


def _route_tables(hist, n_tok):
    tm = MOE_TILE
    h = hist[:, 0, :].astype(jnp.int32)
    h = h.reshape(-1, RANK_TILE // ROW_TILE, LANES).sum(axis=1)
    counts = jnp.sum(h, axis=0)
    padded = (counts + tm - 1) // tm * tm
    pad_ends = jnp.cumsum(padded)
    pad_starts = pad_ends - padded
    base = (pad_starts[None, :] + jnp.cumsum(h, axis=0) - h).astype(F32)[:, None, :]
    n_blocks = pl.cdiv(n_tok * TOP_K, tm) + N_EXPERTS
    n_active = (pad_ends[N_EXPERTS - 1] // tm).astype(jnp.int32)
    blk = jnp.arange(n_blocks, dtype=jnp.int32)
    blk_c = jnp.minimum(blk, n_active - 1)
    ends = pad_ends[:N_EXPERTS]
    block_e = jnp.minimum(jnp.sum((ends[None, :] <= (blk_c * tm)[:, None]).astype(jnp.int32), axis=1),
                          N_EXPERTS - 1).astype(jnp.int32)
    used_end = (pad_starts + counts)[:N_EXPERTS][block_e]
    block_rows = jnp.where(blk < n_active, jnp.clip(used_end - blk * tm, 0, tm), 0).astype(jnp.int32)
    experts = jnp.arange(N_EXPERTS, dtype=jnp.int32)
    later = (experts[None, :] > block_e[:, None]) & (counts[None, :N_EXPERTS] > 0)
    next_e = jnp.min(jnp.where(later, experts[None, :], N_EXPERTS), axis=1).astype(jnp.int32)
    return base, block_e, block_rows, next_e, n_blocks


def _sc_mesh():
    return plsc.VectorSubcoreMesh(core_axis_name="c", subcore_axis_name="s")


def sc_dispatch(x_rows, idx, n_out):
    n, wd = x_rows.shape
    win = SC_WIN
    nwin = n // win

    @functools.partial(pl.kernel, out_type=jax.ShapeDtypeStruct((n_out, wd), x_rows.dtype),
                       mesh=_sc_mesh(), scratch_types=[])
    def k(x_hbm, i_hbm, o_hbm):
        def body(x_vmem, i_vmem):
            pltpu.sync_copy(x_vmem, o_hbm.at[i_vmem.at[0]])

        pltpu.emit_pipeline(
            body,
            grid=(idx.shape[1] // win,),
            in_specs=[pl.BlockSpec((win, wd), lambda i: (i % nwin, 0)),
                      pl.BlockSpec((1, win), lambda i: (0, i))],
            out_specs=[],
            core_axis_name=("c", "s"),
            dimension_semantics=(pltpu.PARALLEL,),
        )(x_hbm, i_hbm)

    return k(x_rows, idx)


def sc_combine(y_rows, idx):
    wd = y_rows.shape[1]
    m = idx.shape[1]
    win = SC_WIN

    @functools.partial(pl.kernel, out_type=jax.ShapeDtypeStruct((m, wd), y_rows.dtype),
                       mesh=_sc_mesh(), scratch_types=[])
    def k(y_hbm, i_hbm, o_hbm):
        def body(i_vmem, o_vmem):
            pltpu.sync_copy(y_hbm.at[i_vmem.at[0]], o_vmem)

        pltpu.emit_pipeline(
            body,
            grid=(m // win,),
            in_specs=[pl.BlockSpec((1, win), lambda i: (0, i))],
            out_specs=[pl.BlockSpec((win, wd), lambda i: (i, 0))],
            core_axis_name=("c", "s"),
            dimension_semantics=(pltpu.PARALLEL,),
        )(i_hbm, o_hbm)

    return k(y_rows, idx)


def _moe_kernel(be_ref, nr_ref, ne_ref, xa_ref, xb_ref, wup_hbm, bup_ref, wdn_hbm, bdn_ref, ya_ref, yb_ref,
                wup_f32, wdn_f32, wup_sc, wdn_sc, sem):
    i = pl.program_id(0)
    n_real = nr_ref[i]
    expert = be_ref[i]
    prev = be_ref[jnp.maximum(i - 1, 0)]
    fresh = (i == 0) | (expert != prev)

    def weight_copies(ex):
        return (pltpu.make_async_copy(wup_hbm.at[ex], wup_f32, sem.at[0]),
                pltpu.make_async_copy(wdn_hbm.at[ex], wdn_f32, sem.at[1]))

    @pl.when(i == 0)
    def _():
        for cp in weight_copies(expert):
            cp.start()

    @pl.when((n_real > 0) & fresh)
    def _():
        for cp in weight_copies(expert):
            cp.wait()
        wup_sc[...] = wup_f32[...].astype(BF16)
        wdn_sc[...] = wdn_f32[...].astype(BF16)
        nxt = ne_ref[i]

        @pl.when(nxt < N_EXPERTS)
        def _():
            for cp in weight_copies(nxt):
                cp.start()

    n_sub = MOE_TILE // MOE_SUB
    live_subs = (n_real + (MOE_SUB - 1)) // MOE_SUB
    for live in range(n_sub + 1):
        m = live * MOE_SUB

        @pl.when(live_subs == live)
        def _():
            if m > 0:
                xb = _unpack_rows(jnp.concatenate([xa_ref[:m, :], xb_ref[:m, :]], axis=1)).astype(BF16)
                hu = jnp.dot(xb, wup_sc[...], preferred_element_type=F32) + bup_ref[0]
                gate = jnp.minimum(hu[:, :D_FF], SWIGLU_LIMIT)
                lin = jnp.clip(hu[:, D_FF:], -SWIGLU_LIMIT, SWIGLU_LIMIT)
                act = gate * jax.nn.sigmoid(SWIGLU_ALPHA * gate) * (lin + 1.0)
                y = jnp.dot(act.astype(BF16), wdn_sc[...], preferred_element_type=F32) + bdn_ref[0]
                words = _pack_rows(y)
                ya_ref[:m, :] = words[:, :PERM_W]
                yb_ref[:m, :] = words[:, PERM_W:]
            if m < MOE_TILE:
                ya_ref[m:, :] = jnp.zeros((MOE_TILE - m, PERM_W), jnp.uint32)
                yb_ref[m:, :] = jnp.zeros((MOE_TILE - m, PERM_W), jnp.uint32)


def moe_experts(xs_a, xs_b, block_e, block_rows, next_e, w_up, b_up, w_down, b_down):
    n_rows = xs_a.shape[0]
    tm = MOE_TILE
    n_blocks = n_rows // tm
    emap3 = lambda i, be, nr, ne: (be[i], 0, 0)
    rows = lambda i, be, nr, ne: (i, 0)
    grid_spec = pltpu.PrefetchScalarGridSpec(
        num_scalar_prefetch=3,
        grid=(n_blocks,),
        in_specs=[pl.BlockSpec((tm, PERM_W), rows),
                  pl.BlockSpec((tm, PERM_W), rows),
                  pl.BlockSpec(memory_space=pl.ANY),
                  pl.BlockSpec((1, 1, 2 * D_FF), emap3),
                  pl.BlockSpec(memory_space=pl.ANY),
                  pl.BlockSpec((1, 1, D_MODEL), emap3)],
        out_specs=[pl.BlockSpec((tm, PERM_W), rows),
                   pl.BlockSpec((tm, PERM_W), rows)],
        scratch_shapes=[pltpu.VMEM((D_MODEL, 2 * D_FF), F32), pltpu.VMEM((D_FF, D_MODEL), F32),
                        pltpu.VMEM((D_MODEL, 2 * D_FF), BF16), pltpu.VMEM((D_FF, D_MODEL), BF16),
                        pltpu.SemaphoreType.DMA((2,))],
    )
    return pl.pallas_call(
        _moe_kernel,
        grid_spec=grid_spec,
        out_shape=[jax.ShapeDtypeStruct((n_rows, PERM_W), jnp.uint32)] * 2,
        compiler_params=_cparams(("arbitrary",)),
        name="moe_experts",
    )(block_e, block_rows, next_e, xs_a, xs_b, w_up, b_up.reshape(N_EXPERTS, 1, 2 * D_FF), w_down,
      b_down.reshape(N_EXPERTS, 1, D_MODEL))


def _final_kernel(x1_ref, yga_ref, ygb_ref, tg_ref, gtf_ref, sh_ref, sc_ref, g_ref, y_ref):
    tg = tg_ref[...]
    moe = jnp.zeros(x1_ref.shape, F32)
    for kk in range(TOP_K):
        moe = moe + tg[:, kk:kk + 1] * _unpack_rows(jnp.concatenate([yga_ref[kk], ygb_ref[kk]], axis=1))
    rows = x1_ref.shape[0]
    x2 = x1_ref[...] + _mod_rows(gtf_ref, rows) * moe
    y_ref[...] = _rms(x2, g_ref[...]) * (1.0 + _mod_rows(sc_ref, rows)) + _mod_rows(sh_ref, rows)


def final_out(x1, yg_a, yg_b, tg, row0, mod, mod_f, seq_rows, row_off, g_final):
    n = x1.shape[0]
    tm = ROW_TILE
    t0 = row0 // tm
    row = lambda i: (i, 0)
    mod_specs = [_mod_spec(5, seq_rows, row_off, tm),
                 _mod_spec(0, seq_rows, row_off, tm), _mod_spec(1, seq_rows, row_off, tm)]
    return pl.pallas_call(
        _final_kernel,
        grid=(n // tm,),
        in_specs=[pl.BlockSpec((tm, D_MODEL), row),
                  pl.BlockSpec((TOP_K, tm, PERM_W), lambda i: (0, i + t0, 0)),
                  pl.BlockSpec((TOP_K, tm, PERM_W), lambda i: (0, i + t0, 0)),
                  pl.BlockSpec((tm, LANES), row),
                  *mod_specs,
                  pl.BlockSpec((1, D_MODEL), lambda i: (0, 0))],
        out_specs=pl.BlockSpec((tm, D_MODEL), row),
        out_shape=jax.ShapeDtypeStruct((n, D_MODEL), F32),
        compiler_params=_cparams(("parallel",)),
        name="final_sample" if seq_rows < tm else "final_prompt",
    )(x1, yg_a, yg_b, tg, mod, mod_f, mod_f, g_final)


def _prep_weights(w_in, g_q_a, w_q_b, g_kv_a, w_kv_b, w_gk_b, b_gk, g_mla_out, g_gla_out, w_o,
                  g_norm_mix, g_norm_ffn, w_router, b_router):
    sizes = (MLA_Q_LORA, MLA_KV_LORA, MLA_ROPE, GLA_QK, GLA_QK, GLA_WIDTH, GLA_GATE_RANK, GLA_WIDTH)
    offs = np.cumsum((0,) + sizes)
    part = lambda i: w_in[:, offs[i]:offs[i + 1]]
    half = MLA_ROPE // 2
    k_rope = part(2)
    misc = jnp.concatenate([k_rope, part(6), jnp.zeros((D_MODEL, LANES - MLA_ROPE - GLA_GATE_RANK), F32)], 1)
    swap = jnp.concatenate([-k_rope[:, half:], k_rope[:, :half],
                            jnp.zeros((D_MODEL, LANES - MLA_ROPE), F32)], 1)
    w_in_pad = jnp.concatenate([part(0), part(1), part(3), part(4), part(5), part(7), misc, swap], 1)

    pad_q = jnp.zeros((MLA_Q_LORA, MLA_HEADS, HEAD_PAD - MLA_NOPE - MLA_ROPE), F32)
    wq1 = jnp.concatenate([w_q_b, pad_q], axis=2)
    q_lo = w_q_b[:, :, MLA_NOPE:MLA_NOPE + half]
    q_hi = w_q_b[:, :, MLA_NOPE + half:]
    wq2 = jnp.concatenate([jnp.zeros((MLA_Q_LORA, MLA_HEADS, MLA_NOPE), F32), -q_hi, q_lo, pad_q], axis=2)
    pad_kv = jnp.zeros((MLA_KV_LORA, MLA_HEADS, HEAD_PAD - MLA_NOPE), F32)
    w_uk = w_kv_b[:, :, :MLA_NOPE]
    w_uv = w_kv_b[:, :, MLA_NOPE:]
    wk = jnp.concatenate([w_uk, pad_kv], axis=2)
    wuk_t = jnp.concatenate([jnp.transpose(w_uk, (1, 2, 0)),
                             jnp.zeros((MLA_HEADS, HEAD_PAD - MLA_NOPE, MLA_KV_LORA), F32)], axis=1)
    wuv_h = jnp.transpose(w_uv, (1, 0, 2))
    wgk = jnp.zeros((LANES, GLA_QK), F32).at[MLA_ROPE:MLA_ROPE + GLA_GATE_RANK].set(w_gk_b)
    w_router_pad = jnp.concatenate([w_router, jnp.zeros((D_MODEL, LANES - N_EXPERTS), F32)], axis=1)
    wr_hi = w_router_pad.astype(BF16)
    w_router_pad = jnp.stack([wr_hi, (w_router_pad - wr_hi.astype(F32)).astype(BF16)])
    b_router_pad = jnp.concatenate([b_router, jnp.full((LANES - N_EXPERTS,), -jnp.inf, F32)]).reshape(1, LANES)
    return dict(
        w_in=w_in_pad.astype(BF16), g_norm_mix=g_norm_mix.reshape(1, D_MODEL),
        g_q_a=g_q_a.reshape(1, MLA_Q_LORA),
        wq1=wq1.reshape(MLA_Q_LORA, MLA_PAD).astype(BF16), wq2=wq2.reshape(MLA_Q_LORA, MLA_PAD).astype(BF16),
        g_kv_a=g_kv_a.reshape(1, MLA_KV_LORA),
        wk=wk.reshape(MLA_KV_LORA, MLA_PAD).astype(BF16),
        wv=w_uv.reshape(MLA_KV_LORA, MLA_WIDTH).astype(BF16),
        wuk_t=wuk_t.astype(BF16), wuv_h=wuv_h.astype(BF16),
        wgk=wgk.astype(BF16), b_gk=b_gk.reshape(1, GLA_QK),
        g_mla_out=g_mla_out.reshape(1, MLA_WIDTH), g_gla_out=g_gla_out.reshape(1, GLA_DV),
        w_o=w_o.astype(BF16),
        g_norm_ffn=g_norm_ffn.reshape(1, D_MODEL), w_router_pad=w_router_pad, b_router_pad=b_router_pad,
    )


def _rope_tables(pos, reps):
    half = MLA_ROPE // 2
    inv = ROPE_THETA ** (-jnp.arange(half, dtype=F32) / half)
    ang = pos.astype(F32)[:, None] * inv
    cos, sin = jnp.cos(ang), jnp.sin(ang)
    n = pos.shape[0]
    qc = jnp.concatenate([jnp.full((n, MLA_NOPE), Q_SCALE, F32), Q_SCALE * cos, Q_SCALE * cos,
                          jnp.zeros((n, HEAD_PAD - MLA_NOPE - MLA_ROPE), F32)], axis=1)
    qs = jnp.concatenate([jnp.zeros((n, MLA_NOPE), F32), Q_SCALE * sin, Q_SCALE * sin,
                          jnp.zeros((n, HEAD_PAD - MLA_NOPE - MLA_ROPE), F32)], axis=1)
    kc = jnp.concatenate([cos, cos, jnp.zeros((n, LANES - MLA_ROPE), F32)], axis=1)
    ks = jnp.concatenate([sin, sin, jnp.zeros((n, LANES - MLA_ROPE), F32)], axis=1)
    return tuple(jnp.tile(t, (reps, 1)) for t in (qc, qs, kc, ks))


def kernel(x_prompt, x_sample, cache_ckv, cache_kpe, state_gla, page_table, c_prompt, c_sample, w_ada, b_ada, g_norm_mix, w_in, g_q_a, w_q_b, g_kv_a, w_kv_b, w_gk_b, b_gk, g_mla_out, g_gla_out, w_o, g_norm_ffn, w_router, b_router, w_up, b_up, w_down, b_down, g_norm_final, w_ada_final, b_ada_final):
    B, S, D = x_prompt.shape
    DB, T, _ = x_sample.shape
    depth = w_ada.shape[0]
    assert depth == 1
    past_len = page_table.shape[1] * cache_ckv.shape[2]
    n_p, n_s = B * S, DB * T
    l = 0

    w = _prep_weights(w_in[l], g_q_a[l], w_q_b[l], g_kv_a[l], w_kv_b[l], w_gk_b[l], b_gk[l],
                      g_mla_out[l], g_gla_out[l], w_o[l], g_norm_mix[l], g_norm_ffn[l],
                      w_router[l], b_router[l])

    n_c = B + DB
    n_c_pad = (n_c + 7) // 8 * 8
    c_all = jnp.concatenate([c_sample, c_prompt, jnp.zeros((n_c_pad - n_c, D), F32)], axis=0)
    mod = ada_terms(c_all, w_ada[l], b_ada[l]).reshape(n_c_pad, 1, N_MOD * D)
    mod_f = ada_terms(c_all, w_ada_final, b_ada_final).reshape(n_c_pad, 1, 2 * D)
    off_s, off_p = 0, DB

    xp = x_prompt.reshape(n_p, D)
    xs = x_sample.reshape(n_s, D)
    tabs_p = _rope_tables(jnp.arange(S), 1)
    tabs_s = _rope_tables(past_len + jnp.arange(T), ROW_TILE // T)

    (q_p, k_p, v_p, ckv_p, kpe_p, gq_p, gk_p, gv_p, la_p, gg_p) = in_proj(
        xp, mod, S, off_p, tabs_p, w, True)
    o_mla_p = mla_prefill(q_p, k_p, v_p, B, S)
    s0 = jnp.zeros((B, GLA_HEADS, GLA_DK, GLA_DV), F32)
    o_gla_p, gla_p = gla(gq_p, gk_p, la_p, gv_p, s0, B, S)
    n_all = n_p + n_s
    h2_init = (jnp.zeros((n_all, PERM_W), jnp.uint32), jnp.zeros((n_all, PERM_W), jnp.uint32))
    x1_p, h2a, h2b, ti_p, tg_p, hist_p = mixer_out(xp, o_mla_p, o_gla_p.reshape(n_p, GLA_WIDTH), gg_p,
                                                   mod, S, off_p, w, n_all, 0, h2_init)

    (q_s, ckv_s, kpe_s, gq_s, gk_s, gv_s, la_s, gg_s) = in_proj(
        xs, mod, T, off_s, tabs_s, w, False)
    qlat, qpe = absorb_q(q_s, w["wuk_t"])
    o_lat = mla_decode(qlat, qpe, ckv_s, kpe_s, cache_ckv, cache_kpe, page_table, DB, T)
    o_mla_s = latent_to_values(o_lat, w["wuv_h"], DB, T)
    tpad = GLA_CHUNK
    padt = lambda a: jnp.pad(a.reshape(DB, T, a.shape[-1]), ((0, 0), (0, tpad - T), (0, 0))).reshape(
        DB * tpad, a.shape[-1])
    o_gla_s, gla_s = gla(padt(gq_s), padt(gk_s), padt(la_s), padt(gv_s), state_gla[l], DB, tpad)
    o_gla_s = o_gla_s.reshape(DB, tpad, GLA_WIDTH)[:, :T].reshape(n_s, GLA_WIDTH)
    x1_s, h2a, h2b, ti_s, tg_s, hist_s = mixer_out(xs, o_mla_s, o_gla_s, gg_s,
                                                   mod, T, off_s, w, n_all, n_p, (h2a, h2b))

    ti = jnp.concatenate([ti_p, ti_s], axis=0)
    base, block_e, block_rows, next_e, n_blocks = _route_tables(jnp.concatenate([hist_p, hist_s], axis=0), n_all)
    dest = route_rank(ti, base)
    idx = dest[:, :TOP_K].T.reshape(1, TOP_K * n_all)
    n_rows = n_blocks * MOE_TILE
    xs_a = sc_dispatch(h2a, idx, n_rows)
    xs_b = sc_dispatch(h2b, idx, n_rows)
    ys_a, ys_b = moe_experts(xs_a, xs_b, block_e, block_rows, next_e, w_up[l], b_up[l], w_down[l], b_down[l])
    yg_a = sc_combine(ys_a, idx).reshape(TOP_K, n_all, PERM_W)
    yg_b = sc_combine(ys_b, idx).reshape(TOP_K, n_all, PERM_W)

    g_fin = g_norm_final.reshape(1, D)
    y_p = final_out(x1_p, yg_a, yg_b, tg_p, 0, mod, mod_f, S, off_p, g_fin)
    y_s = final_out(x1_s, yg_a, yg_b, tg_s, n_p, mod, mod_f, T, off_s, g_fin)

    return (y_p.reshape(B, S, D), y_s.reshape(DB, T, D),
            ckv_p.reshape(1, B, S, MLA_KV_LORA), kpe_p.reshape(1, B, S, MLA_ROPE), gla_p[None],
            ckv_s.reshape(1, DB, T, MLA_KV_LORA), kpe_s.reshape(1, DB, T, MLA_ROPE), gla_s[None])
```

```python
import functools
import math

import jax
import jax.numpy as jnp
import numpy as np
from jax import lax
from jax.experimental import pallas as pl
from jax.experimental.pallas import tpu as pltpu
from jax.experimental.pallas import tpu_sc as plsc

F32 = jnp.float32
BF16 = jnp.bfloat16

D_MODEL = 1024
MLA_HEADS = 8
MLA_NOPE = 64
MLA_ROPE = 32
MLA_V = 64
MLA_Q_LORA = 384
MLA_KV_LORA = 256
MLA_SCALE = (MLA_NOPE + MLA_ROPE) ** -0.5
ROPE_THETA = 10000.0
GLA_HEADS = 4
GLA_DK = 64
GLA_DV = 128
GLA_GATE_RANK = 16
GLA_GATE_NORM = 16.0
GLA_CHUNK = 16
GLA_QK = GLA_HEADS * GLA_DK
GLA_WIDTH = GLA_HEADS * GLA_DV
N_EXPERTS = 32
TOP_K = 4
D_FF = D_MODEL
SWIGLU_LIMIT = 7.0
SWIGLU_ALPHA = 1.702
N_MOD = 6
EPS = 1e-6
PAGE_SIZE = 128

LANES = 128
HEAD_PAD = LANES
MLA_PAD = MLA_HEADS * HEAD_PAD
MLA_WIDTH = MLA_HEADS * MLA_V
VMEM_LIMIT = 56 * 1024 * 1024

Q_SCALE = MLA_SCALE * math.log2(math.e)

C_QA = 0
C_KV = C_QA + MLA_Q_LORA
C_GQ = C_KV + MLA_KV_LORA
C_GK = C_GQ + GLA_QK
C_GV = C_GK + GLA_QK
C_GG = C_GV + GLA_WIDTH
C_MISC = C_GG + GLA_WIDTH
C_SWAP = C_MISC + LANES
D_IN_PAD = C_SWAP + LANES

ROW_TILE = 512
RANK_TILE = 1024
ATT_TILE = 512
ATT_HEADS = 4
GLA_BLOCK = 256
DEC_SPANS = 4
SC_WIN = 128
PERM_W = D_MODEL // 4
MOE_TILE = 768
MOE_SUB = 128
NEW_PAD = 16


def _cparams(sem):
    return pltpu.CompilerParams(dimension_semantics=sem, vmem_limit_bytes=VMEM_LIMIT)


def _rms(x, g):
    return x * lax.rsqrt(jnp.mean(x * x, axis=-1, keepdims=True) + EPS) * g


def _mod_spec(term, seq_rows, row_off, tm):
    if seq_rows >= tm:
        tiles_per_seq = seq_rows // tm
        return pl.BlockSpec((1, 1, D_MODEL), lambda i: (row_off + i // tiles_per_seq, 0, term))
    g = tm // seq_rows
    assert row_off % g == 0
    return pl.BlockSpec((g, 1, D_MODEL), lambda i: (row_off // g + i, 0, term))


def _mod_rows(ref, rows):
    m = ref[...]
    g = m.shape[0]
    if g == 1:
        return m[0]
    return jnp.broadcast_to(m, (g, rows // g, m.shape[2])).reshape(rows, m.shape[2])


def _pack_rows(x):
    bits = lax.bitcast_convert_type(x.astype(BF16).astype(F32), jnp.uint32)
    w = x.shape[1] // 2
    return (bits[:, :w] >> 16) | bits[:, w:]


def _unpack_rows(words):
    lo = lax.bitcast_convert_type(words << 16, F32)
    hi = lax.bitcast_convert_type(words & jnp.uint32(0xFFFF0000), F32)
    return jnp.concatenate([lo, hi], axis=1)


def _ada_kernel(c_ref, w_ref, b_ref, o_ref):
    c = c_ref[...]
    a = (c * jax.nn.sigmoid(c)).astype(BF16)
    o_ref[...] = jnp.dot(a, w_ref[...].astype(BF16), preferred_element_type=F32) + b_ref[...]


def ada_terms(c, w, b):
    rows, d = c.shape
    n = w.shape[1]
    tn = 512
    return pl.pallas_call(
        _ada_kernel,
        grid=(n // tn,),
        in_specs=[pl.BlockSpec((rows, d), lambda j: (0, 0)),
                  pl.BlockSpec((d, tn), lambda j: (0, j)),
                  pl.BlockSpec((1, tn), lambda j: (0, j))],
        out_specs=pl.BlockSpec((rows, tn), lambda j: (0, j)),
        out_shape=jax.ShapeDtypeStruct((rows, n), F32),
        compiler_params=_cparams(("arbitrary",)),
        name="ada_terms",
    )(c, w, b.reshape(1, n))


def _in_kernel(with_kv, x_ref, sh_ref, sc_ref, g_ref, win_ref, gqa_ref, wq1_ref, wq2_ref,
               gkv_ref, wk_ref, wv_ref, wgk_ref, bgk_ref, qc_ref, qs_ref, kc_ref, ks_ref, *outs):
    if with_kv:
        q_out, k_out, v_out, ckv_out, kpe_out, gq_out, gk_out, gv_out, la_out, gg_out = outs
    else:
        q_out, ckv_out, kpe_out, gq_out, gk_out, gv_out, la_out, gg_out = outs
    x = x_ref[...]
    h = _rms(x, g_ref[...])
    rows = x.shape[0]
    h = h * (1.0 + _mod_rows(sc_ref, rows)) + _mod_rows(sh_ref, rows)
    proj = jnp.dot(h.astype(BF16), win_ref[...], preferred_element_type=F32)

    qn = _rms(proj[:, C_QA:C_KV], gqa_ref[...]).astype(BF16)
    qa = jnp.dot(qn, wq1_ref[...], preferred_element_type=F32)
    qb = jnp.dot(qn, wq2_ref[...], preferred_element_type=F32)
    qc = qc_ref[...]
    qs = qs_ref[...]
    for hh in range(MLA_HEADS):
        sl = slice(hh * HEAD_PAD, (hh + 1) * HEAD_PAD)
        q_out[:, sl] = (qa[:, sl] * qc + qb[:, sl] * qs).astype(BF16)

    ckv = _rms(proj[:, C_KV:C_GQ], gkv_ref[...])
    ckv_out[...] = ckv
    kpe = proj[:, C_MISC:C_SWAP] * kc_ref[...] + proj[:, C_SWAP:D_IN_PAD] * ks_ref[...]
    kpe_out[...] = kpe[:, :MLA_ROPE]
    if with_kv:
        ckv_b = ckv.astype(BF16)
        kn = jnp.dot(ckv_b, wk_ref[...], preferred_element_type=F32)
        kpe_sh = pltpu.roll(kpe, MLA_NOPE, axis=1)
        for hh in range(MLA_HEADS):
            sl = slice(hh * HEAD_PAD, (hh + 1) * HEAD_PAD)
            k_out[:, sl] = (kn[:, sl] + kpe_sh).astype(BF16)
        v_out[...] = jnp.dot(ckv_b, wv_ref[...], preferred_element_type=F32).astype(BF16)

    gq_out[...] = proj[:, C_GQ:C_GK] * (GLA_DK ** -0.5)
    gk_out[...] = proj[:, C_GK:C_GV]
    gv_out[...] = proj[:, C_GV:C_GG].astype(BF16)
    gg_out[...] = proj[:, C_GG:C_MISC].astype(BF16)
    xg = jnp.dot(proj[:, C_MISC:C_SWAP].astype(BF16), wgk_ref[...],
                 preferred_element_type=F32) + bgk_ref[...]
    la_out[...] = (jnp.minimum(xg, 0.0) - jnp.log(1.0 + jnp.exp(-jnp.abs(xg)))) * (1.0 / GLA_GATE_NORM)


def in_proj(x, mod, seq_rows, row_off, tabs, wts, with_kv):
    n = x.shape[0]
    tm = ROW_TILE
    nt = n // tm
    qc, qs, kc, ks = tabs
    n_tab = qc.shape[0] // tm
    row = lambda i: (i, 0)
    const = lambda i: (0, 0)
    tab_spec = pl.BlockSpec((tm, LANES), lambda i: (i % n_tab, 0))
    w = wts
    in_specs = [
        pl.BlockSpec((tm, D_MODEL), row),
        _mod_spec(0, seq_rows, row_off, tm), _mod_spec(1, seq_rows, row_off, tm),
        pl.BlockSpec((1, D_MODEL), const),
        pl.BlockSpec((D_MODEL, D_IN_PAD), const),
        pl.BlockSpec((1, MLA_Q_LORA), const),
        pl.BlockSpec((MLA_Q_LORA, MLA_PAD), const),
        pl.BlockSpec((MLA_Q_LORA, MLA_PAD), const),
        pl.BlockSpec((1, MLA_KV_LORA), const),
        pl.BlockSpec((MLA_KV_LORA, MLA_PAD), const),
        pl.BlockSpec((MLA_KV_LORA, MLA_WIDTH), const),
        pl.BlockSpec((LANES, GLA_QK), const),
        pl.BlockSpec((1, GLA_QK), const),
        tab_spec, tab_spec, tab_spec, tab_spec,
    ]
    wide = lambda dt: (jax.ShapeDtypeStruct((n, MLA_PAD), dt), pl.BlockSpec((tm, MLA_PAD), row))
    outs = [wide(BF16)]
    if with_kv:
        outs += [wide(BF16),
                 (jax.ShapeDtypeStruct((n, MLA_WIDTH), BF16), pl.BlockSpec((tm, MLA_WIDTH), row))]
    outs += [
        (jax.ShapeDtypeStruct((n, MLA_KV_LORA), F32), pl.BlockSpec((tm, MLA_KV_LORA), row)),
        (jax.ShapeDtypeStruct((n, MLA_ROPE), F32), pl.BlockSpec((tm, MLA_ROPE), row)),
        (jax.ShapeDtypeStruct((n, GLA_QK), F32), pl.BlockSpec((tm, GLA_QK), row)),
        (jax.ShapeDtypeStruct((n, GLA_QK), F32), pl.BlockSpec((tm, GLA_QK), row)),
        (jax.ShapeDtypeStruct((n, GLA_WIDTH), BF16), pl.BlockSpec((tm, GLA_WIDTH), row)),
        (jax.ShapeDtypeStruct((n, GLA_QK), F32), pl.BlockSpec((tm, GLA_QK), row)),
        (jax.ShapeDtypeStruct((n, GLA_WIDTH), BF16), pl.BlockSpec((tm, GLA_WIDTH), row)),
    ]
    return pl.pallas_call(
        functools.partial(_in_kernel, with_kv),
        grid=(nt,),
        in_specs=in_specs,
        out_specs=[o[1] for o in outs],
        out_shape=[o[0] for o in outs],
        compiler_params=_cparams(("parallel",)),
        name="inproj_prompt" if with_kv else "inproj_sample",
    )(x, mod, mod, w["g_norm_mix"], w["w_in"], w["g_q_a"], w["wq1"], w["wq2"], w["g_kv_a"],
      w["wk"], w["wv"], w["wgk"], w["b_gk"], qc, qs, kc, ks)


def _prefill_kernel(q_ref, k_ref, v_ref, o_ref):
    qi = pl.program_id(2)
    t = ATT_TILE
    lanes = [slice(g * HEAD_PAD, (g + 1) * HEAD_PAD) for g in range(ATT_HEADS)]
    v_lanes = [slice(g * MLA_V, (g + 1) * MLA_V) for g in range(ATT_HEADS)]
    qs = [q_ref[0, :, sl] for sl in lanes]

    def update_all(carries, r0, n_keys, mask):
        r0 = pl.multiple_of(r0, t)
        scores = [lax.dot_general(k_ref[0, pl.ds(r0, n_keys), sl], q, (((1,), (1,)), ((), ())),
                                  preferred_element_type=F32) for q, sl in zip(qs, lanes)]
        stats = []
        for (m, l, acc), s in zip(carries, scores):
            if mask is not None:
                s = jnp.where(mask, s, -jnp.inf)
            m_new = jnp.maximum(m, jnp.max(s, axis=0, keepdims=True))
            p = jnp.exp2(s - m_new)
            alpha = jnp.exp2(m - m_new)
            stats.append((m_new, alpha * l + jnp.sum(p, axis=0, keepdims=True), alpha * acc, p.astype(BF16)))
        out = []
        for (m_new, l_new, acc_scaled, p), sl in zip(stats, v_lanes):
            pv = lax.dot_general(v_ref[0, pl.ds(r0, n_keys), sl], p, (((0,), (0,)), ((), ())),
                                 preferred_element_type=F32)
            out.append((m_new, l_new, acc_scaled + pv))
        return tuple(out)

    def body(jj, carries):
        return update_all(carries, jj * (2 * t), 2 * t, None)

    init = (jnp.full((1, t), -jnp.inf, F32), jnp.zeros((1, t), F32), jnp.zeros((MLA_V, t), F32))
    carries = lax.fori_loop(0, qi // 2, body, (init,) * ATT_HEADS)

    def causal(n_keys):
        key_pos = lax.broadcasted_iota(jnp.int32, (n_keys, t), 0)
        qry_pos = lax.broadcasted_iota(jnp.int32, (n_keys, t), 1) + (n_keys - t)
        return key_pos <= qry_pos

    carries = lax.cond(qi % 2 == 1,
                       lambda c: update_all(c, (qi - 1) * t, 2 * t, causal(2 * t)),
                       lambda c: update_all(c, qi * t, t, causal(t)),
                       carries)
    o_ref[0] = jnp.concatenate([(acc / l).T for _, l, acc in carries], axis=1).astype(BF16)


def mla_prefill(q, k, v, batch, seq):
    t = ATT_TILE
    q3 = q.reshape(batch, seq, MLA_PAD)
    k3 = k.reshape(batch, seq, MLA_PAD)
    v3 = v.reshape(batch, seq, MLA_WIDTH)
    o = pl.pallas_call(
        _prefill_kernel,
        grid=(batch, MLA_HEADS // ATT_HEADS, seq // t),
        in_specs=[pl.BlockSpec((1, t, ATT_HEADS * HEAD_PAD), lambda b, h, i: (b, i, h)),
                  pl.BlockSpec((1, seq, ATT_HEADS * HEAD_PAD), lambda b, h, i: (b, 0, h)),
                  pl.BlockSpec((1, seq, ATT_HEADS * MLA_V), lambda b, h, i: (b, 0, h))],
        out_specs=pl.BlockSpec((1, t, ATT_HEADS * MLA_V), lambda b, h, i: (b, i, h)),
        out_shape=jax.ShapeDtypeStruct((batch, seq, MLA_WIDTH), BF16),
        compiler_params=_cparams(("parallel", "parallel", "arbitrary")),
        name="mla_prefill",
    )(q3, k3, v3)
    return o.reshape(batch * seq, MLA_WIDTH)


def _absorb_kernel(q_ref, wuk_ref, qlat_ref, qpe_ref):
    q = q_ref[...]
    qlat_ref[0] = jnp.dot(q, wuk_ref[0], preferred_element_type=F32)
    qf = pltpu.roll(q.astype(F32), HEAD_PAD - MLA_NOPE, axis=1)
    lane = lax.broadcasted_iota(jnp.int32, qf.shape, 1)
    qpe_ref[0] = jnp.where(lane < MLA_ROPE, qf, 0.0)


def absorb_q(q, wuk_t):
    r = q.shape[0]
    return pl.pallas_call(
        _absorb_kernel,
        grid=(MLA_HEADS,),
        in_specs=[pl.BlockSpec((r, HEAD_PAD), lambda h: (0, h)),
                  pl.BlockSpec((1, HEAD_PAD, MLA_KV_LORA), lambda h: (h, 0, 0))],
        out_specs=[pl.BlockSpec((1, r, MLA_KV_LORA), lambda h: (h, 0, 0)),
                   pl.BlockSpec((1, r, HEAD_PAD), lambda h: (h, 0, 0))],
        out_shape=[jax.ShapeDtypeStruct((MLA_HEADS, r, MLA_KV_LORA), F32),
                   jax.ShapeDtypeStruct((MLA_HEADS, r, HEAD_PAD), F32)],
        compiler_params=_cparams(("parallel",)),
        name="absorb_q",
    )(q, wuk_t)


def _decode_kernel(t_new, pt_ref, qlat_ref, qpe_ref, cnew_ref, knew_ref, ckv_hbm, kpe_hbm, o_ref,
                   cbuf, pbuf, sem):
    b = pl.program_id(0)
    nb = pl.num_programs(0)
    n_pages = cbuf.shape[1]
    slot = lax.rem(b, 2)
    rows = MLA_HEADS * t_new

    def page_copies(bb, sl, p):
        pg = pt_ref[bb, p]
        return (pltpu.make_async_copy(ckv_hbm.at[0, pg], cbuf.at[sl, p], sem.at[0, sl]),
                pltpu.make_async_copy(kpe_hbm.at[0, pg], pbuf.at[sl, p], sem.at[1, sl]))

    def fetch(bb, sl):
        def body(p, c):
            latent_cp, rope_cp = page_copies(bb, sl, p)
            latent_cp.start()
            rope_cp.start(priority=1)
            return c
        lax.fori_loop(0, n_pages, body, 0)

    @pl.when(b == 0)
    def _():
        fetch(0, 0)

    @pl.when(b + 1 < nb)
    def _():
        fetch(b + 1, 1 - slot)

    pltpu.make_async_copy(ckv_hbm.at[0, pl.ds(0, n_pages)], cbuf.at[slot], sem.at[0, slot]).wait()
    pltpu.make_async_copy(kpe_hbm.at[0, pl.ds(0, n_pages)], pbuf.at[slot], sem.at[1, slot]).wait()

    qlat = qlat_ref[...].reshape(rows, MLA_KV_LORA).astype(BF16)
    qpe = qpe_ref[...].reshape(rows, HEAD_PAD)[:, :MLA_ROPE].astype(BF16)
    dn = (((1,), (1,)), ((), ()))

    span_pages = n_pages // DEC_SPANS
    values, scores = [], []
    for c in range(DEC_SPANS):
        pages = range(c * span_pages, (c + 1) * span_pages)
        kb = jnp.concatenate([cbuf[slot, p].astype(BF16) for p in pages], axis=0)
        pb = jnp.concatenate([pbuf[slot, p].astype(BF16) for p in pages], axis=1)
        values.append(kb)
        scores.append(lax.dot_general(qlat, kb, dn, preferred_element_type=F32)
                      + jnp.dot(qpe, pb, preferred_element_type=F32))
    cn = cnew_ref[0].astype(BF16)
    kn = knew_ref[0].astype(BF16)
    sn = (lax.dot_general(qlat, cn, dn, preferred_element_type=F32)
          + lax.dot_general(qpe, kn, dn, preferred_element_type=F32))
    tq = lax.broadcasted_iota(jnp.int32, sn.shape, 0) % t_new
    tk = lax.broadcasted_iota(jnp.int32, sn.shape, 1)
    values.append(cn)
    scores.append(jnp.where(tk <= tq, sn, -jnp.inf))

    maxes = [jnp.max(s, axis=-1, keepdims=True) for s in scores]
    m = maxes[0]
    for pm in maxes[1:]:
        m = jnp.maximum(m, pm)
    probs = [jnp.exp2(s - pm) for s, pm in zip(scores, maxes)]
    l = jnp.zeros_like(m)
    acc = jnp.zeros((rows, MLA_KV_LORA), F32)
    for p, pm, vals in zip(probs, maxes, values):
        scale = jnp.exp2(pm - m)
        l = l + scale * jnp.sum(p, axis=-1, keepdims=True)
        acc = acc + scale * jnp.dot(p.astype(BF16), vals, preferred_element_type=F32)
    o_ref[0] = acc / l


def mla_decode(qlat, qpe, ckv_new, kpe_new, cache_ckv, cache_kpe, page_table, dec_batch, t_new):
    n_pages = page_table.shape[1]
    past_len = n_pages * PAGE_SIZE
    rows = MLA_HEADS * t_new
    kpe_t = jnp.swapaxes(cache_kpe, 2, 3)
    qlat4 = qlat.reshape(MLA_HEADS, dec_batch, t_new, MLA_KV_LORA)
    qpe4 = qpe.reshape(MLA_HEADS, dec_batch, t_new, HEAD_PAD)
    t_pad = NEW_PAD
    pad_new = lambda a: jnp.pad(a.reshape(dec_batch, t_new, a.shape[-1]), ((0, 0), (0, t_pad - t_new), (0, 0)))
    cnew = pad_new(ckv_new)
    knew = pad_new(kpe_new)

    in_specs = [
        pl.BlockSpec((MLA_HEADS, 1, t_new, MLA_KV_LORA), lambda b, pt: (0, b, 0, 0)),
        pl.BlockSpec((MLA_HEADS, 1, t_new, HEAD_PAD), lambda b, pt: (0, b, 0, 0)),
        pl.BlockSpec((1, t_pad, MLA_KV_LORA), lambda b, pt: (b, 0, 0)),
        pl.BlockSpec((1, t_pad, MLA_ROPE), lambda b, pt: (b, 0, 0)),
        pl.BlockSpec(memory_space=pl.ANY),
        pl.BlockSpec(memory_space=pl.ANY),
    ]
    grid_spec = pltpu.PrefetchScalarGridSpec(
        num_scalar_prefetch=1,
        grid=(dec_batch,),
        in_specs=in_specs,
        out_specs=pl.BlockSpec((1, rows, MLA_KV_LORA), lambda b, pt: (b, 0, 0)),
        scratch_shapes=[pltpu.VMEM((2, n_pages, PAGE_SIZE, MLA_KV_LORA), F32),
                        pltpu.VMEM((2, n_pages, MLA_ROPE, PAGE_SIZE), F32),
                        pltpu.SemaphoreType.DMA((2, 2))],
    )
    return pl.pallas_call(
        functools.partial(_decode_kernel, t_new),
        grid_spec=grid_spec,
        out_shape=jax.ShapeDtypeStruct((dec_batch, rows, MLA_KV_LORA), F32),
        compiler_params=_cparams(("arbitrary",)),
        name="mla_decode",
    )(page_table, qlat4, qpe4, cnew, knew, cache_ckv, kpe_t)


def _uv_kernel(o_ref, wuv_ref, out_ref):
    outs = []
    for hh in range(2):
        o = o_ref[:, hh]
        o = o.reshape(o.shape[0] * o.shape[1], MLA_KV_LORA).astype(BF16)
        outs.append(jnp.dot(o, wuv_ref[hh], preferred_element_type=F32))
    out_ref[...] = jnp.concatenate(outs, axis=1).astype(BF16)


def latent_to_values(o_lat, wuv, dec_batch, t_new):
    o4 = o_lat.reshape(dec_batch, MLA_HEADS, t_new, MLA_KV_LORA)
    return pl.pallas_call(
        _uv_kernel,
        grid=(MLA_HEADS // 2,),
        in_specs=[pl.BlockSpec((dec_batch, 2, t_new, MLA_KV_LORA), lambda h: (0, h, 0, 0)),
                  pl.BlockSpec((2, MLA_KV_LORA, MLA_V), lambda h: (h, 0, 0))],
        out_specs=pl.BlockSpec((dec_batch * t_new, 2 * MLA_V), lambda h: (0, h)),
        out_shape=jax.ShapeDtypeStruct((dec_batch * t_new, MLA_WIDTH), BF16),
        compiler_params=_cparams(("parallel",)),
        name="latent_to_values",
    )(o4, wuv)


def _gla_kernel(nsb, q_ref, k_ref, la_ref, v_ref, s0_ref, o_ref, sfin_ref, st_sc, kv_sc, sall_sc):
    blk = pl.program_id(1)
    nblk = pl.num_programs(1)
    c = GLA_CHUNK
    tb = q_ref.shape[1]
    nc = tb // c
    cps = nc // nsb
    head_k = [slice(hh * GLA_DK, (hh + 1) * GLA_DK) for hh in range(GLA_HEADS)]
    head_v = [slice(hh * GLA_DV, (hh + 1) * GLA_DV) for hh in range(GLA_HEADS)]

    @pl.when(blk == 0)
    def _():
        for sq in range(nsb):
            for hh in range(GLA_HEADS):
                st_sc[sq, hh] = s0_ref[sq, hh].T

    q = q_ref[0]
    k = k_ref[0]
    la = la_ref[0]
    v = v_ref[0]

    r = lax.broadcasted_iota(jnp.int32, (tb, tb), 0)
    cc = lax.broadcasted_iota(jnp.int32, (tb, tb), 1)
    same = (r >> 4) == (cc >> 4)
    tri = jnp.where(same & (cc <= r), 1.0, 0.0)
    ones = jnp.where(same, 1.0, 0.0)
    sel = jnp.concatenate([tri, ones], axis=0).astype(BF16)
    la_hi = la.astype(BF16)
    rest = la - la_hi.astype(F32)
    la_mid = rest.astype(BF16)
    la_lo = (rest - la_mid.astype(F32)).astype(BF16)
    sums = (jnp.dot(sel, la_hi, preferred_element_type=F32)
            + (jnp.dot(sel, la_mid, preferred_element_type=F32)
               + jnp.dot(sel, la_lo, preferred_element_type=F32)))
    b = sums[:tb]
    b_last = sums[tb:]
    qd = (q * jnp.exp(b)).astype(BF16)
    kd = (k * jnp.exp(b_last - b)).astype(BF16)
    dec = jnp.exp(b_last)

    key_head = lax.broadcasted_iota(jnp.int32, (GLA_QK, GLA_HEADS * c), 0) >> 6
    col = lax.broadcasted_iota(jnp.int32, (GLA_QK, GLA_HEADS * c), 1)
    b3 = b.reshape(nc, c, GLA_QK)
    k3 = k.reshape(nc, c, GLA_QK)
    q3 = q.reshape(nc, c, GLA_QK)
    pos = lax.broadcasted_iota(jnp.int32, (nc, c, GLA_QK), 1)
    scores = jnp.zeros((tb, GLA_HEADS * c), F32)
    for j in range(c):
        e = jnp.exp(b3 - b3[:, j:j + 1, :])
        w = jnp.where(pos >= j, q3 * k3[:, j:j + 1, :] * e, 0.0)
        place = jnp.where(col == key_head * c + j, 1.0, 0.0).astype(BF16)
        scores = scores + jnp.dot(w.reshape(tb, GLA_QK).astype(BF16), place, preferred_element_type=F32)
    scores = scores.astype(BF16)

    for ci in range(nc):
        rs = slice(ci * c, (ci + 1) * c)
        for hh in range(GLA_HEADS):
            o_ref[0, rs, head_v[hh]] = jnp.dot(scores[rs, hh * c:(hh + 1) * c], v[rs, head_v[hh]],
                                               preferred_element_type=F32)
            kv_sc[ci, hh] = lax.dot_general(v[rs, head_v[hh]], kd[rs, head_k[hh]], (((0,), (0,)), ((), ())),
                                            preferred_element_type=F32)

    for sq in range(nsb):
        states = [st_sc[sq, hh] for hh in range(GLA_HEADS)]
        for cj in range(cps):
            ci = sq * cps + cj
            dec_row = dec[ci * c:ci * c + 1, :]
            for hh in range(GLA_HEADS):
                sall_sc[ci, hh] = states[hh].astype(BF16)
                states[hh] = states[hh] * dec_row[:, head_k[hh]] + kv_sc[ci, hh]
        for hh in range(GLA_HEADS):
            st_sc[sq, hh] = states[hh]

    for ci in range(nc):
        rs = slice(ci * c, (ci + 1) * c)
        outs = [lax.dot_general(qd[rs, head_k[hh]], sall_sc[ci, hh], (((1,), (1,)), ((), ())),
                                preferred_element_type=F32) for hh in range(GLA_HEADS)]
        o_ref[0, rs, :] += jnp.concatenate(outs, axis=1)

    @pl.when(blk == nblk - 1)
    def _():
        for sq in range(nsb):
            for hh in range(GLA_HEADS):
                sfin_ref[sq, hh] = st_sc[sq, hh].T


def gla(gq, gk, la, gv, s0, n_seq, seq_len):
    tb = GLA_BLOCK
    nsb = max(1, tb // seq_len)
    nblk = max(1, seq_len // tb)
    n_outer = n_seq // nsb
    nc = tb // GLA_CHUNK
    sh3 = lambda a: a.reshape(n_outer, nblk * tb, a.shape[-1])
    row = lambda b, i: (b, i, 0)
    st_spec = pl.BlockSpec((nsb, GLA_HEADS, GLA_DK, GLA_DV), lambda b, i: (b, 0, 0, 0))
    o, s_fin = pl.pallas_call(
        functools.partial(_gla_kernel, nsb),
        grid=(n_outer, nblk),
        in_specs=[pl.BlockSpec((1, tb, GLA_QK), row), pl.BlockSpec((1, tb, GLA_QK), row),
                  pl.BlockSpec((1, tb, GLA_QK), row), pl.BlockSpec((1, tb, GLA_WIDTH), row), st_spec],
        out_specs=[pl.BlockSpec((1, tb, GLA_WIDTH), row), st_spec],
        out_shape=[jax.ShapeDtypeStruct((n_outer, nblk * tb, GLA_WIDTH), F32),
                   jax.ShapeDtypeStruct((n_seq, GLA_HEADS, GLA_DK, GLA_DV), F32)],
        scratch_shapes=[pltpu.VMEM((nsb, GLA_HEADS, GLA_DV, GLA_DK), F32),
                        pltpu.VMEM((nc, GLA_HEADS, GLA_DV, GLA_DK), F32),
                        pltpu.VMEM((nc, GLA_HEADS, GLA_DV, GLA_DK), BF16)],
        compiler_params=_cparams(("parallel", "arbitrary")),
        name="gla_prompt" if nblk > 1 else "gla_sample",
    )(sh3(gq), sh3(gk), sh3(la), sh3(gv), s0)
    return o.reshape(n_seq * seq_len, GLA_WIDTH), s_fin


def _mix_kernel(x_ref, om_ref, og_ref, gg_ref, gta_ref, shf_ref, scf_ref, gm_ref, ggl_ref, wo_ref,
                gn_ref, wr_ref, br_ref, h2a_in, h2b_in, x1_ref, h2a_ref, h2b_ref, ti_ref, tg_ref, hist_ref):
    del h2a_in, h2b_in
    rows = x_ref.shape[0]
    gta, scf, shf = (_mod_rows(r, rows) for r in (gta_ref, scf_ref, shf_ref))
    part = lambda a, sl: a if a.shape[0] == 1 else a[sl]

    def pre_mix(sl):
        om = _rms(om_ref[sl, :].astype(F32), gm_ref[...])
        og = og_ref[sl, :]
        gg = gg_ref[sl, :].astype(F32)
        gate = gg * jax.nn.sigmoid(gg)
        parts = [_rms(og[:, hs], ggl_ref[...]) * gate[:, hs]
                 for hs in (slice(hh * GLA_DV, (hh + 1) * GLA_DV) for hh in range(GLA_HEADS))]
        return jnp.dot(jnp.concatenate([om] + parts, axis=1).astype(BF16), wo_ref[...],
                       preferred_element_type=F32)

    def residual_and_router(sl, proj):
        x1 = x_ref[sl, :] + part(gta, sl) * proj
        x1_ref[sl, :] = x1
        h2 = _rms(x1, gn_ref[...]) * (1.0 + part(scf, sl)) + part(shf, sl)
        words = _pack_rows(h2)
        h2a_ref[sl, :] = words[:, :PERM_W]
        h2b_ref[sl, :] = words[:, PERM_W:]
        h_hi = h2.astype(BF16)
        h_lo = (h2 - h_hi.astype(F32)).astype(BF16)
        return (jnp.dot(h_hi, wr_ref[0], preferred_element_type=F32)
                + (jnp.dot(h_hi, wr_ref[1], preferred_element_type=F32)
                   + jnp.dot(h_lo, wr_ref[0], preferred_element_type=F32))) + br_ref[...]

    def top4(sl, logits):
        lane_i = lax.broadcasted_iota(jnp.int32, logits.shape, 1)
        lane = lane_i.astype(F32)
        vals, idxs = [], []
        for _ in range(TOP_K):
            mx = jnp.max(logits, axis=-1, keepdims=True)
            ix = jnp.min(jnp.where(logits == mx, lane, float(LANES)), axis=-1, keepdims=True)
            vals.append(mx)
            idxs.append(ix)
            logits = jnp.where(lane == ix, -jnp.inf, logits)
        ex = [jnp.exp(vv - vals[0]) for vv in vals]
        den = ex[0] + ex[1] + ex[2] + ex[3]
        ti = jnp.zeros(logits.shape, F32)
        tg = jnp.zeros(logits.shape, F32)
        onehot = jnp.zeros(logits.shape, F32)
        for kk in range(TOP_K):
            ti = jnp.where(lane_i == kk, idxs[kk], ti)
            tg = jnp.where(lane_i == kk, ex[kk] / den, tg)
            onehot = onehot + jnp.where(lane == idxs[kk], 1.0, 0.0)
        ti_ref[sl, :] = ti.astype(jnp.int32)
        tg_ref[sl, :] = tg
        return jnp.sum(onehot, axis=0, keepdims=True)

    halves = (slice(0, rows // 2), slice(rows // 2, rows))
    projs = [pre_mix(sl) for sl in halves]
    logits = [residual_and_router(sl, p) for sl, p in zip(halves, projs)]
    counts = [top4(sl, lg) for sl, lg in zip(halves, logits)]
    hist_ref[0] = counts[0] + counts[1]


def mixer_out(x, o_mla, o_gla, gg, mod, seq_rows, row_off, w, n_all, row0, h2_buf):
    n = x.shape[0]
    tm = ROW_TILE
    t0 = row0 // tm
    row = lambda i: (i, 0)
    const = lambda i: (0, 0)
    extra_specs = [pl.BlockSpec(memory_space=pl.ANY)] * 2
    extra_args = list(h2_buf)
    mod_specs = [_mod_spec(term, seq_rows, row_off, tm) for term in (2, 3, 4)]
    d_mix = MLA_WIDTH + GLA_WIDTH
    n_in = 13
    return pl.pallas_call(
        _mix_kernel,
        grid=(n // tm,),
        in_specs=[pl.BlockSpec((tm, D_MODEL), row), pl.BlockSpec((tm, MLA_WIDTH), row),
                  pl.BlockSpec((tm, GLA_WIDTH), row), pl.BlockSpec((tm, GLA_WIDTH), row),
                  *mod_specs,
                  pl.BlockSpec((1, MLA_WIDTH), const), pl.BlockSpec((1, GLA_DV), const),
                  pl.BlockSpec((d_mix, D_MODEL), const), pl.BlockSpec((1, D_MODEL), const),
                  pl.BlockSpec((2, D_MODEL, LANES), lambda i: (0, 0, 0)),
                  pl.BlockSpec((1, LANES), const)] + extra_specs,
        out_specs=[pl.BlockSpec((tm, D_MODEL), row),
                   pl.BlockSpec((tm, PERM_W), lambda i: (i + t0, 0)),
                   pl.BlockSpec((tm, PERM_W), lambda i: (i + t0, 0)),
                   pl.BlockSpec((tm, LANES), row), pl.BlockSpec((tm, LANES), row),
                   pl.BlockSpec((1, 1, LANES), lambda i: (i, 0, 0))],
        out_shape=[jax.ShapeDtypeStruct((n, D_MODEL), F32),
                   jax.ShapeDtypeStruct((n_all, PERM_W), jnp.uint32),
                   jax.ShapeDtypeStruct((n_all, PERM_W), jnp.uint32),
                   jax.ShapeDtypeStruct((n, LANES), jnp.int32), jax.ShapeDtypeStruct((n, LANES), F32),
                   jax.ShapeDtypeStruct((n // tm, 1, LANES), F32)],
        input_output_aliases={n_in: 1, n_in + 1: 2},
        compiler_params=_cparams(("parallel",)),
        name="mixer_sample" if seq_rows < tm else "mixer_prompt",
    )(x, o_mla, o_gla, gg, mod, mod, mod, w["g_mla_out"], w["g_gla_out"], w["w_o"],
      w["g_norm_ffn"], w["w_router_pad"], w["b_router_pad"], *extra_args)


def _rank_kernel(ti_ref, base_ref, dest_ref):
    ti = ti_ref[...]
    tm = ti.shape[0]
    lane = lax.broadcasted_iota(jnp.int32, ti.shape, 1)
    cols = [ti[:, kk:kk + 1] for kk in range(TOP_K)]
    onehot = jnp.zeros(ti.shape, F32)
    for kk in range(TOP_K):
        onehot = onehot + jnp.where(lane == cols[kk], 1.0, 0.0)
    r = lax.broadcasted_iota(jnp.int32, (tm, tm), 0)
    c = lax.broadcasted_iota(jnp.int32, (tm, tm), 1)
    earlier = jnp.where(c < r, 1.0, 0.0).astype(BF16)
    pos = jnp.dot(earlier, onehot.astype(BF16), preferred_element_type=F32) + base_ref[0]
    out = jnp.zeros(ti.shape, F32)
    for kk in range(TOP_K):
        dk = jnp.sum(jnp.where(lane == cols[kk], pos, 0.0), axis=-1, keepdims=True)
        out = jnp.where(lane == kk, dk, out)
    dest_ref[...] = out.astype(jnp.int32)


def route_rank(ti, base):
    n = ti.shape[0]
    tm = RANK_TILE
    return pl.pallas_call(
        _rank_kernel,
        grid=(n // tm,),
        in_specs=[pl.BlockSpec((tm, LANES), lambda i: (i, 0)),
                  pl.BlockSpec((1, 1, LANES), lambda i: (i, 0, 0))],
        out_specs=pl.BlockSpec((tm, LANES), lambda i: (i, 0)),
        out_shape=jax.ShapeDtypeStruct((n, LANES), jnp.int32),
        compiler_params=_cparams(("parallel",)),
        name="route_rank",
    )(ti, base)


def _route_tables(hist, n_tok):
    tm = MOE_TILE
    h = hist[:, 0, :].astype(jnp.int32)
    h = h.reshape(-1, RANK_TILE // ROW_TILE, LANES).sum(axis=1)
    counts = jnp.sum(h, axis=0)
    padded = (counts + tm - 1) // tm * tm
    pad_ends = jnp.cumsum(padded)
    pad_starts = pad_ends - padded
    base = (pad_starts[None, :] + jnp.cumsum(h, axis=0) - h).astype(F32)[:, None, :]
    n_blocks = pl.cdiv(n_tok * TOP_K, tm) + N_EXPERTS
    n_active = (pad_ends[N_EXPERTS - 1] // tm).astype(jnp.int32)
    blk = jnp.arange(n_blocks, dtype=jnp.int32)
    blk_c = jnp.minimum(blk, n_active - 1)
    ends = pad_ends[:N_EXPERTS]
    block_e = jnp.minimum(jnp.sum((ends[None, :] <= (blk_c * tm)[:, None]).astype(jnp.int32), axis=1),
                          N_EXPERTS - 1).astype(jnp.int32)
    used_end = (pad_starts + counts)[:N_EXPERTS][block_e]
    block_rows = jnp.where(blk < n_active, jnp.clip(used_end - blk * tm, 0, tm), 0).astype(jnp.int32)
    experts = jnp.arange(N_EXPERTS, dtype=jnp.int32)
    later = (experts[None, :] > block_e[:, None]) & (counts[None, :N_EXPERTS] > 0)
    next_e = jnp.min(jnp.where(later, experts[None, :], N_EXPERTS), axis=1).astype(jnp.int32)
    return base, block_e, block_rows, next_e, n_blocks


def _sc_mesh():
    return plsc.VectorSubcoreMesh(core_axis_name="c", subcore_axis_name="s")


def sc_dispatch(x_rows, idx, n_out):
    n, wd = x_rows.shape
    win = SC_WIN
    nwin = n // win

    @functools.partial(pl.kernel, out_type=jax.ShapeDtypeStruct((n_out, wd), x_rows.dtype),
                       mesh=_sc_mesh(), scratch_types=[])
    def k(x_hbm, i_hbm, o_hbm):
        def body(x_vmem, i_vmem):
            pltpu.sync_copy(x_vmem, o_hbm.at[i_vmem.at[0]])

        pltpu.emit_pipeline(
            body,
            grid=(idx.shape[1] // win,),
            in_specs=[pl.BlockSpec((win, wd), lambda i: (i % nwin, 0)),
                      pl.BlockSpec((1, win), lambda i: (0, i))],
            out_specs=[],
            core_axis_name=("c", "s"),
            dimension_semantics=(pltpu.PARALLEL,),
        )(x_hbm, i_hbm)

    return k(x_rows, idx)


def sc_combine(y_rows, idx):
    wd = y_rows.shape[1]
    m = idx.shape[1]
    win = SC_WIN

    @functools.partial(pl.kernel, out_type=jax.ShapeDtypeStruct((m, wd), y_rows.dtype),
                       mesh=_sc_mesh(), scratch_types=[])
    def k(y_hbm, i_hbm, o_hbm):
        def body(i_vmem, o_vmem):
            pltpu.sync_copy(y_hbm.at[i_vmem.at[0]], o_vmem)

        pltpu.emit_pipeline(
            body,
            grid=(m // win,),
            in_specs=[pl.BlockSpec((1, win), lambda i: (0, i))],
            out_specs=[pl.BlockSpec((win, wd), lambda i: (i, 0))],
            core_axis_name=("c", "s"),
            dimension_semantics=(pltpu.PARALLEL,),
        )(i_hbm, o_hbm)

    return k(y_rows, idx)


def _moe_kernel(be_ref, nr_ref, ne_ref, xa_ref, xb_ref, wup_hbm, bup_ref, wdn_hbm, bdn_ref, ya_ref, yb_ref,
                wup_f32, wdn_f32, wup_sc, wdn_sc, sem):
    i = pl.program_id(0)
    n_real = nr_ref[i]
    expert = be_ref[i]
    prev = be_ref[jnp.maximum(i - 1, 0)]
    fresh = (i == 0) | (expert != prev)

    def weight_copies(ex):
        return (pltpu.make_async_copy(wup_hbm.at[ex], wup_f32, sem.at[0]),
                pltpu.make_async_copy(wdn_hbm.at[ex], wdn_f32, sem.at[1]))

    @pl.when(i == 0)
    def _():
        for cp in weight_copies(expert):
            cp.start()

    @pl.when((n_real > 0) & fresh)
    def _():
        for cp in weight_copies(expert):
            cp.wait()
        wup_sc[...] = wup_f32[...].astype(BF16)
        wdn_sc[...] = wdn_f32[...].astype(BF16)
        nxt = ne_ref[i]

        @pl.when(nxt < N_EXPERTS)
        def _():
            for cp in weight_copies(nxt):
                cp.start()

    n_sub = MOE_TILE // MOE_SUB
    live_subs = (n_real + (MOE_SUB - 1)) // MOE_SUB
    for live in range(n_sub + 1):
        m = live * MOE_SUB

        @pl.when(live_subs == live)
        def _():
            if m > 0:
                xb = _unpack_rows(jnp.concatenate([xa_ref[:m, :], xb_ref[:m, :]], axis=1)).astype(BF16)
                hu = jnp.dot(xb, wup_sc[...], preferred_element_type=F32) + bup_ref[0]
                gate = jnp.minimum(hu[:, :D_FF], SWIGLU_LIMIT)
                lin = jnp.clip(hu[:, D_FF:], -SWIGLU_LIMIT, SWIGLU_LIMIT)
                act = gate * jax.nn.sigmoid(SWIGLU_ALPHA * gate) * (lin + 1.0)
                y = jnp.dot(act.astype(BF16), wdn_sc[...], preferred_element_type=F32) + bdn_ref[0]
                words = _pack_rows(y)
                ya_ref[:m, :] = words[:, :PERM_W]
                yb_ref[:m, :] = words[:, PERM_W:]
            if m < MOE_TILE:
                ya_ref[m:, :] = jnp.zeros((MOE_TILE - m, PERM_W), jnp.uint32)
                yb_ref[m:, :] = jnp.zeros((MOE_TILE - m, PERM_W), jnp.uint32)


def moe_experts(xs_a, xs_b, block_e, block_rows, next_e, w_up, b_up, w_down, b_down):
    n_rows = xs_a.shape[0]
    tm = MOE_TILE
    n_blocks = n_rows // tm
    emap3 = lambda i, be, nr, ne: (be[i], 0, 0)
    rows = lambda i, be, nr, ne: (i, 0)
    grid_spec = pltpu.PrefetchScalarGridSpec(
        num_scalar_prefetch=3,
        grid=(n_blocks,),
        in_specs=[pl.BlockSpec((tm, PERM_W), rows),
                  pl.BlockSpec((tm, PERM_W), rows),
                  pl.BlockSpec(memory_space=pl.ANY),
                  pl.BlockSpec((1, 1, 2 * D_FF), emap3),
                  pl.BlockSpec(memory_space=pl.ANY),
                  pl.BlockSpec((1, 1, D_MODEL), emap3)],
        out_specs=[pl.BlockSpec((tm, PERM_W), rows),
                   pl.BlockSpec((tm, PERM_W), rows)],
        scratch_shapes=[pltpu.VMEM((D_MODEL, 2 * D_FF), F32), pltpu.VMEM((D_FF, D_MODEL), F32),
                        pltpu.VMEM((D_MODEL, 2 * D_FF), BF16), pltpu.VMEM((D_FF, D_MODEL), BF16),
                        pltpu.SemaphoreType.DMA((2,))],
    )
    return pl.pallas_call(
        _moe_kernel,
        grid_spec=grid_spec,
        out_shape=[jax.ShapeDtypeStruct((n_rows, PERM_W), jnp.uint32)] * 2,
        compiler_params=_cparams(("arbitrary",)),
        name="moe_experts",
    )(block_e, block_rows, next_e, xs_a, xs_b, w_up, b_up.reshape(N_EXPERTS, 1, 2 * D_FF), w_down,
      b_down.reshape(N_EXPERTS, 1, D_MODEL))


def _final_kernel(x1_ref, yga_ref, ygb_ref, tg_ref, gtf_ref, sh_ref, sc_ref, g_ref, y_ref):
    tg = tg_ref[...]
    moe = jnp.zeros(x1_ref.shape, F32)
    for kk in range(TOP_K):
        moe = moe + tg[:, kk:kk + 1] * _unpack_rows(jnp.concatenate([yga_ref[kk], ygb_ref[kk]], axis=1))
    rows = x1_ref.shape[0]
    x2 = x1_ref[...] + _mod_rows(gtf_ref, rows) * moe
    y_ref[...] = _rms(x2, g_ref[...]) * (1.0 + _mod_rows(sc_ref, rows)) + _mod_rows(sh_ref, rows)


def final_out(x1, yg_a, yg_b, tg, row0, mod, mod_f, seq_rows, row_off, g_final):
    n = x1.shape[0]
    tm = ROW_TILE
    t0 = row0 // tm
    row = lambda i: (i, 0)
    mod_specs = [_mod_spec(5, seq_rows, row_off, tm),
                 _mod_spec(0, seq_rows, row_off, tm), _mod_spec(1, seq_rows, row_off, tm)]
    return pl.pallas_call(
        _final_kernel,
        grid=(n // tm,),
        in_specs=[pl.BlockSpec((tm, D_MODEL), row),
                  pl.BlockSpec((TOP_K, tm, PERM_W), lambda i: (0, i + t0, 0)),
                  pl.BlockSpec((TOP_K, tm, PERM_W), lambda i: (0, i + t0, 0)),
                  pl.BlockSpec((tm, LANES), row),
                  *mod_specs,
                  pl.BlockSpec((1, D_MODEL), lambda i: (0, 0))],
        out_specs=pl.BlockSpec((tm, D_MODEL), row),
        out_shape=jax.ShapeDtypeStruct((n, D_MODEL), F32),
        compiler_params=_cparams(("parallel",)),
        name="final_sample" if seq_rows < tm else "final_prompt",
    )(x1, yg_a, yg_b, tg, mod, mod_f, mod_f, g_final)


def _prep_weights(w_in, g_q_a, w_q_b, g_kv_a, w_kv_b, w_gk_b, b_gk, g_mla_out, g_gla_out, w_o,
                  g_norm_mix, g_norm_ffn, w_router, b_router):
    sizes = (MLA_Q_LORA, MLA_KV_LORA, MLA_ROPE, GLA_QK, GLA_QK, GLA_WIDTH, GLA_GATE_RANK, GLA_WIDTH)
    offs = np.cumsum((0,) + sizes)
    part = lambda i: w_in[:, offs[i]:offs[i + 1]]
    half = MLA_ROPE // 2
    k_rope = part(2)
    misc = jnp.concatenate([k_rope, part(6), jnp.zeros((D_MODEL, LANES - MLA_ROPE - GLA_GATE_RANK), F32)], 1)
    swap = jnp.concatenate([-k_rope[:, half:], k_rope[:, :half],
                            jnp.zeros((D_MODEL, LANES - MLA_ROPE), F32)], 1)
    w_in_pad = jnp.concatenate([part(0), part(1), part(3), part(4), part(5), part(7), misc, swap], 1)

    pad_q = jnp.zeros((MLA_Q_LORA, MLA_HEADS, HEAD_PAD - MLA_NOPE - MLA_ROPE), F32)
    wq1 = jnp.concatenate([w_q_b, pad_q], axis=2)
    q_lo = w_q_b[:, :, MLA_NOPE:MLA_NOPE + half]
    q_hi = w_q_b[:, :, MLA_NOPE + half:]
    wq2 = jnp.concatenate([jnp.zeros((MLA_Q_LORA, MLA_HEADS, MLA_NOPE), F32), -q_hi, q_lo, pad_q], axis=2)
    pad_kv = jnp.zeros((MLA_KV_LORA, MLA_HEADS, HEAD_PAD - MLA_NOPE), F32)
    w_uk = w_kv_b[:, :, :MLA_NOPE]
    w_uv = w_kv_b[:, :, MLA_NOPE:]
    wk = jnp.concatenate([w_uk, pad_kv], axis=2)
    wuk_t = jnp.concatenate([jnp.transpose(w_uk, (1, 2, 0)),
                             jnp.zeros((MLA_HEADS, HEAD_PAD - MLA_NOPE, MLA_KV_LORA), F32)], axis=1)
    wuv_h = jnp.transpose(w_uv, (1, 0, 2))
    wgk = jnp.zeros((LANES, GLA_QK), F32).at[MLA_ROPE:MLA_ROPE + GLA_GATE_RANK].set(w_gk_b)
    w_router_pad = jnp.concatenate([w_router, jnp.zeros((D_MODEL, LANES - N_EXPERTS), F32)], axis=1)
    wr_hi = w_router_pad.astype(BF16)
    w_router_pad = jnp.stack([wr_hi, (w_router_pad - wr_hi.astype(F32)).astype(BF16)])
    b_router_pad = jnp.concatenate([b_router, jnp.full((LANES - N_EXPERTS,), -jnp.inf, F32)]).reshape(1, LANES)
    return dict(
        w_in=w_in_pad.astype(BF16), g_norm_mix=g_norm_mix.reshape(1, D_MODEL),
        g_q_a=g_q_a.reshape(1, MLA_Q_LORA),
        wq1=wq1.reshape(MLA_Q_LORA, MLA_PAD).astype(BF16), wq2=wq2.reshape(MLA_Q_LORA, MLA_PAD).astype(BF16),
        g_kv_a=g_kv_a.reshape(1, MLA_KV_LORA),
        wk=wk.reshape(MLA_KV_LORA, MLA_PAD).astype(BF16),
        wv=w_uv.reshape(MLA_KV_LORA, MLA_WIDTH).astype(BF16),
        wuk_t=wuk_t.astype(BF16), wuv_h=wuv_h.astype(BF16),
        wgk=wgk.astype(BF16), b_gk=b_gk.reshape(1, GLA_QK),
        g_mla_out=g_mla_out.reshape(1, MLA_WIDTH), g_gla_out=g_gla_out.reshape(1, GLA_DV),
        w_o=w_o.astype(BF16),
        g_norm_ffn=g_norm_ffn.reshape(1, D_MODEL), w_router_pad=w_router_pad, b_router_pad=b_router_pad,
    )


def _rope_tables(pos, reps):
    half = MLA_ROPE // 2
    inv = ROPE_THETA ** (-jnp.arange(half, dtype=F32) / half)
    ang = pos.astype(F32)[:, None] * inv
    cos, sin = jnp.cos(ang), jnp.sin(ang)
    n = pos.shape[0]
    qc = jnp.concatenate([jnp.full((n, MLA_NOPE), Q_SCALE, F32), Q_SCALE * cos, Q_SCALE * cos,
                          jnp.zeros((n, HEAD_PAD - MLA_NOPE - MLA_ROPE), F32)], axis=1)
    qs = jnp.concatenate([jnp.zeros((n, MLA_NOPE), F32), Q_SCALE * sin, Q_SCALE * sin,
                          jnp.zeros((n, HEAD_PAD - MLA_NOPE - MLA_ROPE), F32)], axis=1)
    kc = jnp.concatenate([cos, cos, jnp.zeros((n, LANES - MLA_ROPE), F32)], axis=1)
    ks = jnp.concatenate([sin, sin, jnp.zeros((n, LANES - MLA_ROPE), F32)], axis=1)
    return tuple(jnp.tile(t, (reps, 1)) for t in (qc, qs, kc, ks))


def kernel(x_prompt, x_sample, cache_ckv, cache_kpe, state_gla, page_table, c_prompt, c_sample, w_ada, b_ada, g_norm_mix, w_in, g_q_a, w_q_b, g_kv_a, w_kv_b, w_gk_b, b_gk, g_mla_out, g_gla_out, w_o, g_norm_ffn, w_router, b_router, w_up, b_up, w_down, b_down, g_norm_final, w_ada_final, b_ada_final):
    B, S, D = x_prompt.shape
    DB, T, _ = x_sample.shape
    depth = w_ada.shape[0]
    assert depth == 1
    past_len = page_table.shape[1] * cache_ckv.shape[2]
    n_p, n_s = B * S, DB * T
    l = 0

    w = _prep_weights(w_in[l], g_q_a[l], w_q_b[l], g_kv_a[l], w_kv_b[l], w_gk_b[l], b_gk[l],
                      g_mla_out[l], g_gla_out[l], w_o[l], g_norm_mix[l], g_norm_ffn[l],
                      w_router[l], b_router[l])

    n_c = B + DB
    n_c_pad = (n_c + 7) // 8 * 8
    c_all = jnp.concatenate([c_sample, c_prompt, jnp.zeros((n_c_pad - n_c, D), F32)], axis=0)
    mod = ada_terms(c_all, w_ada[l], b_ada[l]).reshape(n_c_pad, 1, N_MOD * D)
    mod_f = ada_terms(c_all, w_ada_final, b_ada_final).reshape(n_c_pad, 1, 2 * D)
    off_s, off_p = 0, DB

    xp = x_prompt.reshape(n_p, D)
    xs = x_sample.reshape(n_s, D)
    tabs_p = _rope_tables(jnp.arange(S), 1)
    tabs_s = _rope_tables(past_len + jnp.arange(T), ROW_TILE // T)

    (q_p, k_p, v_p, ckv_p, kpe_p, gq_p, gk_p, gv_p, la_p, gg_p) = in_proj(
        xp, mod, S, off_p, tabs_p, w, True)
    o_mla_p = mla_prefill(q_p, k_p, v_p, B, S)
    s0 = jnp.zeros((B, GLA_HEADS, GLA_DK, GLA_DV), F32)
    o_gla_p, gla_p = gla(gq_p, gk_p, la_p, gv_p, s0, B, S)
    n_all = n_p + n_s
    h2_init = (jnp.zeros((n_all, PERM_W), jnp.uint32), jnp.zeros((n_all, PERM_W), jnp.uint32))
    x1_p, h2a, h2b, ti_p, tg_p, hist_p = mixer_out(xp, o_mla_p, o_gla_p.reshape(n_p, GLA_WIDTH), gg_p,
                                                   mod, S, off_p, w, n_all, 0, h2_init)

    (q_s, ckv_s, kpe_s, gq_s, gk_s, gv_s, la_s, gg_s) = in_proj(
        xs, mod, T, off_s, tabs_s, w, False)
    qlat, qpe = absorb_q(q_s, w["wuk_t"])
    o_lat = mla_decode(qlat, qpe, ckv_s, kpe_s, cache_ckv, cache_kpe, page_table, DB, T)
    o_mla_s = latent_to_values(o_lat, w["wuv_h"], DB, T)
    tpad = GLA_CHUNK
    padt = lambda a: jnp.pad(a.reshape(DB, T, a.shape[-1]), ((0, 0), (0, tpad - T), (0, 0))).reshape(
        DB * tpad, a.shape[-1])
    o_gla_s, gla_s = gla(padt(gq_s), padt(gk_s), padt(la_s), padt(gv_s), state_gla[l], DB, tpad)
    o_gla_s = o_gla_s.reshape(DB, tpad, GLA_WIDTH)[:, :T].reshape(n_s, GLA_WIDTH)
    x1_s, h2a, h2b, ti_s, tg_s, hist_s = mixer_out(xs, o_mla_s, o_gla_s, gg_s,
                                                   mod, T, off_s, w, n_all, n_p, (h2a, h2b))

    ti = jnp.concatenate([ti_p, ti_s], axis=0)
    base, block_e, block_rows, next_e, n_blocks = _route_tables(jnp.concatenate([hist_p, hist_s], axis=0), n_all)
    dest = route_rank(ti, base)
    idx = dest[:, :TOP_K].T.reshape(1, TOP_K * n_all)
    n_rows = n_blocks * MOE_TILE
    xs_a = sc_dispatch(h2a, idx, n_rows)
    xs_b = sc_dispatch(h2b, idx, n_rows)
    ys_a, ys_b = moe_experts(xs_a, xs_b, block_e, block_rows, next_e, w_up[l], b_up[l], w_down[l], b_down[l])
    yg_a = sc_combine(ys_a, idx).reshape(TOP_K, n_all, PERM_W)
    yg_b = sc_combine(ys_b, idx).reshape(TOP_K, n_all, PERM_W)

    g_fin = g_norm_final.reshape(1, D)
    y_p = final_out(x1_p, yg_a, yg_b, tg_p, 0, mod, mod_f, S, off_p, g_fin)
    y_s = final_out(x1_s, yg_a, yg_b, tg_s, n_p, mod, mod_f, T, off_s, g_fin)

    return (y_p.reshape(B, S, D), y_s.reshape(DB, T, D),
            ckv_p.reshape(1, B, S, MLA_KV_LORA), kpe_p.reshape(1, B, S, MLA_ROPE), gla_p[None],
            ckv_s.reshape(1, DB, T, MLA_KV_LORA), kpe_s.reshape(1, DB, T, MLA_ROPE), gla_s[None])
```
